```python
import jax, jax.numpy as jnp
from jax import lax
import numpy as np

D_MODEL = 1024
BATCH = 8
SEQ = 4096
DEPTH = 4

N_MEM = 256
D_FF = 2816
D_CONV = D_MODEL // 2
CONV_WIDTH = 31
D_SGU = D_MODEL - D_CONV
SGU_HEADS = 4
SGU_HEAD_DIM = D_SGU // SGU_HEADS
CHUNK = 128
POOL_WINDOWS = (2, 4, 8, 16)
POOL_GROUPS = len(POOL_WINDOWS)
POOL_GROUP_DIM = D_MODEL // POOL_GROUPS
XA_HEADS = 4
XA_HEAD_DIM = D_MODEL // XA_HEADS
N_EVEN = (DEPTH + 1) // 2
N_ODD = DEPTH // 2
EPS = 1e-6

kernel_name = "hybrid_conv_sgu_pool_macaron_xattn"


def rms_norm(x, g):
    xf = x.astype(jnp.float32)
    y = xf * lax.rsqrt(jnp.mean(xf * xf, axis=-1, keepdims=True) + EPS)
    return (y * g.astype(jnp.float32)).astype(x.dtype)


def layer_norm(x, g, b):
    xf = x.astype(jnp.float32)
    mu = jnp.mean(xf, axis=-1, keepdims=True)
    var = jnp.mean(jnp.square(xf - mu), axis=-1, keepdims=True)
    y = (xf - mu) * lax.rsqrt(var + EPS)
    return (y * g.astype(jnp.float32) + b.astype(jnp.float32)).astype(x.dtype)


def swiglu_ffn(h, w_gu, w_down):
    gu = h @ w_gu
    gate, up = jnp.split(gu, 2, axis=-1)
    return (jax.nn.silu(gate) * up) @ w_down


def conformer_conv(a_val, a_gate, conv_w, conv_b, ln_g, ln_b):
    z = a_val * jax.nn.sigmoid(a_gate)
    rhs = conv_w[:, None, :].astype(z.dtype)
    z = lax.conv_general_dilated(
        z, rhs, window_strides=(1,), padding=[(CONV_WIDTH - 1, 0)],
        dimension_numbers=("NWC", "WIO", "NWC"), feature_group_count=z.shape[-1])
    z = z + conv_b
    z = layer_norm(z, ln_g, ln_b)
    return jax.nn.silu(z)


def chunked_causal_sgu(u, v, ln_g, ln_b, w_s, b_s):
    bsz, seq, _ = u.shape
    n_chunk = seq // CHUNK
    v = layer_norm(v, ln_g, ln_b)
    v = v.reshape(bsz, n_chunk, CHUNK, SGU_HEADS, SGU_HEAD_DIM)
    mask = jnp.tril(jnp.ones((CHUNK, CHUNK), dtype=w_s.dtype))
    w = w_s * mask[None]
    mixed = jnp.einsum("hts,bnshc->bnthc", w, v) + b_s.T[:, :, None]
    out = u.reshape(bsz, n_chunk, CHUNK, SGU_HEADS, SGU_HEAD_DIM) * mixed
    return out.reshape(bsz, seq, D_SGU)


def even_mixer(h, w_in, conv_w, conv_b, conv_ln_g, conv_ln_b,
               sgu_ln_g, sgu_ln_b, sgu_w, sgu_b, w_out):
    p = h @ w_in
    a_val = p[..., :D_CONV]
    a_gate = p[..., D_CONV:2 * D_CONV]
    zb = jax.nn.gelu(p[..., 2 * D_CONV:], approximate=False)
    b_u = zb[..., :D_SGU]
    b_v = zb[..., D_SGU:]
    ya = conformer_conv(a_val, a_gate, conv_w, conv_b, conv_ln_g, conv_ln_b)
    yb = chunked_causal_sgu(b_u, b_v, sgu_ln_g, sgu_ln_b, sgu_w, sgu_b)
    return jnp.concatenate([ya, yb], axis=-1) @ w_out


def odd_mixer(h, w_in, w_group, scale, w_out):
    p = h @ w_in
    bsz, seq, _ = p.shape
    pf = p.astype(jnp.float32)
    csum = jnp.cumsum(pf, axis=1)
    c_pad = jnp.concatenate([jnp.zeros((bsz, 1, D_MODEL), jnp.float32), csum], axis=1)
    pos = jnp.arange(1, seq + 1, dtype=jnp.int32)
    pooled = []
    for g, w in enumerate(POOL_WINDOWS):
        sl = slice(g * POOL_GROUP_DIM, (g + 1) * POOL_GROUP_DIM)
        upper = c_pad[:, 1:, sl]
        lower = jnp.concatenate(
            [jnp.zeros((bsz, w - 1, POOL_GROUP_DIM), jnp.float32), c_pad[:, :seq + 1 - w, sl]], axis=1)
        count = jnp.minimum(pos, w).astype(jnp.float32)[None, :, None]
        pooled.append((upper - lower) / count)
    d = (jnp.concatenate(pooled, axis=-1) - pf).astype(p.dtype)
    d = d.reshape(bsz, seq, POOL_GROUPS, POOL_GROUP_DIM)
    d = jnp.einsum("bsgc,gcd->bsgd", d, w_group).reshape(bsz, seq, D_MODEL)
    return (d * scale) @ w_out


def memory_cross_attention(h, mem, mem_g, w_q, w_kv, w_o):
    bsz, seq, _ = h.shape
    q = (h @ w_q).reshape(bsz, seq, XA_HEADS, XA_HEAD_DIM)
    kv = rms_norm(mem, mem_g) @ w_kv
    k, v = jnp.split(kv, 2, axis=-1)
    k = k.reshape(bsz, -1, XA_HEADS, XA_HEAD_DIM)
    v = v.reshape(bsz, -1, XA_HEADS, XA_HEAD_DIM)
    s = jnp.einsum("bshd,bmhd->bhsm", q, k).astype(jnp.float32) * (XA_HEAD_DIM ** -0.5)
    a = jax.nn.softmax(s, axis=-1).astype(v.dtype)
    o = jnp.einsum("bhsm,bmhd->bshd", a, v).reshape(bsz, seq, D_MODEL)
    return o @ w_o


def _fwd_setup_inputs(seed: int = 0) -> dict:
    key = jax.random.key(seed)
    ks = jax.random.split(key, 32)
    f32 = jnp.float32

    def nrm(k, shape, fan_in):
        return jax.random.normal(k, shape, f32) * (fan_in ** -0.5)

    def gain(k, shape):
        return 1.0 + 0.05 * jax.random.normal(k, shape, f32)

    def small(k, shape):
        return 0.02 * jax.random.normal(k, shape, f32)

    return {
        "x": jax.random.normal(ks[0], (BATCH, SEQ, D_MODEL), f32),
        "mem": jax.random.normal(ks[1], (BATCH, N_MEM, D_MODEL), f32),
        "ffn1_pre_g": gain(ks[2], (DEPTH, D_MODEL)),
        "ffn1_w_gu": nrm(ks[3], (DEPTH, D_MODEL, 2 * D_FF), D_MODEL),
        "ffn1_w_down": nrm(ks[4], (DEPTH, D_FF, D_MODEL), D_FF),
        "ffn1_post_g": gain(ks[5], (DEPTH, D_MODEL)),
        "mix_pre_g": gain(ks[6], (DEPTH, D_MODEL)),
        "mix_post_g": gain(ks[7], (DEPTH, D_MODEL)),
        "ev_w_in": nrm(ks[8], (N_EVEN, D_MODEL, 2 * D_CONV + 2 * D_SGU), D_MODEL),
        "ev_conv_w": nrm(ks[9], (N_EVEN, CONV_WIDTH, D_CONV), CONV_WIDTH),
        "ev_conv_b": small(ks[10], (N_EVEN, D_CONV)),
        "ev_conv_ln_g": gain(ks[11], (N_EVEN, D_CONV)),
        "ev_conv_ln_b": small(ks[12], (N_EVEN, D_CONV)),
        "ev_sgu_ln_g": gain(ks[13], (N_EVEN, D_SGU)),
        "ev_sgu_ln_b": small(ks[14], (N_EVEN, D_SGU)),
        "ev_sgu_w": nrm(ks[15], (N_EVEN, SGU_HEADS, CHUNK, CHUNK), CHUNK),
        "ev_sgu_b": gain(ks[16], (N_EVEN, SGU_HEADS, CHUNK)),
        "ev_w_out": nrm(ks[17], (N_EVEN, D_CONV + D_SGU, D_MODEL), D_CONV + D_SGU),
        "od_w_in": nrm(ks[18], (N_ODD, D_MODEL, D_MODEL), D_MODEL),
        "od_w_group": nrm(ks[19], (N_ODD, POOL_GROUPS, POOL_GROUP_DIM, POOL_GROUP_DIM), POOL_GROUP_DIM),
        "od_scale": gain(ks[20], (N_ODD, D_MODEL)),
        "od_w_out": nrm(ks[21], (N_ODD, D_MODEL, D_MODEL), D_MODEL),
        "xa_pre_g": gain(ks[22], (DEPTH, D_MODEL)),
        "xa_mem_g": gain(ks[23], (DEPTH, D_MODEL)),
        "xa_w_q": nrm(ks[24], (DEPTH, D_MODEL, D_MODEL), D_MODEL),
        "xa_w_kv": nrm(ks[25], (DEPTH, D_MODEL, 2 * D_MODEL), D_MODEL),
        "xa_w_o": nrm(ks[26], (DEPTH, D_MODEL, D_MODEL), D_MODEL),
        "xa_post_g": gain(ks[27], (DEPTH, D_MODEL)),
        "ffn2_pre_g": gain(ks[28], (DEPTH, D_MODEL)),
        "ffn2_w_gu": nrm(ks[29], (DEPTH, D_MODEL, 2 * D_FF), D_MODEL),
        "ffn2_w_down": nrm(ks[30], (DEPTH, D_FF, D_MODEL), D_FF),
        "ffn2_post_g": gain(ks[31], (DEPTH, D_MODEL)),
    }


def _fwd_reference(x, mem, ffn1_pre_g, ffn1_w_gu, ffn1_w_down, ffn1_post_g,
              mix_pre_g, mix_post_g,
              ev_w_in, ev_conv_w, ev_conv_b, ev_conv_ln_g, ev_conv_ln_b,
              ev_sgu_ln_g, ev_sgu_ln_b, ev_sgu_w, ev_sgu_b, ev_w_out,
              od_w_in, od_w_group, od_scale, od_w_out,
              xa_pre_g, xa_mem_g, xa_w_q, xa_w_kv, xa_w_o, xa_post_g,
              ffn2_pre_g, ffn2_w_gu, ffn2_w_down, ffn2_post_g):
    h = x
    for i in range(DEPTH):
        f = swiglu_ffn(rms_norm(h, ffn1_pre_g[i]), ffn1_w_gu[i], ffn1_w_down[i])
        h = h + 0.5 * rms_norm(f, ffn1_post_g[i])
        hn = rms_norm(h, mix_pre_g[i])
        if i % 2 == 0:
            e = i // 2
            m = even_mixer(hn, ev_w_in[e], ev_conv_w[e], ev_conv_b[e],
                           ev_conv_ln_g[e], ev_conv_ln_b[e], ev_sgu_ln_g[e],
                           ev_sgu_ln_b[e], ev_sgu_w[e], ev_sgu_b[e], ev_w_out[e])
        else:
            o = i // 2
            m = odd_mixer(hn, od_w_in[o], od_w_group[o], od_scale[o], od_w_out[o])
        h = h + rms_norm(m, mix_post_g[i])
        c = memory_cross_attention(rms_norm(h, xa_pre_g[i]), mem, xa_mem_g[i],
                                   xa_w_q[i], xa_w_kv[i], xa_w_o[i])
        h = h + rms_norm(c, xa_post_g[i])
        f = swiglu_ffn(rms_norm(h, ffn2_pre_g[i]), ffn2_w_gu[i], ffn2_w_down[i])
        h = h + 0.5 * rms_norm(f, ffn2_post_g[i])
    return h


import jax as _jax
import jax.numpy as _jnp

TWIN_FORMAT = 'train_step'
FWD_PARAMS = ['x', 'mem', 'ffn1_pre_g', 'ffn1_w_gu', 'ffn1_w_down', 'ffn1_post_g', 'mix_pre_g', 'mix_post_g', 'ev_w_in', 'ev_conv_w', 'ev_conv_b', 'ev_conv_ln_g', 'ev_conv_ln_b', 'ev_sgu_ln_g', 'ev_sgu_ln_b', 'ev_sgu_w', 'ev_sgu_b', 'ev_w_out', 'od_w_in', 'od_w_group', 'od_scale', 'od_w_out', 'xa_pre_g', 'xa_mem_g', 'xa_w_q', 'xa_w_kv', 'xa_w_o', 'xa_post_g', 'ffn2_pre_g', 'ffn2_w_gu', 'ffn2_w_down', 'ffn2_post_g']
TWIN_WEIGHTS = ['ffn1_pre_g', 'ffn1_w_gu', 'ffn1_w_down', 'ffn1_post_g', 'mix_pre_g', 'mix_post_g', 'ev_w_in', 'ev_conv_w', 'ev_conv_b', 'ev_conv_ln_g', 'ev_conv_ln_b', 'ev_sgu_ln_g', 'ev_sgu_ln_b', 'ev_sgu_w', 'ev_sgu_b', 'ev_w_out', 'od_w_in', 'od_w_group', 'od_scale', 'od_w_out', 'xa_pre_g', 'xa_mem_g', 'xa_w_q', 'xa_w_kv', 'xa_w_o', 'xa_post_g', 'ffn2_pre_g', 'ffn2_w_gu', 'ffn2_w_down', 'ffn2_post_g']
TWIN_DIFF_INPUT = 'x'
TWIN_INPUTS = ['x', 'mem', 'ffn1_pre_g', 'ffn1_w_gu', 'ffn1_w_down', 'ffn1_post_g', 'mix_pre_g', 'mix_post_g', 'ev_w_in', 'ev_conv_w', 'ev_conv_b', 'ev_conv_ln_g', 'ev_conv_ln_b', 'ev_sgu_ln_g', 'ev_sgu_ln_b', 'ev_sgu_w', 'ev_sgu_b', 'ev_w_out', 'od_w_in', 'od_w_group', 'od_scale', 'od_w_out', 'xa_pre_g', 'xa_mem_g', 'xa_w_q', 'xa_w_kv', 'xa_w_o', 'xa_post_g', 'ffn2_pre_g', 'ffn2_w_gu', 'ffn2_w_down', 'ffn2_post_g', 'loss_target', 'm_ffn1_pre_g', 'm_ffn1_w_gu', 'm_ffn1_w_down', 'm_ffn1_post_g', 'm_mix_pre_g', 'm_mix_post_g', 'm_ev_w_in', 'm_ev_conv_w', 'm_ev_conv_b', 'm_ev_conv_ln_g', 'm_ev_conv_ln_b', 'm_ev_sgu_ln_g', 'm_ev_sgu_ln_b', 'm_ev_sgu_w', 'm_ev_sgu_b', 'm_ev_w_out', 'm_od_w_in', 'm_od_w_group', 'm_od_scale', 'm_od_w_out', 'm_xa_pre_g', 'm_xa_mem_g', 'm_xa_w_q', 'm_xa_w_kv', 'm_xa_w_o', 'm_xa_post_g', 'm_ffn2_pre_g', 'm_ffn2_w_gu', 'm_ffn2_w_down', 'm_ffn2_post_g', 'v_ffn1_pre_g', 'v_ffn1_w_gu', 'v_ffn1_w_down', 'v_ffn1_post_g', 'v_mix_pre_g', 'v_mix_post_g', 'v_ev_w_in', 'v_ev_conv_w', 'v_ev_conv_b', 'v_ev_conv_ln_g', 'v_ev_conv_ln_b', 'v_ev_sgu_ln_g', 'v_ev_sgu_ln_b', 'v_ev_sgu_w', 'v_ev_sgu_b', 'v_ev_w_out', 'v_od_w_in', 'v_od_w_group', 'v_od_scale', 'v_od_w_out', 'v_xa_pre_g', 'v_xa_mem_g', 'v_xa_w_q', 'v_xa_w_kv', 'v_xa_w_o', 'v_xa_post_g', 'v_ffn2_pre_g', 'v_ffn2_w_gu', 'v_ffn2_w_down', 'v_ffn2_post_g']
TWIN_OUTPUTS = ['loss', 'grad_x', 'grad_ffn1_pre_g', 'grad_ffn1_w_gu', 'grad_ffn1_w_down', 'grad_ffn1_post_g', 'grad_mix_pre_g', 'grad_mix_post_g', 'grad_ev_w_in', 'grad_ev_conv_w', 'grad_ev_conv_b', 'grad_ev_conv_ln_g', 'grad_ev_conv_ln_b', 'grad_ev_sgu_ln_g', 'grad_ev_sgu_ln_b', 'grad_ev_sgu_w', 'grad_ev_sgu_b', 'grad_ev_w_out', 'grad_od_w_in', 'grad_od_w_group', 'grad_od_scale', 'grad_od_w_out', 'grad_xa_pre_g', 'grad_xa_mem_g', 'grad_xa_w_q', 'grad_xa_w_kv', 'grad_xa_w_o', 'grad_xa_post_g', 'grad_ffn2_pre_g', 'grad_ffn2_w_gu', 'grad_ffn2_w_down', 'grad_ffn2_post_g', 'delta_ffn1_pre_g', 'delta_ffn1_w_gu', 'delta_ffn1_w_down', 'delta_ffn1_post_g', 'delta_mix_pre_g', 'delta_mix_post_g', 'delta_ev_w_in', 'delta_ev_conv_w', 'delta_ev_conv_b', 'delta_ev_conv_ln_g', 'delta_ev_conv_ln_b', 'delta_ev_sgu_ln_g', 'delta_ev_sgu_ln_b', 'delta_ev_sgu_w', 'delta_ev_sgu_b', 'delta_ev_w_out', 'delta_od_w_in', 'delta_od_w_group', 'delta_od_scale', 'delta_od_w_out', 'delta_xa_pre_g', 'delta_xa_mem_g', 'delta_xa_w_q', 'delta_xa_w_kv', 'delta_xa_w_o', 'delta_xa_post_g', 'delta_ffn2_pre_g', 'delta_ffn2_w_gu', 'delta_ffn2_w_down', 'delta_ffn2_post_g', 'new_m_ffn1_pre_g', 'new_m_ffn1_w_gu', 'new_m_ffn1_w_down', 'new_m_ffn1_post_g', 'new_m_mix_pre_g', 'new_m_mix_post_g', 'new_m_ev_w_in', 'new_m_ev_conv_w', 'new_m_ev_conv_b', 'new_m_ev_conv_ln_g', 'new_m_ev_conv_ln_b', 'new_m_ev_sgu_ln_g', 'new_m_ev_sgu_ln_b', 'new_m_ev_sgu_w', 'new_m_ev_sgu_b', 'new_m_ev_w_out', 'new_m_od_w_in', 'new_m_od_w_group', 'new_m_od_scale', 'new_m_od_w_out', 'new_m_xa_pre_g', 'new_m_xa_mem_g', 'new_m_xa_w_q', 'new_m_xa_w_kv', 'new_m_xa_w_o', 'new_m_xa_post_g', 'new_m_ffn2_pre_g', 'new_m_ffn2_w_gu', 'new_m_ffn2_w_down', 'new_m_ffn2_post_g', 'new_v_ffn1_pre_g', 'new_v_ffn1_w_gu', 'new_v_ffn1_w_down', 'new_v_ffn1_post_g', 'new_v_mix_pre_g', 'new_v_mix_post_g', 'new_v_ev_w_in', 'new_v_ev_conv_w', 'new_v_ev_conv_b', 'new_v_ev_conv_ln_g', 'new_v_ev_conv_ln_b', 'new_v_ev_sgu_ln_g', 'new_v_ev_sgu_ln_b', 'new_v_ev_sgu_w', 'new_v_ev_sgu_b', 'new_v_ev_w_out', 'new_v_od_w_in', 'new_v_od_w_group', 'new_v_od_scale', 'new_v_od_w_out', 'new_v_xa_pre_g', 'new_v_xa_mem_g', 'new_v_xa_w_q', 'new_v_xa_w_kv', 'new_v_xa_w_o', 'new_v_xa_post_g', 'new_v_ffn2_pre_g', 'new_v_ffn2_w_gu', 'new_v_ffn2_w_down', 'new_v_ffn2_post_g']
TWIN_LEAF_KINDS = {'loss': 'loss', 'grad_x': 'grad_x', 'grad_ffn1_pre_g': 'grad_w', 'grad_ffn1_w_gu': 'grad_w', 'grad_ffn1_w_down': 'grad_w', 'grad_ffn1_post_g': 'grad_w', 'grad_mix_pre_g': 'grad_w', 'grad_mix_post_g': 'grad_w', 'grad_ev_w_in': 'grad_w', 'grad_ev_conv_w': 'grad_w', 'grad_ev_conv_b': 'grad_w', 'grad_ev_conv_ln_g': 'grad_w', 'grad_ev_conv_ln_b': 'grad_w', 'grad_ev_sgu_ln_g': 'grad_w', 'grad_ev_sgu_ln_b': 'grad_w', 'grad_ev_sgu_w': 'grad_w', 'grad_ev_sgu_b': 'grad_w', 'grad_ev_w_out': 'grad_w', 'grad_od_w_in': 'grad_w', 'grad_od_w_group': 'grad_w', 'grad_od_scale': 'grad_w', 'grad_od_w_out': 'grad_w', 'grad_xa_pre_g': 'grad_w', 'grad_xa_mem_g': 'grad_w', 'grad_xa_w_q': 'grad_w', 'grad_xa_w_kv': 'grad_w', 'grad_xa_w_o': 'grad_w', 'grad_xa_post_g': 'grad_w', 'grad_ffn2_pre_g': 'grad_w', 'grad_ffn2_w_gu': 'grad_w', 'grad_ffn2_w_down': 'grad_w', 'grad_ffn2_post_g': 'grad_w', 'delta_ffn1_pre_g': 'delta_w', 'delta_ffn1_w_gu': 'delta_w', 'delta_ffn1_w_down': 'delta_w', 'delta_ffn1_post_g': 'delta_w', 'delta_mix_pre_g': 'delta_w', 'delta_mix_post_g': 'delta_w', 'delta_ev_w_in': 'delta_w', 'delta_ev_conv_w': 'delta_w', 'delta_ev_conv_b': 'delta_w', 'delta_ev_conv_ln_g': 'delta_w', 'delta_ev_conv_ln_b': 'delta_w', 'delta_ev_sgu_ln_g': 'delta_w', 'delta_ev_sgu_ln_b': 'delta_w', 'delta_ev_sgu_w': 'delta_w', 'delta_ev_sgu_b': 'delta_w', 'delta_ev_w_out': 'delta_w', 'delta_od_w_in': 'delta_w', 'delta_od_w_group': 'delta_w', 'delta_od_scale': 'delta_w', 'delta_od_w_out': 'delta_w', 'delta_xa_pre_g': 'delta_w', 'delta_xa_mem_g': 'delta_w', 'delta_xa_w_q': 'delta_w', 'delta_xa_w_kv': 'delta_w', 'delta_xa_w_o': 'delta_w', 'delta_xa_post_g': 'delta_w', 'delta_ffn2_pre_g': 'delta_w', 'delta_ffn2_w_gu': 'delta_w', 'delta_ffn2_w_down': 'delta_w', 'delta_ffn2_post_g': 'delta_w', 'new_m_ffn1_pre_g': 'new_m', 'new_m_ffn1_w_gu': 'new_m', 'new_m_ffn1_w_down': 'new_m', 'new_m_ffn1_post_g': 'new_m', 'new_m_mix_pre_g': 'new_m', 'new_m_mix_post_g': 'new_m', 'new_m_ev_w_in': 'new_m', 'new_m_ev_conv_w': 'new_m', 'new_m_ev_conv_b': 'new_m', 'new_m_ev_conv_ln_g': 'new_m', 'new_m_ev_conv_ln_b': 'new_m', 'new_m_ev_sgu_ln_g': 'new_m', 'new_m_ev_sgu_ln_b': 'new_m', 'new_m_ev_sgu_w': 'new_m', 'new_m_ev_sgu_b': 'new_m', 'new_m_ev_w_out': 'new_m', 'new_m_od_w_in': 'new_m', 'new_m_od_w_group': 'new_m', 'new_m_od_scale': 'new_m', 'new_m_od_w_out': 'new_m', 'new_m_xa_pre_g': 'new_m', 'new_m_xa_mem_g': 'new_m', 'new_m_xa_w_q': 'new_m', 'new_m_xa_w_kv': 'new_m', 'new_m_xa_w_o': 'new_m', 'new_m_xa_post_g': 'new_m', 'new_m_ffn2_pre_g': 'new_m', 'new_m_ffn2_w_gu': 'new_m', 'new_m_ffn2_w_down': 'new_m', 'new_m_ffn2_post_g': 'new_m', 'new_v_ffn1_pre_g': 'new_v', 'new_v_ffn1_w_gu': 'new_v', 'new_v_ffn1_w_down': 'new_v', 'new_v_ffn1_post_g': 'new_v', 'new_v_mix_pre_g': 'new_v', 'new_v_mix_post_g': 'new_v', 'new_v_ev_w_in': 'new_v', 'new_v_ev_conv_w': 'new_v', 'new_v_ev_conv_b': 'new_v', 'new_v_ev_conv_ln_g': 'new_v', 'new_v_ev_conv_ln_b': 'new_v', 'new_v_ev_sgu_ln_g': 'new_v', 'new_v_ev_sgu_ln_b': 'new_v', 'new_v_ev_sgu_w': 'new_v', 'new_v_ev_sgu_b': 'new_v', 'new_v_ev_w_out': 'new_v', 'new_v_od_w_in': 'new_v', 'new_v_od_w_group': 'new_v', 'new_v_od_scale': 'new_v', 'new_v_od_w_out': 'new_v', 'new_v_xa_pre_g': 'new_v', 'new_v_xa_mem_g': 'new_v', 'new_v_xa_w_q': 'new_v', 'new_v_xa_w_kv': 'new_v', 'new_v_xa_w_o': 'new_v', 'new_v_xa_post_g': 'new_v', 'new_v_ffn2_pre_g': 'new_v', 'new_v_ffn2_w_gu': 'new_v', 'new_v_ffn2_w_down': 'new_v', 'new_v_ffn2_post_g': 'new_v'}


def _forward(args):
    return _fwd_reference(*[args[k] for k in FWD_PARAMS])


def _output_shape():
    out = _jax.eval_shape(lambda: _forward(_fwd_setup_inputs(0)))
    return out.shape, out.dtype

N_MICROBATCH = 1
ADAM_LR = 0.001
ADAM_B1 = 0.9
ADAM_B2 = 0.999
ADAM_EPS = 1e-08
ADAM_WD = 0.01
ADAM_STEP = 10
PER_EXAMPLE_BATCH_AXIS = {'x': 0, 'mem': 0, 'loss_target': 0}
SHARED_INPUTS = []
_WEIGHT_DTYPES = {'ffn1_pre_g': _jnp.float32, 'ffn1_w_gu': _jnp.float32, 'ffn1_w_down': _jnp.float32, 'ffn1_post_g': _jnp.float32, 'mix_pre_g': _jnp.float32, 'mix_post_g': _jnp.float32, 'ev_w_in': _jnp.float32, 'ev_conv_w': _jnp.float32, 'ev_conv_b': _jnp.float32, 'ev_conv_ln_g': _jnp.float32, 'ev_conv_ln_b': _jnp.float32, 'ev_sgu_ln_g': _jnp.float32, 'ev_sgu_ln_b': _jnp.float32, 'ev_sgu_w': _jnp.float32, 'ev_sgu_b': _jnp.float32, 'ev_w_out': _jnp.float32, 'od_w_in': _jnp.float32, 'od_w_group': _jnp.float32, 'od_scale': _jnp.float32, 'od_w_out': _jnp.float32, 'xa_pre_g': _jnp.float32, 'xa_mem_g': _jnp.float32, 'xa_w_q': _jnp.float32, 'xa_w_kv': _jnp.float32, 'xa_w_o': _jnp.float32, 'xa_post_g': _jnp.float32, 'ffn2_pre_g': _jnp.float32, 'ffn2_w_gu': _jnp.float32, 'ffn2_w_down': _jnp.float32, 'ffn2_post_g': _jnp.float32}
MOMENT_SCALE = {'ffn1_pre_g': 2.161954e+00, 'ffn1_w_gu': 8.799869e-01, 'ffn1_w_down': 1.663964e+00, 'ffn1_post_g': 7.565943e+00, 'mix_pre_g': 3.052912e+00, 'mix_post_g': 3.296795e+01, 'ev_w_in': 2.501911e+00, 'ev_conv_w': 3.853898e+00, 'ev_conv_b': 3.270506e+01, 'ev_conv_ln_g': 1.343503e+01, 'ev_conv_ln_b': 1.944613e+01, 'ev_sgu_ln_g': 8.056587e-01, 'ev_sgu_ln_b': 8.723774e-01, 'ev_sgu_w': 7.924780e-01, 'ev_sgu_b': 1.134022e+00, 'ev_w_out': 1.007005e+01, 'od_w_in': 2.171113e+00, 'od_w_group': 2.229591e+00, 'od_scale': 2.423180e+00, 'od_w_out': 2.303519e+00, 'xa_pre_g': 4.484649e+00, 'xa_mem_g': 1.389466e+01, 'xa_w_q': 4.439660e+00, 'xa_w_kv': 1.000903e+01, 'xa_w_o': 1.360192e+01, 'xa_post_g': 3.553296e+01, 'ffn2_pre_g': 2.645501e+00, 'ffn2_w_gu': 1.120886e+00, 'ffn2_w_down': 2.196315e+00, 'ffn2_post_g': 8.310616e+00}


def _to_microbatches(a, axis):
    t = _jnp.moveaxis(a, axis, 0)
    t = t.reshape((N_MICROBATCH, t.shape[0] // N_MICROBATCH) + t.shape[1:])
    return _jnp.moveaxis(t, 1, axis + 1)


def setup_inputs(seed: int = 0) -> dict:
    inp = _fwd_setup_inputs(seed)
    key = _jax.random.fold_in(_jax.random.key(seed), 7919)
    shape, _ = _output_shape()
    out = dict(inp)
    out["loss_target"] = _jax.random.normal(_jax.random.fold_in(key, 0), shape, _jnp.float32)
    for i, name in enumerate(TWIN_WEIGHTS):
        w = inp[name].astype(_jnp.float32)
        if MOMENT_SCALE is None:
            s = _jnp.sqrt(_jnp.mean(_jnp.square(w)) + 1e-30)
        else:
            s = MOMENT_SCALE[name]
        km, kv = _jax.random.split(_jax.random.fold_in(key, i + 1))
        out[name] = w
        out["m_" + name] = s * _jax.random.normal(km, w.shape, _jnp.float32)
        out["v_" + name] = (s * s) * _jax.random.uniform(kv, w.shape, _jnp.float32, 0.5, 1.5)
    if N_MICROBATCH > 1:
        for name, axis in PER_EXAMPLE_BATCH_AXIS.items():
            out[name] = _to_microbatches(out[name], axis)
    return {'x': out['x'], 'mem': out['mem'], 'ffn1_pre_g': out['ffn1_pre_g'], 'ffn1_w_gu': out['ffn1_w_gu'], 'ffn1_w_down': out['ffn1_w_down'], 'ffn1_post_g': out['ffn1_post_g'], 'mix_pre_g': out['mix_pre_g'], 'mix_post_g': out['mix_post_g'], 'ev_w_in': out['ev_w_in'], 'ev_conv_w': out['ev_conv_w'], 'ev_conv_b': out['ev_conv_b'], 'ev_conv_ln_g': out['ev_conv_ln_g'], 'ev_conv_ln_b': out['ev_conv_ln_b'], 'ev_sgu_ln_g': out['ev_sgu_ln_g'], 'ev_sgu_ln_b': out['ev_sgu_ln_b'], 'ev_sgu_w': out['ev_sgu_w'], 'ev_sgu_b': out['ev_sgu_b'], 'ev_w_out': out['ev_w_out'], 'od_w_in': out['od_w_in'], 'od_w_group': out['od_w_group'], 'od_scale': out['od_scale'], 'od_w_out': out['od_w_out'], 'xa_pre_g': out['xa_pre_g'], 'xa_mem_g': out['xa_mem_g'], 'xa_w_q': out['xa_w_q'], 'xa_w_kv': out['xa_w_kv'], 'xa_w_o': out['xa_w_o'], 'xa_post_g': out['xa_post_g'], 'ffn2_pre_g': out['ffn2_pre_g'], 'ffn2_w_gu': out['ffn2_w_gu'], 'ffn2_w_down': out['ffn2_w_down'], 'ffn2_post_g': out['ffn2_post_g'], 'loss_target': out['loss_target'], 'm_ffn1_pre_g': out['m_ffn1_pre_g'], 'm_ffn1_w_gu': out['m_ffn1_w_gu'], 'm_ffn1_w_down': out['m_ffn1_w_down'], 'm_ffn1_post_g': out['m_ffn1_post_g'], 'm_mix_pre_g': out['m_mix_pre_g'], 'm_mix_post_g': out['m_mix_post_g'], 'm_ev_w_in': out['m_ev_w_in'], 'm_ev_conv_w': out['m_ev_conv_w'], 'm_ev_conv_b': out['m_ev_conv_b'], 'm_ev_conv_ln_g': out['m_ev_conv_ln_g'], 'm_ev_conv_ln_b': out['m_ev_conv_ln_b'], 'm_ev_sgu_ln_g': out['m_ev_sgu_ln_g'], 'm_ev_sgu_ln_b': out['m_ev_sgu_ln_b'], 'm_ev_sgu_w': out['m_ev_sgu_w'], 'm_ev_sgu_b': out['m_ev_sgu_b'], 'm_ev_w_out': out['m_ev_w_out'], 'm_od_w_in': out['m_od_w_in'], 'm_od_w_group': out['m_od_w_group'], 'm_od_scale': out['m_od_scale'], 'm_od_w_out': out['m_od_w_out'], 'm_xa_pre_g': out['m_xa_pre_g'], 'm_xa_mem_g': out['m_xa_mem_g'], 'm_xa_w_q': out['m_xa_w_q'], 'm_xa_w_kv': out['m_xa_w_kv'], 'm_xa_w_o': out['m_xa_w_o'], 'm_xa_post_g': out['m_xa_post_g'], 'm_ffn2_pre_g': out['m_ffn2_pre_g'], 'm_ffn2_w_gu': out['m_ffn2_w_gu'], 'm_ffn2_w_down': out['m_ffn2_w_down'], 'm_ffn2_post_g': out['m_ffn2_post_g'], 'v_ffn1_pre_g': out['v_ffn1_pre_g'], 'v_ffn1_w_gu': out['v_ffn1_w_gu'], 'v_ffn1_w_down': out['v_ffn1_w_down'], 'v_ffn1_post_g': out['v_ffn1_post_g'], 'v_mix_pre_g': out['v_mix_pre_g'], 'v_mix_post_g': out['v_mix_post_g'], 'v_ev_w_in': out['v_ev_w_in'], 'v_ev_conv_w': out['v_ev_conv_w'], 'v_ev_conv_b': out['v_ev_conv_b'], 'v_ev_conv_ln_g': out['v_ev_conv_ln_g'], 'v_ev_conv_ln_b': out['v_ev_conv_ln_b'], 'v_ev_sgu_ln_g': out['v_ev_sgu_ln_g'], 'v_ev_sgu_ln_b': out['v_ev_sgu_ln_b'], 'v_ev_sgu_w': out['v_ev_sgu_w'], 'v_ev_sgu_b': out['v_ev_sgu_b'], 'v_ev_w_out': out['v_ev_w_out'], 'v_od_w_in': out['v_od_w_in'], 'v_od_w_group': out['v_od_w_group'], 'v_od_scale': out['v_od_scale'], 'v_od_w_out': out['v_od_w_out'], 'v_xa_pre_g': out['v_xa_pre_g'], 'v_xa_mem_g': out['v_xa_mem_g'], 'v_xa_w_q': out['v_xa_w_q'], 'v_xa_w_kv': out['v_xa_w_kv'], 'v_xa_w_o': out['v_xa_w_o'], 'v_xa_post_g': out['v_xa_post_g'], 'v_ffn2_pre_g': out['v_ffn2_pre_g'], 'v_ffn2_w_gu': out['v_ffn2_w_gu'], 'v_ffn2_w_down': out['v_ffn2_w_down'], 'v_ffn2_post_g': out['v_ffn2_post_g']}


def _loss(weights, diff, rest, loss_target):
    with _jax.named_scope("forward"):
        args = {**rest, TWIN_DIFF_INPUT: diff, **{k: w.astype(_WEIGHT_DTYPES[k]) for k, w in weights.items()}}
        y = _forward(args)
    with _jax.named_scope("loss_head"):
        err = _jnp.square(y.astype(_jnp.float32) - loss_target)
        return 0.5 * _jnp.sum(_jnp.mean(err, axis=-1)) if err.ndim else 0.5 * err


def _adamw(w, g, m, v):
    m = ADAM_B1 * m + (1.0 - ADAM_B1) * g
    v = ADAM_B2 * v + (1.0 - ADAM_B2) * _jnp.square(g)
    m_hat = m / (1.0 - ADAM_B1 ** ADAM_STEP)
    v_hat = v / (1.0 - ADAM_B2 ** ADAM_STEP)
    delta = -ADAM_LR * (m_hat / (_jnp.sqrt(v_hat) + ADAM_EPS) + ADAM_WD * w)
    return delta, m, v


def reference(x, mem, ffn1_pre_g, ffn1_w_gu, ffn1_w_down, ffn1_post_g, mix_pre_g, mix_post_g, ev_w_in, ev_conv_w, ev_conv_b, ev_conv_ln_g, ev_conv_ln_b, ev_sgu_ln_g, ev_sgu_ln_b, ev_sgu_w, ev_sgu_b, ev_w_out, od_w_in, od_w_group, od_scale, od_w_out, xa_pre_g, xa_mem_g, xa_w_q, xa_w_kv, xa_w_o, xa_post_g, ffn2_pre_g, ffn2_w_gu, ffn2_w_down, ffn2_post_g, loss_target, m_ffn1_pre_g, m_ffn1_w_gu, m_ffn1_w_down, m_ffn1_post_g, m_mix_pre_g, m_mix_post_g, m_ev_w_in, m_ev_conv_w, m_ev_conv_b, m_ev_conv_ln_g, m_ev_conv_ln_b, m_ev_sgu_ln_g, m_ev_sgu_ln_b, m_ev_sgu_w, m_ev_sgu_b, m_ev_w_out, m_od_w_in, m_od_w_group, m_od_scale, m_od_w_out, m_xa_pre_g, m_xa_mem_g, m_xa_w_q, m_xa_w_kv, m_xa_w_o, m_xa_post_g, m_ffn2_pre_g, m_ffn2_w_gu, m_ffn2_w_down, m_ffn2_post_g, v_ffn1_pre_g, v_ffn1_w_gu, v_ffn1_w_down, v_ffn1_post_g, v_mix_pre_g, v_mix_post_g, v_ev_w_in, v_ev_conv_w, v_ev_conv_b, v_ev_conv_ln_g, v_ev_conv_ln_b, v_ev_sgu_ln_g, v_ev_sgu_ln_b, v_ev_sgu_w, v_ev_sgu_b, v_ev_w_out, v_od_w_in, v_od_w_group, v_od_scale, v_od_w_out, v_xa_pre_g, v_xa_mem_g, v_xa_w_q, v_xa_w_kv, v_xa_w_o, v_xa_post_g, v_ffn2_pre_g, v_ffn2_w_gu, v_ffn2_w_down, v_ffn2_post_g):
    given = dict(x=x, mem=mem, ffn1_pre_g=ffn1_pre_g, ffn1_w_gu=ffn1_w_gu, ffn1_w_down=ffn1_w_down, ffn1_post_g=ffn1_post_g, mix_pre_g=mix_pre_g, mix_post_g=mix_post_g, ev_w_in=ev_w_in, ev_conv_w=ev_conv_w, ev_conv_b=ev_conv_b, ev_conv_ln_g=ev_conv_ln_g, ev_conv_ln_b=ev_conv_ln_b, ev_sgu_ln_g=ev_sgu_ln_g, ev_sgu_ln_b=ev_sgu_ln_b, ev_sgu_w=ev_sgu_w, ev_sgu_b=ev_sgu_b, ev_w_out=ev_w_out, od_w_in=od_w_in, od_w_group=od_w_group, od_scale=od_scale, od_w_out=od_w_out, xa_pre_g=xa_pre_g, xa_mem_g=xa_mem_g, xa_w_q=xa_w_q, xa_w_kv=xa_w_kv, xa_w_o=xa_w_o, xa_post_g=xa_post_g, ffn2_pre_g=ffn2_pre_g, ffn2_w_gu=ffn2_w_gu, ffn2_w_down=ffn2_w_down, ffn2_post_g=ffn2_post_g, loss_target=loss_target, m_ffn1_pre_g=m_ffn1_pre_g, m_ffn1_w_gu=m_ffn1_w_gu, m_ffn1_w_down=m_ffn1_w_down, m_ffn1_post_g=m_ffn1_post_g, m_mix_pre_g=m_mix_pre_g, m_mix_post_g=m_mix_post_g, m_ev_w_in=m_ev_w_in, m_ev_conv_w=m_ev_conv_w, m_ev_conv_b=m_ev_conv_b, m_ev_conv_ln_g=m_ev_conv_ln_g, m_ev_conv_ln_b=m_ev_conv_ln_b, m_ev_sgu_ln_g=m_ev_sgu_ln_g, m_ev_sgu_ln_b=m_ev_sgu_ln_b, m_ev_sgu_w=m_ev_sgu_w, m_ev_sgu_b=m_ev_sgu_b, m_ev_w_out=m_ev_w_out, m_od_w_in=m_od_w_in, m_od_w_group=m_od_w_group, m_od_scale=m_od_scale, m_od_w_out=m_od_w_out, m_xa_pre_g=m_xa_pre_g, m_xa_mem_g=m_xa_mem_g, m_xa_w_q=m_xa_w_q, m_xa_w_kv=m_xa_w_kv, m_xa_w_o=m_xa_w_o, m_xa_post_g=m_xa_post_g, m_ffn2_pre_g=m_ffn2_pre_g, m_ffn2_w_gu=m_ffn2_w_gu, m_ffn2_w_down=m_ffn2_w_down, m_ffn2_post_g=m_ffn2_post_g, v_ffn1_pre_g=v_ffn1_pre_g, v_ffn1_w_gu=v_ffn1_w_gu, v_ffn1_w_down=v_ffn1_w_down, v_ffn1_post_g=v_ffn1_post_g, v_mix_pre_g=v_mix_pre_g, v_mix_post_g=v_mix_post_g, v_ev_w_in=v_ev_w_in, v_ev_conv_w=v_ev_conv_w, v_ev_conv_b=v_ev_conv_b, v_ev_conv_ln_g=v_ev_conv_ln_g, v_ev_conv_ln_b=v_ev_conv_ln_b, v_ev_sgu_ln_g=v_ev_sgu_ln_g, v_ev_sgu_ln_b=v_ev_sgu_ln_b, v_ev_sgu_w=v_ev_sgu_w, v_ev_sgu_b=v_ev_sgu_b, v_ev_w_out=v_ev_w_out, v_od_w_in=v_od_w_in, v_od_w_group=v_od_w_group, v_od_scale=v_od_scale, v_od_w_out=v_od_w_out, v_xa_pre_g=v_xa_pre_g, v_xa_mem_g=v_xa_mem_g, v_xa_w_q=v_xa_w_q, v_xa_w_kv=v_xa_w_kv, v_xa_w_o=v_xa_w_o, v_xa_post_g=v_xa_post_g, v_ffn2_pre_g=v_ffn2_pre_g, v_ffn2_w_gu=v_ffn2_w_gu, v_ffn2_w_down=v_ffn2_w_down, v_ffn2_post_g=v_ffn2_post_g)
    weights = {n: given[n] for n in TWIN_WEIGHTS}
    shared = {n: given[n] for n in SHARED_INPUTS}
    per_example = {n: given[n] for n in ['x', 'mem']}
    grad_fn = _jax.value_and_grad(_loss, argnums=(0, 1))

    def one_microbatch(ex, loss_target):
        ex = dict(ex)
        diff = ex.pop(TWIN_DIFF_INPUT)
        return grad_fn(weights, diff, {**shared, **ex}, loss_target)

    if N_MICROBATCH == 1:
        loss, (grad_w, grad_x) = one_microbatch(per_example, given["loss_target"])
    else:
        def body(carry, xs):
            loss_sum, grad_sum = carry
            l_k, (gw_k, gx_k) = one_microbatch(xs[0], xs[1])
            with _jax.named_scope("update"):
                return (loss_sum + l_k, _jax.tree.map(_jnp.add, grad_sum, gw_k)), gx_k

        init = (_jnp.zeros((), _jnp.float32), _jax.tree.map(_jnp.zeros_like, weights))
        (loss, grad_w), grad_x = _jax.lax.scan(body, init, (per_example, given["loss_target"]))
    with _jax.named_scope("update"):
        delta_w, new_m, new_v = {}, {}, {}
        for n in TWIN_WEIGHTS:
            delta_w[n], new_m[n], new_v[n] = _adamw(weights[n], grad_w[n], given["m_" + n], given["v_" + n])
    return (loss, grad_x, *[grad_w[n] for n in TWIN_WEIGHTS], *[delta_w[n] for n in TWIN_WEIGHTS],
            *[new_m[n] for n in TWIN_WEIGHTS], *[new_v[n] for n in TWIN_WEIGHTS])
```

```python
import functools
import math

import jax
import jax.numpy as jnp
from jax import lax
from jax.experimental import pallas as pl
from jax.experimental.pallas import tpu as pltpu

F32 = jnp.float32
BF16 = jnp.bfloat16
MESH = pl.DeviceIdType.MESH
ANY = pl.BlockSpec(memory_space=pl.ANY)

D = 1024
DFF = 2816
NMEM = 256
DC = 512
CONV_W = 31
CHUNK = 128
NHEAD_SGU = 4
POOL_WINDOWS = (2, 4, 8, 16)
POOL_GD = 256
XA_HEADS = 4
XA_HD = 256
EPS = 1e-6
NCHIP = 4
HALO = 32

ADAM_LR, ADAM_B1, ADAM_B2, ADAM_EPS, ADAM_WD, ADAM_STEP = 0.001, 0.9, 0.999, 1e-08, 0.01, 10

V7X_VMEM_BYTES = 64 * 1024 * 1024
VMEM_LIMIT = V7X_VMEM_BYTES - 8 * 1024 * 1024


def _params(sem):
    return pltpu.CompilerParams(dimension_semantics=sem, vmem_limit_bytes=VMEM_LIMIT)


def _matmul(a, b, kind, out_dtype, name, tm, tn, a_l=None, b_l=None, out_l=None, out_stack=None, out_buf=None,
            n_outer=True):
    a2, b2 = a.shape[-2:], b.shape[-2:]
    if kind == "nn":
        (m, k), (k2, n) = a2, b2
        dims = (((1,), (0,)), ((), ()))
    elif kind == "nt":
        (m, k), (n, k2) = a2, b2
        dims = (((1,), (1,)), ((), ()))
    else:
        (k, m), (k2, n) = a2, b2
        dims = (((0,), (0,)), ((), ()))
    assert k == k2 and m % tm == 0 and n % tn == 0, (name, a.shape, b.shape, tm, tn)
    if n_outer:
        grid = (n // tn, m // tm)
        ij = lambda p, q: (q, p)
    else:
        grid = (m // tm, n // tn)
        ij = lambda p, q: (p, q)

    def spec(arr, layer, blk, idx):
        if arr.ndim == 3:
            return pl.BlockSpec((None,) + blk, lambda p, q: (layer,) + idx(*ij(p, q)))
        return pl.BlockSpec(blk, lambda p, q: idx(*ij(p, q)))

    a_spec = spec(a, a_l, (k, tm) if kind == "tn" else (tm, k), (lambda i, j: (0, i)) if kind == "tn" else (lambda i, j: (i, 0)))
    b_spec = spec(b, b_l, (tn, k) if kind == "nt" else (k, tn), (lambda i, j: (j, 0)) if kind == "nt" else (lambda i, j: (0, j)))
    if out_stack is None:
        out_shape = jax.ShapeDtypeStruct((m, n), out_dtype)
        o_spec = pl.BlockSpec((tm, tn), lambda p, q: ij(p, q))
    else:
        out_shape = jax.ShapeDtypeStruct((out_stack, m, n), out_dtype)
        o_spec = pl.BlockSpec((None, tm, tn), lambda p, q: (out_l,) + ij(p, q))

    def body(a_ref, b_ref, *rest):
        o_ref = rest[-1]
        o_ref[...] = lax.dot_general(a_ref[...], b_ref[...], dims, preferred_element_type=F32).astype(o_ref.dtype)

    in_specs, args, aliases = [a_spec, b_spec], [a, b], {}
    if out_buf is not None:
        in_specs.append(ANY)
        args.append(out_buf)
        aliases = {2: 0}
    return pl.pallas_call(body, name=name, grid=grid, in_specs=in_specs, out_specs=o_spec, out_shape=out_shape,
                          input_output_aliases=aliases, compiler_params=_params(("parallel", "parallel")))(*args)


def _rowwise(fn, name, rows, tile, row_ins, full_ins, row_outs, acc_outs=(), scratch=()):
    assert rows % tile == 0, (name, rows, tile)
    in_specs, args = [], []
    for r in row_ins:
        if isinstance(r, tuple):
            arr, br, bc, imap = r
            in_specs.append(pl.BlockSpec((br, bc), imap))
        else:
            arr = r
            in_specs.append(pl.BlockSpec((tile, arr.shape[1]), lambda i: (i, 0)))
        args.append(arr)
    for f in full_ins:
        in_specs.append(pl.BlockSpec(f.shape, functools.partial(lambda nd, i: (0,) * nd, f.ndim)))
        args.append(f)
    out_specs, out_shape = [], []
    for w, dt in row_outs:
        out_specs.append(pl.BlockSpec((tile, w), lambda i: (i, 0)))
        out_shape.append(jax.ShapeDtypeStruct((rows, w), dt))
    for shp, dt in acc_outs:
        out_specs.append(pl.BlockSpec(shp, functools.partial(lambda nd, i: (0,) * nd, len(shp))))
        out_shape.append(jax.ShapeDtypeStruct(shp, dt))
    n_in, n_out = len(args), len(out_shape)

    def body(*refs):
        fn(pl.program_id(0), refs[:n_in], refs[n_in:n_in + n_out], refs[n_in + n_out:])

    sem = ("arbitrary",) if acc_outs else ("parallel",)
    res = pl.pallas_call(body, name=name, grid=(rows // tile,), in_specs=in_specs, out_specs=out_specs, out_shape=out_shape,
                         scratch_shapes=list(scratch), compiler_params=_params(sem))(*args)
    return res


def _accum(ref, i, val):
    @pl.when(i == 0)
    def _():
        ref[...] = val

    @pl.when(i > 0)
    def _():
        ref[...] += val


def _prev_halo(arr, tile, cols=None):
    r = tile // HALO
    return (arr, HALO, cols or arr.shape[1], lambda i: (jnp.maximum(i * r - 1, 0), 0))


def _next_halo(arr, tile, cols=None):
    r = tile // HALO
    last = arr.shape[0] // HALO - 1
    return (arr, HALO, cols or arr.shape[1], lambda i: (jnp.minimum((i + 1) * r, last), 0))


def _sigmoid(x):
    return 1.0 / (1.0 + jnp.exp(-x))


def _rms_hat(x):
    r = lax.rsqrt(jnp.mean(x * x, axis=-1, keepdims=True) + EPS)
    return x * r, r


def _rms_bwd(x, g, dy):
    xhat, r = _rms_hat(x)
    dxhat = dy * g
    dx = r * (dxhat - xhat * jnp.mean(dxhat * xhat, axis=-1, keepdims=True))
    return dx, jnp.sum(dy * xhat, axis=0, keepdims=True)


def _ln_hat(x):
    mu = jnp.mean(x, axis=-1, keepdims=True)
    xc = x - mu
    r = lax.rsqrt(jnp.mean(xc * xc, axis=-1, keepdims=True) + EPS)
    return xc * r, r


def _ln_bwd(xhat, r, g, dy):
    dxhat = dy * g
    dx = r * (dxhat - jnp.mean(dxhat, axis=-1, keepdims=True) - xhat * jnp.mean(dxhat * xhat, axis=-1, keepdims=True))
    return dx, jnp.sum(dy * xhat, axis=0, keepdims=True), jnp.sum(dy, axis=0, keepdims=True)


def _silu_grad(x):
    s = _sigmoid(x)
    return s * (1.0 + x * (1.0 - s))


_SQRT_HALF = math.sqrt(0.5)
_INV_SQRT_2PI = 1.0 / math.sqrt(2.0 * math.pi)


def _gelu(x):
    return 0.5 * x * (1.0 + lax.erf(x * _SQRT_HALF))


def _gelu_grad(x):
    return 0.5 * (1.0 + lax.erf(x * _SQRT_HALF)) + x * jnp.exp(-0.5 * x * x) * _INV_SQRT_2PI


def _dot(a, b, kind="nn"):
    dims = {"nn": (((1,), (0,)), ((), ())), "nt": (((1,), (1,)), ((), ())), "tn": (((0,), (0,)), ((), ()))}[kind]
    return lax.dot_general(a, b, dims, preferred_element_type=F32)


def _prenorm(h, g, tile):
    def fn(i, ins, outs, _):
        h_ref, g_ref = ins
        outs[0][...] = (_rms_hat(h_ref[...])[0] * g_ref[...]).astype(BF16)

    return _rowwise(fn, "prenorm", h.shape[0], tile, [h], [g], [(D, BF16)])[0]


def _post(h, f, g_post, scale, g_next, tile):
    def fn(i, ins, outs, _):
        hn = ins[0][...] + scale * (_rms_hat(ins[1][...])[0] * ins[2][...])
        outs[0][...] = hn
        if g_next is not None:
            outs[1][...] = (_rms_hat(hn)[0] * ins[3][...]).astype(BF16)

    fulls = [g_post] + ([g_next] if g_next is not None else [])
    outs = [(D, F32)] + ([(D, BF16)] if g_next is not None else [])
    res = _rowwise(fn, "post", h.shape[0], tile, [h, f], fulls, outs)
    return res[0], (res[1] if g_next is not None else None)


def _loss_top(h, target, f, g_post, scale, tile):
    def fn(i, ins, outs, _):
        err = ins[0][...] - ins[1][...]
        per_row = jnp.mean(err * err, axis=-1, keepdims=True)
        _accum(outs[2], i, jnp.broadcast_to(0.5 * jnp.sum(per_row, axis=0, keepdims=True), (1, 128)))
        dh = err * (1.0 / D)
        outs[0][...] = dh
        df, dg = _rms_bwd(ins[2][...], ins[3][...], scale * dh)
        outs[1][...] = df.astype(BF16)
        _accum(outs[3], i, dg)

    return _rowwise(fn, "loss_top", h.shape[0], tile, [h, target, f], [g_post], [(D, F32), (D, BF16)],
                    [((1, 128), F32), ((1, D), F32)])


def _boundary_bwd(h, g_pre, dn, dh_in, f_prev, g_post_prev, scale_prev, tile):
    has_prev = f_prev is not None

    def fn(i, ins, outs, _):
        dx, dg = _rms_bwd(ins[0][...], ins[3 + has_prev][...], ins[1][...])
        dh = ins[2][...] + dx
        outs[0][...] = dh
        _accum(outs[1 + has_prev], i, dg)
        if has_prev:
            df, dgp = _rms_bwd(ins[3][...], ins[5][...], scale_prev * dh)
            outs[1][...] = df.astype(BF16)
            _accum(outs[3], i, dgp)

    rows = [h, dn, dh_in] + ([f_prev] if has_prev else [])
    fulls = [g_pre] + ([g_post_prev] if has_prev else [])
    outs = [(D, F32)] + ([(D, BF16)] if has_prev else [])
    accs = [((1, D), F32)] + ([((1, D), F32)] if has_prev else [])
    return _rowwise(fn, "boundary_bwd", h.shape[0], tile, rows, fulls, outs, accs)


def _swiglu(gu, tile):
    def fn(i, ins, outs, _):
        g = ins[0][:, :DFF]
        outs[0][...] = (g * _sigmoid(g) * ins[0][:, DFF:]).astype(BF16)

    return _rowwise(fn, "swiglu", gu.shape[0], tile, [gu], [], [(DFF, BF16)])[0]


def _swiglu_bwd(da, gu, tile):
    def fn(i, ins, outs, _):
        da_v = ins[0][...]
        g = ins[1][:, :DFF]
        u = ins[1][:, DFF:]
        outs[0][:, :DFF] = (da_v * u * _silu_grad(g)).astype(BF16)
        outs[0][:, DFF:] = (da_v * g * _sigmoid(g)).astype(BF16)

    return _rowwise(fn, "swiglu_bwd", gu.shape[0], tile, [da, gu], [], [(2 * DFF, BF16)])[0]


def _softmax_rows(s):
    e = jnp.exp(s - jnp.max(s, axis=-1, keepdims=True))
    return e / jnp.sum(e, axis=-1, keepdims=True)


def _attn(q, kv, tile):
    def fn(i, ins, outs, _):
        q_ref, kv_ref = ins
        for hd in range(XA_HEADS):
            c0 = hd * XA_HD
            p = _softmax_rows(_dot(q_ref[:, c0:c0 + XA_HD], kv_ref[:, c0:c0 + XA_HD], "nt") * (XA_HD ** -0.5))
            outs[0][:, c0:c0 + XA_HD] = _dot(p.astype(BF16), kv_ref[:, D + c0:D + c0 + XA_HD]).astype(BF16)

    return _rowwise(fn, "attn", q.shape[0], tile, [q], [kv], [(D, BF16)])[0]


def _attn_bwd(q, kv, do, tile):
    def fn(i, ins, outs, _):
        q_ref, do_ref, kv_ref = ins
        for hd in range(XA_HEADS):
            c0 = hd * XA_HD
            qh, kh, vh = q_ref[:, c0:c0 + XA_HD], kv_ref[:, c0:c0 + XA_HD], kv_ref[:, D + c0:D + c0 + XA_HD]
            doh = do_ref[:, c0:c0 + XA_HD]
            p = _softmax_rows(_dot(qh, kh, "nt") * (XA_HD ** -0.5))
            dp = _dot(doh, vh, "nt")
            ds = (p * (dp - jnp.sum(dp * p, axis=-1, keepdims=True)) * (XA_HD ** -0.5)).astype(BF16)
            outs[0][:, c0:c0 + XA_HD] = _dot(ds, kh).astype(BF16)
            dk = _dot(ds, qh, "tn")
            dv = _dot(p.astype(BF16), doh, "tn")

            @pl.when(i == 0)
            def _():
                outs[1][:, c0:c0 + XA_HD] = dk
                outs[1][:, D + c0:D + c0 + XA_HD] = dv

            @pl.when(i > 0)
            def _():
                outs[1][:, c0:c0 + XA_HD] += dk
                outs[1][:, D + c0:D + c0 + XA_HD] += dv

    return _rowwise(fn, "attn_bwd", q.shape[0], tile, [q, do], [kv], [(D, BF16)], [((NMEM, 2 * D), F32)])


def _mem_norm(mem, g):
    def fn(i, ins, outs, _):
        outs[0][...] = (_rms_hat(ins[0][...])[0] * ins[1][...]).astype(BF16)

    return _rowwise(fn, "mem_norm", NMEM, NMEM, [mem], [g], [(D, BF16)])[0]


def _mem_gain_bwd(mem, dmn):
    def fn(i, ins, outs, _):
        outs[0][...] = jnp.sum(ins[1][...] * _rms_hat(ins[0][...])[0], axis=0, keepdims=True)

    return _rowwise(fn, "mem_gain_bwd", NMEM, NMEM, [mem, dmn], [], [], [((1, D), F32)])[0]


def _conv_taps(w_ref, zext_ref, tile, shift0):
    acc = None
    for k in range(CONV_W):
        term = w_ref[k:k + 1, :] * zext_ref[pl.ds(shift0(k), tile), :]
        acc = term if acc is None else acc + term
    return acc


def _tril_mask(transpose=False):
    r, c = (lax.broadcasted_iota(jnp.int32, (CHUNK, CHUNK), a) for a in (0, 1))
    return (r <= c) if transpose else (r >= c)


def _even_core(p, cw, cb, clg, clb, slg, slb, ws, bst, tile):
    def fn(i, ins, outs, scr):
        p_ref, ph_ref, cw_ref, cb_ref, clg_ref, clb_ref, slg_ref, slb_ref, ws_ref, bst_ref = ins
        y_ref, z_ref = outs
        zext = scr[0]
        z = p_ref[:, :DC] * _sigmoid(p_ref[:, DC:2 * DC])
        zh = ph_ref[:, :DC] * _sigmoid(ph_ref[:, DC:2 * DC])
        zext[:HALO, :] = jnp.where(i > 0, zh, 0.0)
        zext[HALO:, :] = z
        z_ref[...] = z
        conv = _conv_taps(cw_ref, zext, tile, lambda k: HALO - (CONV_W - 1) + k) + cb_ref[...]
        yl = _ln_hat(conv)[0] * clg_ref[...] + clb_ref[...]
        y_ref[:, :DC] = (yl * _sigmoid(yl)).astype(BF16)
        zb = _gelu(p_ref[:, 2 * DC:])
        vln = (_ln_hat(zb[:, DC:])[0] * slg_ref[...] + slb_ref[...]).astype(BF16)
        mask = _tril_mask()
        for hd in range(NHEAD_SGU):
            wm = jnp.where(mask, ws_ref[hd], 0.0).astype(BF16)
            for ci in range(tile // CHUNK):
                r0, c0 = ci * CHUNK, hd * CHUNK
                mixed = _dot(wm, vln[r0:r0 + CHUNK, c0:c0 + CHUNK]) + bst_ref[:, hd:hd + 1]
                y_ref[r0:r0 + CHUNK, DC + c0:DC + c0 + CHUNK] = (zb[r0:r0 + CHUNK, c0:c0 + CHUNK] * mixed).astype(BF16)

    return _rowwise(fn, "even_core", p.shape[0], tile, [p, _prev_halo(p, tile, D)], [cw, cb, clg, clb, slg, slb, ws, bst],
                    [(D, BF16), (DC, F32)], scratch=[pltpu.VMEM((tile + HALO, DC), F32)])


def _even_bwd_a(p, z, dy, cw, cb, clg, clb, slg, slb, ws, wst, bst, tile):
    def fn(i, ins, outs, scr):
        p_ref, z_ref, zh_ref, dy_ref, cw_ref, cb_ref, clg_ref, clb_ref, slg_ref, slb_ref, ws_ref, wst_ref, bst_ref = ins
        dc_ref, dpb_ref, vec_ref, dws_ref, dbst_ref = outs
        zext, dvln_s = scr
        zext[:HALO, :] = jnp.where(i > 0, zh_ref[...], 0.0)
        zext[HALO:, :] = z_ref[...]
        conv = _conv_taps(cw_ref, zext, tile, lambda k: HALO - (CONV_W - 1) + k) + cb_ref[...]
        chat, cr = _ln_hat(conv)
        yl = chat * clg_ref[...] + clb_ref[...]
        dyl = dy_ref[:, :DC] * _silu_grad(yl)
        dconv, dclg, dclb = _ln_bwd(chat, cr, clg_ref[...], dyl)
        dc_ref[...] = dconv
        xb = p_ref[:, 2 * DC:]
        zb = _gelu(xb)
        vhat, vr = _ln_hat(zb[:, DC:])
        vln = (vhat * slg_ref[...] + slb_ref[...]).astype(BF16)
        dyb = dy_ref[:, DC:]
        mask, mask_t = _tril_mask(), _tril_mask(transpose=True)
        dbu_parts = []
        for hd in range(NHEAD_SGU):
            wm = jnp.where(mask, ws_ref[hd], 0.0).astype(BF16)
            wt = jnp.where(mask_t, wst_ref[hd], 0.0).astype(BF16)
            dws_h, dbs_h, rows = None, None, []
            for ci in range(tile // CHUNK):
                r0, c0 = ci * CHUNK, hd * CHUNK
                vblk = vln[r0:r0 + CHUNK, c0:c0 + CHUNK]
                mixed = _dot(wm, vblk) + bst_ref[:, hd:hd + 1]
                dyb_blk = dyb[r0:r0 + CHUNK, c0:c0 + CHUNK]
                rows.append(dyb_blk * mixed)
                dmixed = dyb_blk * zb[r0:r0 + CHUNK, c0:c0 + CHUNK]
                dmb = dmixed.astype(BF16)
                dvln_s[r0:r0 + CHUNK, c0:c0 + CHUNK] = _dot(wt, dmb)
                dw = _dot(dmb, vblk, "nt")
                db = jnp.sum(dmixed, axis=-1, keepdims=True)
                dws_h = dw if dws_h is None else dws_h + dw
                dbs_h = db if dbs_h is None else dbs_h + db
            dbu_parts.append(jnp.concatenate(rows, axis=0))
            dws_h = jnp.where(mask, dws_h, 0.0)

            @pl.when(i == 0)
            def _():
                dws_ref[hd] = dws_h
                dbst_ref[:, hd:hd + 1] = dbs_h

            @pl.when(i > 0)
            def _():
                dws_ref[hd] += dws_h
                dbst_ref[:, hd:hd + 1] += dbs_h

        dbu = jnp.concatenate(dbu_parts, axis=1)
        dbv, dslg, dslb = _ln_bwd(vhat, vr, slg_ref[...], dvln_s[...])
        dpb_ref[:, :DC] = (dbu * _gelu_grad(xb[:, :DC])).astype(BF16)
        dpb_ref[:, DC:] = (dbv * _gelu_grad(xb[:, DC:])).astype(BF16)
        @pl.when(i == 0)
        def _():
            vec_ref[...] = jnp.zeros_like(vec_ref)

        for r, val in enumerate([jnp.sum(dconv, axis=0, keepdims=True), dclg, dclb, dslg, dslb]):
            vec_ref[r:r + 1, :] += val

    return _rowwise(fn, "even_bwd_a", p.shape[0], tile, [p, z, _prev_halo(z, tile), dy],
                    [cw, cb, clg, clb, slg, slb, ws, wst, bst], [(DC, F32), (D, BF16)],
                    [((8, DC), F32), ((NHEAD_SGU, CHUNK, CHUNK), F32), ((CHUNK, NHEAD_SGU), F32)],
                    scratch=[pltpu.VMEM((tile + HALO, DC), F32), pltpu.VMEM((tile, DC), F32)])


def _even_bwd_b(p, z, dconv, dpb, cw, tile):
    def fn(i, ins, outs, scr):
        p_ref, z_ref, zh_ref, dc_ref, dcn_ref, dpb_ref, cw_ref = ins
        dp_ref, dcw_ref = outs
        zext, dcext = scr
        last = pl.num_programs(0) - 1
        zext[:HALO, :] = jnp.where(i > 0, zh_ref[...], 0.0)
        zext[HALO:, :] = z_ref[...]
        dcext[:tile, :] = dc_ref[...]
        dcext[tile:, :] = jnp.where(i < last, dcn_ref[...], 0.0)
        dz = _conv_taps(cw_ref, dcext, tile, lambda k: (CONV_W - 1) - k)
        gate = p_ref[:, DC:2 * DC]
        s = _sigmoid(gate)
        dp_ref[:, :DC] = (dz * s).astype(BF16)
        dp_ref[:, DC:2 * DC] = (dz * p_ref[:, :DC] * s * (1.0 - s)).astype(BF16)
        dp_ref[:, 2 * DC:] = dpb_ref[...]
        dcv = dc_ref[...]

        @pl.when(i == 0)
        def _():
            dcw_ref[...] = jnp.zeros_like(dcw_ref)

        for k in range(CONV_W):
            dcw_ref[k:k + 1, :] += jnp.sum(dcv * zext[pl.ds(HALO - (CONV_W - 1) + k, tile), :], axis=0, keepdims=True)

    return _rowwise(fn, "even_bwd_b", p.shape[0], tile,
                    [(p, tile, D, lambda i: (i, 0)), z, _prev_halo(z, tile), dconv, _next_halo(dconv, tile), dpb], [cw],
                    [(2 * D, BF16)], [((HALO, DC), F32)],
                    scratch=[pltpu.VMEM((tile + HALO, DC), F32), pltpu.VMEM((tile + HALO, DC), F32)])


def _row_count(i, tile, nrows, offset, w):
    t = i * tile + offset + lax.broadcasted_iota(jnp.int32, (nrows, POOL_GD), 0)
    return jnp.minimum(t + 1, w).astype(F32)


def _odd_core(p, wg, scale, tile):
    def fn(i, ins, outs, scr):
        p_ref, ph_ref, wg_ref, sc_ref = ins
        d_ref, e_ref, es_ref = outs
        pext = scr[0]
        pext[:HALO, :] = jnp.where(i > 0, ph_ref[...], 0.0)
        pext[HALO:, :] = p_ref[...]
        for g, w in enumerate(POOL_WINDOWS):
            c0 = g * POOL_GD
            s = p_ref[:, c0:c0 + POOL_GD]
            for r in range(1, w):
                s = s + pext[pl.ds(HALO - r, tile), c0:c0 + POOL_GD]
            dg = (s / _row_count(i, tile, tile, 0, w) - p_ref[:, c0:c0 + POOL_GD]).astype(BF16)
            d_ref[:, c0:c0 + POOL_GD] = dg
            e = _dot(dg, wg_ref[g])
            e_ref[:, c0:c0 + POOL_GD] = e
            es_ref[:, c0:c0 + POOL_GD] = (e * sc_ref[:, c0:c0 + POOL_GD]).astype(BF16)

    return _rowwise(fn, "odd_core", p.shape[0], tile, [p, _prev_halo(p, tile)], [wg, scale],
                    [(D, BF16), (D, F32), (D, BF16)], scratch=[pltpu.VMEM((tile + HALO, D), F32)])


def _odd_bwd(des, e, d, wg, scale, tile):
    def fn(i, ins, outs, scr):
        des_ref, desn_ref, e_ref, d_ref, wg_ref, sc_ref = ins
        dp_ref, dsc_ref, dwg_ref = outs
        qext = scr[0]
        last = pl.num_programs(0) - 1
        desv = des_ref[...]
        _accum(dsc_ref, i, jnp.sum(desv * e_ref[...], axis=0, keepdims=True))
        de = (desv * sc_ref[...]).astype(BF16)
        den = (jnp.where(i < last, desn_ref[...], 0.0) * sc_ref[...]).astype(BF16)
        for g, w in enumerate(POOL_WINDOWS):
            c0 = g * POOL_GD
            deg = de[:, c0:c0 + POOL_GD]
            dd = _dot(deg, wg_ref[g], "nt")
            ddn = _dot(den[:, c0:c0 + POOL_GD], wg_ref[g], "nt")
            qext[:tile, c0:c0 + POOL_GD] = dd / _row_count(i, tile, tile, 0, w)
            qext[tile:, c0:c0 + POOL_GD] = ddn / _row_count(i, tile, HALO, tile, w)
            s = qext[:tile, c0:c0 + POOL_GD]
            for r in range(1, w):
                s = s + qext[pl.ds(r, tile), c0:c0 + POOL_GD]
            dp_ref[:, c0:c0 + POOL_GD] = (s - dd).astype(BF16)
            dw = _dot(d_ref[:, c0:c0 + POOL_GD], deg, "tn")

            @pl.when(i == 0)
            def _():
                dwg_ref[g] = dw

            @pl.when(i > 0)
            def _():
                dwg_ref[g] += dw

    return _rowwise(fn, "odd_bwd", des.shape[0], tile, [des, _next_halo(des, tile), e, d], [wg, scale], [(D, BF16)],
                    [((1, D), F32), ((len(POOL_WINDOWS), POOL_GD, POOL_GD), F32)],
                    scratch=[pltpu.VMEM((tile + HALO, D), F32)])


def _axis_slice(ref, axis, start, size):
    idx = [slice(None)] * len(ref.shape)
    idx[axis] = pl.ds(start, size)
    return ref.at[tuple(idx)]


def _gather_weights(shards, axes):
    n = len(shards)
    out_shape = []
    for s, ax in zip(shards, axes):
        shp = list(s.shape)
        shp[ax] *= NCHIP
        out_shape.append(jax.ShapeDtypeStruct(tuple(shp), s.dtype))

    def body(*refs):
        ins, outs = refs[:n], refs[n:2 * n]
        send_sems, recv_sems, loc_sems = refs[2 * n:]
        x, y, c = lax.axis_index("x"), lax.axis_index("y"), lax.axis_index("c")
        mine = 2 * x + y
        chips = [(1 - x, y), (x, 1 - y), (1 - x, 1 - y)]
        copies = []
        for t in range(n):
            size = ins[t].shape[axes[t]]
            dst = _axis_slice(outs[t], axes[t], mine * size, size)
            loc = pltpu.make_async_copy(ins[t], dst, loc_sems.at[t])
            loc.start()
            copies.append(loc)
            for k, chip in enumerate(chips):
                cp = pltpu.make_async_remote_copy(src_ref=ins[t], dst_ref=dst, send_sem=send_sems.at[t, k],
                                                  recv_sem=recv_sems.at[t, k], device_id=(*chip, c), device_id_type=MESH)
                cp.start()
                copies.append(cp)
        for cp in copies:
            cp.wait()

    return pl.pallas_call(body, name="gather_weights", in_specs=[ANY] * n, out_specs=[ANY] * n, out_shape=out_shape,
                          scratch_shapes=[pltpu.SemaphoreType.DMA((n, 3)), pltpu.SemaphoreType.DMA((n, 3)),
                                          pltpu.SemaphoreType.DMA((n,))])(*shards)


def _peer(x, y, c, k):
    return (1 - x if k & 4 else x, 1 - y if k & 2 else y, 1 - c if k & 1 else c)


def _scatter_grads(grads, kinds):
    n = len(grads)
    geo = []
    for g, kind in zip(grads, kinds):
        L, R, C = g.shape
        geo.append((L, R // (2 * NCHIP), C) if kind == "row" else (L, R // 2, C // NCHIP))
    out_shape = [jax.ShapeDtypeStruct((8,) + s, BF16) for s in geo]

    def body(*refs):
        ins, outs = refs[:n], refs[n:2 * n]
        send_sems, recv_sems, loc_sems = refs[2 * n:]
        x, y, c = lax.axis_index("x"), lax.axis_index("y"), lax.axis_index("c")
        copies = []
        for t in range(n):
            L, rh, cs = geo[t]
            for k in range(8):
                px, py, pc = _peer(x, y, c, k)
                chip = 2 * px + py
                if kinds[t] == "row":
                    src = ins[t].at[:, pl.ds((2 * chip + pc) * rh, rh), :]
                else:
                    src = ins[t].at[:, pl.ds(pc * rh, rh), pl.ds(chip * cs, cs)]
                if k == 0:
                    cp = pltpu.make_async_copy(src, outs[t].at[0], loc_sems.at[t])
                else:
                    cp = pltpu.make_async_remote_copy(src_ref=src, dst_ref=outs[t].at[k], send_sem=send_sems.at[t, k - 1],
                                                      recv_sem=recv_sems.at[t, k - 1], device_id=(px, py, pc),
                                                      device_id_type=MESH)
                cp.start()
                copies.append(cp)
        for cp in copies:
            cp.wait()

    return pl.pallas_call(body, name="scatter_grads", in_specs=[ANY] * n, out_specs=[ANY] * n, out_shape=out_shape,
                          scratch_shapes=[pltpu.SemaphoreType.DMA((n, 7)), pltpu.SemaphoreType.DMA((n, 7)),
                                          pltpu.SemaphoreType.DMA((n,))])(*grads)


def _swap_halves(halves):
    n = len(halves)
    out_shape = [jax.ShapeDtypeStruct((h.shape[0], 2 * h.shape[1], h.shape[2]), h.dtype) for h in halves]

    def body(*refs):
        ins, outs = refs[:n], refs[n:2 * n]
        send_sems, recv_sems, loc_sems = refs[2 * n:]
        x, y, c = lax.axis_index("x"), lax.axis_index("y"), lax.axis_index("c")
        copies = []
        for t in range(n):
            rh = ins[t].shape[1]
            dst = outs[t].at[:, pl.ds(c * rh, rh), :]
            loc = pltpu.make_async_copy(ins[t], dst, loc_sems.at[t])
            cp = pltpu.make_async_remote_copy(src_ref=ins[t], dst_ref=dst, send_sem=send_sems.at[t], recv_sem=recv_sems.at[t],
                                              device_id=(x, y, 1 - c), device_id_type=MESH)
            loc.start()
            cp.start()
            copies += [loc, cp]
        for cp in copies:
            cp.wait()

    return pl.pallas_call(body, name="swap_halves", in_specs=[ANY] * n, out_specs=[ANY] * n, out_shape=out_shape,
                          scratch_shapes=[pltpu.SemaphoreType.DMA((n,)), pltpu.SemaphoreType.DMA((n,)),
                                          pltpu.SemaphoreType.DMA((n,))])(*halves)


def _share_small(packed):
    rows = packed.shape[0]

    def body(in_ref, out_ref, send_sems, recv_sems, loc_sem):
        x, y, c = lax.axis_index("x"), lax.axis_index("y"), lax.axis_index("c")
        me = 4 * x + 2 * y + c
        copies = [pltpu.make_async_copy(in_ref, out_ref.at[me], loc_sem)]
        for k in range(1, 8):
            copies.append(pltpu.make_async_remote_copy(src_ref=in_ref, dst_ref=out_ref.at[me], send_sem=send_sems.at[k - 1],
                                                       recv_sem=recv_sems.at[k - 1], device_id=_peer(x, y, c, k),
                                                       device_id_type=MESH))
        for cp in copies:
            cp.start()
        for cp in copies:
            cp.wait()

    return pl.pallas_call(body, name="share_small", in_specs=[ANY], out_specs=ANY,
                          out_shape=jax.ShapeDtypeStruct((8, rows, 128), F32),
                          scratch_shapes=[pltpu.SemaphoreType.DMA((7,)), pltpu.SemaphoreType.DMA((7,)),
                                          pltpu.SemaphoreType.DMA])(packed)


def _sum_slots(buf, tile, name):
    _, rows, cols = buf.shape

    def body(b_ref, o_ref):
        acc = b_ref[0].astype(F32)
        for s in range(1, 8):
            acc = acc + b_ref[s].astype(F32)
        o_ref[...] = acc

    return pl.pallas_call(body, name=name, grid=(rows // tile,), in_specs=[pl.BlockSpec((8, tile, cols), lambda i: (0, i, 0))],
                          out_specs=pl.BlockSpec((tile, cols), lambda i: (i, 0)),
                          out_shape=jax.ShapeDtypeStruct((rows, cols), F32), compiler_params=_params(("parallel",)))(buf)


def _adamw(w, g, m, v, tile, name):
    rows, cols = w.shape

    def body(w_ref, g_ref, m_ref, v_ref, d_ref, mo_ref, vo_ref):
        gv = g_ref[...]
        mn = ADAM_B1 * m_ref[...] + (1.0 - ADAM_B1) * gv
        vn = ADAM_B2 * v_ref[...] + (1.0 - ADAM_B2) * (gv * gv)
        m_hat = mn / (1.0 - ADAM_B1 ** ADAM_STEP)
        v_hat = vn / (1.0 - ADAM_B2 ** ADAM_STEP)
        d_ref[...] = -ADAM_LR * (m_hat / (jnp.sqrt(v_hat) + ADAM_EPS) + ADAM_WD * w_ref[...])
        mo_ref[...] = mn
        vo_ref[...] = vn

    spec = pl.BlockSpec((tile, cols), lambda i: (i, 0))
    sds = jax.ShapeDtypeStruct((rows, cols), F32)
    return pl.pallas_call(body, name=name, grid=(rows // tile,), in_specs=[spec] * 4, out_specs=[spec] * 3,
                          out_shape=[sds] * 3, compiler_params=_params(("parallel",)))(w, g, m, v)


def _row_tile(rows, cap):
    best = 8
    for t in range(8, min(rows, cap) + 1, 8):
        if rows % t == 0:
            best = t
    return best


def _pack(arrs):
    parts = []
    for a in arrs:
        r = a.size // 128
        r8 = -(-r // 8) * 8
        parts.append(jnp.pad(a.reshape(r, 128).astype(F32), ((0, r8 - r), (0, 0))))
    return jnp.concatenate(parts, axis=0)


def _unpack(packed, shapes):
    out, r0 = [], 0
    for shp in shapes:
        r = math.prod(shp) // 128
        out.append(packed[r0:r0 + r].reshape(shp))
        r0 += -(-r // 8) * 8
    return out


BIG = {
    "ffn1_w_gu": "col", "ffn1_w_down": "row", "ev_w_in": "col", "ev_w_out": "row", "od_w_in": "row", "od_w_group": "row",
    "od_w_out": "row", "xa_w_q": "row", "xa_w_kv": "col", "xa_w_o": "row", "ffn2_w_gu": "col", "ffn2_w_down": "row",
}
TINY_SHARDED = {"ev_conv_w": 2, "od_scale": 1}
WEIGHTS = ['ffn1_pre_g', 'ffn1_w_gu', 'ffn1_w_down', 'ffn1_post_g', 'mix_pre_g', 'mix_post_g', 'ev_w_in', 'ev_conv_w',
           'ev_conv_b', 'ev_conv_ln_g', 'ev_conv_ln_b', 'ev_sgu_ln_g', 'ev_sgu_ln_b', 'ev_sgu_w', 'ev_sgu_b', 'ev_w_out',
           'od_w_in', 'od_w_group', 'od_scale', 'od_w_out', 'xa_pre_g', 'xa_mem_g', 'xa_w_q', 'xa_w_kv', 'xa_w_o', 'xa_post_g',
           'ffn2_pre_g', 'ffn2_w_gu', 'ffn2_w_down', 'ffn2_post_g']


def _as3d(a):
    return a.reshape((-1,) + a.shape[-2:]) if a.ndim == 4 else a


class _Grads:
    def __init__(self):
        self.buf = {}

    def add(self, name, layer, nlayers, a, b, tm, tn, n_outer=True):
        self.buf[name] = _matmul(a, b, "tn", BF16, "dw_" + name, tm, tn, out_l=layer, out_stack=nlayers,
                                 out_buf=self.buf.get(name), n_outer=n_outer)


def _local_step(x, mem, target, W, S, tiles=None):
    T, TW, tm = tiles or (min(512, S), min(256, S), min(512, S))
    row = lambda v: v.reshape(1, -1)
    subs = []
    small = {k: [None] * W[k].shape[0] for k in WEIGHTS if k not in BIG}
    g = _Grads()

    def ffn_fwd(tag, l, h, n, g_next):
        gu = _matmul(n, W[tag + "_w_gu"], "nn", F32, tag + "_up", tm, 1408, b_l=l)
        a = _swiglu(gu, TW)
        f = _matmul(a, W[tag + "_w_down"], "nn", F32, tag + "_down", tm, D, b_l=l)
        h2, n2 = _post(h, f, row(W[tag + "_post_g"][l]), 0.5, g_next, T)
        subs.append(dict(kind="ffn", tag=tag, l=l, h=h, n=n, gu=gu, a=a, f=f, scale=0.5, pre=tag + "_pre_g", post=tag + "_post_g"))
        return h2, n2

    def ffn_bwd(s, df):
        tag, l = s["tag"], s["l"]
        da = _matmul(df, W[tag + "_w_down"], "nt", F32, tag + "_down_dx", tm, 1408, b_l=l)
        g.add(tag + "_w_down", l, 4, s["a"], df, 256, D, n_outer=False)
        dgu = _swiglu_bwd(da, s["gu"], TW)
        dn = _matmul(dgu, W[tag + "_w_gu"], "nt", F32, tag + "_up_dx", tm, D, b_l=l)
        g.add(tag + "_w_gu", l, 4, s["n"], dgu, D, 512, n_outer=False)
        return dn

    def xa_fwd(l, h, n, g_next):
        q = _matmul(n, W["xa_w_q"], "nn", BF16, "xa_q", tm, D, b_l=l)
        mn = _mem_norm(mem, row(W["xa_mem_g"][l]))
        kv = _matmul(mn, W["xa_w_kv"], "nn", BF16, "xa_kv", NMEM, D, b_l=l)
        o = _attn(q, kv, T)
        cx = _matmul(o, W["xa_w_o"], "nn", F32, "xa_o", tm, D, b_l=l)
        h2, n2 = _post(h, cx, row(W["xa_post_g"][l]), 1.0, g_next, T)
        subs.append(dict(kind="xa", l=l, h=h, n=n, q=q, mn=mn, kv=kv, o=o, f=cx, scale=1.0, pre="xa_pre_g", post="xa_post_g"))
        return h2, n2

    def xa_bwd(s, dc):
        l = s["l"]
        do = _matmul(dc, W["xa_w_o"], "nt", BF16, "xa_o_dx", tm, D, b_l=l)
        g.add("xa_w_o", l, 4, s["o"], dc, D, 512)
        dq, dkv = _attn_bwd(s["q"], s["kv"], do, T)
        dn = _matmul(dq, W["xa_w_q"], "nt", F32, "xa_q_dx", tm, D, b_l=l)
        g.add("xa_w_q", l, 4, s["n"], dq, D, 512)
        dkvb = dkv.astype(BF16)
        g.add("xa_w_kv", l, 4, s["mn"], dkvb, D, 512)
        dmn = _matmul(dkvb, W["xa_w_kv"], "nt", F32, "xa_kv_dx", NMEM, D, b_l=l)
        small["xa_mem_g"][l] = _mem_gain_bwd(mem, dmn)[0]
        return dn

    def even_params(e):
        return (W["ev_conv_w"][e], row(W["ev_conv_b"][e]), row(W["ev_conv_ln_g"][e]), row(W["ev_conv_ln_b"][e]),
                row(W["ev_sgu_ln_g"][e]), row(W["ev_sgu_ln_b"][e]), W["ev_sgu_w"][e])

    def even_fwd(l, h, n, g_next):
        e = l // 2
        p = _matmul(n, W["ev_w_in"], "nn", F32, "ev_in", tm, D, b_l=e)
        cw, cb, clg, clb, slg, slb, ws = even_params(e)
        y, z = _even_core(p, cw, cb, clg, clb, slg, slb, ws, W["ev_sgu_b"][e].T, TW)
        m = _matmul(y, W["ev_w_out"], "nn", F32, "ev_out", tm, D, b_l=e)
        h2, n2 = _post(h, m, row(W["mix_post_g"][l]), 1.0, g_next, T)
        subs.append(dict(kind="even", l=l, h=h, n=n, p=p, y=y, z=z, f=m, scale=1.0, pre="mix_pre_g", post="mix_post_g"))
        return h2, n2

    def even_bwd(s, dm):
        l = s["l"]
        e = l // 2
        dy = _matmul(dm, W["ev_w_out"], "nt", F32, "ev_out_dx", tm, D, b_l=e)
        g.add("ev_w_out", e, 2, s["y"], dm, D, 512)
        cw, cb, clg, clb, slg, slb, ws = even_params(e)
        dconv, dpb, vecs, dws, dbst = _even_bwd_a(s["p"], s["z"], dy, cw, cb, clg, clb, slg, slb, ws,
                                                   jnp.swapaxes(ws, 1, 2), W["ev_sgu_b"][e].T, TW)
        dp, dcw = _even_bwd_b(s["p"], s["z"], dconv, dpb, cw, TW)
        for r, name in enumerate(["ev_conv_b", "ev_conv_ln_g", "ev_conv_ln_b", "ev_sgu_ln_g", "ev_sgu_ln_b"]):
            small[name][e] = vecs[r]
        small["ev_sgu_w"][e] = dws
        small["ev_sgu_b"][e] = dbst.T
        small["ev_conv_w"][e] = dcw[:CONV_W]
        dn = _matmul(dp, W["ev_w_in"], "nt", F32, "ev_in_dx", tm, D, b_l=e)
        g.add("ev_w_in", e, 2, s["n"], dp, D, 512)
        return dn

    def odd_fwd(l, h, n, g_next):
        o = l // 2
        p = _matmul(n, W["od_w_in"], "nn", F32, "od_in", tm, D, b_l=o)
        d, e, es = _odd_core(p, W["od_w_group"][o], row(W["od_scale"][o]), T)
        m = _matmul(es, W["od_w_out"], "nn", F32, "od_out", tm, D, b_l=o)
        h2, n2 = _post(h, m, row(W["mix_post_g"][l]), 1.0, g_next, T)
        subs.append(dict(kind="odd", l=l, h=h, n=n, d=d, e=e, es=es, f=m, scale=1.0, pre="mix_pre_g", post="mix_post_g"))
        return h2, n2

    def odd_bwd(s, dm):
        l = s["l"]
        o = l // 2
        des = _matmul(dm, W["od_w_out"], "nt", F32, "od_out_dx", tm, D, b_l=o)
        g.add("od_w_out", o, 2, s["es"], dm, D, 512)
        dp, dsc, dwg = _odd_bwd(des, s["e"], s["d"], W["od_w_group"][o], row(W["od_scale"][o]), T)
        small["od_scale"][o] = dsc[0]
        small["od_w_group_full"][o] = dwg
        dn = _matmul(dp, W["od_w_in"], "nt", F32, "od_in_dx", tm, D, b_l=o)
        g.add("od_w_in", o, 2, s["n"], dp, D, 512)
        return dn

    small["od_w_group_full"] = [None, None]
    order = []
    for l in range(4):
        order += [("ffn1", l), ("even" if l % 2 == 0 else "odd", l), ("xa", l), ("ffn2", l)]
    pre_of = {"ffn1": "ffn1_pre_g", "even": "mix_pre_g", "odd": "mix_pre_g", "xa": "xa_pre_g", "ffn2": "ffn2_pre_g"}
    h = x
    n = _prenorm(h, row(W["ffn1_pre_g"][0]), T)
    for idx, (kind, l) in enumerate(order):
        g_next = row(W[pre_of[order[idx + 1][0]]][order[idx + 1][1]]) if idx + 1 < len(order) else None
        if kind in ("ffn1", "ffn2"):
            h, n = ffn_fwd(kind, l, h, n, g_next)
        elif kind == "xa":
            h, n = xa_fwd(l, h, n, g_next)
        elif kind == "even":
            h, n = even_fwd(l, h, n, g_next)
        else:
            h, n = odd_fwd(l, h, n, g_next)

    top = subs[-1]
    dh, df, loss_acc, dgp = _loss_top(h, target, top["f"], row(W[top["post"]][top["l"]]), top["scale"], T)
    small[top["post"]][top["l"]] = dgp[0]
    for idx in range(len(subs) - 1, -1, -1):
        s = subs[idx]
        dn = {"ffn": ffn_bwd, "xa": xa_bwd, "even": even_bwd, "odd": odd_bwd}[s["kind"]](s, df)
        if idx > 0:
            sp = subs[idx - 1]
            dh, df, dg_pre, dg_post = _boundary_bwd(s["h"], row(W[s["pre"]][s["l"]]), dn, dh, sp["f"],
                                                    row(W[sp["post"]][sp["l"]]), sp["scale"], T)
            small[sp["post"]][sp["l"]] = dg_post[0]
        else:
            dh, dg_pre = _boundary_bwd(s["h"], row(W[s["pre"]][s["l"]]), dn, dh, None, None, None, T)
        small[s["pre"]][s["l"]] = dg_pre[0]
    return loss_acc[0, 0], dh, g.buf, small


def _step(P, S):
    x, mem, target = P["x"][0], P["mem"][0], P["loss_target"][0]
    names = list(BIG) + list(TINY_SHARDED)
    shards = [_as3d(P[k]).astype(BF16) for k in BIG] + [P[k] for k in TINY_SHARDED]
    axes = [1 if BIG[k] == "row" else 2 for k in BIG] + list(TINY_SHARDED.values())
    full = dict(zip(names, _gather_weights(shards, axes)))
    W = {k: P[k] for k in WEIGHTS if k not in full}
    W.update(full)
    W["od_w_group"] = W["od_w_group"].reshape(2, 4, POOL_GD, POOL_GD)

    loss, grad_x, big, small = _local_step(x, mem, target, W, S)
    dwg = small.pop("od_w_group_full")
    big["od_w_group"] = jnp.stack(dwg).astype(BF16).reshape(8, POOL_GD, POOL_GD)

    order = list(BIG)
    recv = _scatter_grads([big[k] for k in order], [BIG[k] for k in order])
    halves = []
    for k, r in zip(order, recv):
        _, L, rh, cs = r.shape
        halves.append(_sum_slots(r.reshape(8, L * rh, cs), _row_tile(L * rh, 512), "sum_" + k).reshape(L, rh, cs))
    gsh = dict(zip(order, _swap_halves(halves)))

    small_names = [k for k in WEIGHTS if k not in BIG]
    small_full = [jnp.stack(small[k]) for k in small_names]
    packed = _pack(small_full)
    summed = _sum_slots(_share_small(packed), _row_tile(packed.shape[0], 512), "sum_small")
    gsmall = dict(zip(small_names, _unpack(summed, [a.shape for a in small_full])))
    chip = 2 * lax.axis_index("x") + lax.axis_index("y")
    for k, ax in TINY_SHARDED.items():
        size = P[k].shape[ax]
        gsmall[k] = lax.dynamic_slice_in_dim(gsmall[k], chip * size, size, axis=ax)

    grads, delta, new_m, new_v = {}, {}, {}, {}
    for k in order:
        shp = P[k].shape
        cols = shp[-1]
        flat = lambda a: a.reshape(-1, cols)
        gk = flat(gsh[k])
        d_, m_, v_ = _adamw(flat(P[k]), gk, flat(P["m_" + k]), flat(P["v_" + k]), _row_tile(gk.shape[0], 256), "adamw_" + k)
        grads[k], delta[k], new_m[k], new_v[k] = gk.reshape(shp), d_.reshape(shp), m_.reshape(shp), v_.reshape(shp)
    pk = lambda pre: _pack([P[pre + k] for k in small_names])
    d_, m_, v_ = _adamw(pk(""), _pack([gsmall[k] for k in small_names]), pk("m_"), pk("v_"),
                        _row_tile(pk("").shape[0], 256), "adamw_small")
    shapes = [P[k].shape for k in small_names]
    for dst, src in ((delta, d_), (new_m, m_), (new_v, v_)):
        dst.update(zip(small_names, _unpack(src, shapes)))
    grads.update(gsmall)

    loss = lax.psum(loss, ("x", "y", "c"))
    out = [loss, grad_x[None]]
    for grp in (grads, delta, new_m, new_v):
        out += [grp[k] for k in WEIGHTS]
    return tuple(out)


def kernel(x, mem, ffn1_pre_g, ffn1_w_gu, ffn1_w_down, ffn1_post_g, mix_pre_g, mix_post_g, ev_w_in, ev_conv_w, ev_conv_b, ev_conv_ln_g, ev_conv_ln_b, ev_sgu_ln_g, ev_sgu_ln_b, ev_sgu_w, ev_sgu_b, ev_w_out, od_w_in, od_w_group, od_scale, od_w_out, xa_pre_g, xa_mem_g, xa_w_q, xa_w_kv, xa_w_o, xa_post_g, ffn2_pre_g, ffn2_w_gu, ffn2_w_down, ffn2_post_g, loss_target, m_ffn1_pre_g, m_ffn1_w_gu, m_ffn1_w_down, m_ffn1_post_g, m_mix_pre_g, m_mix_post_g, m_ev_w_in, m_ev_conv_w, m_ev_conv_b, m_ev_conv_ln_g, m_ev_conv_ln_b, m_ev_sgu_ln_g, m_ev_sgu_ln_b, m_ev_sgu_w, m_ev_sgu_b, m_ev_w_out, m_od_w_in, m_od_w_group, m_od_scale, m_od_w_out, m_xa_pre_g, m_xa_mem_g, m_xa_w_q, m_xa_w_kv, m_xa_w_o, m_xa_post_g, m_ffn2_pre_g, m_ffn2_w_gu, m_ffn2_w_down, m_ffn2_post_g, v_ffn1_pre_g, v_ffn1_w_gu, v_ffn1_w_down, v_ffn1_post_g, v_mix_pre_g, v_mix_post_g, v_ev_w_in, v_ev_conv_w, v_ev_conv_b, v_ev_conv_ln_g, v_ev_conv_ln_b, v_ev_sgu_ln_g, v_ev_sgu_ln_b, v_ev_sgu_w, v_ev_sgu_b, v_ev_w_out, v_od_w_in, v_od_w_group, v_od_scale, v_od_w_out, v_xa_pre_g, v_xa_mem_g, v_xa_w_q, v_xa_w_kv, v_xa_w_o, v_xa_post_g, v_ffn2_pre_g, v_ffn2_w_gu, v_ffn2_w_down, v_ffn2_post_g):
    P = dict(locals())
    return _step(P, x.shape[1])
```

```python
import functools
import math

import jax
import jax.numpy as jnp
from jax import lax
from jax.experimental import pallas as pl
from jax.experimental.pallas import tpu as pltpu

F32 = jnp.float32
BF16 = jnp.bfloat16
MESH = pl.DeviceIdType.MESH
ANY = pl.BlockSpec(memory_space=pl.ANY)
HBM = pl.BlockSpec(memory_space=pltpu.HBM)


def _hbm(x):
    return pltpu.with_memory_space_constraint(x, pltpu.HBM)

D = 1024
DFF = 2816
NMEM = 256
DC = 512
CONV_W = 31
CHUNK = 128
NHEAD_SGU = 4
POOL_WINDOWS = (2, 4, 8, 16)
POOL_GD = 256
XA_HEADS = 4
XA_HD = 256
EPS = 1e-6
NCHIP = 4
HALO = 32

ADAM_LR, ADAM_B1, ADAM_B2, ADAM_EPS, ADAM_WD, ADAM_STEP = 0.001, 0.9, 0.999, 1e-08, 0.01, 10

V7X_VMEM_BYTES = 64 * 1024 * 1024
VMEM_LIMIT = V7X_VMEM_BYTES - 8 * 1024 * 1024


def _params(sem):
    return pltpu.CompilerParams(dimension_semantics=sem, vmem_limit_bytes=VMEM_LIMIT)


def _matmul(a, b, kind, out_dtype, name, tm, tn, a_l=None, b_l=None, out_l=None, out_stack=None, out_buf=None,
            n_outer=True):
    a2, b2 = a.shape[-2:], b.shape[-2:]
    if kind == "nn":
        (m, k), (k2, n) = a2, b2
        dims = (((1,), (0,)), ((), ()))
    elif kind == "nt":
        (m, k), (n, k2) = a2, b2
        dims = (((1,), (1,)), ((), ()))
    else:
        (k, m), (k2, n) = a2, b2
        dims = (((0,), (0,)), ((), ()))
    assert k == k2 and m % tm == 0 and n % tn == 0, (name, a.shape, b.shape, tm, tn)
    if n_outer:
        grid = (n // tn, m // tm)
        ij = lambda p, q: (q, p)
    else:
        grid = (m // tm, n // tn)
        ij = lambda p, q: (p, q)

    def spec(arr, layer, blk, idx):
        if arr.ndim == 3:
            return pl.BlockSpec((None,) + blk, lambda p, q: (layer,) + idx(*ij(p, q)))
        return pl.BlockSpec(blk, lambda p, q: idx(*ij(p, q)))

    a_spec = spec(a, a_l, (k, tm) if kind == "tn" else (tm, k), (lambda i, j: (0, i)) if kind == "tn" else (lambda i, j: (i, 0)))
    b_spec = spec(b, b_l, (tn, k) if kind == "nt" else (k, tn), (lambda i, j: (j, 0)) if kind == "nt" else (lambda i, j: (0, j)))
    if out_stack is None:
        out_shape = jax.ShapeDtypeStruct((m, n), out_dtype)
        o_spec = pl.BlockSpec((tm, tn), lambda p, q: ij(p, q))
    else:
        out_shape = jax.ShapeDtypeStruct((out_stack, m, n), out_dtype)
        o_spec = pl.BlockSpec((None, tm, tn), lambda p, q: (out_l,) + ij(p, q))

    def body(a_ref, b_ref, *rest):
        o_ref = rest[-1]
        o_ref[...] = lax.dot_general(a_ref[...], b_ref[...], dims, preferred_element_type=F32).astype(o_ref.dtype)

    in_specs, args, aliases = [a_spec, b_spec], [a, _hbm(b) if b.ndim == 3 else b], {}
    if out_buf is not None:
        in_specs.append(ANY)
        args.append(_hbm(out_buf))
        aliases = {2: 0}
    return pl.pallas_call(body, name=name, grid=grid, in_specs=in_specs, out_specs=o_spec, out_shape=out_shape,
                          input_output_aliases=aliases, compiler_params=_params(("parallel", "parallel")))(*args)


def _rowwise(fn, name, rows, tile, row_ins, full_ins, row_outs, acc_outs=(), scratch=(), deps=()):
    assert rows % tile == 0, (name, rows, tile)
    in_specs, args = [], []
    for r in row_ins:
        if isinstance(r, tuple):
            arr, br, bc, imap = r
            in_specs.append(pl.BlockSpec((br, bc), imap))
        else:
            arr = r
            in_specs.append(pl.BlockSpec((tile, arr.shape[1]), lambda i: (i, 0)))
        args.append(arr)
    for f in full_ins:
        if isinstance(f, tuple):
            arr, blk, imap = f
            in_specs.append(pl.BlockSpec(blk, imap))
            args.append(_hbm(arr))
        else:
            in_specs.append(pl.BlockSpec(f.shape, functools.partial(lambda nd, i: (0,) * nd, f.ndim)))
            args.append(f)
    for d in deps:
        in_specs.append(ANY)
        args.append(_hbm(d))
    out_specs, out_shape = [], []
    for w, dt in row_outs:
        out_specs.append(pl.BlockSpec((tile, w), lambda i: (i, 0)))
        out_shape.append(jax.ShapeDtypeStruct((rows, w), dt))
    for shp, dt in acc_outs:
        out_specs.append(pl.BlockSpec(shp, functools.partial(lambda nd, i: (0,) * nd, len(shp))))
        out_shape.append(jax.ShapeDtypeStruct(shp, dt))
    n_in, n_out = len(args), len(out_shape)

    def body(*refs):
        fn(pl.program_id(0), refs[:n_in], refs[n_in:n_in + n_out], refs[n_in + n_out:])

    sem = ("arbitrary",) if acc_outs else ("parallel",)
    res = pl.pallas_call(body, name=name, grid=(rows // tile,), in_specs=in_specs, out_specs=out_specs, out_shape=out_shape,
                         scratch_shapes=list(scratch), compiler_params=_params(sem))(*args)
    return res


def _accum(ref, i, val):
    @pl.when(i == 0)
    def _():
        ref[...] = val

    @pl.when(i > 0)
    def _():
        ref[...] += val


def _prev_halo(arr, tile, cols=None):
    r = tile // HALO
    return (arr, HALO, cols or arr.shape[1], lambda i: (jnp.maximum(i * r - 1, 0), 0))


def _next_halo(arr, tile, cols=None):
    r = tile // HALO
    last = arr.shape[0] // HALO - 1
    return (arr, HALO, cols or arr.shape[1], lambda i: (jnp.minimum((i + 1) * r, last), 0))


def _sigmoid(x):
    return 1.0 / (1.0 + jnp.exp(-x))


def _rms_hat(x):
    r = lax.rsqrt(jnp.mean(x * x, axis=-1, keepdims=True) + EPS)
    return x * r, r


def _rms_bwd(x, g, dy):
    xhat, r = _rms_hat(x)
    dxhat = dy * g
    dx = r * (dxhat - xhat * jnp.mean(dxhat * xhat, axis=-1, keepdims=True))
    return dx, jnp.sum(dy * xhat, axis=0, keepdims=True)


def _ln_hat(x):
    mu = jnp.mean(x, axis=-1, keepdims=True)
    xc = x - mu
    r = lax.rsqrt(jnp.mean(xc * xc, axis=-1, keepdims=True) + EPS)
    return xc * r, r


def _ln_bwd(xhat, r, g, dy):
    dxhat = dy * g
    dx = r * (dxhat - jnp.mean(dxhat, axis=-1, keepdims=True) - xhat * jnp.mean(dxhat * xhat, axis=-1, keepdims=True))
    return dx, jnp.sum(dy * xhat, axis=0, keepdims=True), jnp.sum(dy, axis=0, keepdims=True)


def _silu_grad(x):
    s = _sigmoid(x)
    return s * (1.0 + x * (1.0 - s))


_SQRT_HALF = math.sqrt(0.5)
_INV_SQRT_2PI = 1.0 / math.sqrt(2.0 * math.pi)


def _gelu(x):
    return 0.5 * x * (1.0 + lax.erf(x * _SQRT_HALF))


def _gelu_grad(x):
    return 0.5 * (1.0 + lax.erf(x * _SQRT_HALF)) + x * jnp.exp(-0.5 * x * x) * _INV_SQRT_2PI


def _dot(a, b, kind="nn"):
    dims = {"nn": (((1,), (0,)), ((), ())), "nt": (((1,), (1,)), ((), ())), "tn": (((0,), (0,)), ((), ()))}[kind]
    return lax.dot_general(a, b, dims, preferred_element_type=F32)


def _prenorm(h, g, tile):
    def fn(i, ins, outs, _):
        h_ref, g_ref = ins
        outs[0][...] = (_rms_hat(h_ref[...])[0] * g_ref[...]).astype(BF16)

    return _rowwise(fn, "prenorm", h.shape[0], tile, [h], [g], [(D, BF16)])[0]


def _post(h, f, g_post, scale, g_next, tile):
    def fn(i, ins, outs, _):
        hn = ins[0][...] + scale * (_rms_hat(ins[1][...])[0] * ins[2][...])
        outs[0][...] = hn
        if g_next is not None:
            outs[1][...] = (_rms_hat(hn)[0] * ins[3][...]).astype(BF16)

    fulls = [g_post] + ([g_next] if g_next is not None else [])
    outs = [(D, F32)] + ([(D, BF16)] if g_next is not None else [])
    res = _rowwise(fn, "post", h.shape[0], tile, [h, f], fulls, outs)
    return res[0], (res[1] if g_next is not None else None)


def _loss_top(h, target, f, g_post, scale, tile):
    def fn(i, ins, outs, _):
        err = ins[0][...] - ins[1][...]
        per_row = jnp.mean(err * err, axis=-1, keepdims=True)
        _accum(outs[2], i, jnp.broadcast_to(0.5 * jnp.sum(per_row, axis=0, keepdims=True), (1, 128)))
        dh = err * (1.0 / D)
        outs[0][...] = dh
        df, dg = _rms_bwd(ins[2][...], ins[3][...], scale * dh)
        outs[1][...] = df.astype(BF16)
        _accum(outs[3], i, dg)

    return _rowwise(fn, "loss_top", h.shape[0], tile, [h, target, f], [g_post], [(D, F32), (D, BF16)],
                    [((1, 128), F32), ((1, D), F32)])


def _boundary_bwd(h, g_pre, dn, dh_in, f_prev, g_post_prev, scale_prev, tile, deps=()):
    has_prev = f_prev is not None

    def fn(i, ins, outs, _):
        dx, dg = _rms_bwd(ins[0][...], ins[3 + has_prev][...], ins[1][...])
        dh = ins[2][...] + dx
        outs[0][...] = dh
        _accum(outs[1 + has_prev], i, dg)
        if has_prev:
            df, dgp = _rms_bwd(ins[3][...], ins[5][...], scale_prev * dh)
            outs[1][...] = df.astype(BF16)
            _accum(outs[3], i, dgp)

    rows = [h, dn, dh_in] + ([f_prev] if has_prev else [])
    fulls = [g_pre] + ([g_post_prev] if has_prev else [])
    outs = [(D, F32)] + ([(D, BF16)] if has_prev else [])
    accs = [((1, D), F32)] + ([((1, D), F32)] if has_prev else [])
    return _rowwise(fn, "boundary_bwd", h.shape[0], tile, rows, fulls, outs, accs, deps=deps)


def _swiglu(gu, tile):
    def fn(i, ins, outs, _):
        g = ins[0][:, :DFF]
        outs[0][...] = (g * _sigmoid(g) * ins[0][:, DFF:]).astype(BF16)

    return _rowwise(fn, "swiglu", gu.shape[0], tile, [gu], [], [(DFF, BF16)])[0]


def _swiglu_bwd(da, gu, tile):
    def fn(i, ins, outs, _):
        da_v = ins[0][...]
        g = ins[1][:, :DFF]
        u = ins[1][:, DFF:]
        outs[0][:, :DFF] = (da_v * u * _silu_grad(g)).astype(BF16)
        outs[0][:, DFF:] = (da_v * g * _sigmoid(g)).astype(BF16)

    return _rowwise(fn, "swiglu_bwd", gu.shape[0], tile, [da, gu], [], [(2 * DFF, BF16)])[0]


def _softmax_rows(s):
    e = jnp.exp(s - jnp.max(s, axis=-1, keepdims=True))
    return e / jnp.sum(e, axis=-1, keepdims=True)


def _attn(q, kv, tile):
    def fn(i, ins, outs, _):
        q_ref, kv_ref = ins
        for hd in range(XA_HEADS):
            c0 = hd * XA_HD
            p = _softmax_rows(_dot(q_ref[:, c0:c0 + XA_HD], kv_ref[:, c0:c0 + XA_HD], "nt") * (XA_HD ** -0.5))
            outs[0][:, c0:c0 + XA_HD] = _dot(p.astype(BF16), kv_ref[:, D + c0:D + c0 + XA_HD]).astype(BF16)

    return _rowwise(fn, "attn", q.shape[0], tile, [q], [kv], [(D, BF16)])[0]


def _attn_bwd(q, kv, do, tile):
    def fn(i, ins, outs, _):
        q_ref, do_ref, kv_ref = ins
        for hd in range(XA_HEADS):
            c0 = hd * XA_HD
            qh, kh, vh = q_ref[:, c0:c0 + XA_HD], kv_ref[:, c0:c0 + XA_HD], kv_ref[:, D + c0:D + c0 + XA_HD]
            doh = do_ref[:, c0:c0 + XA_HD]
            p = _softmax_rows(_dot(qh, kh, "nt") * (XA_HD ** -0.5))
            dp = _dot(doh, vh, "nt")
            ds = (p * (dp - jnp.sum(dp * p, axis=-1, keepdims=True)) * (XA_HD ** -0.5)).astype(BF16)
            outs[0][:, c0:c0 + XA_HD] = _dot(ds, kh).astype(BF16)
            dk = _dot(ds, qh, "tn")
            dv = _dot(p.astype(BF16), doh, "tn")

            @pl.when(i == 0)
            def _():
                outs[1][:, c0:c0 + XA_HD] = dk
                outs[1][:, D + c0:D + c0 + XA_HD] = dv

            @pl.when(i > 0)
            def _():
                outs[1][:, c0:c0 + XA_HD] += dk
                outs[1][:, D + c0:D + c0 + XA_HD] += dv

    return _rowwise(fn, "attn_bwd", q.shape[0], tile, [q, do], [kv], [(D, BF16)], [((NMEM, 2 * D), F32)])


def _mem_norm(mem, g):
    def fn(i, ins, outs, _):
        outs[0][...] = (_rms_hat(ins[0][...])[0] * ins[1][...]).astype(BF16)

    return _rowwise(fn, "mem_norm", NMEM, NMEM, [mem], [g], [(D, BF16)])[0]


def _mem_gain_bwd(mem, dmn):
    def fn(i, ins, outs, _):
        outs[0][...] = jnp.sum(ins[1][...] * _rms_hat(ins[0][...])[0], axis=0, keepdims=True)

    return _rowwise(fn, "mem_gain_bwd", NMEM, NMEM, [mem, dmn], [], [], [((1, D), F32)])[0]


def _conv_taps(w_ref, zext_ref, tile, shift0):
    acc = None
    for k in range(CONV_W):
        term = w_ref[k:k + 1, :] * zext_ref[pl.ds(shift0(k), tile), :]
        acc = term if acc is None else acc + term
    return acc


def _tril_mask(transpose=False):
    r, c = (lax.broadcasted_iota(jnp.int32, (CHUNK, CHUNK), a) for a in (0, 1))
    return (r <= c) if transpose else (r >= c)


def _even_core(p, cw, cb, clg, clb, slg, slb, ws, bst, tile):
    def fn(i, ins, outs, scr):
        p_ref, ph_ref, cw_ref, cb_ref, clg_ref, clb_ref, slg_ref, slb_ref, ws_ref, bst_ref = ins
        y_ref, z_ref = outs
        zext = scr[0]
        z = p_ref[:, :DC] * _sigmoid(p_ref[:, DC:2 * DC])
        zh = ph_ref[:, :DC] * _sigmoid(ph_ref[:, DC:2 * DC])
        zext[:HALO, :] = jnp.where(i > 0, zh, 0.0)
        zext[HALO:, :] = z
        z_ref[...] = z
        conv = _conv_taps(cw_ref, zext, tile, lambda k: HALO - (CONV_W - 1) + k) + cb_ref[...]
        yl = _ln_hat(conv)[0] * clg_ref[...] + clb_ref[...]
        y_ref[:, :DC] = (yl * _sigmoid(yl)).astype(BF16)
        zb = _gelu(p_ref[:, 2 * DC:])
        vln = (_ln_hat(zb[:, DC:])[0] * slg_ref[...] + slb_ref[...]).astype(BF16)
        mask = _tril_mask()
        for hd in range(NHEAD_SGU):
            wm = jnp.where(mask, ws_ref[hd], 0.0).astype(BF16)
            for ci in range(tile // CHUNK):
                r0, c0 = ci * CHUNK, hd * CHUNK
                mixed = _dot(wm, vln[r0:r0 + CHUNK, c0:c0 + CHUNK]) + bst_ref[:, hd:hd + 1]
                y_ref[r0:r0 + CHUNK, DC + c0:DC + c0 + CHUNK] = (zb[r0:r0 + CHUNK, c0:c0 + CHUNK] * mixed).astype(BF16)

    return _rowwise(fn, "even_core", p.shape[0], tile, [p, _prev_halo(p, tile, D)], [cw, cb, clg, clb, slg, slb, ws, bst],
                    [(D, BF16), (DC, F32)], scratch=[pltpu.VMEM((tile + HALO, DC), F32)])


def _even_bwd_a(p, z, dy, cw, cb, clg, clb, slg, slb, ws, wst, bst, tile):
    def fn(i, ins, outs, scr):
        p_ref, z_ref, zh_ref, dy_ref, cw_ref, cb_ref, clg_ref, clb_ref, slg_ref, slb_ref, ws_ref, wst_ref, bst_ref = ins
        dc_ref, dpb_ref, vec_ref, dws_ref, dbst_ref = outs
        zext, dvln_s = scr
        zext[:HALO, :] = jnp.where(i > 0, zh_ref[...], 0.0)
        zext[HALO:, :] = z_ref[...]
        conv = _conv_taps(cw_ref, zext, tile, lambda k: HALO - (CONV_W - 1) + k) + cb_ref[...]
        chat, cr = _ln_hat(conv)
        yl = chat * clg_ref[...] + clb_ref[...]
        dyl = dy_ref[:, :DC] * _silu_grad(yl)
        dconv, dclg, dclb = _ln_bwd(chat, cr, clg_ref[...], dyl)
        dc_ref[...] = dconv
        xb = p_ref[:, 2 * DC:]
        zb = _gelu(xb)
        vhat, vr = _ln_hat(zb[:, DC:])
        vln = (vhat * slg_ref[...] + slb_ref[...]).astype(BF16)
        dyb = dy_ref[:, DC:]
        mask, mask_t = _tril_mask(), _tril_mask(transpose=True)
        dbu_parts = []
        for hd in range(NHEAD_SGU):
            wm = jnp.where(mask, ws_ref[hd], 0.0).astype(BF16)
            wt = jnp.where(mask_t, wst_ref[hd], 0.0).astype(BF16)
            dws_h, dbs_h, rows = None, None, []
            for ci in range(tile // CHUNK):
                r0, c0 = ci * CHUNK, hd * CHUNK
                vblk = vln[r0:r0 + CHUNK, c0:c0 + CHUNK]
                mixed = _dot(wm, vblk) + bst_ref[:, hd:hd + 1]
                dyb_blk = dyb[r0:r0 + CHUNK, c0:c0 + CHUNK]
                rows.append(dyb_blk * mixed)
                dmixed = dyb_blk * zb[r0:r0 + CHUNK, c0:c0 + CHUNK]
                dmb = dmixed.astype(BF16)
                dvln_s[r0:r0 + CHUNK, c0:c0 + CHUNK] = _dot(wt, dmb)
                dw = _dot(dmb, vblk, "nt")
                db = jnp.sum(dmixed, axis=-1, keepdims=True)
                dws_h = dw if dws_h is None else dws_h + dw
                dbs_h = db if dbs_h is None else dbs_h + db
            dbu_parts.append(jnp.concatenate(rows, axis=0))
            dws_h = jnp.where(mask, dws_h, 0.0)

            @pl.when(i == 0)
            def _():
                dws_ref[hd] = dws_h
                dbst_ref[:, hd:hd + 1] = dbs_h

            @pl.when(i > 0)
            def _():
                dws_ref[hd] += dws_h
                dbst_ref[:, hd:hd + 1] += dbs_h

        dbu = jnp.concatenate(dbu_parts, axis=1)
        dbv, dslg, dslb = _ln_bwd(vhat, vr, slg_ref[...], dvln_s[...])
        dpb_ref[:, :DC] = (dbu * _gelu_grad(xb[:, :DC])).astype(BF16)
        dpb_ref[:, DC:] = (dbv * _gelu_grad(xb[:, DC:])).astype(BF16)
        @pl.when(i == 0)
        def _():
            vec_ref[...] = jnp.zeros_like(vec_ref)

        for r, val in enumerate([jnp.sum(dconv, axis=0, keepdims=True), dclg, dclb, dslg, dslb]):
            vec_ref[r:r + 1, :] += val

    return _rowwise(fn, "even_bwd_a", p.shape[0], tile, [p, z, _prev_halo(z, tile), dy],
                    [cw, cb, clg, clb, slg, slb, ws, wst, bst], [(DC, F32), (D, BF16)],
                    [((8, DC), F32), ((NHEAD_SGU, CHUNK, CHUNK), F32), ((CHUNK, NHEAD_SGU), F32)],
                    scratch=[pltpu.VMEM((tile + HALO, DC), F32), pltpu.VMEM((tile, DC), F32)])


def _even_bwd_b(p, z, dconv, dpb, cw, tile):
    def fn(i, ins, outs, scr):
        p_ref, z_ref, zh_ref, dc_ref, dcn_ref, dpb_ref, cw_ref = ins
        dp_ref, dcw_ref = outs
        zext, dcext = scr
        last = pl.num_programs(0) - 1
        zext[:HALO, :] = jnp.where(i > 0, zh_ref[...], 0.0)
        zext[HALO:, :] = z_ref[...]
        dcext[:tile, :] = dc_ref[...]
        dcext[tile:, :] = jnp.where(i < last, dcn_ref[...], 0.0)
        dz = _conv_taps(cw_ref, dcext, tile, lambda k: (CONV_W - 1) - k)
        gate = p_ref[:, DC:2 * DC]
        s = _sigmoid(gate)
        dp_ref[:, :DC] = (dz * s).astype(BF16)
        dp_ref[:, DC:2 * DC] = (dz * p_ref[:, :DC] * s * (1.0 - s)).astype(BF16)
        dp_ref[:, 2 * DC:] = dpb_ref[...]
        dcv = dc_ref[...]

        @pl.when(i == 0)
        def _():
            dcw_ref[...] = jnp.zeros_like(dcw_ref)

        for k in range(CONV_W):
            dcw_ref[k:k + 1, :] += jnp.sum(dcv * zext[pl.ds(HALO - (CONV_W - 1) + k, tile), :], axis=0, keepdims=True)

    return _rowwise(fn, "even_bwd_b", p.shape[0], tile,
                    [(p, tile, D, lambda i: (i, 0)), z, _prev_halo(z, tile), dconv, _next_halo(dconv, tile), dpb], [cw],
                    [(2 * D, BF16)], [((HALO, DC), F32)],
                    scratch=[pltpu.VMEM((tile + HALO, DC), F32), pltpu.VMEM((tile + HALO, DC), F32)])


def _row_count(i, tile, nrows, offset, w):
    t = i * tile + offset + lax.broadcasted_iota(jnp.int32, (nrows, POOL_GD), 0)
    return jnp.minimum(t + 1, w).astype(F32)


def _odd_core(p, wg, scale, tile):
    def fn(i, ins, outs, scr):
        p_ref, ph_ref, wg_ref, sc_ref = ins
        d_ref, e_ref, es_ref = outs
        pext = scr[0]
        pext[:HALO, :] = jnp.where(i > 0, ph_ref[...], 0.0)
        pext[HALO:, :] = p_ref[...]
        for g, w in enumerate(POOL_WINDOWS):
            c0 = g * POOL_GD
            s = p_ref[:, c0:c0 + POOL_GD]
            for r in range(1, w):
                s = s + pext[pl.ds(HALO - r, tile), c0:c0 + POOL_GD]
            dg = (s / _row_count(i, tile, tile, 0, w) - p_ref[:, c0:c0 + POOL_GD]).astype(BF16)
            d_ref[:, c0:c0 + POOL_GD] = dg
            e = _dot(dg, wg_ref[g])
            e_ref[:, c0:c0 + POOL_GD] = e
            es_ref[:, c0:c0 + POOL_GD] = (e * sc_ref[:, c0:c0 + POOL_GD]).astype(BF16)

    return _rowwise(fn, "odd_core", p.shape[0], tile, [p, _prev_halo(p, tile)], [wg, scale],
                    [(D, BF16), (D, F32), (D, BF16)], scratch=[pltpu.VMEM((tile + HALO, D), F32)])


def _odd_bwd(des, e, d, wg, scale, tile):
    def fn(i, ins, outs, scr):
        des_ref, desn_ref, e_ref, d_ref, wg_ref, sc_ref = ins
        dp_ref, dsc_ref, dwg_ref = outs
        qext = scr[0]
        last = pl.num_programs(0) - 1
        desv = des_ref[...]
        _accum(dsc_ref, i, jnp.sum(desv * e_ref[...], axis=0, keepdims=True))
        de = (desv * sc_ref[...]).astype(BF16)
        den = (jnp.where(i < last, desn_ref[...], 0.0) * sc_ref[...]).astype(BF16)
        for g, w in enumerate(POOL_WINDOWS):
            c0 = g * POOL_GD
            deg = de[:, c0:c0 + POOL_GD]
            dd = _dot(deg, wg_ref[g], "nt")
            ddn = _dot(den[:, c0:c0 + POOL_GD], wg_ref[g], "nt")
            qext[:tile, c0:c0 + POOL_GD] = dd / _row_count(i, tile, tile, 0, w)
            qext[tile:, c0:c0 + POOL_GD] = ddn / _row_count(i, tile, HALO, tile, w)
            s = qext[:tile, c0:c0 + POOL_GD]
            for r in range(1, w):
                s = s + qext[pl.ds(r, tile), c0:c0 + POOL_GD]
            dp_ref[:, c0:c0 + POOL_GD] = (s - dd).astype(BF16)
            dw = _dot(d_ref[:, c0:c0 + POOL_GD], deg, "tn")

            @pl.when(i == 0)
            def _():
                dwg_ref[g] = dw

            @pl.when(i > 0)
            def _():
                dwg_ref[g] += dw

    return _rowwise(fn, "odd_bwd", des.shape[0], tile, [des, _next_halo(des, tile), e, d], [wg, scale], [(D, BF16)],
                    [((1, D), F32), ((len(POOL_WINDOWS), POOL_GD, POOL_GD), F32)],
                    scratch=[pltpu.VMEM((tile + HALO, D), F32)])


def _axis_slice(ref, axis, start, size):
    idx = [slice(None)] * len(ref.shape)
    idx[axis] = pl.ds(start, size)
    return ref.at[tuple(idx)]


def _cast_into_slot(pos, shard, kind, name):
    L, rs, cs = shard.shape
    tr = _row_tile(rs, 512)
    nr = rs // tr
    if kind == "row":
        full, omap = (L, NCHIP * rs, cs), (lambda l, i, p: (l, p[0] * nr + i, 0))
    else:
        full, omap = (L, rs, NCHIP * cs), (lambda l, i, p: (l, i, p[0]))

    def body(p_ref, s_ref, o_ref):
        o_ref[...] = s_ref[...].astype(BF16)

    grid_spec = pltpu.PrefetchScalarGridSpec(
        num_scalar_prefetch=1, grid=(L, nr), in_specs=[pl.BlockSpec((None, tr, cs), lambda l, i, p: (l, i, 0))],
        out_specs=pl.BlockSpec((None, tr, cs), omap))
    return pl.pallas_call(body, name=name, grid_spec=grid_spec, out_shape=jax.ShapeDtypeStruct(full, BF16),
                          compiler_params=_params(("parallel", "parallel")))(pos, shard)


def _peer(x, y, c, k):
    return (1 - x if k & 4 else x, 1 - y if k & 2 else y, 1 - c if k & 1 else c)


def _half_geometry(shape, kind):
    L, R, C = shape
    return (L, R // (2 * NCHIP), C) if kind == "row" else (L, R // 2, C // NCHIP)


def _sum_into_half(pos, recv, grad, kind, name):
    _, L, rh, cs = recv.shape
    if kind == "row":
        gmap = lambda l, p: (l, 2 * p[0] + p[1], 0)
    else:
        gmap = lambda l, p: (l, p[1], p[0])

    def body(p_ref, r_ref, g_ref, o_ref):
        acc = g_ref[...].astype(F32)
        for s in range(7):
            acc = acc + r_ref[s].astype(F32)
        o_ref[...] = acc

    grid_spec = pltpu.PrefetchScalarGridSpec(
        num_scalar_prefetch=1, grid=(L,),
        in_specs=[pl.BlockSpec((7, None, rh, cs), lambda l, p: (0, l, 0, 0)), pl.BlockSpec((None, rh, cs), gmap)],
        out_specs=pl.BlockSpec((None, rh, cs), lambda l, p: (l, p[1], 0)))
    return pl.pallas_call(body, name=name, grid_spec=grid_spec, out_shape=jax.ShapeDtypeStruct((L, 2 * rh, cs), F32),
                          compiler_params=_params(("parallel",)))(pos, _hbm(recv), _hbm(grad))


def _swap_halves(shards):
    n = len(shards)

    def body(*refs):
        outs = refs[n:2 * n]
        send_sems, recv_sems = refs[2 * n:]
        x, y, c = lax.axis_index("x"), lax.axis_index("y"), lax.axis_index("c")
        copies = []
        for t in range(n):
            rh = outs[t].shape[1] // 2
            half = outs[t].at[:, pl.ds(c * rh, rh), :]
            cp = pltpu.make_async_remote_copy(src_ref=half, dst_ref=half, send_sem=send_sems.at[t], recv_sem=recv_sems.at[t],
                                              device_id=(x, y, 1 - c), device_id_type=MESH)
            cp.start()
            copies.append(cp)
        for cp in copies:
            cp.wait()

    return pl.pallas_call(body, name="swap_halves", in_specs=[HBM] * n, out_specs=[HBM] * n,
                          out_shape=[pltpu.HBM(s.shape, s.dtype) for s in shards],
                          input_output_aliases={t: t for t in range(n)},
                          scratch_shapes=[pltpu.SemaphoreType.DMA((n,)), pltpu.SemaphoreType.DMA((n,))])(*map(_hbm, shards))


SEM = pl.BlockSpec(memory_space=pltpu.SEMAPHORE)
DATAFLOW = pltpu.SideEffectType.DATAFLOW_SIDE_EFFECTING
MAX_LAYERS = 4


def _gsem(t, layer, k):
    return (t * MAX_LAYERS + layer) * 3 + k


def _slot_of(ref, kind, d0, nd0, mine):
    if kind == "row":
        size = ref.shape[1] // NCHIP
        return ref.at[pl.ds(d0, nd0), pl.ds(mine * size, size), :]
    size = ref.shape[2] // NCHIP
    return ref.at[pl.ds(d0, nd0), :, pl.ds(mine * size, size)]


def _gather_start(bufs, kinds, groups):
    n = len(bufs)

    def body(*refs):
        send_sems, recv_sems = refs[n], refs[n + 1]
        outs = refs[n + 2:]
        x, y, c = lax.axis_index("x"), lax.axis_index("y"), lax.axis_index("c")
        mine = 2 * x + y
        chips = [(1 - x, y), (x, 1 - y), (1 - x, 1 - y)]
        for group in groups:
            for t, li, d0, nd0 in group:
                slot = _slot_of(outs[t], kinds[t], d0, nd0, mine)
                for k, chip in enumerate(chips):
                    pltpu.make_async_remote_copy(src_ref=slot, dst_ref=slot, send_sem=send_sems.at[_gsem(t, li, k)],
                                                 recv_sem=recv_sems.at[_gsem(t, li, k)], device_id=(*chip, c),
                                                 device_id_type=MESH).start()

    sems = pltpu.SemaphoreType.DMA((n * MAX_LAYERS * 3,))
    res = pl.pallas_call(body, name="gather_start", in_specs=[HBM] * n, out_specs=[SEM, SEM] + [HBM] * n,
                         out_shape=[sems, sems] + [pltpu.HBM(b.shape, b.dtype) for b in bufs],
                         input_output_aliases={t: t + 2 for t in range(n)},
                         compiler_params=pltpu.CompilerParams(has_side_effects=DATAFLOW))(*map(_hbm, bufs))
    return res[0], res[1], list(res[2:])


def _gather_wait(bufs, kinds, group, send_sems, recv_sems, after, name):
    m = len(bufs)

    def body(*refs):
        send, recv = refs[m], refs[m + 1]
        outs = refs[m + 3:]
        x, y, c = lax.axis_index("x"), lax.axis_index("y"), lax.axis_index("c")
        mine = 2 * x + y
        chips = [(1 - x, y), (x, 1 - y), (1 - x, 1 - y)]
        for j, (t, li, d0, nd0) in enumerate(group):
            slot = _slot_of(outs[j], kinds[j], d0, nd0, mine)
            for k, chip in enumerate(chips):
                cp = pltpu.make_async_remote_copy(src_ref=slot, dst_ref=slot, send_sem=send.at[_gsem(t, li, k)],
                                                  recv_sem=recv.at[_gsem(t, li, k)], device_id=(*chip, c), device_id_type=MESH)
                cp.wait_send()
                cp.wait_recv()

    res = pl.pallas_call(body, name=name, in_specs=[HBM] * m + [SEM, SEM, ANY], out_specs=[HBM] * m,
                         out_shape=[pltpu.HBM(b.shape, b.dtype) for b in bufs],
                         input_output_aliases={j: j for j in range(m)},
                         compiler_params=pltpu.CompilerParams(has_side_effects=DATAFLOW))(*map(_hbm, bufs), send_sems, recv_sems,
                                                                                          after)
    return list(res)


def _gather_tiny(tiny, axes):
    n = len(tiny)
    out_shape = []
    for s_, ax in zip(tiny, axes):
        shp = list(s_.shape)
        shp[ax] *= NCHIP
        out_shape.append(jax.ShapeDtypeStruct(tuple(shp), s_.dtype))

    def body(*refs):
        ins, outs = refs[:n], refs[n:2 * n]
        send_sems, recv_sems, loc_sems = refs[2 * n:]
        x, y, c = lax.axis_index("x"), lax.axis_index("y"), lax.axis_index("c")
        mine = 2 * x + y
        chips = [(1 - x, y), (x, 1 - y), (1 - x, 1 - y)]
        copies = []
        for t in range(n):
            size = ins[t].shape[axes[t]]
            dst = _axis_slice(outs[t], axes[t], mine * size, size)
            copies.append(pltpu.make_async_copy(ins[t], dst, loc_sems.at[t]))
            for k, chip in enumerate(chips):
                copies.append(pltpu.make_async_remote_copy(src_ref=ins[t], dst_ref=dst, send_sem=send_sems.at[t, k],
                                                           recv_sem=recv_sems.at[t, k], device_id=(*chip, c),
                                                           device_id_type=MESH))
        for cp in copies:
            cp.start()
        for cp in copies:
            cp.wait()

    return pl.pallas_call(body, name="gather_tiny", in_specs=[ANY] * n, out_specs=[ANY] * n, out_shape=out_shape,
                          scratch_shapes=[pltpu.SemaphoreType.DMA((n, 3)), pltpu.SemaphoreType.DMA((n, 3)),
                                          pltpu.SemaphoreType.DMA((n,))])(*tiny)


def _grad_slice(ref, kind, d0, nd0, chip, core, rh, cs):
    if kind == "row":
        return ref.at[pl.ds(d0, nd0), pl.ds((2 * chip + core) * rh, rh), :]
    return ref.at[pl.ds(d0, nd0), pl.ds(core * rh, rh), pl.ds(chip * cs, cs)]


def _scatter_start(grads, recv, kinds, items, name):
    m = len(grads)

    def body(*refs):
        send_sems, recv_sems = refs[2 * m], refs[2 * m + 1]
        gout, rout = refs[2 * m + 2:3 * m + 2], refs[3 * m + 2:]
        x, y, c = lax.axis_index("x"), lax.axis_index("y"), lax.axis_index("c")
        for t in range(m):
            _, _, rh, cs = rout[t].shape
            d0, nd0 = items[t]
            for k in range(1, 8):
                px, py, pc = _peer(x, y, c, k)
                src = _grad_slice(gout[t], kinds[t], d0, nd0, 2 * px + py, pc, rh, cs)
                pltpu.make_async_remote_copy(src_ref=src, dst_ref=rout[t].at[k - 1, pl.ds(d0, nd0)],
                                             send_sem=send_sems.at[7 * t + k - 1], recv_sem=recv_sems.at[7 * t + k - 1],
                                             device_id=(px, py, pc), device_id_type=MESH).start()

    sems = pltpu.SemaphoreType.DMA((7 * m,))
    res = pl.pallas_call(body, name=name, in_specs=[HBM] * (2 * m), out_specs=[SEM, SEM] + [HBM] * (2 * m),
                         out_shape=[sems, sems] + [pltpu.HBM(a.shape, a.dtype) for a in list(grads) + list(recv)],
                         input_output_aliases={j: j + 2 for j in range(2 * m)},
                         compiler_params=pltpu.CompilerParams(has_side_effects=DATAFLOW))(*map(_hbm, grads), *map(_hbm, recv))
    return res[0], res[1], list(res[2:m + 2]), list(res[m + 2:])


def _scatter_wait(recv, items, send_sems, recv_sems, name):
    m = len(recv)

    def body(*refs):
        send, rcv = refs[m], refs[m + 1]
        outs = refs[m + 2:]
        x, y, c = lax.axis_index("x"), lax.axis_index("y"), lax.axis_index("c")
        for t in range(m):
            d0, nd0 = items[t]
            for k in range(1, 8):
                land = outs[t].at[k - 1, pl.ds(d0, nd0)]
                cp = pltpu.make_async_remote_copy(src_ref=land, dst_ref=land, send_sem=send.at[7 * t + k - 1],
                                                  recv_sem=rcv.at[7 * t + k - 1], device_id=_peer(x, y, c, k), device_id_type=MESH)
                cp.wait_send()
                cp.wait_recv()

    res = pl.pallas_call(body, name=name, in_specs=[HBM] * m + [SEM, SEM], out_specs=[HBM] * m,
                         out_shape=[pltpu.HBM(a.shape, a.dtype) for a in recv],
                         input_output_aliases={j: j for j in range(m)},
                         compiler_params=pltpu.CompilerParams(has_side_effects=DATAFLOW))(*map(_hbm, recv), send_sems, recv_sems)
    return list(res)


def _share_small(packed, after):
    rows = packed.shape[0]

    def body(in_ref, after_ref, out_ref, send_sems, recv_sems, loc_sem):
        x, y, c = lax.axis_index("x"), lax.axis_index("y"), lax.axis_index("c")
        me = 4 * x + 2 * y + c
        copies = [pltpu.make_async_copy(in_ref, out_ref.at[me], loc_sem)]
        for k in range(1, 8):
            copies.append(pltpu.make_async_remote_copy(src_ref=in_ref, dst_ref=out_ref.at[me], send_sem=send_sems.at[k - 1],
                                                       recv_sem=recv_sems.at[k - 1], device_id=_peer(x, y, c, k),
                                                       device_id_type=MESH))
        for cp in copies:
            cp.start()
        for cp in copies:
            cp.wait()

    return pl.pallas_call(body, name="share_small", in_specs=[ANY, ANY], out_specs=ANY,
                          out_shape=jax.ShapeDtypeStruct((8, rows, 128), F32),
                          scratch_shapes=[pltpu.SemaphoreType.DMA((7,)), pltpu.SemaphoreType.DMA((7,)),
                                          pltpu.SemaphoreType.DMA])(packed, _hbm(after))


def _sum_slots(buf, tile, name):
    _, rows, cols = buf.shape

    def body(b_ref, o_ref):
        acc = b_ref[0].astype(F32)
        for s in range(1, 8):
            acc = acc + b_ref[s].astype(F32)
        o_ref[...] = acc

    return pl.pallas_call(body, name=name, grid=(rows // tile,), in_specs=[pl.BlockSpec((8, tile, cols), lambda i: (0, i, 0))],
                          out_specs=pl.BlockSpec((tile, cols), lambda i: (i, 0)),
                          out_shape=jax.ShapeDtypeStruct((rows, cols), F32), compiler_params=_params(("parallel",)))(buf)


def _adamw(w, g, m, v, tile, name):
    rows, cols = w.shape

    def body(w_ref, g_ref, m_ref, v_ref, d_ref, mo_ref, vo_ref):
        gv = g_ref[...]
        mn = ADAM_B1 * m_ref[...] + (1.0 - ADAM_B1) * gv
        vn = ADAM_B2 * v_ref[...] + (1.0 - ADAM_B2) * (gv * gv)
        m_hat = mn / (1.0 - ADAM_B1 ** ADAM_STEP)
        v_hat = vn / (1.0 - ADAM_B2 ** ADAM_STEP)
        d_ref[...] = -ADAM_LR * (m_hat / (jnp.sqrt(v_hat) + ADAM_EPS) + ADAM_WD * w_ref[...])
        mo_ref[...] = mn
        vo_ref[...] = vn

    spec = pl.BlockSpec((tile, cols), lambda i: (i, 0))
    sds = jax.ShapeDtypeStruct((rows, cols), F32)
    return pl.pallas_call(body, name=name, grid=(rows // tile,), in_specs=[spec] * 4, out_specs=[spec] * 3,
                          out_shape=[sds] * 3, compiler_params=_params(("parallel",)))(w, g, m, v)


def _row_tile(rows, cap):
    best = 8
    for t in range(8, min(rows, cap) + 1, 8):
        if rows % t == 0:
            best = t
    return best


def _pack(arrs):
    parts = []
    for a in arrs:
        r = a.size // 128
        r8 = -(-r // 8) * 8
        parts.append(jnp.pad(a.reshape(r, 128).astype(F32), ((0, r8 - r), (0, 0))))
    return jnp.concatenate(parts, axis=0)


def _unpack(packed, shapes):
    out, r0 = [], 0
    for shp in shapes:
        r = math.prod(shp) // 128
        out.append(packed[r0:r0 + r].reshape(shp))
        r0 += -(-r // 8) * 8
    return out


BIG = {
    "ffn1_w_gu": "col", "ffn1_w_down": "row", "ev_w_in": "col", "ev_w_out": "row", "od_w_in": "row", "od_w_group": "row",
    "od_w_out": "row", "xa_w_q": "row", "xa_w_kv": "col", "xa_w_o": "row", "ffn2_w_gu": "col", "ffn2_w_down": "row",
}
TINY_SHARDED = {"ev_conv_w": 2, "od_scale": 1}
WEIGHTS = ['ffn1_pre_g', 'ffn1_w_gu', 'ffn1_w_down', 'ffn1_post_g', 'mix_pre_g', 'mix_post_g', 'ev_w_in', 'ev_conv_w',
           'ev_conv_b', 'ev_conv_ln_g', 'ev_conv_ln_b', 'ev_sgu_ln_g', 'ev_sgu_ln_b', 'ev_sgu_w', 'ev_sgu_b', 'ev_w_out',
           'od_w_in', 'od_w_group', 'od_scale', 'od_w_out', 'xa_pre_g', 'xa_mem_g', 'xa_w_q', 'xa_w_kv', 'xa_w_o', 'xa_post_g',
           'ffn2_pre_g', 'ffn2_w_gu', 'ffn2_w_down', 'ffn2_post_g']


def _as3d(a):
    return a.reshape((-1,) + a.shape[-2:]) if a.ndim == 4 else a


class _WeightView:
    def __init__(self, store, view):
        self.store, self.view = store, view

    def __getitem__(self, k):
        return self.view(k)


class _Grads:
    def __init__(self):
        self.buf = {}
        self.fresh = []

    def add(self, name, layer, nlayers, a, b, tm, tn, n_outer=True):
        self.buf[name] = _matmul(a, b, "tn", BF16, "dw_" + name, tm, tn, out_l=layer, out_stack=nlayers,
                                 out_buf=self.buf.get(name), n_outer=n_outer)
        self.fresh.append(name)

    def drain(self):
        names, self.fresh = self.fresh, []
        return [self.buf[k] for k in names]


def _local_step(x, mem, target, W, S, tiles=None, before_sub=None, after_layer_bwd=None):
    T, TW, tm = tiles or (min(512, S), min(256, S), min(512, S))
    row = lambda v: v.reshape(1, -1)
    subs = []
    small = {k: [None] * W[k].shape[0] for k in WEIGHTS if k not in BIG}
    g = _Grads()

    def ffn_fwd(tag, l, h, n, g_next):
        gu = _matmul(n, W[tag + "_w_gu"], "nn", F32, tag + "_up", tm, 1408, b_l=l)
        a = _swiglu(gu, TW)
        f = _matmul(a, W[tag + "_w_down"], "nn", F32, tag + "_down", tm, D, b_l=l)
        h2, n2 = _post(h, f, row(W[tag + "_post_g"][l]), 0.5, g_next, T)
        subs.append(dict(kind="ffn", tag=tag, l=l, h=h, n=n, gu=gu, a=a, f=f, scale=0.5, pre=tag + "_pre_g", post=tag + "_post_g"))
        return h2, n2

    def ffn_bwd(s, df):
        tag, l = s["tag"], s["l"]
        da = _matmul(df, W[tag + "_w_down"], "nt", F32, tag + "_down_dx", tm, 1408, b_l=l)
        g.add(tag + "_w_down", l, 4, s["a"], df, 256, D, n_outer=False)
        dgu = _swiglu_bwd(da, s["gu"], TW)
        dn = _matmul(dgu, W[tag + "_w_gu"], "nt", F32, tag + "_up_dx", tm, D, b_l=l)
        g.add(tag + "_w_gu", l, 4, s["n"], dgu, D, 512, n_outer=False)
        return dn

    def xa_fwd(l, h, n, g_next):
        q = _matmul(n, W["xa_w_q"], "nn", BF16, "xa_q", tm, D, b_l=l)
        mn = _mem_norm(mem, row(W["xa_mem_g"][l]))
        kv = _matmul(mn, W["xa_w_kv"], "nn", BF16, "xa_kv", NMEM, D, b_l=l)
        o = _attn(q, kv, T)
        cx = _matmul(o, W["xa_w_o"], "nn", F32, "xa_o", tm, D, b_l=l)
        h2, n2 = _post(h, cx, row(W["xa_post_g"][l]), 1.0, g_next, T)
        subs.append(dict(kind="xa", l=l, h=h, n=n, q=q, mn=mn, kv=kv, o=o, f=cx, scale=1.0, pre="xa_pre_g", post="xa_post_g"))
        return h2, n2

    def xa_bwd(s, dc):
        l = s["l"]
        do = _matmul(dc, W["xa_w_o"], "nt", BF16, "xa_o_dx", tm, D, b_l=l)
        g.add("xa_w_o", l, 4, s["o"], dc, D, 512)
        dq, dkv = _attn_bwd(s["q"], s["kv"], do, T)
        dn = _matmul(dq, W["xa_w_q"], "nt", F32, "xa_q_dx", tm, D, b_l=l)
        g.add("xa_w_q", l, 4, s["n"], dq, D, 512)
        dkvb = dkv.astype(BF16)
        g.add("xa_w_kv", l, 4, s["mn"], dkvb, D, 512)
        dmn = _matmul(dkvb, W["xa_w_kv"], "nt", F32, "xa_kv_dx", NMEM, D, b_l=l)
        small["xa_mem_g"][l] = _mem_gain_bwd(mem, dmn)[0]
        return dn

    def even_params(e):
        return (W["ev_conv_w"][e], row(W["ev_conv_b"][e]), row(W["ev_conv_ln_g"][e]), row(W["ev_conv_ln_b"][e]),
                row(W["ev_sgu_ln_g"][e]), row(W["ev_sgu_ln_b"][e]), W["ev_sgu_w"][e])

    def even_fwd(l, h, n, g_next):
        e = l // 2
        p = _matmul(n, W["ev_w_in"], "nn", F32, "ev_in", tm, D, b_l=e)
        cw, cb, clg, clb, slg, slb, ws = even_params(e)
        y, z = _even_core(p, cw, cb, clg, clb, slg, slb, ws, W["ev_sgu_b"][e].T, TW)
        m = _matmul(y, W["ev_w_out"], "nn", F32, "ev_out", tm, D, b_l=e)
        h2, n2 = _post(h, m, row(W["mix_post_g"][l]), 1.0, g_next, T)
        subs.append(dict(kind="even", l=l, h=h, n=n, p=p, y=y, z=z, f=m, scale=1.0, pre="mix_pre_g", post="mix_post_g"))
        return h2, n2

    def even_bwd(s, dm):
        l = s["l"]
        e = l // 2
        dy = _matmul(dm, W["ev_w_out"], "nt", F32, "ev_out_dx", tm, D, b_l=e)
        g.add("ev_w_out", e, 2, s["y"], dm, D, 512)
        cw, cb, clg, clb, slg, slb, ws = even_params(e)
        dconv, dpb, vecs, dws, dbst = _even_bwd_a(s["p"], s["z"], dy, cw, cb, clg, clb, slg, slb, ws,
                                                   jnp.swapaxes(ws, 1, 2), W["ev_sgu_b"][e].T, TW)
        dp, dcw = _even_bwd_b(s["p"], s["z"], dconv, dpb, cw, TW)
        for r, name in enumerate(["ev_conv_b", "ev_conv_ln_g", "ev_conv_ln_b", "ev_sgu_ln_g", "ev_sgu_ln_b"]):
            small[name][e] = vecs[r]
        small["ev_sgu_w"][e] = dws
        small["ev_sgu_b"][e] = dbst.T
        small["ev_conv_w"][e] = dcw[:CONV_W]
        dn = _matmul(dp, W["ev_w_in"], "nt", F32, "ev_in_dx", tm, D, b_l=e)
        g.add("ev_w_in", e, 2, s["n"], dp, D, 512)
        return dn

    def group_w(o):
        ng = len(POOL_WINDOWS)
        return (W["od_w_group"].reshape(-1, POOL_GD, POOL_GD), (ng, POOL_GD, POOL_GD), lambda i: (o, 0, 0))

    def odd_fwd(l, h, n, g_next):
        o = l // 2
        p = _matmul(n, W["od_w_in"], "nn", F32, "od_in", tm, D, b_l=o)
        d, e, es = _odd_core(p, group_w(o), row(W["od_scale"][o]), T)
        m = _matmul(es, W["od_w_out"], "nn", F32, "od_out", tm, D, b_l=o)
        h2, n2 = _post(h, m, row(W["mix_post_g"][l]), 1.0, g_next, T)
        subs.append(dict(kind="odd", l=l, h=h, n=n, d=d, e=e, es=es, f=m, scale=1.0, pre="mix_pre_g", post="mix_post_g"))
        return h2, n2

    def odd_bwd(s, dm):
        l = s["l"]
        o = l // 2
        des = _matmul(dm, W["od_w_out"], "nt", F32, "od_out_dx", tm, D, b_l=o)
        g.add("od_w_out", o, 2, s["es"], dm, D, 512)
        dp, dsc, dwg = _odd_bwd(des, s["e"], s["d"], group_w(o), row(W["od_scale"][o]), T)
        small["od_scale"][o] = dsc[0]
        small["od_w_group_full"][o] = dwg
        dn = _matmul(dp, W["od_w_in"], "nt", F32, "od_in_dx", tm, D, b_l=o)
        g.add("od_w_in", o, 2, s["n"], dp, D, 512)
        return dn

    small["od_w_group_full"] = [None, None]
    order = []
    for l in range(4):
        order += [("ffn1", l), ("even" if l % 2 == 0 else "odd", l), ("xa", l), ("ffn2", l)]
    pre_of = {"ffn1": "ffn1_pre_g", "even": "mix_pre_g", "odd": "mix_pre_g", "xa": "xa_pre_g", "ffn2": "ffn2_pre_g"}
    h = x
    n = _prenorm(h, row(W["ffn1_pre_g"][0]), T)
    for idx, (kind, l) in enumerate(order):
        if before_sub is not None:
            before_sub(kind, l, h)
        g_next = row(W[pre_of[order[idx + 1][0]]][order[idx + 1][1]]) if idx + 1 < len(order) else None
        if kind in ("ffn1", "ffn2"):
            h, n = ffn_fwd(kind, l, h, n, g_next)
        elif kind == "xa":
            h, n = xa_fwd(l, h, n, g_next)
        elif kind == "even":
            h, n = even_fwd(l, h, n, g_next)
        else:
            h, n = odd_fwd(l, h, n, g_next)

    top = subs[-1]
    dh, df, loss_acc, dgp = _loss_top(h, target, top["f"], row(W[top["post"]][top["l"]]), top["scale"], T)
    small[top["post"]][top["l"]] = dgp[0]
    for idx in range(len(subs) - 1, -1, -1):
        s = subs[idx]
        dn = {"ffn": ffn_bwd, "xa": xa_bwd, "even": even_bwd, "odd": odd_bwd}[s["kind"]](s, df)
        if idx > 0:
            sp = subs[idx - 1]
            dh, df, dg_pre, dg_post = _boundary_bwd(s["h"], row(W[s["pre"]][s["l"]]), dn, dh, sp["f"],
                                                    row(W[sp["post"]][sp["l"]]), sp["scale"], T, deps=g.drain())
            small[sp["post"]][sp["l"]] = dg_post[0]
        else:
            dh, dg_pre = _boundary_bwd(s["h"], row(W[s["pre"]][s["l"]]), dn, dh, None, None, None, T, deps=g.drain())
        small[s["pre"]][s["l"]] = dg_pre[0]
        if after_layer_bwd is not None and idx % 4 == 0:
            after_layer_bwd(s["l"], g.buf, small)
    return loss_acc[0, 0], dh, g.buf, small


def _step(P, S):
    x, mem, target = P["x"][0], P["mem"][0], P["loss_target"][0]
    chip = 2 * lax.axis_index("x") + lax.axis_index("y")
    pos = jnp.stack([chip, lax.axis_index("c")]).astype(jnp.int32)
    order = list(BIG)
    tix = {k: t for t, k in enumerate(order)}
    kinds = [BIG[k] for k in order]
    per = {k: (4 if k == "od_w_group" else 1) for k in order}
    members = {"ffn1": ["ffn1_w_gu", "ffn1_w_down"], "ffn2": ["ffn2_w_gu", "ffn2_w_down"], "xa": ["xa_w_q", "xa_w_kv", "xa_w_o"],
               "even": ["ev_w_in", "ev_w_out"], "odd": ["od_w_in", "od_w_group", "od_w_out"]}
    index_of = lambda kind, l: l // 2 if kind in ("even", "odd") else l
    sub_order = []
    for l in range(4):
        sub_order += [("ffn1", l), ("even" if l % 2 == 0 else "odd", l), ("xa", l), ("ffn2", l)]
    groups = {(kind, l): [(tix[k], index_of(kind, l), index_of(kind, l) * per[k], per[k]) for k in members[kind]]
              for kind, l in sub_order}
    slots = [_cast_into_slot(pos, _as3d(P[k]), BIG[k], "cast_" + k) for k in order]
    g_send, g_recv, bufs = _gather_start(slots, kinds, [groups[s_] for s_ in sub_order])
    W = {k: P[k] for k in WEIGHTS if k not in BIG}
    W.update(zip(TINY_SHARDED, _gather_tiny([P[k] for k in TINY_SHARDED], list(TINY_SHARDED.values()))))
    W.update(zip(order, bufs))

    def view(k):
        return W[k].reshape(2, 4, POOL_GD, POOL_GD) if k == "od_w_group" else W[k]

    Wv = _WeightView(W, view)

    def before_sub(kind, l, h):
        ks = members[kind]
        arrived = _gather_wait([W[k] for k in ks], [BIG[k] for k in ks], groups[(kind, l)], g_send, g_recv, h,
                               "gather_wait_%s%d" % (kind, l))
        W.update(zip(ks, arrived))

    recv = {k: None for k in order}
    pending = []

    def after_layer_bwd(l, big, small):
        ks = members["ffn1"] + members["ffn2"] + members["xa"] + [k for k in members["even" if l % 2 == 0 else "odd"]
                                                                   if k != "od_w_group"]
        items = [(index_of("even", l) if k[:2] in ("ev", "od") else l, 1) for k in ks]
        if l == 0:
            big["od_w_group"] = jnp.stack(small.pop("od_w_group_full")).astype(BF16).reshape(8, POOL_GD, POOL_GD)
            ks, items = ks + ["od_w_group"], items + [(0, 8)]
        for k in ks:
            if recv[k] is None:
                recv[k] = lax.empty((7,) + _half_geometry(big[k].shape, BIG[k]), BF16)
        s_send, s_recv, g_thru, r_thru = _scatter_start([big[k] for k in ks], [recv[k] for k in ks], [BIG[k] for k in ks],
                                                        items, "scatter_start_%d" % l)
        big.update(zip(ks, g_thru))
        recv.update(zip(ks, r_thru))
        pending.append((l, ks, items, s_send, s_recv))

    loss, grad_x, big, small = _local_step(x, mem, target, Wv, S, before_sub=before_sub, after_layer_bwd=after_layer_bwd)
    for l, ks, items, s_send, s_recv in pending:
        recv.update(zip(ks, _scatter_wait([recv[k] for k in ks], items, s_send, s_recv, "scatter_wait_%d" % l)))
    gsh = dict(zip(order, _swap_halves([_sum_into_half(pos, recv[k], big[k], BIG[k], "sum_" + k) for k in order])))

    small_names = [k for k in WEIGHTS if k not in BIG]
    small_full = [jnp.stack(small[k]) for k in small_names]
    packed = _pack(small_full)
    summed = _sum_slots(_share_small(packed, big[order[0]]), _row_tile(packed.shape[0], 512), "sum_small")
    gsmall = dict(zip(small_names, _unpack(summed, [a.shape for a in small_full])))
    for k, ax in TINY_SHARDED.items():
        size = P[k].shape[ax]
        gsmall[k] = lax.dynamic_slice_in_dim(gsmall[k], chip * size, size, axis=ax)

    grads, delta, new_m, new_v = {}, {}, {}, {}
    for k in order:
        shp = P[k].shape
        cols = shp[-1]
        flat = lambda a: a.reshape(-1, cols)
        gk = flat(gsh[k])
        d_, m_, v_ = _adamw(flat(P[k]), gk, flat(P["m_" + k]), flat(P["v_" + k]), _row_tile(gk.shape[0], 256), "adamw_" + k)
        grads[k], delta[k], new_m[k], new_v[k] = gk.reshape(shp), d_.reshape(shp), m_.reshape(shp), v_.reshape(shp)
    pk = lambda pre: _pack([P[pre + k] for k in small_names])
    d_, m_, v_ = _adamw(pk(""), _pack([gsmall[k] for k in small_names]), pk("m_"), pk("v_"),
                        _row_tile(pk("").shape[0], 256), "adamw_small")
    shapes = [P[k].shape for k in small_names]
    for dst, src in ((delta, d_), (new_m, m_), (new_v, v_)):
        dst.update(zip(small_names, _unpack(src, shapes)))
    grads.update(gsmall)

    loss = lax.psum(loss, ("x", "y", "c"))
    out = [loss, grad_x[None]]
    for grp in (grads, delta, new_m, new_v):
        out += [grp[k] for k in WEIGHTS]
    return tuple(out)


def kernel(x, mem, ffn1_pre_g, ffn1_w_gu, ffn1_w_down, ffn1_post_g, mix_pre_g, mix_post_g, ev_w_in, ev_conv_w, ev_conv_b, ev_conv_ln_g, ev_conv_ln_b, ev_sgu_ln_g, ev_sgu_ln_b, ev_sgu_w, ev_sgu_b, ev_w_out, od_w_in, od_w_group, od_scale, od_w_out, xa_pre_g, xa_mem_g, xa_w_q, xa_w_kv, xa_w_o, xa_post_g, ffn2_pre_g, ffn2_w_gu, ffn2_w_down, ffn2_post_g, loss_target, m_ffn1_pre_g, m_ffn1_w_gu, m_ffn1_w_down, m_ffn1_post_g, m_mix_pre_g, m_mix_post_g, m_ev_w_in, m_ev_conv_w, m_ev_conv_b, m_ev_conv_ln_g, m_ev_conv_ln_b, m_ev_sgu_ln_g, m_ev_sgu_ln_b, m_ev_sgu_w, m_ev_sgu_b, m_ev_w_out, m_od_w_in, m_od_w_group, m_od_scale, m_od_w_out, m_xa_pre_g, m_xa_mem_g, m_xa_w_q, m_xa_w_kv, m_xa_w_o, m_xa_post_g, m_ffn2_pre_g, m_ffn2_w_gu, m_ffn2_w_down, m_ffn2_post_g, v_ffn1_pre_g, v_ffn1_w_gu, v_ffn1_w_down, v_ffn1_post_g, v_mix_pre_g, v_mix_post_g, v_ev_w_in, v_ev_conv_w, v_ev_conv_b, v_ev_conv_ln_g, v_ev_conv_ln_b, v_ev_sgu_ln_g, v_ev_sgu_ln_b, v_ev_sgu_w, v_ev_sgu_b, v_ev_w_out, v_od_w_in, v_od_w_group, v_od_scale, v_od_w_out, v_xa_pre_g, v_xa_mem_g, v_xa_w_q, v_xa_w_kv, v_xa_w_o, v_xa_post_g, v_ffn2_pre_g, v_ffn2_w_gu, v_ffn2_w_down, v_ffn2_post_g):
    P = dict(locals())
    return _step(P, x.shape[1])
```

```python
import functools
import math

import jax
import jax.numpy as jnp
from jax import lax
from jax.experimental import pallas as pl
from jax.experimental.pallas import tpu as pltpu

F32 = jnp.float32
BF16 = jnp.bfloat16
MESH = pl.DeviceIdType.MESH
ANY = pl.BlockSpec(memory_space=pl.ANY)
HBM = pl.BlockSpec(memory_space=pltpu.HBM)


def _hbm(x):
    return pltpu.with_memory_space_constraint(x, pltpu.HBM)

D = 1024
DFF = 2816
NMEM = 256
DC = 512
CONV_W = 31
CHUNK = 128
NHEAD_SGU = 4
POOL_WINDOWS = (2, 4, 8, 16)
POOL_GD = 256
XA_HEADS = 4
XA_HD = 256
EPS = 1e-6
NCHIP = 4
HALO = 32

ADAM_LR, ADAM_B1, ADAM_B2, ADAM_EPS, ADAM_WD, ADAM_STEP = 0.001, 0.9, 0.999, 1e-08, 0.01, 10

V7X_VMEM_BYTES = 64 * 1024 * 1024
VMEM_LIMIT = V7X_VMEM_BYTES - 8 * 1024 * 1024


def _params(sem):
    return pltpu.CompilerParams(dimension_semantics=sem, vmem_limit_bytes=VMEM_LIMIT)


def _matmul(a, b, kind, out_dtype, name, tm, tn, a_l=None, b_l=None, out_l=None, out_stack=None, out_buf=None,
            n_outer=True):
    a2, b2 = a.shape[-2:], b.shape[-2:]
    if kind == "nn":
        (m, k), (k2, n) = a2, b2
        dims = (((1,), (0,)), ((), ()))
    elif kind == "nt":
        (m, k), (n, k2) = a2, b2
        dims = (((1,), (1,)), ((), ()))
    else:
        (k, m), (k2, n) = a2, b2
        dims = (((0,), (0,)), ((), ()))
    assert k == k2 and m % tm == 0 and n % tn == 0, (name, a.shape, b.shape, tm, tn)
    if n_outer:
        grid = (n // tn, m // tm)
        ij = lambda p, q: (q, p)
    else:
        grid = (m // tm, n // tn)
        ij = lambda p, q: (p, q)

    def spec(arr, layer, blk, idx):
        if arr.ndim == 3:
            return pl.BlockSpec((None,) + blk, lambda p, q: (layer,) + idx(*ij(p, q)))
        return pl.BlockSpec(blk, lambda p, q: idx(*ij(p, q)))

    a_spec = spec(a, a_l, (k, tm) if kind == "tn" else (tm, k), (lambda i, j: (0, i)) if kind == "tn" else (lambda i, j: (i, 0)))
    b_spec = spec(b, b_l, (tn, k) if kind == "nt" else (k, tn), (lambda i, j: (j, 0)) if kind == "nt" else (lambda i, j: (0, j)))
    if out_stack is None:
        out_shape = jax.ShapeDtypeStruct((m, n), out_dtype)
        o_spec = pl.BlockSpec((tm, tn), lambda p, q: ij(p, q))
    else:
        out_shape = jax.ShapeDtypeStruct((out_stack, m, n), out_dtype)
        o_spec = pl.BlockSpec((None, tm, tn), lambda p, q: (out_l,) + ij(p, q))

    def body(a_ref, b_ref, *rest):
        o_ref = rest[-1]
        o_ref[...] = lax.dot_general(a_ref[...], b_ref[...], dims, preferred_element_type=F32).astype(o_ref.dtype)

    in_specs, args, aliases = [a_spec, b_spec], [a, _hbm(b) if b.ndim == 3 else b], {}
    if out_buf is not None:
        in_specs.append(ANY)
        args.append(_hbm(out_buf))
        aliases = {2: 0}
    return pl.pallas_call(body, name=name, grid=grid, in_specs=in_specs, out_specs=o_spec, out_shape=out_shape,
                          input_output_aliases=aliases, compiler_params=_params(("parallel", "parallel")))(*args)


def _rowwise(fn, name, rows, tile, row_ins, full_ins, row_outs, acc_outs=(), scratch=(), deps=()):
    assert rows % tile == 0, (name, rows, tile)
    in_specs, args = [], []
    for r in row_ins:
        if isinstance(r, tuple):
            arr, br, bc, imap = r
            in_specs.append(pl.BlockSpec((br, bc), imap))
        else:
            arr = r
            in_specs.append(pl.BlockSpec((tile, arr.shape[1]), lambda i: (i, 0)))
        args.append(arr)
    for f in full_ins:
        if isinstance(f, tuple):
            arr, blk, imap = f
            in_specs.append(pl.BlockSpec(blk, imap))
            args.append(_hbm(arr))
        else:
            in_specs.append(pl.BlockSpec(f.shape, functools.partial(lambda nd, i: (0,) * nd, f.ndim)))
            args.append(f)
    for d in deps:
        in_specs.append(ANY)
        args.append(_hbm(d))
    out_specs, out_shape = [], []
    for w, dt in row_outs:
        out_specs.append(pl.BlockSpec((tile, w), lambda i: (i, 0)))
        out_shape.append(jax.ShapeDtypeStruct((rows, w), dt))
    for shp, dt in acc_outs:
        out_specs.append(pl.BlockSpec(shp, functools.partial(lambda nd, i: (0,) * nd, len(shp))))
        out_shape.append(jax.ShapeDtypeStruct(shp, dt))
    n_in, n_out = len(args), len(out_shape)

    def body(*refs):
        fn(pl.program_id(0), refs[:n_in], refs[n_in:n_in + n_out], refs[n_in + n_out:])

    sem = ("arbitrary",) if acc_outs else ("parallel",)
    res = pl.pallas_call(body, name=name, grid=(rows // tile,), in_specs=in_specs, out_specs=out_specs, out_shape=out_shape,
                         scratch_shapes=list(scratch), compiler_params=_params(sem))(*args)
    return res


def _accum(ref, i, val):
    @pl.when(i == 0)
    def _():
        ref[...] = val

    @pl.when(i > 0)
    def _():
        ref[...] += val


def _prev_halo(arr, tile, cols=None):
    r = tile // HALO
    return (arr, HALO, cols or arr.shape[1], lambda i: (jnp.maximum(i * r - 1, 0), 0))


def _next_halo(arr, tile, cols=None):
    r = tile // HALO
    last = arr.shape[0] // HALO - 1
    return (arr, HALO, cols or arr.shape[1], lambda i: (jnp.minimum((i + 1) * r, last), 0))


def _sigmoid(x):
    return 1.0 / (1.0 + jnp.exp(-x))


def _rms_hat(x):
    r = lax.rsqrt(jnp.mean(x * x, axis=-1, keepdims=True) + EPS)
    return x * r, r


def _rms_bwd(x, g, dy):
    xhat, r = _rms_hat(x)
    dxhat = dy * g
    dx = r * (dxhat - xhat * jnp.mean(dxhat * xhat, axis=-1, keepdims=True))
    return dx, jnp.sum(dy * xhat, axis=0, keepdims=True)


def _ln_hat(x):
    mu = jnp.mean(x, axis=-1, keepdims=True)
    xc = x - mu
    r = lax.rsqrt(jnp.mean(xc * xc, axis=-1, keepdims=True) + EPS)
    return xc * r, r


def _ln_bwd(xhat, r, g, dy):
    dxhat = dy * g
    dx = r * (dxhat - jnp.mean(dxhat, axis=-1, keepdims=True) - xhat * jnp.mean(dxhat * xhat, axis=-1, keepdims=True))
    return dx, jnp.sum(dy * xhat, axis=0, keepdims=True), jnp.sum(dy, axis=0, keepdims=True)


def _silu_grad(x):
    s = _sigmoid(x)
    return s * (1.0 + x * (1.0 - s))


_SQRT_HALF = math.sqrt(0.5)
_INV_SQRT_2PI = 1.0 / math.sqrt(2.0 * math.pi)


def _gelu(x):
    return 0.5 * x * (1.0 + lax.erf(x * _SQRT_HALF))


def _gelu_grad(x):
    return 0.5 * (1.0 + lax.erf(x * _SQRT_HALF)) + x * jnp.exp(-0.5 * x * x) * _INV_SQRT_2PI


def _dot(a, b, kind="nn"):
    dims = {"nn": (((1,), (0,)), ((), ())), "nt": (((1,), (1,)), ((), ())), "tn": (((0,), (0,)), ((), ()))}[kind]
    return lax.dot_general(a, b, dims, preferred_element_type=F32)


def _prenorm(h, g, tile):
    def fn(i, ins, outs, _):
        h_ref, g_ref = ins
        outs[0][...] = (_rms_hat(h_ref[...])[0] * g_ref[...]).astype(BF16)

    return _rowwise(fn, "prenorm", h.shape[0], tile, [h], [g], [(D, BF16)])[0]


def _post(h, f, g_post, scale, g_next, tile):
    def fn(i, ins, outs, _):
        hn = ins[0][...] + scale * (_rms_hat(ins[1][...])[0] * ins[2][...])
        outs[0][...] = hn
        if g_next is not None:
            outs[1][...] = (_rms_hat(hn)[0] * ins[3][...]).astype(BF16)

    fulls = [g_post] + ([g_next] if g_next is not None else [])
    outs = [(D, F32)] + ([(D, BF16)] if g_next is not None else [])
    res = _rowwise(fn, "post", h.shape[0], tile, [h, f], fulls, outs)
    return res[0], (res[1] if g_next is not None else None)


def _loss_top(h, target, f, g_post, scale, tile):
    def fn(i, ins, outs, _):
        err = ins[0][...] - ins[1][...]
        per_row = jnp.mean(err * err, axis=-1, keepdims=True)
        _accum(outs[2], i, jnp.broadcast_to(0.5 * jnp.sum(per_row, axis=0, keepdims=True), (1, 128)))
        dh = err * (1.0 / D)
        outs[0][...] = dh
        df, dg = _rms_bwd(ins[2][...], ins[3][...], scale * dh)
        outs[1][...] = df.astype(BF16)
        _accum(outs[3], i, dg)

    return _rowwise(fn, "loss_top", h.shape[0], tile, [h, target, f], [g_post], [(D, F32), (D, BF16)],
                    [((1, 128), F32), ((1, D), F32)])


def _boundary_bwd(h, g_pre, dn, dh_in, f_prev, g_post_prev, scale_prev, tile, deps=()):
    has_prev = f_prev is not None

    def fn(i, ins, outs, _):
        dx, dg = _rms_bwd(ins[0][...], ins[3 + has_prev][...], ins[1][...])
        dh = ins[2][...] + dx
        outs[0][...] = dh
        _accum(outs[1 + has_prev], i, dg)
        if has_prev:
            df, dgp = _rms_bwd(ins[3][...], ins[5][...], scale_prev * dh)
            outs[1][...] = df.astype(BF16)
            _accum(outs[3], i, dgp)

    rows = [h, dn, dh_in] + ([f_prev] if has_prev else [])
    fulls = [g_pre] + ([g_post_prev] if has_prev else [])
    outs = [(D, F32)] + ([(D, BF16)] if has_prev else [])
    accs = [((1, D), F32)] + ([((1, D), F32)] if has_prev else [])
    return _rowwise(fn, "boundary_bwd", h.shape[0], tile, rows, fulls, outs, accs, deps=deps)


def _swiglu(gu, tile):
    def fn(i, ins, outs, _):
        g = ins[0][:, :DFF]
        outs[0][...] = (g * _sigmoid(g) * ins[0][:, DFF:]).astype(BF16)

    return _rowwise(fn, "swiglu", gu.shape[0], tile, [gu], [], [(DFF, BF16)])[0]


def _swiglu_bwd(da, gu, tile):
    def fn(i, ins, outs, _):
        da_v = ins[0][...]
        g = ins[1][:, :DFF]
        u = ins[1][:, DFF:]
        outs[0][:, :DFF] = (da_v * u * _silu_grad(g)).astype(BF16)
        outs[0][:, DFF:] = (da_v * g * _sigmoid(g)).astype(BF16)

    return _rowwise(fn, "swiglu_bwd", gu.shape[0], tile, [da, gu], [], [(2 * DFF, BF16)])[0]


def _softmax_rows(s):
    e = jnp.exp(s - jnp.max(s, axis=-1, keepdims=True))
    return e / jnp.sum(e, axis=-1, keepdims=True)


def _attn(q, kv, tile):
    def fn(i, ins, outs, _):
        q_ref, kv_ref = ins
        for hd in range(XA_HEADS):
            c0 = hd * XA_HD
            p = _softmax_rows(_dot(q_ref[:, c0:c0 + XA_HD], kv_ref[:, c0:c0 + XA_HD], "nt") * (XA_HD ** -0.5))
            outs[0][:, c0:c0 + XA_HD] = _dot(p.astype(BF16), kv_ref[:, D + c0:D + c0 + XA_HD]).astype(BF16)

    return _rowwise(fn, "attn", q.shape[0], tile, [q], [kv], [(D, BF16)])[0]


def _attn_bwd(q, kv, do, tile):
    def fn(i, ins, outs, _):
        q_ref, do_ref, kv_ref = ins
        for hd in range(XA_HEADS):
            c0 = hd * XA_HD
            qh, kh, vh = q_ref[:, c0:c0 + XA_HD], kv_ref[:, c0:c0 + XA_HD], kv_ref[:, D + c0:D + c0 + XA_HD]
            doh = do_ref[:, c0:c0 + XA_HD]
            p = _softmax_rows(_dot(qh, kh, "nt") * (XA_HD ** -0.5))
            dp = _dot(doh, vh, "nt")
            ds = (p * (dp - jnp.sum(dp * p, axis=-1, keepdims=True)) * (XA_HD ** -0.5)).astype(BF16)
            outs[0][:, c0:c0 + XA_HD] = _dot(ds, kh).astype(BF16)
            dk = _dot(ds, qh, "tn")
            dv = _dot(p.astype(BF16), doh, "tn")

            @pl.when(i == 0)
            def _():
                outs[1][:, c0:c0 + XA_HD] = dk
                outs[1][:, D + c0:D + c0 + XA_HD] = dv

            @pl.when(i > 0)
            def _():
                outs[1][:, c0:c0 + XA_HD] += dk
                outs[1][:, D + c0:D + c0 + XA_HD] += dv

    return _rowwise(fn, "attn_bwd", q.shape[0], tile, [q, do], [kv], [(D, BF16)], [((NMEM, 2 * D), F32)])


def _mem_norm(mem, g):
    def fn(i, ins, outs, _):
        outs[0][...] = (_rms_hat(ins[0][...])[0] * ins[1][...]).astype(BF16)

    return _rowwise(fn, "mem_norm", NMEM, NMEM, [mem], [g], [(D, BF16)])[0]


def _mem_gain_bwd(mem, dmn):
    def fn(i, ins, outs, _):
        outs[0][...] = jnp.sum(ins[1][...] * _rms_hat(ins[0][...])[0], axis=0, keepdims=True)

    return _rowwise(fn, "mem_gain_bwd", NMEM, NMEM, [mem, dmn], [], [], [((1, D), F32)])[0]


def _conv_taps(w_ref, zext_ref, tile, shift0):
    acc = None
    for k in range(CONV_W):
        term = w_ref[k:k + 1, :] * zext_ref[pl.ds(shift0(k), tile), :]
        acc = term if acc is None else acc + term
    return acc


def _tril_mask(transpose=False):
    r, c = (lax.broadcasted_iota(jnp.int32, (CHUNK, CHUNK), a) for a in (0, 1))
    return (r <= c) if transpose else (r >= c)


def _even_core(p, cw, cb, clg, clb, slg, slb, ws, bst, tile):
    def fn(i, ins, outs, scr):
        p_ref, ph_ref, cw_ref, cb_ref, clg_ref, clb_ref, slg_ref, slb_ref, ws_ref, bst_ref = ins
        y_ref, z_ref = outs
        zext = scr[0]
        z = p_ref[:, :DC] * _sigmoid(p_ref[:, DC:2 * DC])
        zh = ph_ref[:, :DC] * _sigmoid(ph_ref[:, DC:2 * DC])
        zext[:HALO, :] = jnp.where(i > 0, zh, 0.0)
        zext[HALO:, :] = z
        z_ref[...] = z
        conv = _conv_taps(cw_ref, zext, tile, lambda k: HALO - (CONV_W - 1) + k) + cb_ref[...]
        yl = _ln_hat(conv)[0] * clg_ref[...] + clb_ref[...]
        y_ref[:, :DC] = (yl * _sigmoid(yl)).astype(BF16)
        zb = _gelu(p_ref[:, 2 * DC:])
        vln = (_ln_hat(zb[:, DC:])[0] * slg_ref[...] + slb_ref[...]).astype(BF16)
        mask = _tril_mask()
        for hd in range(NHEAD_SGU):
            wm = jnp.where(mask, ws_ref[hd], 0.0).astype(BF16)
            for ci in range(tile // CHUNK):
                r0, c0 = ci * CHUNK, hd * CHUNK
                mixed = _dot(wm, vln[r0:r0 + CHUNK, c0:c0 + CHUNK]) + bst_ref[:, hd:hd + 1]
                y_ref[r0:r0 + CHUNK, DC + c0:DC + c0 + CHUNK] = (zb[r0:r0 + CHUNK, c0:c0 + CHUNK] * mixed).astype(BF16)

    return _rowwise(fn, "even_core", p.shape[0], tile, [p, _prev_halo(p, tile, D)], [cw, cb, clg, clb, slg, slb, ws, bst],
                    [(D, BF16), (DC, F32)], scratch=[pltpu.VMEM((tile + HALO, DC), F32)])


def _even_bwd_a(p, z, dy, cw, cb, clg, clb, slg, slb, ws, wst, bst, tile):
    def fn(i, ins, outs, scr):
        p_ref, z_ref, zh_ref, dy_ref, cw_ref, cb_ref, clg_ref, clb_ref, slg_ref, slb_ref, ws_ref, wst_ref, bst_ref = ins
        dc_ref, dpb_ref, vec_ref, dws_ref, dbst_ref = outs
        zext, dvln_s = scr
        zext[:HALO, :] = jnp.where(i > 0, zh_ref[...], 0.0)
        zext[HALO:, :] = z_ref[...]
        conv = _conv_taps(cw_ref, zext, tile, lambda k: HALO - (CONV_W - 1) + k) + cb_ref[...]
        chat, cr = _ln_hat(conv)
        yl = chat * clg_ref[...] + clb_ref[...]
        dyl = dy_ref[:, :DC] * _silu_grad(yl)
        dconv, dclg, dclb = _ln_bwd(chat, cr, clg_ref[...], dyl)
        dc_ref[...] = dconv
        xb = p_ref[:, 2 * DC:]
        zb = _gelu(xb)
        vhat, vr = _ln_hat(zb[:, DC:])
        vln = (vhat * slg_ref[...] + slb_ref[...]).astype(BF16)
        dyb = dy_ref[:, DC:]
        mask, mask_t = _tril_mask(), _tril_mask(transpose=True)
        dbu_parts = []
        for hd in range(NHEAD_SGU):
            wm = jnp.where(mask, ws_ref[hd], 0.0).astype(BF16)
            wt = jnp.where(mask_t, wst_ref[hd], 0.0).astype(BF16)
            dws_h, dbs_h, rows = None, None, []
            for ci in range(tile // CHUNK):
                r0, c0 = ci * CHUNK, hd * CHUNK
                vblk = vln[r0:r0 + CHUNK, c0:c0 + CHUNK]
                mixed = _dot(wm, vblk) + bst_ref[:, hd:hd + 1]
                dyb_blk = dyb[r0:r0 + CHUNK, c0:c0 + CHUNK]
                rows.append(dyb_blk * mixed)
                dmixed = dyb_blk * zb[r0:r0 + CHUNK, c0:c0 + CHUNK]
                dmb = dmixed.astype(BF16)
                dvln_s[r0:r0 + CHUNK, c0:c0 + CHUNK] = _dot(wt, dmb)
                dw = _dot(dmb, vblk, "nt")
                db = jnp.sum(dmixed, axis=-1, keepdims=True)
                dws_h = dw if dws_h is None else dws_h + dw
                dbs_h = db if dbs_h is None else dbs_h + db
            dbu_parts.append(jnp.concatenate(rows, axis=0))
            dws_h = jnp.where(mask, dws_h, 0.0)

            @pl.when(i == 0)
            def _():
                dws_ref[hd] = dws_h
                dbst_ref[:, hd:hd + 1] = dbs_h

            @pl.when(i > 0)
            def _():
                dws_ref[hd] += dws_h
                dbst_ref[:, hd:hd + 1] += dbs_h

        dbu = jnp.concatenate(dbu_parts, axis=1)
        dbv, dslg, dslb = _ln_bwd(vhat, vr, slg_ref[...], dvln_s[...])
        dpb_ref[:, :DC] = (dbu * _gelu_grad(xb[:, :DC])).astype(BF16)
        dpb_ref[:, DC:] = (dbv * _gelu_grad(xb[:, DC:])).astype(BF16)
        @pl.when(i == 0)
        def _():
            vec_ref[...] = jnp.zeros_like(vec_ref)

        for r, val in enumerate([jnp.sum(dconv, axis=0, keepdims=True), dclg, dclb, dslg, dslb]):
            vec_ref[r:r + 1, :] += val

    return _rowwise(fn, "even_bwd_a", p.shape[0], tile, [p, z, _prev_halo(z, tile), dy],
                    [cw, cb, clg, clb, slg, slb, ws, wst, bst], [(DC, F32), (D, BF16)],
                    [((8, DC), F32), ((NHEAD_SGU, CHUNK, CHUNK), F32), ((CHUNK, NHEAD_SGU), F32)],
                    scratch=[pltpu.VMEM((tile + HALO, DC), F32), pltpu.VMEM((tile, DC), F32)])


def _even_bwd_b(p, z, dconv, dpb, cw, tile):
    def fn(i, ins, outs, scr):
        p_ref, z_ref, zh_ref, dc_ref, dcn_ref, dpb_ref, cw_ref = ins
        dp_ref, dcw_ref = outs
        zext, dcext = scr
        last = pl.num_programs(0) - 1
        zext[:HALO, :] = jnp.where(i > 0, zh_ref[...], 0.0)
        zext[HALO:, :] = z_ref[...]
        dcext[:tile, :] = dc_ref[...]
        dcext[tile:, :] = jnp.where(i < last, dcn_ref[...], 0.0)
        dz = _conv_taps(cw_ref, dcext, tile, lambda k: (CONV_W - 1) - k)
        gate = p_ref[:, DC:2 * DC]
        s = _sigmoid(gate)
        dp_ref[:, :DC] = (dz * s).astype(BF16)
        dp_ref[:, DC:2 * DC] = (dz * p_ref[:, :DC] * s * (1.0 - s)).astype(BF16)
        dp_ref[:, 2 * DC:] = dpb_ref[...]
        dcv = dc_ref[...]

        @pl.when(i == 0)
        def _():
            dcw_ref[...] = jnp.zeros_like(dcw_ref)

        for k in range(CONV_W):
            dcw_ref[k:k + 1, :] += jnp.sum(dcv * zext[pl.ds(HALO - (CONV_W - 1) + k, tile), :], axis=0, keepdims=True)

    return _rowwise(fn, "even_bwd_b", p.shape[0], tile,
                    [(p, tile, D, lambda i: (i, 0)), z, _prev_halo(z, tile), dconv, _next_halo(dconv, tile), dpb], [cw],
                    [(2 * D, BF16)], [((HALO, DC), F32)],
                    scratch=[pltpu.VMEM((tile + HALO, DC), F32), pltpu.VMEM((tile + HALO, DC), F32)])


def _row_count(i, tile, nrows, offset, w):
    t = i * tile + offset + lax.broadcasted_iota(jnp.int32, (nrows, POOL_GD), 0)
    return jnp.minimum(t + 1, w).astype(F32)


def _odd_core(p, wg, scale, tile):
    def fn(i, ins, outs, scr):
        p_ref, ph_ref, wg_ref, sc_ref = ins
        d_ref, e_ref, es_ref = outs
        pext = scr[0]
        pext[:HALO, :] = jnp.where(i > 0, ph_ref[...], 0.0)
        pext[HALO:, :] = p_ref[...]
        for g, w in enumerate(POOL_WINDOWS):
            c0 = g * POOL_GD
            s = p_ref[:, c0:c0 + POOL_GD]
            for r in range(1, w):
                s = s + pext[pl.ds(HALO - r, tile), c0:c0 + POOL_GD]
            dg = (s / _row_count(i, tile, tile, 0, w) - p_ref[:, c0:c0 + POOL_GD]).astype(BF16)
            d_ref[:, c0:c0 + POOL_GD] = dg
            e = _dot(dg, wg_ref[g])
            e_ref[:, c0:c0 + POOL_GD] = e
            es_ref[:, c0:c0 + POOL_GD] = (e * sc_ref[:, c0:c0 + POOL_GD]).astype(BF16)

    return _rowwise(fn, "odd_core", p.shape[0], tile, [p, _prev_halo(p, tile)], [wg, scale],
                    [(D, BF16), (D, F32), (D, BF16)], scratch=[pltpu.VMEM((tile + HALO, D), F32)])


def _odd_bwd(des, e, d, wg, scale, tile):
    def fn(i, ins, outs, scr):
        des_ref, desn_ref, e_ref, d_ref, wg_ref, sc_ref = ins
        dp_ref, dsc_ref, dwg_ref = outs
        qext = scr[0]
        last = pl.num_programs(0) - 1
        desv = des_ref[...]
        _accum(dsc_ref, i, jnp.sum(desv * e_ref[...], axis=0, keepdims=True))
        de = (desv * sc_ref[...]).astype(BF16)
        den = (jnp.where(i < last, desn_ref[...], 0.0) * sc_ref[...]).astype(BF16)
        for g, w in enumerate(POOL_WINDOWS):
            c0 = g * POOL_GD
            deg = de[:, c0:c0 + POOL_GD]
            dd = _dot(deg, wg_ref[g], "nt")
            ddn = _dot(den[:, c0:c0 + POOL_GD], wg_ref[g], "nt")
            qext[:tile, c0:c0 + POOL_GD] = dd / _row_count(i, tile, tile, 0, w)
            qext[tile:, c0:c0 + POOL_GD] = ddn / _row_count(i, tile, HALO, tile, w)
            s = qext[:tile, c0:c0 + POOL_GD]
            for r in range(1, w):
                s = s + qext[pl.ds(r, tile), c0:c0 + POOL_GD]
            dp_ref[:, c0:c0 + POOL_GD] = (s - dd).astype(BF16)
            dw = _dot(d_ref[:, c0:c0 + POOL_GD], deg, "tn")

            @pl.when(i == 0)
            def _():
                dwg_ref[g] = dw

            @pl.when(i > 0)
            def _():
                dwg_ref[g] += dw

    return _rowwise(fn, "odd_bwd", des.shape[0], tile, [des, _next_halo(des, tile), e, d], [wg, scale], [(D, BF16)],
                    [((1, D), F32), ((len(POOL_WINDOWS), POOL_GD, POOL_GD), F32)],
                    scratch=[pltpu.VMEM((tile + HALO, D), F32)])


def _axis_slice(ref, axis, start, size):
    idx = [slice(None)] * len(ref.shape)
    idx[axis] = pl.ds(start, size)
    return ref.at[tuple(idx)]


def _cast_into_slot(pos, shard, kind, name):
    L, rs, cs = shard.shape
    tr = _row_tile(rs, 512)
    nr = rs // tr
    if kind == "row":
        full, omap = (L, NCHIP * rs, cs), (lambda l, i, p: (l, p[0] * nr + i, 0))
    else:
        full, omap = (L, rs, NCHIP * cs), (lambda l, i, p: (l, i, p[0]))

    def body(p_ref, s_ref, o_ref):
        o_ref[...] = s_ref[...].astype(BF16)

    grid_spec = pltpu.PrefetchScalarGridSpec(
        num_scalar_prefetch=1, grid=(L, nr), in_specs=[pl.BlockSpec((None, tr, cs), lambda l, i, p: (l, i, 0))],
        out_specs=pl.BlockSpec((None, tr, cs), omap))
    return pl.pallas_call(body, name=name, grid_spec=grid_spec, out_shape=jax.ShapeDtypeStruct(full, BF16),
                          compiler_params=_params(("parallel", "parallel")))(pos, shard)


def _peer(x, y, c, k):
    return (1 - x if k & 4 else x, 1 - y if k & 2 else y, 1 - c if k & 1 else c)


def _half_geometry(shape, kind):
    L, R, C = shape
    return (L, R // (2 * NCHIP), C) if kind == "row" else (L, R // 2, C // NCHIP)


def _sum_into_half(pos, recv, grad, kind, name):
    _, L, rh, cs = recv.shape
    if kind == "row":
        gmap = lambda l, p: (l, 2 * p[0] + p[1], 0)
    else:
        gmap = lambda l, p: (l, p[1], p[0])

    def body(p_ref, r_ref, g_ref, o_ref):
        acc = g_ref[...].astype(F32)
        for s in range(7):
            acc = acc + r_ref[s].astype(F32)
        o_ref[...] = acc

    grid_spec = pltpu.PrefetchScalarGridSpec(
        num_scalar_prefetch=1, grid=(L,),
        in_specs=[pl.BlockSpec((7, None, rh, cs), lambda l, p: (0, l, 0, 0)), pl.BlockSpec((None, rh, cs), gmap)],
        out_specs=pl.BlockSpec((None, rh, cs), lambda l, p: (l, p[1], 0)))
    return pl.pallas_call(body, name=name, grid_spec=grid_spec, out_shape=jax.ShapeDtypeStruct((L, 2 * rh, cs), F32),
                          compiler_params=_params(("parallel",)))(pos, _hbm(recv), _hbm(grad))


def _swap_halves(shards):
    n = len(shards)

    def body(*refs):
        outs = refs[n:2 * n]
        send_sems, recv_sems = refs[2 * n:]
        x, y, c = lax.axis_index("x"), lax.axis_index("y"), lax.axis_index("c")
        copies = []
        for t in range(n):
            rh = outs[t].shape[1] // 2
            half = outs[t].at[:, pl.ds(c * rh, rh), :]
            cp = pltpu.make_async_remote_copy(src_ref=half, dst_ref=half, send_sem=send_sems.at[t], recv_sem=recv_sems.at[t],
                                              device_id=(x, y, 1 - c), device_id_type=MESH)
            cp.start()
            copies.append(cp)
        for cp in copies:
            cp.wait()

    return pl.pallas_call(body, name="swap_halves", in_specs=[HBM] * n, out_specs=[HBM] * n,
                          out_shape=[pltpu.HBM(s.shape, s.dtype) for s in shards],
                          input_output_aliases={t: t for t in range(n)},
                          scratch_shapes=[pltpu.SemaphoreType.DMA((n,)), pltpu.SemaphoreType.DMA((n,))])(*map(_hbm, shards))


SEM = pl.BlockSpec(memory_space=pltpu.SEMAPHORE)
DATAFLOW = pltpu.SideEffectType.DATAFLOW_SIDE_EFFECTING
MAX_LAYERS = 4


def _gsem(t, layer, k):
    return (t * MAX_LAYERS + layer) * 3 + k


def _slot_of(ref, kind, d0, nd0, mine):
    if kind == "row":
        size = ref.shape[1] // NCHIP
        return ref.at[pl.ds(d0, nd0), pl.ds(mine * size, size), :]
    size = ref.shape[2] // NCHIP
    return ref.at[pl.ds(d0, nd0), :, pl.ds(mine * size, size)]


def _gather_start(bufs, kinds, groups, after):
    n, na = len(bufs), len(after)

    def body(*refs):
        send_sems, recv_sems = refs[n + na], refs[n + na + 1]
        outs = refs[n + na + 2:]
        x, y, c = lax.axis_index("x"), lax.axis_index("y"), lax.axis_index("c")
        mine = 2 * x + y
        chips = [(1 - x, y), (x, 1 - y), (1 - x, 1 - y)]
        for group in groups:
            for t, li, d0, nd0 in group:
                slot = _slot_of(outs[t], kinds[t], d0, nd0, mine)
                for k, chip in enumerate(chips):
                    pltpu.make_async_remote_copy(src_ref=slot, dst_ref=slot, send_sem=send_sems.at[_gsem(t, li, k)],
                                                 recv_sem=recv_sems.at[_gsem(t, li, k)], device_id=(*chip, c),
                                                 device_id_type=MESH).start()

    sems = pltpu.SemaphoreType.DMA((n * MAX_LAYERS * 3,))
    res = pl.pallas_call(body, name="gather_start", in_specs=[HBM] * n + [ANY] * na, out_specs=[SEM, SEM] + [HBM] * n,
                         out_shape=[sems, sems] + [pltpu.HBM(b.shape, b.dtype) for b in bufs],
                         input_output_aliases={t: t + 2 for t in range(n)},
                         compiler_params=pltpu.CompilerParams(has_side_effects=DATAFLOW))(*map(_hbm, bufs), *after)
    return res[0], res[1], list(res[2:])


def _gather_wait(bufs, kinds, group, send_sems, recv_sems, after, name):
    m = len(bufs)

    def body(*refs):
        send, recv = refs[m], refs[m + 1]
        outs = refs[m + 3:]
        x, y, c = lax.axis_index("x"), lax.axis_index("y"), lax.axis_index("c")
        mine = 2 * x + y
        chips = [(1 - x, y), (x, 1 - y), (1 - x, 1 - y)]
        for j, (t, li, d0, nd0) in enumerate(group):
            slot = _slot_of(outs[j], kinds[j], d0, nd0, mine)
            for k, chip in enumerate(chips):
                cp = pltpu.make_async_remote_copy(src_ref=slot, dst_ref=slot, send_sem=send.at[_gsem(t, li, k)],
                                                  recv_sem=recv.at[_gsem(t, li, k)], device_id=(*chip, c), device_id_type=MESH)
                cp.wait_send()
                cp.wait_recv()

    res = pl.pallas_call(body, name=name, in_specs=[HBM] * m + [SEM, SEM, ANY], out_specs=[HBM] * m,
                         out_shape=[pltpu.HBM(b.shape, b.dtype) for b in bufs],
                         input_output_aliases={j: j for j in range(m)},
                         compiler_params=pltpu.CompilerParams(has_side_effects=DATAFLOW))(*map(_hbm, bufs), send_sems, recv_sems,
                                                                                          after)
    return list(res)


def _gather_tiny(tiny, axes):
    n = len(tiny)
    out_shape = []
    for s_, ax in zip(tiny, axes):
        shp = list(s_.shape)
        shp[ax] *= NCHIP
        out_shape.append(jax.ShapeDtypeStruct(tuple(shp), s_.dtype))

    def body(*refs):
        ins, outs = refs[:n], refs[n:2 * n]
        send_sems, recv_sems, loc_sems = refs[2 * n:]
        x, y, c = lax.axis_index("x"), lax.axis_index("y"), lax.axis_index("c")
        mine = 2 * x + y
        chips = [(1 - x, y), (x, 1 - y), (1 - x, 1 - y)]
        copies = []
        for t in range(n):
            size = ins[t].shape[axes[t]]
            dst = _axis_slice(outs[t], axes[t], mine * size, size)
            copies.append(pltpu.make_async_copy(ins[t], dst, loc_sems.at[t]))
            for k, chip in enumerate(chips):
                copies.append(pltpu.make_async_remote_copy(src_ref=ins[t], dst_ref=dst, send_sem=send_sems.at[t, k],
                                                           recv_sem=recv_sems.at[t, k], device_id=(*chip, c),
                                                           device_id_type=MESH))
        for cp in copies:
            cp.start()
        for cp in copies:
            cp.wait()

    return pl.pallas_call(body, name="gather_tiny", in_specs=[ANY] * n, out_specs=[ANY] * n, out_shape=out_shape,
                          scratch_shapes=[pltpu.SemaphoreType.DMA((n, 3)), pltpu.SemaphoreType.DMA((n, 3)),
                                          pltpu.SemaphoreType.DMA((n,))])(*tiny)


def _grad_slice(ref, kind, d0, nd0, chip, core, rh, cs):
    if kind == "row":
        return ref.at[pl.ds(d0, nd0), pl.ds((2 * chip + core) * rh, rh), :]
    return ref.at[pl.ds(d0, nd0), pl.ds(core * rh, rh), pl.ds(chip * cs, cs)]


def _scatter_start(grads, recv, kinds, items, name, after=()):
    m, na = len(grads), len(after)

    def body(*refs):
        send_sems, recv_sems = refs[2 * m + na], refs[2 * m + na + 1]
        gout, rout = refs[2 * m + na + 2:3 * m + na + 2], refs[3 * m + na + 2:]
        x, y, c = lax.axis_index("x"), lax.axis_index("y"), lax.axis_index("c")
        for t in range(m):
            _, _, rh, cs = rout[t].shape
            d0, nd0 = items[t]
            for k in range(1, 8):
                px, py, pc = _peer(x, y, c, k)
                src = _grad_slice(gout[t], kinds[t], d0, nd0, 2 * px + py, pc, rh, cs)
                pltpu.make_async_remote_copy(src_ref=src, dst_ref=rout[t].at[k - 1, pl.ds(d0, nd0)],
                                             send_sem=send_sems.at[7 * t + k - 1], recv_sem=recv_sems.at[7 * t + k - 1],
                                             device_id=(px, py, pc), device_id_type=MESH).start()

    sems = pltpu.SemaphoreType.DMA((7 * m,))
    res = pl.pallas_call(body, name=name, in_specs=[HBM] * (2 * m) + [ANY] * na, out_specs=[SEM, SEM] + [HBM] * (2 * m),
                         out_shape=[sems, sems] + [pltpu.HBM(a.shape, a.dtype) for a in list(grads) + list(recv)],
                         input_output_aliases={j: j + 2 for j in range(2 * m)},
                         compiler_params=pltpu.CompilerParams(has_side_effects=DATAFLOW))(*map(_hbm, grads), *map(_hbm, recv),
                                                                                          *after)
    return res[0], res[1], list(res[2:m + 2]), list(res[m + 2:])


def _scatter_wait(recv, items, send_sems, recv_sems, name, after=()):
    m = len(recv)

    def body(*refs):
        send, rcv = refs[m], refs[m + 1]
        outs = refs[m + 2 + len(after):]
        x, y, c = lax.axis_index("x"), lax.axis_index("y"), lax.axis_index("c")
        for t in range(m):
            d0, nd0 = items[t]
            for k in range(1, 8):
                land = outs[t].at[k - 1, pl.ds(d0, nd0)]
                cp = pltpu.make_async_remote_copy(src_ref=land, dst_ref=land, send_sem=send.at[7 * t + k - 1],
                                                  recv_sem=rcv.at[7 * t + k - 1], device_id=_peer(x, y, c, k), device_id_type=MESH)
                cp.wait_send()
                cp.wait_recv()

    res = pl.pallas_call(body, name=name, in_specs=[HBM] * m + [SEM, SEM] + [ANY] * len(after), out_specs=[HBM] * m,
                         out_shape=[pltpu.HBM(a.shape, a.dtype) for a in recv],
                         input_output_aliases={j: j for j in range(m)},
                         compiler_params=pltpu.CompilerParams(has_side_effects=DATAFLOW))(*map(_hbm, recv), send_sems, recv_sems,
                                                                                          *map(_hbm, after))
    return list(res)


def _share_small(packed):
    rows = packed.shape[0]

    def body(in_ref, out_ref, send_sems, recv_sems, loc_sem):
        x, y, c = lax.axis_index("x"), lax.axis_index("y"), lax.axis_index("c")
        me = 4 * x + 2 * y + c
        copies = [pltpu.make_async_copy(in_ref, out_ref.at[me], loc_sem)]
        for k in range(1, 8):
            copies.append(pltpu.make_async_remote_copy(src_ref=in_ref, dst_ref=out_ref.at[me], send_sem=send_sems.at[k - 1],
                                                       recv_sem=recv_sems.at[k - 1], device_id=_peer(x, y, c, k),
                                                       device_id_type=MESH))
        for cp in copies:
            cp.start()
        for cp in copies:
            cp.wait()

    return pl.pallas_call(body, name="share_small", in_specs=[ANY], out_specs=ANY,
                          out_shape=jax.ShapeDtypeStruct((8, rows, 128), F32),
                          scratch_shapes=[pltpu.SemaphoreType.DMA((7,)), pltpu.SemaphoreType.DMA((7,)),
                                          pltpu.SemaphoreType.DMA])(packed)


def _sum_slots(buf, tile, name):
    _, rows, cols = buf.shape

    def body(b_ref, o_ref):
        acc = b_ref[0].astype(F32)
        for s in range(1, 8):
            acc = acc + b_ref[s].astype(F32)
        o_ref[...] = acc

    return pl.pallas_call(body, name=name, grid=(rows // tile,), in_specs=[pl.BlockSpec((8, tile, cols), lambda i: (0, i, 0))],
                          out_specs=pl.BlockSpec((tile, cols), lambda i: (i, 0)),
                          out_shape=jax.ShapeDtypeStruct((rows, cols), F32), compiler_params=_params(("parallel",)))(buf)


def _adamw(w, g, m, v, tile, name, after=()):
    rows, cols = w.shape

    def body(w_ref, g_ref, m_ref, v_ref, *rest):
        d_ref, mo_ref, vo_ref = rest[len(after):]
        gv = g_ref[...]
        mn = ADAM_B1 * m_ref[...] + (1.0 - ADAM_B1) * gv
        vn = ADAM_B2 * v_ref[...] + (1.0 - ADAM_B2) * (gv * gv)
        m_hat = mn / (1.0 - ADAM_B1 ** ADAM_STEP)
        v_hat = vn / (1.0 - ADAM_B2 ** ADAM_STEP)
        d_ref[...] = -ADAM_LR * (m_hat / (jnp.sqrt(v_hat) + ADAM_EPS) + ADAM_WD * w_ref[...])
        mo_ref[...] = mn
        vo_ref[...] = vn

    spec = pl.BlockSpec((tile, cols), lambda i: (i, 0))
    sds = jax.ShapeDtypeStruct((rows, cols), F32)
    return pl.pallas_call(body, name=name, grid=(rows // tile,), in_specs=[spec] * 4 + [ANY] * len(after), out_specs=[spec] * 3,
                          out_shape=[sds] * 3, compiler_params=_params(("parallel",)))(w, g, m, v, *map(_hbm, after))


def _row_tile(rows, cap):
    best = 8
    for t in range(8, min(rows, cap) + 1, 8):
        if rows % t == 0:
            best = t
    return best


def _pack(arrs):
    parts = []
    for a in arrs:
        r = a.size // 128
        r8 = -(-r // 8) * 8
        parts.append(jnp.pad(a.reshape(r, 128).astype(F32), ((0, r8 - r), (0, 0))))
    return jnp.concatenate(parts, axis=0)


def _unpack(packed, shapes):
    out, r0 = [], 0
    for shp in shapes:
        r = math.prod(shp) // 128
        out.append(packed[r0:r0 + r].reshape(shp))
        r0 += -(-r // 8) * 8
    return out


BIG = {
    "ffn1_w_gu": "col", "ffn1_w_down": "row", "ev_w_in": "col", "ev_w_out": "row", "od_w_in": "row", "od_w_group": "row",
    "od_w_out": "row", "xa_w_q": "row", "xa_w_kv": "col", "xa_w_o": "row", "ffn2_w_gu": "col", "ffn2_w_down": "row",
}
TINY_SHARDED = {"ev_conv_w": 2, "od_scale": 1}
WEIGHTS = ['ffn1_pre_g', 'ffn1_w_gu', 'ffn1_w_down', 'ffn1_post_g', 'mix_pre_g', 'mix_post_g', 'ev_w_in', 'ev_conv_w',
           'ev_conv_b', 'ev_conv_ln_g', 'ev_conv_ln_b', 'ev_sgu_ln_g', 'ev_sgu_ln_b', 'ev_sgu_w', 'ev_sgu_b', 'ev_w_out',
           'od_w_in', 'od_w_group', 'od_scale', 'od_w_out', 'xa_pre_g', 'xa_mem_g', 'xa_w_q', 'xa_w_kv', 'xa_w_o', 'xa_post_g',
           'ffn2_pre_g', 'ffn2_w_gu', 'ffn2_w_down', 'ffn2_post_g']


def _as3d(a):
    return a.reshape((-1,) + a.shape[-2:]) if a.ndim == 4 else a


class _WeightView:
    def __init__(self, store, view):
        self.store, self.view = store, view

    def __getitem__(self, k):
        return self.view(k)


class _Grads:
    def __init__(self):
        self.buf = {}
        self.fresh = []

    def add(self, name, layer, nlayers, a, b, tm, tn, n_outer=True):
        self.buf[name] = _matmul(a, b, "tn", BF16, "dw_" + name, tm, tn, out_l=layer, out_stack=nlayers,
                                 out_buf=self.buf.get(name), n_outer=n_outer)
        self.fresh.append(name)

    def take(self):
        names, self.fresh = self.fresh, []
        return names


def _local_step(x, mem, target, W, S, tiles=None, before_sub=None, after_sub_bwd=None):
    T, TW, tm = tiles or (min(512, S), min(256, S), min(512, S))
    row = lambda v: v.reshape(1, -1)
    subs = []
    small = {k: [None] * W[k].shape[0] for k in WEIGHTS if k not in BIG}
    g = _Grads()

    def ffn_fwd(tag, l, h, n, g_next):
        gu = _matmul(n, W[tag + "_w_gu"], "nn", F32, tag + "_up", tm, 1408, b_l=l)
        a = _swiglu(gu, TW)
        f = _matmul(a, W[tag + "_w_down"], "nn", F32, tag + "_down", tm, D, b_l=l)
        h2, n2 = _post(h, f, row(W[tag + "_post_g"][l]), 0.5, g_next, T)
        subs.append(dict(kind="ffn", tag=tag, l=l, h=h, n=n, gu=gu, a=a, f=f, scale=0.5, pre=tag + "_pre_g", post=tag + "_post_g"))
        return h2, n2

    def ffn_bwd(s, df):
        tag, l = s["tag"], s["l"]
        da = _matmul(df, W[tag + "_w_down"], "nt", F32, tag + "_down_dx", tm, 1408, b_l=l)
        g.add(tag + "_w_down", l, 4, s["a"], df, 256, D, n_outer=False)
        dgu = _swiglu_bwd(da, s["gu"], TW)
        dn = _matmul(dgu, W[tag + "_w_gu"], "nt", F32, tag + "_up_dx", tm, D, b_l=l)
        g.add(tag + "_w_gu", l, 4, s["n"], dgu, D, 512, n_outer=False)
        return dn

    def xa_fwd(l, h, n, g_next):
        q = _matmul(n, W["xa_w_q"], "nn", BF16, "xa_q", tm, D, b_l=l)
        mn = _mem_norm(mem, row(W["xa_mem_g"][l]))
        kv = _matmul(mn, W["xa_w_kv"], "nn", BF16, "xa_kv", NMEM, D, b_l=l)
        o = _attn(q, kv, T)
        cx = _matmul(o, W["xa_w_o"], "nn", F32, "xa_o", tm, D, b_l=l)
        h2, n2 = _post(h, cx, row(W["xa_post_g"][l]), 1.0, g_next, T)
        subs.append(dict(kind="xa", l=l, h=h, n=n, q=q, mn=mn, kv=kv, o=o, f=cx, scale=1.0, pre="xa_pre_g", post="xa_post_g"))
        return h2, n2

    def xa_bwd(s, dc):
        l = s["l"]
        do = _matmul(dc, W["xa_w_o"], "nt", BF16, "xa_o_dx", tm, D, b_l=l)
        g.add("xa_w_o", l, 4, s["o"], dc, D, 512)
        dq, dkv = _attn_bwd(s["q"], s["kv"], do, T)
        dn = _matmul(dq, W["xa_w_q"], "nt", F32, "xa_q_dx", tm, D, b_l=l)
        g.add("xa_w_q", l, 4, s["n"], dq, D, 512)
        dkvb = dkv.astype(BF16)
        g.add("xa_w_kv", l, 4, s["mn"], dkvb, D, 512)
        dmn = _matmul(dkvb, W["xa_w_kv"], "nt", F32, "xa_kv_dx", NMEM, D, b_l=l)
        small["xa_mem_g"][l] = _mem_gain_bwd(mem, dmn)[0]
        return dn

    def even_params(e):
        return (W["ev_conv_w"][e], row(W["ev_conv_b"][e]), row(W["ev_conv_ln_g"][e]), row(W["ev_conv_ln_b"][e]),
                row(W["ev_sgu_ln_g"][e]), row(W["ev_sgu_ln_b"][e]), W["ev_sgu_w"][e])

    def even_fwd(l, h, n, g_next):
        e = l // 2
        p = _matmul(n, W["ev_w_in"], "nn", F32, "ev_in", tm, D, b_l=e)
        cw, cb, clg, clb, slg, slb, ws = even_params(e)
        y, z = _even_core(p, cw, cb, clg, clb, slg, slb, ws, W["ev_sgu_b"][e].T, TW)
        m = _matmul(y, W["ev_w_out"], "nn", F32, "ev_out", tm, D, b_l=e)
        h2, n2 = _post(h, m, row(W["mix_post_g"][l]), 1.0, g_next, T)
        subs.append(dict(kind="even", l=l, h=h, n=n, p=p, y=y, z=z, f=m, scale=1.0, pre="mix_pre_g", post="mix_post_g"))
        return h2, n2

    def even_bwd(s, dm):
        l = s["l"]
        e = l // 2
        dy = _matmul(dm, W["ev_w_out"], "nt", F32, "ev_out_dx", tm, D, b_l=e)
        g.add("ev_w_out", e, 2, s["y"], dm, D, 512)
        cw, cb, clg, clb, slg, slb, ws = even_params(e)
        dconv, dpb, vecs, dws, dbst = _even_bwd_a(s["p"], s["z"], dy, cw, cb, clg, clb, slg, slb, ws,
                                                   jnp.swapaxes(ws, 1, 2), W["ev_sgu_b"][e].T, TW)
        dp, dcw = _even_bwd_b(s["p"], s["z"], dconv, dpb, cw, TW)
        for r, name in enumerate(["ev_conv_b", "ev_conv_ln_g", "ev_conv_ln_b", "ev_sgu_ln_g", "ev_sgu_ln_b"]):
            small[name][e] = vecs[r]
        small["ev_sgu_w"][e] = dws
        small["ev_sgu_b"][e] = dbst.T
        small["ev_conv_w"][e] = dcw[:CONV_W]
        dn = _matmul(dp, W["ev_w_in"], "nt", F32, "ev_in_dx", tm, D, b_l=e)
        g.add("ev_w_in", e, 2, s["n"], dp, D, 512)
        return dn

    def group_w(o):
        ng = len(POOL_WINDOWS)
        return (W["od_w_group"].reshape(-1, POOL_GD, POOL_GD), (ng, POOL_GD, POOL_GD), lambda i: (o, 0, 0))

    def odd_fwd(l, h, n, g_next):
        o = l // 2
        p = _matmul(n, W["od_w_in"], "nn", F32, "od_in", tm, D, b_l=o)
        d, e, es = _odd_core(p, group_w(o), row(W["od_scale"][o]), T)
        m = _matmul(es, W["od_w_out"], "nn", F32, "od_out", tm, D, b_l=o)
        h2, n2 = _post(h, m, row(W["mix_post_g"][l]), 1.0, g_next, T)
        subs.append(dict(kind="odd", l=l, h=h, n=n, d=d, e=e, es=es, f=m, scale=1.0, pre="mix_pre_g", post="mix_post_g"))
        return h2, n2

    def odd_bwd(s, dm):
        l = s["l"]
        o = l // 2
        des = _matmul(dm, W["od_w_out"], "nt", F32, "od_out_dx", tm, D, b_l=o)
        g.add("od_w_out", o, 2, s["es"], dm, D, 512)
        dp, dsc, dwg = _odd_bwd(des, s["e"], s["d"], group_w(o), row(W["od_scale"][o]), T)
        small["od_scale"][o] = dsc[0]
        small["od_w_group_full"][o] = dwg
        dn = _matmul(dp, W["od_w_in"], "nt", F32, "od_in_dx", tm, D, b_l=o)
        g.add("od_w_in", o, 2, s["n"], dp, D, 512)
        return dn

    small["od_w_group_full"] = [None, None]
    order = []
    for l in range(4):
        order += [("ffn1", l), ("even" if l % 2 == 0 else "odd", l), ("xa", l), ("ffn2", l)]
    pre_of = {"ffn1": "ffn1_pre_g", "even": "mix_pre_g", "odd": "mix_pre_g", "xa": "xa_pre_g", "ffn2": "ffn2_pre_g"}
    h = x
    n = _prenorm(h, row(W["ffn1_pre_g"][0]), T)
    for idx, (kind, l) in enumerate(order):
        if before_sub is not None:
            before_sub(kind, l, h)
        g_next = row(W[pre_of[order[idx + 1][0]]][order[idx + 1][1]]) if idx + 1 < len(order) else None
        if kind in ("ffn1", "ffn2"):
            h, n = ffn_fwd(kind, l, h, n, g_next)
        elif kind == "xa":
            h, n = xa_fwd(l, h, n, g_next)
        elif kind == "even":
            h, n = even_fwd(l, h, n, g_next)
        else:
            h, n = odd_fwd(l, h, n, g_next)

    top = subs[-1]
    dh, df, loss_acc, dgp = _loss_top(h, target, top["f"], row(W[top["post"]][top["l"]]), top["scale"], T)
    small[top["post"]][top["l"]] = dgp[0]
    for idx in range(len(subs) - 1, -1, -1):
        s = subs[idx]
        dn = {"ffn": ffn_bwd, "xa": xa_bwd, "even": even_bwd, "odd": odd_bwd}[s["kind"]](s, df)
        names = g.take()
        deps = [g.buf[k] for k in names]
        if idx > 0:
            sp = subs[idx - 1]
            dh, df, dg_pre, dg_post = _boundary_bwd(s["h"], row(W[s["pre"]][s["l"]]), dn, dh, sp["f"],
                                                    row(W[sp["post"]][sp["l"]]), sp["scale"], T, deps=deps)
            small[sp["post"]][sp["l"]] = dg_post[0]
        else:
            dh, dg_pre = _boundary_bwd(s["h"], row(W[s["pre"]][s["l"]]), dn, dh, None, None, None, T, deps=deps)
        small[s["pre"]][s["l"]] = dg_pre[0]
        if after_sub_bwd is not None:
            after_sub_bwd(order[idx][0], s["l"], g.buf, small)
            g.fresh = names + g.fresh
    return loss_acc[0, 0], dh, g.buf, small


def _step(P, S):
    x, mem, target = P["x"][0], P["mem"][0], P["loss_target"][0]
    chip = 2 * lax.axis_index("x") + lax.axis_index("y")
    pos = jnp.stack([chip, lax.axis_index("c")]).astype(jnp.int32)
    order = list(BIG)
    tix = {k: t for t, k in enumerate(order)}
    kinds = [BIG[k] for k in order]
    per = {k: (4 if k == "od_w_group" else 1) for k in order}
    members = {"ffn1": ["ffn1_w_gu", "ffn1_w_down"], "ffn2": ["ffn2_w_gu", "ffn2_w_down"], "xa": ["xa_w_q", "xa_w_kv", "xa_w_o"],
               "even": ["ev_w_in", "ev_w_out"], "odd": ["od_w_in", "od_w_group", "od_w_out"]}
    index_of = lambda kind, l: l // 2 if kind in ("even", "odd") else l
    sub_order = []
    for l in range(4):
        sub_order += [("ffn1", l), ("even" if l % 2 == 0 else "odd", l), ("xa", l), ("ffn2", l)]
    groups = {(kind, l): [(tix[k], index_of(kind, l), index_of(kind, l) * per[k], per[k]) for k in members[kind]]
              for kind, l in sub_order}
    tiny = _gather_tiny([P[k] for k in TINY_SHARDED], list(TINY_SHARDED.values()))
    slots = [_cast_into_slot(pos, _as3d(P[k]), BIG[k], "cast_" + k) for k in order]
    g_send, g_recv, bufs = _gather_start(slots, kinds, [groups[s_] for s_ in sub_order], tiny)
    W = {k: P[k] for k in WEIGHTS if k not in BIG}
    W.update(zip(TINY_SHARDED, tiny))
    W.update(zip(order, bufs))

    def view(k):
        return W[k].reshape(2, 4, POOL_GD, POOL_GD) if k == "od_w_group" else W[k]

    Wv = _WeightView(W, view)

    def before_sub(kind, l, h):
        ks = members[kind]
        arrived = _gather_wait([W[k] for k in ks], [BIG[k] for k in ks], groups[(kind, l)], g_send, g_recv, h,
                               "gather_wait_%s%d" % (kind, l))
        W.update(zip(ks, arrived))

    recv = {k: None for k in order}
    pending = []
    small_names = [k for k in WEIGHTS if k not in BIG]
    gsmall = {}

    def after_sub_bwd(kind, l, big, small):
        ks = [k for k in members[kind] if k != "od_w_group"]
        items = [(index_of(kind, l), 1)] * len(ks)
        after = ()
        if (kind, l) == ("odd", 1):
            big["od_w_group"] = jnp.stack(small.pop("od_w_group_full")).astype(BF16).reshape(8, POOL_GD, POOL_GD)
            ks, items = ks + ["od_w_group"], items + [(0, 8)]
        if (kind, l) == ("ffn1", 0):
            small_full = [jnp.stack(small[k]) for k in small_names]
            packed = _pack(small_full)
            summed = _sum_slots(_share_small(packed), _row_tile(packed.shape[0], 512), "sum_small")
            gsmall.update(zip(small_names, _unpack(summed, [a.shape for a in small_full])))
            after = (summed,)
        for k in ks:
            if recv[k] is None:
                recv[k] = lax.empty((7,) + _half_geometry(big[k].shape, BIG[k]), BF16)
        s_send, s_recv, g_thru, r_thru = _scatter_start([big[k] for k in ks], [recv[k] for k in ks], [BIG[k] for k in ks],
                                                        items, "scatter_start_%s%d" % (kind, l), after)
        big.update(zip(ks, g_thru))
        recv.update(zip(ks, r_thru))
        pending.append((kind, l, ks, items, s_send, s_recv))

    loss, grad_x, big, small = _local_step(x, mem, target, Wv, S, before_sub=before_sub, after_sub_bwd=after_sub_bwd)
    last = (big[members["ffn1"][0]],)
    for kind, l, ks, items, s_send, s_recv in pending:
        recv.update(zip(ks, _scatter_wait([recv[k] for k in ks], items, s_send, s_recv, "scatter_wait_%s%d" % (kind, l), last)))
    gsh = dict(zip(order, _swap_halves([_sum_into_half(pos, recv[k], big[k], BIG[k], "sum_" + k) for k in order])))
    for k, ax in TINY_SHARDED.items():
        size = P[k].shape[ax]
        gsmall[k] = lax.dynamic_slice_in_dim(gsmall[k], chip * size, size, axis=ax)

    grads, delta, new_m, new_v = {}, {}, {}, {}
    for k in order:
        shp = P[k].shape
        cols = shp[-1]
        flat = lambda a: a.reshape(-1, cols)
        gk = flat(gsh[k])
        d_, m_, v_ = _adamw(flat(P[k]), gk, flat(P["m_" + k]), flat(P["v_" + k]), _row_tile(gk.shape[0], 256), "adamw_" + k)
        grads[k], delta[k], new_m[k], new_v[k] = gk.reshape(shp), d_.reshape(shp), m_.reshape(shp), v_.reshape(shp)
    pk = lambda pre: _pack([P[pre + k] for k in small_names])
    d_, m_, v_ = _adamw(pk(""), _pack([gsmall[k] for k in small_names]), pk("m_"), pk("v_"),
                        _row_tile(pk("").shape[0], 256), "adamw_small", last)
    shapes = [P[k].shape for k in small_names]
    for dst, src in ((delta, d_), (new_m, m_), (new_v, v_)):
        dst.update(zip(small_names, _unpack(src, shapes)))
    grads.update(gsmall)

    loss = lax.psum(loss, ("x", "y", "c"))
    out = [loss, grad_x[None]]
    for grp in (grads, delta, new_m, new_v):
        out += [grp[k] for k in WEIGHTS]
    return tuple(out)


def kernel(x, mem, ffn1_pre_g, ffn1_w_gu, ffn1_w_down, ffn1_post_g, mix_pre_g, mix_post_g, ev_w_in, ev_conv_w, ev_conv_b, ev_conv_ln_g, ev_conv_ln_b, ev_sgu_ln_g, ev_sgu_ln_b, ev_sgu_w, ev_sgu_b, ev_w_out, od_w_in, od_w_group, od_scale, od_w_out, xa_pre_g, xa_mem_g, xa_w_q, xa_w_kv, xa_w_o, xa_post_g, ffn2_pre_g, ffn2_w_gu, ffn2_w_down, ffn2_post_g, loss_target, m_ffn1_pre_g, m_ffn1_w_gu, m_ffn1_w_down, m_ffn1_post_g, m_mix_pre_g, m_mix_post_g, m_ev_w_in, m_ev_conv_w, m_ev_conv_b, m_ev_conv_ln_g, m_ev_conv_ln_b, m_ev_sgu_ln_g, m_ev_sgu_ln_b, m_ev_sgu_w, m_ev_sgu_b, m_ev_w_out, m_od_w_in, m_od_w_group, m_od_scale, m_od_w_out, m_xa_pre_g, m_xa_mem_g, m_xa_w_q, m_xa_w_kv, m_xa_w_o, m_xa_post_g, m_ffn2_pre_g, m_ffn2_w_gu, m_ffn2_w_down, m_ffn2_post_g, v_ffn1_pre_g, v_ffn1_w_gu, v_ffn1_w_down, v_ffn1_post_g, v_mix_pre_g, v_mix_post_g, v_ev_w_in, v_ev_conv_w, v_ev_conv_b, v_ev_conv_ln_g, v_ev_conv_ln_b, v_ev_sgu_ln_g, v_ev_sgu_ln_b, v_ev_sgu_w, v_ev_sgu_b, v_ev_w_out, v_od_w_in, v_od_w_group, v_od_scale, v_od_w_out, v_xa_pre_g, v_xa_mem_g, v_xa_w_q, v_xa_w_kv, v_xa_w_o, v_xa_post_g, v_ffn2_pre_g, v_ffn2_w_gu, v_ffn2_w_down, v_ffn2_post_g):
    P = dict(locals())
    return _step(P, x.shape[1])
```

```python
import functools
import math

import jax
import jax.numpy as jnp
from jax import lax
from jax.experimental import pallas as pl
from jax.experimental.pallas import tpu as pltpu

F32 = jnp.float32
BF16 = jnp.bfloat16
MESH = pl.DeviceIdType.MESH
ANY = pl.BlockSpec(memory_space=pl.ANY)
HBM = pl.BlockSpec(memory_space=pltpu.HBM)


def _hbm(x):
    return pltpu.with_memory_space_constraint(x, pltpu.HBM)

D = 1024
DFF = 2816
NMEM = 256
DC = 512
CONV_W = 31
CHUNK = 128
NHEAD_SGU = 4
POOL_WINDOWS = (2, 4, 8, 16)
POOL_GD = 256
XA_HEADS = 4
XA_HD = 256
EPS = 1e-6
NCHIP = 4
HALO = 32

ADAM_LR, ADAM_B1, ADAM_B2, ADAM_EPS, ADAM_WD, ADAM_STEP = 0.001, 0.9, 0.999, 1e-08, 0.01, 10

V7X_VMEM_BYTES = 64 * 1024 * 1024
VMEM_LIMIT = V7X_VMEM_BYTES - 8 * 1024 * 1024


def _params(sem):
    return pltpu.CompilerParams(dimension_semantics=sem, vmem_limit_bytes=VMEM_LIMIT)


def _matmul(a, b, kind, out_dtype, name, tm, tn, a_l=None, b_l=None, out_l=None, out_stack=None, out_buf=None,
            n_outer=True):
    a2, b2 = a.shape[-2:], b.shape[-2:]
    if kind == "nn":
        (m, k), (k2, n) = a2, b2
        dims = (((1,), (0,)), ((), ()))
    elif kind == "nt":
        (m, k), (n, k2) = a2, b2
        dims = (((1,), (1,)), ((), ()))
    else:
        (k, m), (k2, n) = a2, b2
        dims = (((0,), (0,)), ((), ()))
    assert k == k2 and m % tm == 0 and n % tn == 0, (name, a.shape, b.shape, tm, tn)
    if n_outer:
        grid = (n // tn, m // tm)
        ij = lambda p, q: (q, p)
    else:
        grid = (m // tm, n // tn)
        ij = lambda p, q: (p, q)

    def spec(arr, layer, blk, idx):
        if arr.ndim == 3:
            return pl.BlockSpec((None,) + blk, lambda p, q: (layer,) + idx(*ij(p, q)))
        return pl.BlockSpec(blk, lambda p, q: idx(*ij(p, q)))

    a_spec = spec(a, a_l, (k, tm) if kind == "tn" else (tm, k), (lambda i, j: (0, i)) if kind == "tn" else (lambda i, j: (i, 0)))
    b_spec = spec(b, b_l, (tn, k) if kind == "nt" else (k, tn), (lambda i, j: (j, 0)) if kind == "nt" else (lambda i, j: (0, j)))
    if out_stack is None:
        out_shape = jax.ShapeDtypeStruct((m, n), out_dtype)
        o_spec = pl.BlockSpec((tm, tn), lambda p, q: ij(p, q))
    else:
        out_shape = jax.ShapeDtypeStruct((out_stack, m, n), out_dtype)
        o_spec = pl.BlockSpec((None, tm, tn), lambda p, q: (out_l,) + ij(p, q))

    def body(a_ref, b_ref, *rest):
        o_ref = rest[-1]
        o_ref[...] = lax.dot_general(a_ref[...], b_ref[...], dims, preferred_element_type=F32).astype(o_ref.dtype)

    in_specs, args, aliases = [a_spec, b_spec], [a, _hbm(b) if b.ndim == 3 else b], {}
    if out_buf is not None:
        in_specs.append(ANY)
        args.append(_hbm(out_buf))
        aliases = {2: 0}
    return pl.pallas_call(body, name=name, grid=grid, in_specs=in_specs, out_specs=o_spec, out_shape=out_shape,
                          input_output_aliases=aliases, compiler_params=_params(("parallel", "parallel")))(*args)


def _rowwise(fn, name, rows, tile, row_ins, full_ins, row_outs, acc_outs=(), scratch=(), deps=()):
    assert rows % tile == 0, (name, rows, tile)
    in_specs, args = [], []
    for r in row_ins:
        if isinstance(r, tuple):
            arr, br, bc, imap = r
            in_specs.append(pl.BlockSpec((br, bc), imap))
        else:
            arr = r
            in_specs.append(pl.BlockSpec((tile, arr.shape[1]), lambda i: (i, 0)))
        args.append(arr)
    for f in full_ins:
        if isinstance(f, tuple):
            arr, blk, imap = f
            in_specs.append(pl.BlockSpec(blk, imap, pipeline_mode=pl.Buffered(1)))
            args.append(_hbm(arr))
        else:
            in_specs.append(pl.BlockSpec(f.shape, functools.partial(lambda nd, i: (0,) * nd, f.ndim)))
            args.append(f)
    for d in deps:
        in_specs.append(ANY)
        args.append(_hbm(d))
    out_specs, out_shape = [], []
    for w, dt in row_outs:
        out_specs.append(pl.BlockSpec((tile, w), lambda i: (i, 0)))
        out_shape.append(jax.ShapeDtypeStruct((rows, w), dt))
    for shp, dt in acc_outs:
        out_specs.append(pl.BlockSpec(shp, functools.partial(lambda nd, i: (0,) * nd, len(shp))))
        out_shape.append(jax.ShapeDtypeStruct(shp, dt))
    n_in, n_out = len(args), len(out_shape)

    def body(*refs):
        fn(pl.program_id(0), refs[:n_in], refs[n_in:n_in + n_out], refs[n_in + n_out:])

    sem = ("arbitrary",) if acc_outs else ("parallel",)
    res = pl.pallas_call(body, name=name, grid=(rows // tile,), in_specs=in_specs, out_specs=out_specs, out_shape=out_shape,
                         scratch_shapes=list(scratch), compiler_params=_params(sem))(*args)
    return res


def _accum(ref, i, val):
    @pl.when(i == 0)
    def _():
        ref[...] = val

    @pl.when(i > 0)
    def _():
        ref[...] += val


def _prev_halo(arr, tile, cols=None):
    r = tile // HALO
    return (arr, HALO, cols or arr.shape[1], lambda i: (jnp.maximum(i * r - 1, 0), 0))


def _next_halo(arr, tile, cols=None):
    r = tile // HALO
    last = arr.shape[0] // HALO - 1
    return (arr, HALO, cols or arr.shape[1], lambda i: (jnp.minimum((i + 1) * r, last), 0))


def _sigmoid(x):
    return 1.0 / (1.0 + jnp.exp(-x))


def _rms_hat(x):
    r = lax.rsqrt(jnp.mean(x * x, axis=-1, keepdims=True) + EPS)
    return x * r, r


def _rms_bwd(x, g, dy):
    xhat, r = _rms_hat(x)
    dxhat = dy * g
    dx = r * (dxhat - xhat * jnp.mean(dxhat * xhat, axis=-1, keepdims=True))
    return dx, jnp.sum(dy * xhat, axis=0, keepdims=True)


def _ln_hat(x):
    mu = jnp.mean(x, axis=-1, keepdims=True)
    xc = x - mu
    r = lax.rsqrt(jnp.mean(xc * xc, axis=-1, keepdims=True) + EPS)
    return xc * r, r


def _ln_bwd(xhat, r, g, dy):
    dxhat = dy * g
    dx = r * (dxhat - jnp.mean(dxhat, axis=-1, keepdims=True) - xhat * jnp.mean(dxhat * xhat, axis=-1, keepdims=True))
    return dx, jnp.sum(dy * xhat, axis=0, keepdims=True), jnp.sum(dy, axis=0, keepdims=True)


def _silu_grad(x):
    s = _sigmoid(x)
    return s * (1.0 + x * (1.0 - s))


_SQRT_HALF = math.sqrt(0.5)
_INV_SQRT_2PI = 1.0 / math.sqrt(2.0 * math.pi)


def _gelu(x):
    return 0.5 * x * (1.0 + lax.erf(x * _SQRT_HALF))


def _gelu_grad(x):
    return 0.5 * (1.0 + lax.erf(x * _SQRT_HALF)) + x * jnp.exp(-0.5 * x * x) * _INV_SQRT_2PI


def _dot(a, b, kind="nn"):
    dims = {"nn": (((1,), (0,)), ((), ())), "nt": (((1,), (1,)), ((), ())), "tn": (((0,), (0,)), ((), ()))}[kind]
    return lax.dot_general(a, b, dims, preferred_element_type=F32)


def _prenorm(h, g, tile):
    def fn(i, ins, outs, _):
        h_ref, g_ref = ins
        outs[0][...] = (_rms_hat(h_ref[...])[0] * g_ref[...]).astype(BF16)

    return _rowwise(fn, "prenorm", h.shape[0], tile, [h], [g], [(D, BF16)])[0]


def _loss_top(h, target, f, g_post, scale, tile):
    def fn(i, ins, outs, _):
        err = ins[0][...] - ins[1][...]
        per_row = jnp.mean(err * err, axis=-1, keepdims=True)
        _accum(outs[2], i, jnp.broadcast_to(0.5 * jnp.sum(per_row, axis=0, keepdims=True), (1, 128)))
        dh = err * (1.0 / D)
        outs[0][...] = dh
        df, dg = _rms_bwd(ins[2][...], ins[3][...], scale * dh)
        outs[1][...] = df.astype(BF16)
        _accum(outs[3], i, dg)

    return _rowwise(fn, "loss_top", h.shape[0], tile, [h, target, f], [g_post], [(D, F32), (D, BF16)],
                    [((1, 128), F32), ((1, D), F32)])


def _layer_of(w3, l):
    return (w3, (None,) + tuple(w3.shape[1:]), lambda i: (l, 0, 0))


def _post_mm(h, a, w3, l, g_post, scale, g_next, tile):
    def fn(i, ins, outs, _):
        f = _dot(ins[1][...], ins[2][...])
        outs[0][...] = f
        hn = ins[0][...] + scale * (_rms_hat(f)[0] * ins[3][...])
        outs[1][...] = hn
        if g_next is not None:
            outs[2][...] = (_rms_hat(hn)[0] * ins[4][...]).astype(BF16)

    fulls = [_layer_of(w3, l), g_post] + ([g_next] if g_next is not None else [])
    outs = [(D, F32), (D, F32)] + ([(D, BF16)] if g_next is not None else [])
    res = _rowwise(fn, "post_mm", h.shape[0], tile, [h, a], fulls, outs)
    return res[0], res[1], (res[2] if g_next is not None else None)


def _boundary_mm(a, w3, l, h, g_pre, dh_in, f_prev, g_post_prev, scale_prev, tile, deps=()):
    has_prev = f_prev is not None

    def fn(i, ins, outs, _):
        a_ref, h_ref, dhin_ref = ins[:3]
        w_ref, g_ref = ins[3 + has_prev], ins[4 + has_prev]
        dn = _dot(a_ref[...], w_ref[...], "nt")
        dx, dg = _rms_bwd(h_ref[...], g_ref[...], dn)
        dh = dhin_ref[...] + dx
        outs[0][...] = dh
        _accum(outs[1 + has_prev], i, dg)
        if has_prev:
            df, dgp = _rms_bwd(ins[3][...], ins[6][...], scale_prev * dh)
            outs[1][...] = df.astype(BF16)
            _accum(outs[3], i, dgp)

    rows = [a, h, dh_in] + ([f_prev] if has_prev else [])
    fulls = [_layer_of(w3, l), g_pre] + ([g_post_prev] if has_prev else [])
    outs = [(D, F32)] + ([(D, BF16)] if has_prev else [])
    accs = [((1, D), F32)] + ([((1, D), F32)] if has_prev else [])
    return _rowwise(fn, "boundary_mm", h.shape[0], tile, rows, fulls, outs, accs, deps=deps)


FF_CHUNK = DFF // 2


def _up_swiglu(n, w3, l, tile):
    def fn(i, ins, outs, _):
        n_ref, w_ref = ins
        nv = n_ref[...]
        for c0 in range(0, DFF, FF_CHUNK):
            g = _dot(nv, w_ref[:, c0:c0 + FF_CHUNK])
            u = _dot(nv, w_ref[:, DFF + c0:DFF + c0 + FF_CHUNK])
            outs[0][:, c0:c0 + FF_CHUNK] = g.astype(BF16)
            outs[0][:, DFF + c0:DFF + c0 + FF_CHUNK] = u.astype(BF16)
            outs[1][:, c0:c0 + FF_CHUNK] = (g * _sigmoid(g) * u).astype(BF16)

    return _rowwise(fn, "up_swiglu", n.shape[0], tile, [n], [_layer_of(w3, l)], [(2 * DFF, BF16), (DFF, BF16)])


def _down_dx_swiglu_bwd(df, gu, w3, l, tile):
    def fn(i, ins, outs, _):
        df_ref, gu_ref, w_ref = ins
        dfv = df_ref[...]
        for c0 in range(0, DFF, FF_CHUNK):
            da = _dot(dfv, w_ref[c0:c0 + FF_CHUNK, :], "nt")
            g = gu_ref[:, c0:c0 + FF_CHUNK].astype(F32)
            u = gu_ref[:, DFF + c0:DFF + c0 + FF_CHUNK].astype(F32)
            outs[0][:, c0:c0 + FF_CHUNK] = (da * u * _silu_grad(g)).astype(BF16)
            outs[0][:, DFF + c0:DFF + c0 + FF_CHUNK] = (da * g * _sigmoid(g)).astype(BF16)

    return _rowwise(fn, "down_dx_swiglu_bwd", df.shape[0], tile, [df, gu], [_layer_of(w3, l)], [(2 * DFF, BF16)])[0]


def _softmax_rows(s):
    e = jnp.exp(s - jnp.max(s, axis=-1, keepdims=True))
    return e / jnp.sum(e, axis=-1, keepdims=True)


def _attn(q, kv, tile):
    def fn(i, ins, outs, _):
        q_ref, kv_ref = ins
        for hd in range(XA_HEADS):
            c0 = hd * XA_HD
            p = _softmax_rows(_dot(q_ref[:, c0:c0 + XA_HD], kv_ref[:, c0:c0 + XA_HD], "nt") * (XA_HD ** -0.5))
            outs[0][:, c0:c0 + XA_HD] = _dot(p.astype(BF16), kv_ref[:, D + c0:D + c0 + XA_HD]).astype(BF16)

    return _rowwise(fn, "attn", q.shape[0], tile, [q], [kv], [(D, BF16)])[0]


def _attn_bwd(q, kv, do, tile):
    def fn(i, ins, outs, _):
        q_ref, do_ref, kv_ref = ins
        for hd in range(XA_HEADS):
            c0 = hd * XA_HD
            qh, kh, vh = q_ref[:, c0:c0 + XA_HD], kv_ref[:, c0:c0 + XA_HD], kv_ref[:, D + c0:D + c0 + XA_HD]
            doh = do_ref[:, c0:c0 + XA_HD]
            p = _softmax_rows(_dot(qh, kh, "nt") * (XA_HD ** -0.5))
            dp = _dot(doh, vh, "nt")
            ds = (p * (dp - jnp.sum(dp * p, axis=-1, keepdims=True)) * (XA_HD ** -0.5)).astype(BF16)
            outs[0][:, c0:c0 + XA_HD] = _dot(ds, kh).astype(BF16)
            dk = _dot(ds, qh, "tn")
            dv = _dot(p.astype(BF16), doh, "tn")

            @pl.when(i == 0)
            def _():
                outs[1][:, c0:c0 + XA_HD] = dk
                outs[1][:, D + c0:D + c0 + XA_HD] = dv

            @pl.when(i > 0)
            def _():
                outs[1][:, c0:c0 + XA_HD] += dk
                outs[1][:, D + c0:D + c0 + XA_HD] += dv

    return _rowwise(fn, "attn_bwd", q.shape[0], tile, [q, do], [kv], [(D, BF16)], [((NMEM, 2 * D), F32)])


def _mem_norm(mem, g):
    def fn(i, ins, outs, _):
        outs[0][...] = (_rms_hat(ins[0][...])[0] * ins[1][...]).astype(BF16)

    return _rowwise(fn, "mem_norm", NMEM, NMEM, [mem], [g], [(D, BF16)])[0]


def _mem_gain_bwd(mem, dmn):
    def fn(i, ins, outs, _):
        outs[0][...] = jnp.sum(ins[1][...] * _rms_hat(ins[0][...])[0], axis=0, keepdims=True)

    return _rowwise(fn, "mem_gain_bwd", NMEM, NMEM, [mem, dmn], [], [], [((1, D), F32)])[0]


def _conv_taps(w_ref, zext_ref, tile, shift0):
    acc = None
    for k in range(CONV_W):
        term = w_ref[k:k + 1, :] * zext_ref[pl.ds(shift0(k), tile), :]
        acc = term if acc is None else acc + term
    return acc


def _tril_mask(transpose=False):
    r, c = (lax.broadcasted_iota(jnp.int32, (CHUNK, CHUNK), a) for a in (0, 1))
    return (r <= c) if transpose else (r >= c)


def _even_core(p, cw, cb, clg, clb, slg, slb, ws, bst, tile):
    def fn(i, ins, outs, scr):
        p_ref, ph_ref, cw_ref, cb_ref, clg_ref, clb_ref, slg_ref, slb_ref, ws_ref, bst_ref = ins
        y_ref, z_ref = outs
        zext = scr[0]
        z = p_ref[:, :DC] * _sigmoid(p_ref[:, DC:2 * DC])
        zh = ph_ref[:, :DC] * _sigmoid(ph_ref[:, DC:2 * DC])
        zext[:HALO, :] = jnp.where(i > 0, zh, 0.0)
        zext[HALO:, :] = z
        z_ref[...] = z
        conv = _conv_taps(cw_ref, zext, tile, lambda k: HALO - (CONV_W - 1) + k) + cb_ref[...]
        yl = _ln_hat(conv)[0] * clg_ref[...] + clb_ref[...]
        y_ref[:, :DC] = (yl * _sigmoid(yl)).astype(BF16)
        zb = _gelu(p_ref[:, 2 * DC:])
        vln = (_ln_hat(zb[:, DC:])[0] * slg_ref[...] + slb_ref[...]).astype(BF16)
        mask = _tril_mask()
        for hd in range(NHEAD_SGU):
            wm = jnp.where(mask, ws_ref[hd], 0.0).astype(BF16)
            for ci in range(tile // CHUNK):
                r0, c0 = ci * CHUNK, hd * CHUNK
                mixed = _dot(wm, vln[r0:r0 + CHUNK, c0:c0 + CHUNK]) + bst_ref[:, hd:hd + 1]
                y_ref[r0:r0 + CHUNK, DC + c0:DC + c0 + CHUNK] = (zb[r0:r0 + CHUNK, c0:c0 + CHUNK] * mixed).astype(BF16)

    return _rowwise(fn, "even_core", p.shape[0], tile, [p, _prev_halo(p, tile, D)], [cw, cb, clg, clb, slg, slb, ws, bst],
                    [(D, BF16), (DC, F32)], scratch=[pltpu.VMEM((tile + HALO, DC), F32)])


def _even_bwd_a(p, z, dy, cw, cb, clg, clb, slg, slb, ws, wst, bst, tile):
    def fn(i, ins, outs, scr):
        p_ref, z_ref, zh_ref, dy_ref, cw_ref, cb_ref, clg_ref, clb_ref, slg_ref, slb_ref, ws_ref, wst_ref, bst_ref = ins
        dc_ref, dpb_ref, vec_ref, dws_ref, dbst_ref = outs
        zext, dvln_s = scr
        zext[:HALO, :] = jnp.where(i > 0, zh_ref[...], 0.0)
        zext[HALO:, :] = z_ref[...]
        conv = _conv_taps(cw_ref, zext, tile, lambda k: HALO - (CONV_W - 1) + k) + cb_ref[...]
        chat, cr = _ln_hat(conv)
        yl = chat * clg_ref[...] + clb_ref[...]
        dyl = dy_ref[:, :DC] * _silu_grad(yl)
        dconv, dclg, dclb = _ln_bwd(chat, cr, clg_ref[...], dyl)
        dc_ref[...] = dconv
        xb = p_ref[:, 2 * DC:]
        zb = _gelu(xb)
        vhat, vr = _ln_hat(zb[:, DC:])
        vln = (vhat * slg_ref[...] + slb_ref[...]).astype(BF16)
        dyb = dy_ref[:, DC:]
        mask, mask_t = _tril_mask(), _tril_mask(transpose=True)
        dbu_parts = []
        for hd in range(NHEAD_SGU):
            wm = jnp.where(mask, ws_ref[hd], 0.0).astype(BF16)
            wt = jnp.where(mask_t, wst_ref[hd], 0.0).astype(BF16)
            dws_h, dbs_h, rows = None, None, []
            for ci in range(tile // CHUNK):
                r0, c0 = ci * CHUNK, hd * CHUNK
                vblk = vln[r0:r0 + CHUNK, c0:c0 + CHUNK]
                mixed = _dot(wm, vblk) + bst_ref[:, hd:hd + 1]
                dyb_blk = dyb[r0:r0 + CHUNK, c0:c0 + CHUNK]
                rows.append(dyb_blk * mixed)
                dmixed = dyb_blk * zb[r0:r0 + CHUNK, c0:c0 + CHUNK]
                dmb = dmixed.astype(BF16)
                dvln_s[r0:r0 + CHUNK, c0:c0 + CHUNK] = _dot(wt, dmb)
                dw = _dot(dmb, vblk, "nt")
                db = jnp.sum(dmixed, axis=-1, keepdims=True)
                dws_h = dw if dws_h is None else dws_h + dw
                dbs_h = db if dbs_h is None else dbs_h + db
            dbu_parts.append(jnp.concatenate(rows, axis=0))
            dws_h = jnp.where(mask, dws_h, 0.0)

            @pl.when(i == 0)
            def _():
                dws_ref[hd] = dws_h
                dbst_ref[:, hd:hd + 1] = dbs_h

            @pl.when(i > 0)
            def _():
                dws_ref[hd] += dws_h
                dbst_ref[:, hd:hd + 1] += dbs_h

        dbu = jnp.concatenate(dbu_parts, axis=1)
        dbv, dslg, dslb = _ln_bwd(vhat, vr, slg_ref[...], dvln_s[...])
        dpb_ref[:, :DC] = (dbu * _gelu_grad(xb[:, :DC])).astype(BF16)
        dpb_ref[:, DC:] = (dbv * _gelu_grad(xb[:, DC:])).astype(BF16)
        @pl.when(i == 0)
        def _():
            vec_ref[...] = jnp.zeros_like(vec_ref)

        for r, val in enumerate([jnp.sum(dconv, axis=0, keepdims=True), dclg, dclb, dslg, dslb]):
            vec_ref[r:r + 1, :] += val

    return _rowwise(fn, "even_bwd_a", p.shape[0], tile, [p, z, _prev_halo(z, tile), dy],
                    [cw, cb, clg, clb, slg, slb, ws, wst, bst], [(DC, F32), (D, BF16)],
                    [((8, DC), F32), ((NHEAD_SGU, CHUNK, CHUNK), F32), ((CHUNK, NHEAD_SGU), F32)],
                    scratch=[pltpu.VMEM((tile + HALO, DC), F32), pltpu.VMEM((tile, DC), F32)])


def _even_bwd_b(p, z, dconv, dpb, cw, tile):
    def fn(i, ins, outs, scr):
        p_ref, z_ref, zh_ref, dc_ref, dcn_ref, dpb_ref, cw_ref = ins
        dp_ref, dcw_ref = outs
        zext, dcext = scr
        last = pl.num_programs(0) - 1
        zext[:HALO, :] = jnp.where(i > 0, zh_ref[...], 0.0)
        zext[HALO:, :] = z_ref[...]
        dcext[:tile, :] = dc_ref[...]
        dcext[tile:, :] = jnp.where(i < last, dcn_ref[...], 0.0)
        dz = _conv_taps(cw_ref, dcext, tile, lambda k: (CONV_W - 1) - k)
        gate = p_ref[:, DC:2 * DC]
        s = _sigmoid(gate)
        dp_ref[:, :DC] = (dz * s).astype(BF16)
        dp_ref[:, DC:2 * DC] = (dz * p_ref[:, :DC] * s * (1.0 - s)).astype(BF16)
        dp_ref[:, 2 * DC:] = dpb_ref[...]
        dcv = dc_ref[...]

        @pl.when(i == 0)
        def _():
            dcw_ref[...] = jnp.zeros_like(dcw_ref)

        for k in range(CONV_W):
            dcw_ref[k:k + 1, :] += jnp.sum(dcv * zext[pl.ds(HALO - (CONV_W - 1) + k, tile), :], axis=0, keepdims=True)

    return _rowwise(fn, "even_bwd_b", p.shape[0], tile,
                    [(p, tile, D, lambda i: (i, 0)), z, _prev_halo(z, tile), dconv, _next_halo(dconv, tile), dpb], [cw],
                    [(2 * D, BF16)], [((HALO, DC), F32)],
                    scratch=[pltpu.VMEM((tile + HALO, DC), F32), pltpu.VMEM((tile + HALO, DC), F32)])


def _row_count(i, tile, nrows, offset, w):
    t = i * tile + offset + lax.broadcasted_iota(jnp.int32, (nrows, POOL_GD), 0)
    return jnp.minimum(t + 1, w).astype(F32)


def _odd_core(p, wg, scale, tile):
    def fn(i, ins, outs, scr):
        p_ref, ph_ref, wg_ref, sc_ref = ins
        d_ref, e_ref, es_ref = outs
        pext = scr[0]
        pext[:HALO, :] = jnp.where(i > 0, ph_ref[...], 0.0)
        pext[HALO:, :] = p_ref[...]
        for g, w in enumerate(POOL_WINDOWS):
            c0 = g * POOL_GD
            s = p_ref[:, c0:c0 + POOL_GD]
            for r in range(1, w):
                s = s + pext[pl.ds(HALO - r, tile), c0:c0 + POOL_GD]
            dg = (s / _row_count(i, tile, tile, 0, w) - p_ref[:, c0:c0 + POOL_GD]).astype(BF16)
            d_ref[:, c0:c0 + POOL_GD] = dg
            e = _dot(dg, wg_ref[g])
            e_ref[:, c0:c0 + POOL_GD] = e
            es_ref[:, c0:c0 + POOL_GD] = (e * sc_ref[:, c0:c0 + POOL_GD]).astype(BF16)

    return _rowwise(fn, "odd_core", p.shape[0], tile, [p, _prev_halo(p, tile)], [wg, scale],
                    [(D, BF16), (D, F32), (D, BF16)], scratch=[pltpu.VMEM((tile + HALO, D), F32)])


def _odd_bwd(des, e, d, wg, scale, tile):
    def fn(i, ins, outs, scr):
        des_ref, desn_ref, e_ref, d_ref, wg_ref, sc_ref = ins
        dp_ref, dsc_ref, dwg_ref = outs
        qext = scr[0]
        last = pl.num_programs(0) - 1
        desv = des_ref[...]
        _accum(dsc_ref, i, jnp.sum(desv * e_ref[...], axis=0, keepdims=True))
        de = (desv * sc_ref[...]).astype(BF16)
        den = (jnp.where(i < last, desn_ref[...], 0.0) * sc_ref[...]).astype(BF16)
        for g, w in enumerate(POOL_WINDOWS):
            c0 = g * POOL_GD
            deg = de[:, c0:c0 + POOL_GD]
            dd = _dot(deg, wg_ref[g], "nt")
            ddn = _dot(den[:, c0:c0 + POOL_GD], wg_ref[g], "nt")
            qext[:tile, c0:c0 + POOL_GD] = dd / _row_count(i, tile, tile, 0, w)
            qext[tile:, c0:c0 + POOL_GD] = ddn / _row_count(i, tile, HALO, tile, w)
            s = qext[:tile, c0:c0 + POOL_GD]
            for r in range(1, w):
                s = s + qext[pl.ds(r, tile), c0:c0 + POOL_GD]
            dp_ref[:, c0:c0 + POOL_GD] = (s - dd).astype(BF16)
            dw = _dot(d_ref[:, c0:c0 + POOL_GD], deg, "tn")

            @pl.when(i == 0)
            def _():
                dwg_ref[g] = dw

            @pl.when(i > 0)
            def _():
                dwg_ref[g] += dw

    return _rowwise(fn, "odd_bwd", des.shape[0], tile, [des, _next_halo(des, tile), e, d], [wg, scale], [(D, BF16)],
                    [((1, D), F32), ((len(POOL_WINDOWS), POOL_GD, POOL_GD), F32)],
                    scratch=[pltpu.VMEM((tile + HALO, D), F32)])


def _axis_slice(ref, axis, start, size):
    idx = [slice(None)] * len(ref.shape)
    idx[axis] = pl.ds(start, size)
    return ref.at[tuple(idx)]


def _cast_into_slot(pos, shard, kind, name):
    L, rs, cs = shard.shape
    tr = _row_tile(rs, 512)
    nr = rs // tr
    if kind == "row":
        full, omap = (L, NCHIP * rs, cs), (lambda l, i, p: (l, p[0] * nr + i, 0))
    else:
        full, omap = (L, rs, NCHIP * cs), (lambda l, i, p: (l, i, p[0]))

    def body(p_ref, s_ref, o_ref):
        o_ref[...] = s_ref[...].astype(BF16)

    grid_spec = pltpu.PrefetchScalarGridSpec(
        num_scalar_prefetch=1, grid=(L, nr), in_specs=[pl.BlockSpec((None, tr, cs), lambda l, i, p: (l, i, 0))],
        out_specs=pl.BlockSpec((None, tr, cs), omap))
    return pl.pallas_call(body, name=name, grid_spec=grid_spec, out_shape=jax.ShapeDtypeStruct(full, BF16),
                          compiler_params=_params(("parallel", "parallel")))(pos, shard)


def _peer(x, y, c, k):
    return (1 - x if k & 4 else x, 1 - y if k & 2 else y, 1 - c if k & 1 else c)


def _half_geometry(shape, kind):
    L, R, C = shape
    return (L, R // (2 * NCHIP), C) if kind == "row" else (L, R // 2, C // NCHIP)


def _sum_into_half(pos, recv, grad, kind, name):
    _, L, rh, cs = recv.shape
    if kind == "row":
        gmap = lambda l, p: (l, 2 * p[0] + p[1], 0)
    else:
        gmap = lambda l, p: (l, p[1], p[0])

    def body(p_ref, r_ref, g_ref, o_ref):
        acc = g_ref[...].astype(F32)
        for s in range(7):
            acc = acc + r_ref[s].astype(F32)
        o_ref[...] = acc

    grid_spec = pltpu.PrefetchScalarGridSpec(
        num_scalar_prefetch=1, grid=(L,),
        in_specs=[pl.BlockSpec((7, None, rh, cs), lambda l, p: (0, l, 0, 0)), pl.BlockSpec((None, rh, cs), gmap)],
        out_specs=pl.BlockSpec((None, rh, cs), lambda l, p: (l, p[1], 0)))
    return pl.pallas_call(body, name=name, grid_spec=grid_spec, out_shape=jax.ShapeDtypeStruct((L, 2 * rh, cs), F32),
                          compiler_params=_params(("parallel",)))(pos, _hbm(recv), _hbm(grad))


def _swap_halves(shards):
    n = len(shards)

    def body(*refs):
        outs = refs[n:2 * n]
        send_sems, recv_sems = refs[2 * n:]
        x, y, c = lax.axis_index("x"), lax.axis_index("y"), lax.axis_index("c")
        copies = []
        for t in range(n):
            rh = outs[t].shape[1] // 2
            half = outs[t].at[:, pl.ds(c * rh, rh), :]
            cp = pltpu.make_async_remote_copy(src_ref=half, dst_ref=half, send_sem=send_sems.at[t], recv_sem=recv_sems.at[t],
                                              device_id=(x, y, 1 - c), device_id_type=MESH)
            cp.start()
            copies.append(cp)
        for cp in copies:
            cp.wait()

    return pl.pallas_call(body, name="swap_halves", in_specs=[HBM] * n, out_specs=[HBM] * n,
                          out_shape=[pltpu.HBM(s.shape, s.dtype) for s in shards],
                          input_output_aliases={t: t for t in range(n)},
                          scratch_shapes=[pltpu.SemaphoreType.DMA((n,)), pltpu.SemaphoreType.DMA((n,))])(*map(_hbm, shards))


SEM = pl.BlockSpec(memory_space=pltpu.SEMAPHORE)
DATAFLOW = pltpu.SideEffectType.DATAFLOW_SIDE_EFFECTING
MAX_LAYERS = 4


def _gsem(t, layer, k):
    return (t * MAX_LAYERS + layer) * 3 + k


def _slot_of(ref, kind, d0, nd0, mine):
    if kind == "row":
        size = ref.shape[1] // NCHIP
        return ref.at[pl.ds(d0, nd0), pl.ds(mine * size, size), :]
    size = ref.shape[2] // NCHIP
    return ref.at[pl.ds(d0, nd0), :, pl.ds(mine * size, size)]


def _gather_start(bufs, kinds, groups, after):
    n, na = len(bufs), len(after)

    def body(*refs):
        send_sems, recv_sems = refs[n + na], refs[n + na + 1]
        outs = refs[n + na + 2:]
        x, y, c = lax.axis_index("x"), lax.axis_index("y"), lax.axis_index("c")
        mine = 2 * x + y
        chips = [(1 - x, y), (x, 1 - y), (1 - x, 1 - y)]
        for group in groups:
            for t, li, d0, nd0 in group:
                slot = _slot_of(outs[t], kinds[t], d0, nd0, mine)
                for k, chip in enumerate(chips):
                    pltpu.make_async_remote_copy(src_ref=slot, dst_ref=slot, send_sem=send_sems.at[_gsem(t, li, k)],
                                                 recv_sem=recv_sems.at[_gsem(t, li, k)], device_id=(*chip, c),
                                                 device_id_type=MESH).start()

    sems = pltpu.SemaphoreType.DMA((n * MAX_LAYERS * 3,))
    res = pl.pallas_call(body, name="gather_start", in_specs=[HBM] * n + [ANY] * na, out_specs=[SEM, SEM] + [HBM] * n,
                         out_shape=[sems, sems] + [pltpu.HBM(b.shape, b.dtype) for b in bufs],
                         input_output_aliases={t: t + 2 for t in range(n)},
                         compiler_params=pltpu.CompilerParams(has_side_effects=DATAFLOW))(*map(_hbm, bufs), *after)
    return res[0], res[1], list(res[2:])


def _gather_wait(bufs, kinds, group, send_sems, recv_sems, after, name):
    m = len(bufs)

    def body(*refs):
        send, recv = refs[m], refs[m + 1]
        outs = refs[m + 3:]
        x, y, c = lax.axis_index("x"), lax.axis_index("y"), lax.axis_index("c")
        mine = 2 * x + y
        chips = [(1 - x, y), (x, 1 - y), (1 - x, 1 - y)]
        for j, (t, li, d0, nd0) in enumerate(group):
            slot = _slot_of(outs[j], kinds[j], d0, nd0, mine)
            for k, chip in enumerate(chips):
                cp = pltpu.make_async_remote_copy(src_ref=slot, dst_ref=slot, send_sem=send.at[_gsem(t, li, k)],
                                                  recv_sem=recv.at[_gsem(t, li, k)], device_id=(*chip, c), device_id_type=MESH)
                cp.wait_send()
                cp.wait_recv()

    res = pl.pallas_call(body, name=name, in_specs=[HBM] * m + [SEM, SEM, ANY], out_specs=[HBM] * m,
                         out_shape=[pltpu.HBM(b.shape, b.dtype) for b in bufs],
                         input_output_aliases={j: j for j in range(m)},
                         compiler_params=pltpu.CompilerParams(has_side_effects=DATAFLOW))(*map(_hbm, bufs), send_sems, recv_sems,
                                                                                          after)
    return list(res)


def _gather_tiny(tiny, axes):
    n = len(tiny)
    out_shape = []
    for s_, ax in zip(tiny, axes):
        shp = list(s_.shape)
        shp[ax] *= NCHIP
        out_shape.append(jax.ShapeDtypeStruct(tuple(shp), s_.dtype))

    def body(*refs):
        ins, outs = refs[:n], refs[n:2 * n]
        send_sems, recv_sems, loc_sems = refs[2 * n:]
        x, y, c = lax.axis_index("x"), lax.axis_index("y"), lax.axis_index("c")
        mine = 2 * x + y
        chips = [(1 - x, y), (x, 1 - y), (1 - x, 1 - y)]
        copies = []
        for t in range(n):
            size = ins[t].shape[axes[t]]
            dst = _axis_slice(outs[t], axes[t], mine * size, size)
            copies.append(pltpu.make_async_copy(ins[t], dst, loc_sems.at[t]))
            for k, chip in enumerate(chips):
                copies.append(pltpu.make_async_remote_copy(src_ref=ins[t], dst_ref=dst, send_sem=send_sems.at[t, k],
                                                           recv_sem=recv_sems.at[t, k], device_id=(*chip, c),
                                                           device_id_type=MESH))
        for cp in copies:
            cp.start()
        for cp in copies:
            cp.wait()

    return pl.pallas_call(body, name="gather_tiny", in_specs=[ANY] * n, out_specs=[ANY] * n, out_shape=out_shape,
                          scratch_shapes=[pltpu.SemaphoreType.DMA((n, 3)), pltpu.SemaphoreType.DMA((n, 3)),
                                          pltpu.SemaphoreType.DMA((n,))])(*tiny)


def _grad_slice(ref, kind, d0, nd0, chip, core, rh, cs):
    if kind == "row":
        return ref.at[pl.ds(d0, nd0), pl.ds((2 * chip + core) * rh, rh), :]
    return ref.at[pl.ds(d0, nd0), pl.ds(core * rh, rh), pl.ds(chip * cs, cs)]


def _scatter_start(grads, recv, kinds, items, name, after=()):
    m, na = len(grads), len(after)

    def body(*refs):
        send_sems, recv_sems = refs[2 * m + na], refs[2 * m + na + 1]
        gout, rout = refs[2 * m + na + 2:3 * m + na + 2], refs[3 * m + na + 2:]
        x, y, c = lax.axis_index("x"), lax.axis_index("y"), lax.axis_index("c")
        for t in range(m):
            _, _, rh, cs = rout[t].shape
            d0, nd0 = items[t]
            for k in range(1, 8):
                px, py, pc = _peer(x, y, c, k)
                src = _grad_slice(gout[t], kinds[t], d0, nd0, 2 * px + py, pc, rh, cs)
                pltpu.make_async_remote_copy(src_ref=src, dst_ref=rout[t].at[k - 1, pl.ds(d0, nd0)],
                                             send_sem=send_sems.at[7 * t + k - 1], recv_sem=recv_sems.at[7 * t + k - 1],
                                             device_id=(px, py, pc), device_id_type=MESH).start()

    sems = pltpu.SemaphoreType.DMA((7 * m,))
    res = pl.pallas_call(body, name=name, in_specs=[HBM] * (2 * m) + [ANY] * na, out_specs=[SEM, SEM] + [HBM] * (2 * m),
                         out_shape=[sems, sems] + [pltpu.HBM(a.shape, a.dtype) for a in list(grads) + list(recv)],
                         input_output_aliases={j: j + 2 for j in range(2 * m)},
                         compiler_params=pltpu.CompilerParams(has_side_effects=DATAFLOW))(*map(_hbm, grads), *map(_hbm, recv),
                                                                                          *after)
    return res[0], res[1], list(res[2:m + 2]), list(res[m + 2:])


def _scatter_wait(recv, items, send_sems, recv_sems, name, after=()):
    m = len(recv)

    def body(*refs):
        send, rcv = refs[m], refs[m + 1]
        outs = refs[m + 2 + len(after):]
        x, y, c = lax.axis_index("x"), lax.axis_index("y"), lax.axis_index("c")
        for t in range(m):
            d0, nd0 = items[t]
            for k in range(1, 8):
                land = outs[t].at[k - 1, pl.ds(d0, nd0)]
                cp = pltpu.make_async_remote_copy(src_ref=land, dst_ref=land, send_sem=send.at[7 * t + k - 1],
                                                  recv_sem=rcv.at[7 * t + k - 1], device_id=_peer(x, y, c, k), device_id_type=MESH)
                cp.wait_send()
                cp.wait_recv()

    res = pl.pallas_call(body, name=name, in_specs=[HBM] * m + [SEM, SEM] + [ANY] * len(after), out_specs=[HBM] * m,
                         out_shape=[pltpu.HBM(a.shape, a.dtype) for a in recv],
                         input_output_aliases={j: j for j in range(m)},
                         compiler_params=pltpu.CompilerParams(has_side_effects=DATAFLOW))(*map(_hbm, recv), send_sems, recv_sems,
                                                                                          *map(_hbm, after))
    return list(res)


def _share_small(packed):
    rows = packed.shape[0]

    def body(in_ref, out_ref, send_sems, recv_sems, loc_sem):
        x, y, c = lax.axis_index("x"), lax.axis_index("y"), lax.axis_index("c")
        me = 4 * x + 2 * y + c
        copies = [pltpu.make_async_copy(in_ref, out_ref.at[me], loc_sem)]
        for k in range(1, 8):
            copies.append(pltpu.make_async_remote_copy(src_ref=in_ref, dst_ref=out_ref.at[me], send_sem=send_sems.at[k - 1],
                                                       recv_sem=recv_sems.at[k - 1], device_id=_peer(x, y, c, k),
                                                       device_id_type=MESH))
        for cp in copies:
            cp.start()
        for cp in copies:
            cp.wait()

    return pl.pallas_call(body, name="share_small", in_specs=[ANY], out_specs=ANY,
                          out_shape=jax.ShapeDtypeStruct((8, rows, 128), F32),
                          scratch_shapes=[pltpu.SemaphoreType.DMA((7,)), pltpu.SemaphoreType.DMA((7,)),
                                          pltpu.SemaphoreType.DMA])(packed)


def _sum_slots(buf, tile, name):
    _, rows, cols = buf.shape

    def body(b_ref, o_ref):
        acc = b_ref[0].astype(F32)
        for s in range(1, 8):
            acc = acc + b_ref[s].astype(F32)
        o_ref[...] = acc

    return pl.pallas_call(body, name=name, grid=(rows // tile,), in_specs=[pl.BlockSpec((8, tile, cols), lambda i: (0, i, 0))],
                          out_specs=pl.BlockSpec((tile, cols), lambda i: (i, 0)),
                          out_shape=jax.ShapeDtypeStruct((rows, cols), F32), compiler_params=_params(("parallel",)))(buf)


def _adamw(w, g, m, v, tile, name, after=()):
    rows, cols = w.shape

    def body(w_ref, g_ref, m_ref, v_ref, *rest):
        d_ref, mo_ref, vo_ref = rest[len(after):]
        gv = g_ref[...]
        mn = ADAM_B1 * m_ref[...] + (1.0 - ADAM_B1) * gv
        vn = ADAM_B2 * v_ref[...] + (1.0 - ADAM_B2) * (gv * gv)
        m_hat = mn / (1.0 - ADAM_B1 ** ADAM_STEP)
        v_hat = vn / (1.0 - ADAM_B2 ** ADAM_STEP)
        d_ref[...] = -ADAM_LR * (m_hat / (jnp.sqrt(v_hat) + ADAM_EPS) + ADAM_WD * w_ref[...])
        mo_ref[...] = mn
        vo_ref[...] = vn

    spec = pl.BlockSpec((tile, cols), lambda i: (i, 0))
    sds = jax.ShapeDtypeStruct((rows, cols), F32)
    return pl.pallas_call(body, name=name, grid=(rows // tile,), in_specs=[spec] * 4 + [ANY] * len(after), out_specs=[spec] * 3,
                          out_shape=[sds] * 3, compiler_params=_params(("parallel",)))(w, g, m, v, *map(_hbm, after))


def _row_tile(rows, cap):
    best = 8
    for t in range(8, min(rows, cap) + 1, 8):
        if rows % t == 0:
            best = t
    return best


def _pack(arrs):
    parts = []
    for a in arrs:
        r = a.size // 128
        r8 = -(-r // 8) * 8
        parts.append(jnp.pad(a.reshape(r, 128).astype(F32), ((0, r8 - r), (0, 0))))
    return jnp.concatenate(parts, axis=0)


def _unpack(packed, shapes):
    out, r0 = [], 0
    for shp in shapes:
        r = math.prod(shp) // 128
        out.append(packed[r0:r0 + r].reshape(shp))
        r0 += -(-r // 8) * 8
    return out


BIG = {
    "ffn1_w_gu": "col", "ffn1_w_down": "row", "ev_w_in": "col", "ev_w_out": "row", "od_w_in": "row", "od_w_group": "row",
    "od_w_out": "row", "xa_w_q": "row", "xa_w_kv": "col", "xa_w_o": "row", "ffn2_w_gu": "col", "ffn2_w_down": "row",
}
TINY_SHARDED = {"ev_conv_w": 2, "od_scale": 1}
WEIGHTS = ['ffn1_pre_g', 'ffn1_w_gu', 'ffn1_w_down', 'ffn1_post_g', 'mix_pre_g', 'mix_post_g', 'ev_w_in', 'ev_conv_w',
           'ev_conv_b', 'ev_conv_ln_g', 'ev_conv_ln_b', 'ev_sgu_ln_g', 'ev_sgu_ln_b', 'ev_sgu_w', 'ev_sgu_b', 'ev_w_out',
           'od_w_in', 'od_w_group', 'od_scale', 'od_w_out', 'xa_pre_g', 'xa_mem_g', 'xa_w_q', 'xa_w_kv', 'xa_w_o', 'xa_post_g',
           'ffn2_pre_g', 'ffn2_w_gu', 'ffn2_w_down', 'ffn2_post_g']


def _as3d(a):
    return a.reshape((-1,) + a.shape[-2:]) if a.ndim == 4 else a


class _WeightView:
    def __init__(self, store, view):
        self.store, self.view = store, view

    def __getitem__(self, k):
        return self.view(k)


class _Grads:
    def __init__(self):
        self.buf = {}
        self.fresh = []

    def add(self, name, layer, nlayers, a, b, tm, tn, n_outer=True):
        self.buf[name] = _matmul(a, b, "tn", BF16, "dw_" + name, tm, tn, out_l=layer, out_stack=nlayers,
                                 out_buf=self.buf.get(name), n_outer=n_outer)
        self.fresh.append(name)

    def take(self):
        names, self.fresh = self.fresh, []
        return names


def _local_step(x, mem, target, W, S, tiles=None, before_sub=None, after_sub_bwd=None):
    T, TW, tm = tiles or (min(512, S), min(256, S), min(512, S))
    row = lambda v: v.reshape(1, -1)
    subs = []
    small = {k: [None] * W[k].shape[0] for k in WEIGHTS if k not in BIG}
    g = _Grads()

    def ffn_fwd(tag, l, h, n, g_next):
        gu, a = _up_swiglu(n, W[tag + "_w_gu"], l, TW)
        f, h2, n2 = _post_mm(h, a, W[tag + "_w_down"], l, row(W[tag + "_post_g"][l]), 0.5, g_next, T)
        subs.append(dict(kind="ffn", tag=tag, l=l, h=h, n=n, gu=gu, a=a, f=f, scale=0.5, pre=tag + "_pre_g", post=tag + "_post_g"))
        return h2, n2

    def ffn_bwd(s, df):
        tag, l = s["tag"], s["l"]
        g.add(tag + "_w_down", l, 4, s["a"], df, 256, D, n_outer=False)
        dgu = _down_dx_swiglu_bwd(df, s["gu"], W[tag + "_w_down"], l, TW)
        g.add(tag + "_w_gu", l, 4, s["n"], dgu, D, 512, n_outer=False)
        return dgu, tag + "_w_gu", l, TW

    def xa_fwd(l, h, n, g_next):
        q = _matmul(n, W["xa_w_q"], "nn", BF16, "xa_q", tm, D, b_l=l)
        mn = _mem_norm(mem, row(W["xa_mem_g"][l]))
        kv = _matmul(mn, W["xa_w_kv"], "nn", BF16, "xa_kv", NMEM, D, b_l=l)
        o = _attn(q, kv, T)
        cx, h2, n2 = _post_mm(h, o, W["xa_w_o"], l, row(W["xa_post_g"][l]), 1.0, g_next, T)
        subs.append(dict(kind="xa", l=l, h=h, n=n, q=q, mn=mn, kv=kv, o=o, f=cx, scale=1.0, pre="xa_pre_g", post="xa_post_g"))
        return h2, n2

    def xa_bwd(s, dc):
        l = s["l"]
        do = _matmul(dc, W["xa_w_o"], "nt", BF16, "xa_o_dx", tm, D, b_l=l)
        g.add("xa_w_o", l, 4, s["o"], dc, D, 512)
        dq, dkv = _attn_bwd(s["q"], s["kv"], do, T)
        g.add("xa_w_q", l, 4, s["n"], dq, D, 512)
        dkvb = dkv.astype(BF16)
        g.add("xa_w_kv", l, 4, s["mn"], dkvb, D, 512)
        dmn = _matmul(dkvb, W["xa_w_kv"], "nt", F32, "xa_kv_dx", NMEM, D, b_l=l)
        small["xa_mem_g"][l] = _mem_gain_bwd(mem, dmn)[0]
        return dq, "xa_w_q", l, T

    def even_params(e):
        return (W["ev_conv_w"][e], row(W["ev_conv_b"][e]), row(W["ev_conv_ln_g"][e]), row(W["ev_conv_ln_b"][e]),
                row(W["ev_sgu_ln_g"][e]), row(W["ev_sgu_ln_b"][e]), W["ev_sgu_w"][e])

    def even_fwd(l, h, n, g_next):
        e = l // 2
        p = _matmul(n, W["ev_w_in"], "nn", F32, "ev_in", tm, D, b_l=e)
        cw, cb, clg, clb, slg, slb, ws = even_params(e)
        y, z = _even_core(p, cw, cb, clg, clb, slg, slb, ws, W["ev_sgu_b"][e].T, TW)
        m, h2, n2 = _post_mm(h, y, W["ev_w_out"], e, row(W["mix_post_g"][l]), 1.0, g_next, T)
        subs.append(dict(kind="even", l=l, h=h, n=n, p=p, y=y, z=z, f=m, scale=1.0, pre="mix_pre_g", post="mix_post_g"))
        return h2, n2

    def even_bwd(s, dm):
        l = s["l"]
        e = l // 2
        dy = _matmul(dm, W["ev_w_out"], "nt", F32, "ev_out_dx", tm, D, b_l=e)
        g.add("ev_w_out", e, 2, s["y"], dm, D, 512)
        cw, cb, clg, clb, slg, slb, ws = even_params(e)
        dconv, dpb, vecs, dws, dbst = _even_bwd_a(s["p"], s["z"], dy, cw, cb, clg, clb, slg, slb, ws,
                                                   jnp.swapaxes(ws, 1, 2), W["ev_sgu_b"][e].T, TW)
        dp, dcw = _even_bwd_b(s["p"], s["z"], dconv, dpb, cw, TW)
        for r, name in enumerate(["ev_conv_b", "ev_conv_ln_g", "ev_conv_ln_b", "ev_sgu_ln_g", "ev_sgu_ln_b"]):
            small[name][e] = vecs[r]
        small["ev_sgu_w"][e] = dws
        small["ev_sgu_b"][e] = dbst.T
        small["ev_conv_w"][e] = dcw[:CONV_W]
        g.add("ev_w_in", e, 2, s["n"], dp, D, 512)
        return dp, "ev_w_in", e, T

    def group_w(o):
        ng = len(POOL_WINDOWS)
        return (W["od_w_group"].reshape(-1, POOL_GD, POOL_GD), (ng, POOL_GD, POOL_GD), lambda i: (o, 0, 0))

    def odd_fwd(l, h, n, g_next):
        o = l // 2
        p = _matmul(n, W["od_w_in"], "nn", F32, "od_in", tm, D, b_l=o)
        d, e, es = _odd_core(p, group_w(o), row(W["od_scale"][o]), T)
        m, h2, n2 = _post_mm(h, es, W["od_w_out"], o, row(W["mix_post_g"][l]), 1.0, g_next, T)
        subs.append(dict(kind="odd", l=l, h=h, n=n, d=d, e=e, es=es, f=m, scale=1.0, pre="mix_pre_g", post="mix_post_g"))
        return h2, n2

    def odd_bwd(s, dm):
        l = s["l"]
        o = l // 2
        des = _matmul(dm, W["od_w_out"], "nt", F32, "od_out_dx", tm, D, b_l=o)
        g.add("od_w_out", o, 2, s["es"], dm, D, 512)
        dp, dsc, dwg = _odd_bwd(des, s["e"], s["d"], group_w(o), row(W["od_scale"][o]), T)
        small["od_scale"][o] = dsc[0]
        small["od_w_group_full"][o] = dwg
        g.add("od_w_in", o, 2, s["n"], dp, D, 512)
        return dp, "od_w_in", o, T

    small["od_w_group_full"] = [None, None]
    order = []
    for l in range(4):
        order += [("ffn1", l), ("even" if l % 2 == 0 else "odd", l), ("xa", l), ("ffn2", l)]
    pre_of = {"ffn1": "ffn1_pre_g", "even": "mix_pre_g", "odd": "mix_pre_g", "xa": "xa_pre_g", "ffn2": "ffn2_pre_g"}
    h = x
    n = _prenorm(h, row(W["ffn1_pre_g"][0]), T)
    for idx, (kind, l) in enumerate(order):
        if before_sub is not None:
            before_sub(kind, l, h)
        g_next = row(W[pre_of[order[idx + 1][0]]][order[idx + 1][1]]) if idx + 1 < len(order) else None
        if kind in ("ffn1", "ffn2"):
            h, n = ffn_fwd(kind, l, h, n, g_next)
        elif kind == "xa":
            h, n = xa_fwd(l, h, n, g_next)
        elif kind == "even":
            h, n = even_fwd(l, h, n, g_next)
        else:
            h, n = odd_fwd(l, h, n, g_next)

    top = subs[-1]
    dh, df, loss_acc, dgp = _loss_top(h, target, top["f"], row(W[top["post"]][top["l"]]), top["scale"], T)
    small[top["post"]][top["l"]] = dgp[0]
    for idx in range(len(subs) - 1, -1, -1):
        s = subs[idx]
        da, wname, wl, bt = {"ffn": ffn_bwd, "xa": xa_bwd, "even": even_bwd, "odd": odd_bwd}[s["kind"]](s, df)
        names = g.take()
        deps = [g.buf[k] for k in names]
        if idx > 0:
            sp = subs[idx - 1]
            dh, df, dg_pre, dg_post = _boundary_mm(da, W[wname], wl, s["h"], row(W[s["pre"]][s["l"]]), dh, sp["f"],
                                                   row(W[sp["post"]][sp["l"]]), sp["scale"], bt, deps=deps)
            small[sp["post"]][sp["l"]] = dg_post[0]
        else:
            dh, dg_pre = _boundary_mm(da, W[wname], wl, s["h"], row(W[s["pre"]][s["l"]]), dh, None, None, None, bt, deps=deps)
        small[s["pre"]][s["l"]] = dg_pre[0]
        if after_sub_bwd is not None:
            after_sub_bwd(order[idx][0], s["l"], g.buf, small)
            g.fresh = names + g.fresh
    return loss_acc[0, 0], dh, g.buf, small


def _step(P, S):
    x, mem, target = P["x"][0], P["mem"][0], P["loss_target"][0]
    chip = 2 * lax.axis_index("x") + lax.axis_index("y")
    pos = jnp.stack([chip, lax.axis_index("c")]).astype(jnp.int32)
    order = list(BIG)
    tix = {k: t for t, k in enumerate(order)}
    kinds = [BIG[k] for k in order]
    per = {k: (4 if k == "od_w_group" else 1) for k in order}
    members = {"ffn1": ["ffn1_w_gu", "ffn1_w_down"], "ffn2": ["ffn2_w_gu", "ffn2_w_down"], "xa": ["xa_w_q", "xa_w_kv", "xa_w_o"],
               "even": ["ev_w_in", "ev_w_out"], "odd": ["od_w_in", "od_w_group", "od_w_out"]}
    index_of = lambda kind, l: l // 2 if kind in ("even", "odd") else l
    sub_order = []
    for l in range(4):
        sub_order += [("ffn1", l), ("even" if l % 2 == 0 else "odd", l), ("xa", l), ("ffn2", l)]
    groups = {(kind, l): [(tix[k], index_of(kind, l), index_of(kind, l) * per[k], per[k]) for k in members[kind]]
              for kind, l in sub_order}
    tiny = _gather_tiny([P[k] for k in TINY_SHARDED], list(TINY_SHARDED.values()))
    slots = [_cast_into_slot(pos, _as3d(P[k]), BIG[k], "cast_" + k) for k in order]
    g_send, g_recv, bufs = _gather_start(slots, kinds, [groups[s_] for s_ in sub_order], tiny)
    W = {k: P[k] for k in WEIGHTS if k not in BIG}
    W.update(zip(TINY_SHARDED, tiny))
    W.update(zip(order, bufs))

    def view(k):
        return W[k].reshape(2, 4, POOL_GD, POOL_GD) if k == "od_w_group" else W[k]

    Wv = _WeightView(W, view)

    def before_sub(kind, l, h):
        ks = members[kind]
        arrived = _gather_wait([W[k] for k in ks], [BIG[k] for k in ks], groups[(kind, l)], g_send, g_recv, h,
                               "gather_wait_%s%d" % (kind, l))
        W.update(zip(ks, arrived))

    recv = {k: None for k in order}
    pending = []
    small_names = [k for k in WEIGHTS if k not in BIG]
    gsmall = {}

    def after_sub_bwd(kind, l, big, small):
        ks = [k for k in members[kind] if k != "od_w_group"]
        items = [(index_of(kind, l), 1)] * len(ks)
        after = ()
        if (kind, l) == ("odd", 1):
            big["od_w_group"] = jnp.stack(small.pop("od_w_group_full")).astype(BF16).reshape(8, POOL_GD, POOL_GD)
            ks, items = ks + ["od_w_group"], items + [(0, 8)]
        if (kind, l) == ("ffn1", 0):
            small_full = [jnp.stack(small[k]) for k in small_names]
            packed = _pack(small_full)
            summed = _sum_slots(_share_small(packed), _row_tile(packed.shape[0], 512), "sum_small")
            gsmall.update(zip(small_names, _unpack(summed, [a.shape for a in small_full])))
            after = (summed,)
        for k in ks:
            if recv[k] is None:
                recv[k] = lax.empty((7,) + _half_geometry(big[k].shape, BIG[k]), BF16)
        s_send, s_recv, g_thru, r_thru = _scatter_start([big[k] for k in ks], [recv[k] for k in ks], [BIG[k] for k in ks],
                                                        items, "scatter_start_%s%d" % (kind, l), after)
        big.update(zip(ks, g_thru))
        recv.update(zip(ks, r_thru))
        pending.append((kind, l, ks, items, s_send, s_recv))

    loss, grad_x, big, small = _local_step(x, mem, target, Wv, S, before_sub=before_sub, after_sub_bwd=after_sub_bwd)
    last = (big[members["ffn1"][0]],)
    for kind, l, ks, items, s_send, s_recv in pending:
        recv.update(zip(ks, _scatter_wait([recv[k] for k in ks], items, s_send, s_recv, "scatter_wait_%s%d" % (kind, l), last)))
    gsh = dict(zip(order, _swap_halves([_sum_into_half(pos, recv[k], big[k], BIG[k], "sum_" + k) for k in order])))
    for k, ax in TINY_SHARDED.items():
        size = P[k].shape[ax]
        gsmall[k] = lax.dynamic_slice_in_dim(gsmall[k], chip * size, size, axis=ax)

    grads, delta, new_m, new_v = {}, {}, {}, {}
    for k in order:
        shp = P[k].shape
        cols = shp[-1]
        flat = lambda a: a.reshape(-1, cols)
        gk = flat(gsh[k])
        d_, m_, v_ = _adamw(flat(P[k]), gk, flat(P["m_" + k]), flat(P["v_" + k]), _row_tile(gk.shape[0], 256), "adamw_" + k)
        grads[k], delta[k], new_m[k], new_v[k] = gk.reshape(shp), d_.reshape(shp), m_.reshape(shp), v_.reshape(shp)
    pk = lambda pre: _pack([P[pre + k] for k in small_names])
    d_, m_, v_ = _adamw(pk(""), _pack([gsmall[k] for k in small_names]), pk("m_"), pk("v_"),
                        _row_tile(pk("").shape[0], 256), "adamw_small", last)
    shapes = [P[k].shape for k in small_names]
    for dst, src in ((delta, d_), (new_m, m_), (new_v, v_)):
        dst.update(zip(small_names, _unpack(src, shapes)))
    grads.update(gsmall)

    loss = lax.psum(loss, ("x", "y", "c"))
    out = [loss, grad_x[None]]
    for grp in (grads, delta, new_m, new_v):
        out += [grp[k] for k in WEIGHTS]
    return tuple(out)


def kernel(x, mem, ffn1_pre_g, ffn1_w_gu, ffn1_w_down, ffn1_post_g, mix_pre_g, mix_post_g, ev_w_in, ev_conv_w, ev_conv_b, ev_conv_ln_g, ev_conv_ln_b, ev_sgu_ln_g, ev_sgu_ln_b, ev_sgu_w, ev_sgu_b, ev_w_out, od_w_in, od_w_group, od_scale, od_w_out, xa_pre_g, xa_mem_g, xa_w_q, xa_w_kv, xa_w_o, xa_post_g, ffn2_pre_g, ffn2_w_gu, ffn2_w_down, ffn2_post_g, loss_target, m_ffn1_pre_g, m_ffn1_w_gu, m_ffn1_w_down, m_ffn1_post_g, m_mix_pre_g, m_mix_post_g, m_ev_w_in, m_ev_conv_w, m_ev_conv_b, m_ev_conv_ln_g, m_ev_conv_ln_b, m_ev_sgu_ln_g, m_ev_sgu_ln_b, m_ev_sgu_w, m_ev_sgu_b, m_ev_w_out, m_od_w_in, m_od_w_group, m_od_scale, m_od_w_out, m_xa_pre_g, m_xa_mem_g, m_xa_w_q, m_xa_w_kv, m_xa_w_o, m_xa_post_g, m_ffn2_pre_g, m_ffn2_w_gu, m_ffn2_w_down, m_ffn2_post_g, v_ffn1_pre_g, v_ffn1_w_gu, v_ffn1_w_down, v_ffn1_post_g, v_mix_pre_g, v_mix_post_g, v_ev_w_in, v_ev_conv_w, v_ev_conv_b, v_ev_conv_ln_g, v_ev_conv_ln_b, v_ev_sgu_ln_g, v_ev_sgu_ln_b, v_ev_sgu_w, v_ev_sgu_b, v_ev_w_out, v_od_w_in, v_od_w_group, v_od_scale, v_od_w_out, v_xa_pre_g, v_xa_mem_g, v_xa_w_q, v_xa_w_kv, v_xa_w_o, v_xa_post_g, v_ffn2_pre_g, v_ffn2_w_gu, v_ffn2_w_down, v_ffn2_post_g):
    P = dict(locals())
    return _step(P, x.shape[1])
```

```python
import functools
import math

import jax
import jax.numpy as jnp
from jax import lax
from jax.experimental import pallas as pl
from jax.experimental.pallas import tpu as pltpu

F32 = jnp.float32
BF16 = jnp.bfloat16
MESH = pl.DeviceIdType.MESH
ANY = pl.BlockSpec(memory_space=pl.ANY)
HBM = pl.BlockSpec(memory_space=pltpu.HBM)


def _hbm(x):
    return pltpu.with_memory_space_constraint(x, pltpu.HBM)

D = 1024
DFF = 2816
NMEM = 256
DC = 512
CONV_W = 31
CHUNK = 128
NHEAD_SGU = 4
POOL_WINDOWS = (2, 4, 8, 16)
POOL_GD = 256
XA_HEADS = 4
XA_HD = 256
EPS = 1e-6
NCHIP = 4
HALO = 32

ADAM_LR, ADAM_B1, ADAM_B2, ADAM_EPS, ADAM_WD, ADAM_STEP = 0.001, 0.9, 0.999, 1e-08, 0.01, 10

V7X_VMEM_BYTES = 64 * 1024 * 1024
VMEM_LIMIT = V7X_VMEM_BYTES - 8 * 1024 * 1024


def _params(sem):
    return pltpu.CompilerParams(dimension_semantics=sem, vmem_limit_bytes=VMEM_LIMIT)


def _matmul(a, b, kind, out_dtype, name, tm, tn, a_l=None, b_l=None, out_l=None, out_stack=None, out_buf=None,
            n_outer=True):
    a2, b2 = a.shape[-2:], b.shape[-2:]
    if kind == "nn":
        (m, k), (k2, n) = a2, b2
        dims = (((1,), (0,)), ((), ()))
    elif kind == "nt":
        (m, k), (n, k2) = a2, b2
        dims = (((1,), (1,)), ((), ()))
    else:
        (k, m), (k2, n) = a2, b2
        dims = (((0,), (0,)), ((), ()))
    assert k == k2 and m % tm == 0 and n % tn == 0, (name, a.shape, b.shape, tm, tn)
    if n_outer:
        grid = (n // tn, m // tm)
        ij = lambda p, q: (q, p)
    else:
        grid = (m // tm, n // tn)
        ij = lambda p, q: (p, q)

    def spec(arr, layer, blk, idx):
        if arr.ndim == 3:
            return pl.BlockSpec((None,) + blk, lambda p, q: (layer,) + idx(*ij(p, q)))
        return pl.BlockSpec(blk, lambda p, q: idx(*ij(p, q)))

    a_spec = spec(a, a_l, (k, tm) if kind == "tn" else (tm, k), (lambda i, j: (0, i)) if kind == "tn" else (lambda i, j: (i, 0)))
    b_spec = spec(b, b_l, (tn, k) if kind == "nt" else (k, tn), (lambda i, j: (j, 0)) if kind == "nt" else (lambda i, j: (0, j)))
    if out_stack is None:
        out_shape = jax.ShapeDtypeStruct((m, n), out_dtype)
        o_spec = pl.BlockSpec((tm, tn), lambda p, q: ij(p, q))
    else:
        out_shape = jax.ShapeDtypeStruct((out_stack, m, n), out_dtype)
        o_spec = pl.BlockSpec((None, tm, tn), lambda p, q: (out_l,) + ij(p, q))

    def body(a_ref, b_ref, *rest):
        o_ref = rest[-1]
        o_ref[...] = lax.dot_general(a_ref[...], b_ref[...], dims, preferred_element_type=F32).astype(o_ref.dtype)

    in_specs, args, aliases = [a_spec, b_spec], [a, _hbm(b) if b.ndim == 3 else b], {}
    if out_buf is not None:
        in_specs.append(ANY)
        args.append(_hbm(out_buf))
        aliases = {2: 0}
    return pl.pallas_call(body, name=name, grid=grid, in_specs=in_specs, out_specs=o_spec, out_shape=out_shape,
                          input_output_aliases=aliases, compiler_params=_params(("parallel", "parallel")))(*args)


def _rowwise(fn, name, rows, tile, row_ins, full_ins, row_outs, acc_outs=(), scratch=(), deps=()):
    assert rows % tile == 0, (name, rows, tile)
    in_specs, args = [], []
    for r in row_ins:
        if isinstance(r, tuple):
            arr, br, bc, imap = r
            in_specs.append(pl.BlockSpec((br, bc), imap))
        else:
            arr = r
            in_specs.append(pl.BlockSpec((tile, arr.shape[1]), lambda i: (i, 0)))
        args.append(arr)
    for f in full_ins:
        if isinstance(f, tuple):
            arr, blk, imap = f
            in_specs.append(pl.BlockSpec(blk, imap, pipeline_mode=pl.Buffered(1)))
            args.append(_hbm(arr))
        else:
            in_specs.append(pl.BlockSpec(f.shape, functools.partial(lambda nd, i: (0,) * nd, f.ndim)))
            args.append(f)
    for d in deps:
        in_specs.append(ANY)
        args.append(_hbm(d))
    out_specs, out_shape = [], []
    for w, dt in row_outs:
        out_specs.append(pl.BlockSpec((tile, w), lambda i: (i, 0)))
        out_shape.append(jax.ShapeDtypeStruct((rows, w), dt))
    for shp, dt in acc_outs:
        out_specs.append(pl.BlockSpec(shp, functools.partial(lambda nd, i: (0,) * nd, len(shp))))
        out_shape.append(jax.ShapeDtypeStruct(shp, dt))
    n_in, n_out = len(args), len(out_shape)

    def body(*refs):
        fn(pl.program_id(0), refs[:n_in], refs[n_in:n_in + n_out], refs[n_in + n_out:])

    sem = ("arbitrary",) if acc_outs else ("parallel",)
    res = pl.pallas_call(body, name=name, grid=(rows // tile,), in_specs=in_specs, out_specs=out_specs, out_shape=out_shape,
                         scratch_shapes=list(scratch), compiler_params=_params(sem))(*args)
    return res


def _accum(ref, i, val):
    @pl.when(i == 0)
    def _():
        ref[...] = val

    @pl.when(i > 0)
    def _():
        ref[...] += val


def _prev_halo(arr, tile, cols=None):
    r = tile // HALO
    return (arr, HALO, cols or arr.shape[1], lambda i: (jnp.maximum(i * r - 1, 0), 0))


def _next_halo(arr, tile, cols=None):
    r = tile // HALO
    last = arr.shape[0] // HALO - 1
    return (arr, HALO, cols or arr.shape[1], lambda i: (jnp.minimum((i + 1) * r, last), 0))


def _sigmoid(x):
    return 1.0 / (1.0 + jnp.exp(-x))


def _rms_hat(x):
    r = lax.rsqrt(jnp.mean(x * x, axis=-1, keepdims=True) + EPS)
    return x * r, r


def _rms_bwd(x, g, dy):
    xhat, r = _rms_hat(x)
    dxhat = dy * g
    dx = r * (dxhat - xhat * jnp.mean(dxhat * xhat, axis=-1, keepdims=True))
    return dx, jnp.sum(dy * xhat, axis=0, keepdims=True)


def _ln_hat(x):
    mu = jnp.mean(x, axis=-1, keepdims=True)
    xc = x - mu
    r = lax.rsqrt(jnp.mean(xc * xc, axis=-1, keepdims=True) + EPS)
    return xc * r, r


def _ln_bwd(xhat, r, g, dy):
    dxhat = dy * g
    dx = r * (dxhat - jnp.mean(dxhat, axis=-1, keepdims=True) - xhat * jnp.mean(dxhat * xhat, axis=-1, keepdims=True))
    return dx, jnp.sum(dy * xhat, axis=0, keepdims=True), jnp.sum(dy, axis=0, keepdims=True)


def _silu_grad(x):
    s = _sigmoid(x)
    return s * (1.0 + x * (1.0 - s))


_SQRT_HALF = math.sqrt(0.5)
_INV_SQRT_2PI = 1.0 / math.sqrt(2.0 * math.pi)


def _gelu(x):
    return 0.5 * x * (1.0 + lax.erf(x * _SQRT_HALF))


def _gelu_grad(x):
    return 0.5 * (1.0 + lax.erf(x * _SQRT_HALF)) + x * jnp.exp(-0.5 * x * x) * _INV_SQRT_2PI


def _dot(a, b, kind="nn"):
    dims = {"nn": (((1,), (0,)), ((), ())), "nt": (((1,), (1,)), ((), ())), "tn": (((0,), (0,)), ((), ()))}[kind]
    return lax.dot_general(a, b, dims, preferred_element_type=F32)


def _prenorm(h, g, tile):
    def fn(i, ins, outs, _):
        h_ref, g_ref = ins
        outs[0][...] = (_rms_hat(h_ref[...])[0] * g_ref[...]).astype(BF16)

    return _rowwise(fn, "prenorm", h.shape[0], tile, [h], [g], [(D, BF16)])[0]


def _loss_top(h, target, f, g_post, scale, tile):
    def fn(i, ins, outs, _):
        err = ins[0][...] - ins[1][...]
        per_row = jnp.mean(err * err, axis=-1, keepdims=True)
        _accum(outs[2], i, jnp.broadcast_to(0.5 * jnp.sum(per_row, axis=0, keepdims=True), (1, 128)))
        dh = err * (1.0 / D)
        outs[0][...] = dh
        df, dg = _rms_bwd(ins[2][...], ins[3][...], scale * dh)
        outs[1][...] = df.astype(BF16)
        _accum(outs[3], i, dg)

    return _rowwise(fn, "loss_top", h.shape[0], tile, [h, target, f], [g_post], [(D, F32), (D, BF16)],
                    [((1, 128), F32), ((1, D), F32)])


def _layer_of(w3, l):
    return (w3, (None,) + tuple(w3.shape[1:]), lambda i: (l, 0, 0))


def _post_mm(h, a, w3, l, g_post, scale, g_next, tile):
    def fn(i, ins, outs, _):
        f = _dot(ins[1][...], ins[2][...])
        outs[0][...] = f
        hn = ins[0][...] + scale * (_rms_hat(f)[0] * ins[3][...])
        outs[1][...] = hn
        if g_next is not None:
            outs[2][...] = (_rms_hat(hn)[0] * ins[4][...]).astype(BF16)

    fulls = [_layer_of(w3, l), g_post] + ([g_next] if g_next is not None else [])
    outs = [(D, F32), (D, F32)] + ([(D, BF16)] if g_next is not None else [])
    res = _rowwise(fn, "post_mm", h.shape[0], tile, [h, a], fulls, outs)
    return res[0], res[1], (res[2] if g_next is not None else None)


def _boundary_mm(a, w3, l, h, g_pre, dh_in, f_prev, g_post_prev, scale_prev, tile, deps=()):
    has_prev = f_prev is not None

    def fn(i, ins, outs, _):
        a_ref, h_ref, dhin_ref = ins[:3]
        w_ref, g_ref = ins[3 + has_prev], ins[4 + has_prev]
        dn = _dot(a_ref[...], w_ref[...], "nt")
        dx, dg = _rms_bwd(h_ref[...], g_ref[...], dn)
        dh = dhin_ref[...] + dx
        outs[0][...] = dh
        _accum(outs[1 + has_prev], i, dg)
        if has_prev:
            df, dgp = _rms_bwd(ins[3][...], ins[6][...], scale_prev * dh)
            outs[1][...] = df.astype(BF16)
            _accum(outs[3], i, dgp)

    rows = [a, h, dh_in] + ([f_prev] if has_prev else [])
    fulls = [_layer_of(w3, l), g_pre] + ([g_post_prev] if has_prev else [])
    outs = [(D, F32)] + ([(D, BF16)] if has_prev else [])
    accs = [((1, D), F32)] + ([((1, D), F32)] if has_prev else [])
    return _rowwise(fn, "boundary_mm", h.shape[0], tile, rows, fulls, outs, accs, deps=deps)


FF_CHUNK = DFF // 2


def _up_swiglu(n, w3, l, tile):
    def fn(i, ins, outs, _):
        n_ref, w_ref = ins
        nv = n_ref[...]
        for c0 in range(0, DFF, FF_CHUNK):
            g = _dot(nv, w_ref[:, c0:c0 + FF_CHUNK])
            u = _dot(nv, w_ref[:, DFF + c0:DFF + c0 + FF_CHUNK])
            outs[0][:, c0:c0 + FF_CHUNK] = g.astype(BF16)
            outs[0][:, DFF + c0:DFF + c0 + FF_CHUNK] = u.astype(BF16)
            outs[1][:, c0:c0 + FF_CHUNK] = (g * _sigmoid(g) * u).astype(BF16)

    return _rowwise(fn, "up_swiglu", n.shape[0], tile, [n], [_layer_of(w3, l)], [(2 * DFF, BF16), (DFF, BF16)])


def _down_dx_swiglu_bwd(df, gu, w3, l, tile):
    def fn(i, ins, outs, _):
        df_ref, gu_ref, w_ref = ins
        dfv = df_ref[...]
        for c0 in range(0, DFF, FF_CHUNK):
            da = _dot(dfv, w_ref[c0:c0 + FF_CHUNK, :], "nt")
            g = gu_ref[:, c0:c0 + FF_CHUNK].astype(F32)
            u = gu_ref[:, DFF + c0:DFF + c0 + FF_CHUNK].astype(F32)
            outs[0][:, c0:c0 + FF_CHUNK] = (da * u * _silu_grad(g)).astype(BF16)
            outs[0][:, DFF + c0:DFF + c0 + FF_CHUNK] = (da * g * _sigmoid(g)).astype(BF16)

    return _rowwise(fn, "down_dx_swiglu_bwd", df.shape[0], tile, [df, gu], [_layer_of(w3, l)], [(2 * DFF, BF16)])[0]


def _softmax_rows(s):
    e = jnp.exp(s - jnp.max(s, axis=-1, keepdims=True))
    return e / jnp.sum(e, axis=-1, keepdims=True)


def _attn(q, kv, tile):
    def fn(i, ins, outs, _):
        q_ref, kv_ref = ins
        for hd in range(XA_HEADS):
            c0 = hd * XA_HD
            p = _softmax_rows(_dot(q_ref[:, c0:c0 + XA_HD], kv_ref[:, c0:c0 + XA_HD], "nt") * (XA_HD ** -0.5))
            outs[0][:, c0:c0 + XA_HD] = _dot(p.astype(BF16), kv_ref[:, D + c0:D + c0 + XA_HD]).astype(BF16)

    return _rowwise(fn, "attn", q.shape[0], tile, [q], [kv], [(D, BF16)])[0]


def _attn_bwd(q, kv, do, tile):
    def fn(i, ins, outs, _):
        q_ref, do_ref, kv_ref = ins
        for hd in range(XA_HEADS):
            c0 = hd * XA_HD
            qh, kh, vh = q_ref[:, c0:c0 + XA_HD], kv_ref[:, c0:c0 + XA_HD], kv_ref[:, D + c0:D + c0 + XA_HD]
            doh = do_ref[:, c0:c0 + XA_HD]
            p = _softmax_rows(_dot(qh, kh, "nt") * (XA_HD ** -0.5))
            dp = _dot(doh, vh, "nt")
            ds = (p * (dp - jnp.sum(dp * p, axis=-1, keepdims=True)) * (XA_HD ** -0.5)).astype(BF16)
            outs[0][:, c0:c0 + XA_HD] = _dot(ds, kh).astype(BF16)
            dk = _dot(ds, qh, "tn")
            dv = _dot(p.astype(BF16), doh, "tn")

            @pl.when(i == 0)
            def _():
                outs[1][:, c0:c0 + XA_HD] = dk
                outs[1][:, D + c0:D + c0 + XA_HD] = dv

            @pl.when(i > 0)
            def _():
                outs[1][:, c0:c0 + XA_HD] += dk
                outs[1][:, D + c0:D + c0 + XA_HD] += dv

    return _rowwise(fn, "attn_bwd", q.shape[0], tile, [q, do], [kv], [(D, BF16)], [((NMEM, 2 * D), F32)])


def _mem_norm(mem, g):
    def fn(i, ins, outs, _):
        outs[0][...] = (_rms_hat(ins[0][...])[0] * ins[1][...]).astype(BF16)

    return _rowwise(fn, "mem_norm", NMEM, NMEM, [mem], [g], [(D, BF16)])[0]


def _mem_gain_bwd(mem, dmn):
    def fn(i, ins, outs, _):
        outs[0][...] = jnp.sum(ins[1][...] * _rms_hat(ins[0][...])[0], axis=0, keepdims=True)

    return _rowwise(fn, "mem_gain_bwd", NMEM, NMEM, [mem, dmn], [], [], [((1, D), F32)])[0]


def _conv_taps(w_ref, zext_ref, tile, shift0):
    acc = None
    for k in range(CONV_W):
        term = w_ref[k:k + 1, :] * zext_ref[pl.ds(shift0(k), tile), :]
        acc = term if acc is None else acc + term
    return acc


def _tril_mask(transpose=False):
    r, c = (lax.broadcasted_iota(jnp.int32, (CHUNK, CHUNK), a) for a in (0, 1))
    return (r <= c) if transpose else (r >= c)


def _even_core(p, cw, cb, clg, clb, slg, slb, ws, bst, tile):
    def fn(i, ins, outs, scr):
        p_ref, ph_ref, cw_ref, cb_ref, clg_ref, clb_ref, slg_ref, slb_ref, ws_ref, bst_ref = ins
        y_ref, z_ref = outs
        zext = scr[0]
        z = p_ref[:, :DC] * _sigmoid(p_ref[:, DC:2 * DC])
        zh = ph_ref[:, :DC] * _sigmoid(ph_ref[:, DC:2 * DC])
        zext[:HALO, :] = jnp.where(i > 0, zh, 0.0)
        zext[HALO:, :] = z
        z_ref[...] = z
        conv = _conv_taps(cw_ref, zext, tile, lambda k: HALO - (CONV_W - 1) + k) + cb_ref[...]
        yl = _ln_hat(conv)[0] * clg_ref[...] + clb_ref[...]
        y_ref[:, :DC] = (yl * _sigmoid(yl)).astype(BF16)
        zb = _gelu(p_ref[:, 2 * DC:])
        vln = (_ln_hat(zb[:, DC:])[0] * slg_ref[...] + slb_ref[...]).astype(BF16)
        mask = _tril_mask()
        for hd in range(NHEAD_SGU):
            wm = jnp.where(mask, ws_ref[hd], 0.0).astype(BF16)
            for ci in range(tile // CHUNK):
                r0, c0 = ci * CHUNK, hd * CHUNK
                mixed = _dot(wm, vln[r0:r0 + CHUNK, c0:c0 + CHUNK]) + bst_ref[:, hd:hd + 1]
                y_ref[r0:r0 + CHUNK, DC + c0:DC + c0 + CHUNK] = (zb[r0:r0 + CHUNK, c0:c0 + CHUNK] * mixed).astype(BF16)

    return _rowwise(fn, "even_core", p.shape[0], tile, [p, _prev_halo(p, tile, D)], [cw, cb, clg, clb, slg, slb, ws, bst],
                    [(D, BF16), (DC, F32)], scratch=[pltpu.VMEM((tile + HALO, DC), F32)])


def _even_bwd_a(p, z, dy, cw, cb, clg, clb, slg, slb, ws, wst, bst, tile):
    def fn(i, ins, outs, scr):
        p_ref, z_ref, zh_ref, dy_ref, cw_ref, cb_ref, clg_ref, clb_ref, slg_ref, slb_ref, ws_ref, wst_ref, bst_ref = ins
        dc_ref, dpb_ref, vec_ref, dws_ref, dbst_ref = outs
        zext, dvln_s = scr
        zext[:HALO, :] = jnp.where(i > 0, zh_ref[...], 0.0)
        zext[HALO:, :] = z_ref[...]
        conv = _conv_taps(cw_ref, zext, tile, lambda k: HALO - (CONV_W - 1) + k) + cb_ref[...]
        chat, cr = _ln_hat(conv)
        yl = chat * clg_ref[...] + clb_ref[...]
        dyl = dy_ref[:, :DC] * _silu_grad(yl)
        dconv, dclg, dclb = _ln_bwd(chat, cr, clg_ref[...], dyl)
        dc_ref[...] = dconv
        xb = p_ref[:, 2 * DC:]
        zb = _gelu(xb)
        vhat, vr = _ln_hat(zb[:, DC:])
        vln = (vhat * slg_ref[...] + slb_ref[...]).astype(BF16)
        dyb = dy_ref[:, DC:]
        mask, mask_t = _tril_mask(), _tril_mask(transpose=True)
        dbu_parts = []
        for hd in range(NHEAD_SGU):
            wm = jnp.where(mask, ws_ref[hd], 0.0).astype(BF16)
            wt = jnp.where(mask_t, wst_ref[hd], 0.0).astype(BF16)
            dws_h, dbs_h, rows = None, None, []
            for ci in range(tile // CHUNK):
                r0, c0 = ci * CHUNK, hd * CHUNK
                vblk = vln[r0:r0 + CHUNK, c0:c0 + CHUNK]
                mixed = _dot(wm, vblk) + bst_ref[:, hd:hd + 1]
                dyb_blk = dyb[r0:r0 + CHUNK, c0:c0 + CHUNK]
                rows.append(dyb_blk * mixed)
                dmixed = dyb_blk * zb[r0:r0 + CHUNK, c0:c0 + CHUNK]
                dmb = dmixed.astype(BF16)
                dvln_s[r0:r0 + CHUNK, c0:c0 + CHUNK] = _dot(wt, dmb)
                dw = _dot(dmb, vblk, "nt")
                db = jnp.sum(dmixed, axis=-1, keepdims=True)
                dws_h = dw if dws_h is None else dws_h + dw
                dbs_h = db if dbs_h is None else dbs_h + db
            dbu_parts.append(jnp.concatenate(rows, axis=0))
            dws_h = jnp.where(mask, dws_h, 0.0)

            @pl.when(i == 0)
            def _():
                dws_ref[hd] = dws_h
                dbst_ref[:, hd:hd + 1] = dbs_h

            @pl.when(i > 0)
            def _():
                dws_ref[hd] += dws_h
                dbst_ref[:, hd:hd + 1] += dbs_h

        dbu = jnp.concatenate(dbu_parts, axis=1)
        dbv, dslg, dslb = _ln_bwd(vhat, vr, slg_ref[...], dvln_s[...])
        dpb_ref[:, :DC] = (dbu * _gelu_grad(xb[:, :DC])).astype(BF16)
        dpb_ref[:, DC:] = (dbv * _gelu_grad(xb[:, DC:])).astype(BF16)
        @pl.when(i == 0)
        def _():
            vec_ref[...] = jnp.zeros_like(vec_ref)

        for r, val in enumerate([jnp.sum(dconv, axis=0, keepdims=True), dclg, dclb, dslg, dslb]):
            vec_ref[r:r + 1, :] += val

    return _rowwise(fn, "even_bwd_a", p.shape[0], tile, [p, z, _prev_halo(z, tile), dy],
                    [cw, cb, clg, clb, slg, slb, ws, wst, bst], [(DC, F32), (D, BF16)],
                    [((8, DC), F32), ((NHEAD_SGU, CHUNK, CHUNK), F32), ((CHUNK, NHEAD_SGU), F32)],
                    scratch=[pltpu.VMEM((tile + HALO, DC), F32), pltpu.VMEM((tile, DC), F32)])


def _even_bwd_b(p, z, dconv, dpb, cw, tile):
    def fn(i, ins, outs, scr):
        p_ref, z_ref, zh_ref, dc_ref, dcn_ref, dpb_ref, cw_ref = ins
        dp_ref, dcw_ref = outs
        zext, dcext = scr
        last = pl.num_programs(0) - 1
        zext[:HALO, :] = jnp.where(i > 0, zh_ref[...], 0.0)
        zext[HALO:, :] = z_ref[...]
        dcext[:tile, :] = dc_ref[...]
        dcext[tile:, :] = jnp.where(i < last, dcn_ref[...], 0.0)
        dz = _conv_taps(cw_ref, dcext, tile, lambda k: (CONV_W - 1) - k)
        gate = p_ref[:, DC:2 * DC]
        s = _sigmoid(gate)
        dp_ref[:, :DC] = (dz * s).astype(BF16)
        dp_ref[:, DC:2 * DC] = (dz * p_ref[:, :DC] * s * (1.0 - s)).astype(BF16)
        dp_ref[:, 2 * DC:] = dpb_ref[...]
        dcv = dc_ref[...]

        @pl.when(i == 0)
        def _():
            dcw_ref[...] = jnp.zeros_like(dcw_ref)

        for k in range(CONV_W):
            dcw_ref[k:k + 1, :] += jnp.sum(dcv * zext[pl.ds(HALO - (CONV_W - 1) + k, tile), :], axis=0, keepdims=True)

    return _rowwise(fn, "even_bwd_b", p.shape[0], tile,
                    [(p, tile, D, lambda i: (i, 0)), z, _prev_halo(z, tile), dconv, _next_halo(dconv, tile), dpb], [cw],
                    [(2 * D, BF16)], [((HALO, DC), F32)],
                    scratch=[pltpu.VMEM((tile + HALO, DC), F32), pltpu.VMEM((tile + HALO, DC), F32)])


def _row_count(i, tile, nrows, offset, w):
    t = i * tile + offset + lax.broadcasted_iota(jnp.int32, (nrows, POOL_GD), 0)
    return jnp.minimum(t + 1, w).astype(F32)


def _odd_core(p, wg, scale, tile):
    def fn(i, ins, outs, scr):
        p_ref, ph_ref, wg_ref, sc_ref = ins
        d_ref, e_ref, es_ref = outs
        pext = scr[0]
        pext[:HALO, :] = jnp.where(i > 0, ph_ref[...], 0.0)
        pext[HALO:, :] = p_ref[...]
        for g, w in enumerate(POOL_WINDOWS):
            c0 = g * POOL_GD
            s = p_ref[:, c0:c0 + POOL_GD]
            for r in range(1, w):
                s = s + pext[pl.ds(HALO - r, tile), c0:c0 + POOL_GD]
            dg = (s / _row_count(i, tile, tile, 0, w) - p_ref[:, c0:c0 + POOL_GD]).astype(BF16)
            d_ref[:, c0:c0 + POOL_GD] = dg
            e = _dot(dg, wg_ref[g])
            e_ref[:, c0:c0 + POOL_GD] = e
            es_ref[:, c0:c0 + POOL_GD] = (e * sc_ref[:, c0:c0 + POOL_GD]).astype(BF16)

    return _rowwise(fn, "odd_core", p.shape[0], tile, [p, _prev_halo(p, tile)], [wg, scale],
                    [(D, BF16), (D, F32), (D, BF16)], scratch=[pltpu.VMEM((tile + HALO, D), F32)])


def _odd_bwd(des, e, d, wg, scale, tile):
    def fn(i, ins, outs, scr):
        des_ref, desn_ref, e_ref, d_ref, wg_ref, sc_ref = ins
        dp_ref, dsc_ref, dwg_ref = outs
        qext = scr[0]
        last = pl.num_programs(0) - 1
        desv = des_ref[...]
        _accum(dsc_ref, i, jnp.sum(desv * e_ref[...], axis=0, keepdims=True))
        de = (desv * sc_ref[...]).astype(BF16)
        den = (jnp.where(i < last, desn_ref[...], 0.0) * sc_ref[...]).astype(BF16)
        for g, w in enumerate(POOL_WINDOWS):
            c0 = g * POOL_GD
            deg = de[:, c0:c0 + POOL_GD]
            dd = _dot(deg, wg_ref[g], "nt")
            ddn = _dot(den[:, c0:c0 + POOL_GD], wg_ref[g], "nt")
            qext[:tile, c0:c0 + POOL_GD] = dd / _row_count(i, tile, tile, 0, w)
            qext[tile:, c0:c0 + POOL_GD] = ddn / _row_count(i, tile, HALO, tile, w)
            s = qext[:tile, c0:c0 + POOL_GD]
            for r in range(1, w):
                s = s + qext[pl.ds(r, tile), c0:c0 + POOL_GD]
            dp_ref[:, c0:c0 + POOL_GD] = (s - dd).astype(BF16)
            dw = _dot(d_ref[:, c0:c0 + POOL_GD], deg, "tn")

            @pl.when(i == 0)
            def _():
                dwg_ref[g] = dw

            @pl.when(i > 0)
            def _():
                dwg_ref[g] += dw

    return _rowwise(fn, "odd_bwd", des.shape[0], tile, [des, _next_halo(des, tile), e, d], [wg, scale], [(D, BF16)],
                    [((1, D), F32), ((len(POOL_WINDOWS), POOL_GD, POOL_GD), F32)],
                    scratch=[pltpu.VMEM((tile + HALO, D), F32)])


def _axis_slice(ref, axis, start, size):
    idx = [slice(None)] * len(ref.shape)
    idx[axis] = pl.ds(start, size)
    return ref.at[tuple(idx)]


def _cast_into_slot(pos, shard, kind, name):
    L, rs, cs = shard.shape
    tr = _row_tile(rs, 512)
    nr = rs // tr
    if kind == "row":
        full, omap = (L, NCHIP * rs, cs), (lambda l, i, p: (l, p[0] * nr + i, 0))
    else:
        full, omap = (L, rs, NCHIP * cs), (lambda l, i, p: (l, i, p[0]))

    def body(p_ref, s_ref, o_ref):
        o_ref[...] = s_ref[...].astype(BF16)

    grid_spec = pltpu.PrefetchScalarGridSpec(
        num_scalar_prefetch=1, grid=(L, nr), in_specs=[pl.BlockSpec((None, tr, cs), lambda l, i, p: (l, i, 0))],
        out_specs=pl.BlockSpec((None, tr, cs), omap))
    return pl.pallas_call(body, name=name, grid_spec=grid_spec, out_shape=jax.ShapeDtypeStruct(full, BF16),
                          compiler_params=_params(("parallel", "parallel")))(pos, shard)


def _peer(x, y, c, k):
    return (1 - x if k & 4 else x, 1 - y if k & 2 else y, 1 - c if k & 1 else c)


def _half_geometry(shape, kind):
    L, R, C = shape
    return (L, R // (2 * NCHIP), C) if kind == "row" else (L, R // 2, C // NCHIP)


def _sum_into_half(pos, recv, grad, kind, name):
    _, L, rh, cs = recv.shape
    if kind == "row":
        gmap = lambda l, p: (l, 2 * p[0] + p[1], 0)
    else:
        gmap = lambda l, p: (l, p[1], p[0])

    def body(p_ref, r_ref, g_ref, o_ref):
        acc = g_ref[...].astype(F32)
        for s in range(7):
            acc = acc + r_ref[s].astype(F32)
        o_ref[...] = acc

    grid_spec = pltpu.PrefetchScalarGridSpec(
        num_scalar_prefetch=1, grid=(L,),
        in_specs=[pl.BlockSpec((7, None, rh, cs), lambda l, p: (0, l, 0, 0)), pl.BlockSpec((None, rh, cs), gmap)],
        out_specs=pl.BlockSpec((None, rh, cs), lambda l, p: (l, p[1], 0)))
    return pl.pallas_call(body, name=name, grid_spec=grid_spec, out_shape=jax.ShapeDtypeStruct((L, 2 * rh, cs), F32),
                          compiler_params=_params(("parallel",)))(pos, _hbm(recv), _hbm(grad))


def _swap_halves(shards):
    n = len(shards)

    def body(*refs):
        outs = refs[n:2 * n]
        send_sems, recv_sems = refs[2 * n:]
        x, y, c = lax.axis_index("x"), lax.axis_index("y"), lax.axis_index("c")
        copies = []
        for t in range(n):
            rh = outs[t].shape[1] // 2
            half = outs[t].at[:, pl.ds(c * rh, rh), :]
            cp = pltpu.make_async_remote_copy(src_ref=half, dst_ref=half, send_sem=send_sems.at[t], recv_sem=recv_sems.at[t],
                                              device_id=(x, y, 1 - c), device_id_type=MESH)
            cp.start()
            copies.append(cp)
        for cp in copies:
            cp.wait()

    return pl.pallas_call(body, name="swap_halves", in_specs=[HBM] * n, out_specs=[HBM] * n,
                          out_shape=[pltpu.HBM(s.shape, s.dtype) for s in shards],
                          input_output_aliases={t: t for t in range(n)},
                          scratch_shapes=[pltpu.SemaphoreType.DMA((n,)), pltpu.SemaphoreType.DMA((n,))])(*map(_hbm, shards))


SEM = pl.BlockSpec(memory_space=pltpu.SEMAPHORE)
DATAFLOW = pltpu.SideEffectType.DATAFLOW_SIDE_EFFECTING
MAX_LAYERS = 4


def _gsem(t, layer, k):
    return (t * MAX_LAYERS + layer) * 3 + k


def _half_slot(ref, kind, d0, nd0, chip, core):
    if kind == "row":
        half = ref.shape[1] // (2 * NCHIP)
        return ref.at[pl.ds(d0, nd0), pl.ds((2 * chip + core) * half, half), :]
    size, half = ref.shape[2] // NCHIP, ref.shape[1] // 2
    return ref.at[pl.ds(d0, nd0), pl.ds(core * half, half), pl.ds(chip * size, size)]


def _other_chips(x, y):
    return [(1 - x, y, 2 * (1 - x) + y), (x, 1 - y, 2 * x + 1 - y), (1 - x, 1 - y, 2 * (1 - x) + 1 - y)]


def _gather_start(bufs, kinds, groups, after):
    n, na = len(bufs), len(after)

    def body(*refs):
        send_sems, recv_sems = refs[n + na], refs[n + na + 1]
        outs = refs[n + na + 2:]
        x, y, c = lax.axis_index("x"), lax.axis_index("y"), lax.axis_index("c")
        for group in groups:
            for t, li, d0, nd0 in group:
                mine = _half_slot(outs[t], kinds[t], d0, nd0, 2 * x + y, c)
                for k, (px, py, _) in enumerate(_other_chips(x, y)):
                    pltpu.make_async_remote_copy(src_ref=mine, dst_ref=mine, send_sem=send_sems.at[_gsem(t, li, k)],
                                                 recv_sem=recv_sems.at[_gsem(t, li, k)], device_id=(px, py, c),
                                                 device_id_type=MESH).start()

    sems = pltpu.SemaphoreType.DMA((n * MAX_LAYERS * 3,))
    res = pl.pallas_call(body, name="gather_start", in_specs=[HBM] * n + [ANY] * na, out_specs=[SEM, SEM] + [HBM] * n,
                         out_shape=[sems, sems] + [pltpu.HBM(b.shape, b.dtype) for b in bufs],
                         input_output_aliases={t: t + 2 for t in range(n)},
                         compiler_params=pltpu.CompilerParams(has_side_effects=DATAFLOW))(*map(_hbm, bufs), *after)
    return res[0], res[1], list(res[2:])


def _gather_forward(bufs, kinds, group, send_sems, recv_sems, after, name):
    m, na = len(bufs), len(after)

    def body(*refs):
        send, recv = refs[m], refs[m + 1]
        send2, recv2 = refs[m + 2 + na], refs[m + 3 + na]
        outs = refs[m + 4 + na:]
        x, y, c = lax.axis_index("x"), lax.axis_index("y"), lax.axis_index("c")
        for j, (t, li, d0, nd0) in enumerate(group):
            mine = _half_slot(outs[j], kinds[j], d0, nd0, 2 * x + y, c)
            for k, (px, py, chip) in enumerate(_other_chips(x, y)):
                got = _half_slot(outs[j], kinds[j], d0, nd0, chip, c)
                first = pltpu.make_async_remote_copy(src_ref=mine, dst_ref=got, send_sem=send.at[_gsem(t, li, k)],
                                                     recv_sem=recv.at[_gsem(t, li, k)], device_id=(px, py, c),
                                                     device_id_type=MESH)
                first.wait_send()
                first.wait_recv()
                pltpu.make_async_remote_copy(src_ref=got, dst_ref=got, send_sem=send2.at[3 * j + k], recv_sem=recv2.at[3 * j + k],
                                             device_id=(x, y, 1 - c), device_id_type=MESH).start()

    sems = pltpu.SemaphoreType.DMA((3 * m,))
    res = pl.pallas_call(body, name=name, in_specs=[HBM] * m + [SEM, SEM] + [ANY] * na, out_specs=[SEM, SEM] + [HBM] * m,
                         out_shape=[sems, sems] + [pltpu.HBM(b.shape, b.dtype) for b in bufs],
                         input_output_aliases={j: j + 2 for j in range(m)},
                         compiler_params=pltpu.CompilerParams(has_side_effects=DATAFLOW))(*map(_hbm, bufs), send_sems, recv_sems,
                                                                                          *map(_hbm, after))
    return res[0], res[1], list(res[2:])


def _gather_wait(bufs, kinds, group, send_sems, recv_sems, after, name):
    m, na = len(bufs), len(after)

    def body(*refs):
        send, recv = refs[m], refs[m + 1]
        outs = refs[m + 2 + na:]
        x, y, c = lax.axis_index("x"), lax.axis_index("y"), lax.axis_index("c")
        for j, (t, li, d0, nd0) in enumerate(group):
            for k, (_, _, chip) in enumerate(_other_chips(x, y)):
                cp = pltpu.make_async_remote_copy(src_ref=_half_slot(outs[j], kinds[j], d0, nd0, chip, c),
                                                  dst_ref=_half_slot(outs[j], kinds[j], d0, nd0, chip, 1 - c),
                                                  send_sem=send.at[3 * j + k], recv_sem=recv.at[3 * j + k],
                                                  device_id=(x, y, 1 - c), device_id_type=MESH)
                cp.wait_send()
                cp.wait_recv()

    res = pl.pallas_call(body, name=name, in_specs=[HBM] * m + [SEM, SEM] + [ANY] * na, out_specs=[HBM] * m,
                         out_shape=[pltpu.HBM(b.shape, b.dtype) for b in bufs],
                         input_output_aliases={j: j for j in range(m)},
                         compiler_params=pltpu.CompilerParams(has_side_effects=DATAFLOW))(*map(_hbm, bufs), send_sems, recv_sems,
                                                                                          *map(_hbm, after))
    return list(res)


def _gather_tiny(tiny, axes):
    n = len(tiny)
    out_shape = []
    for s_, ax in zip(tiny, axes):
        shp = list(s_.shape)
        shp[ax] *= NCHIP
        out_shape.append(jax.ShapeDtypeStruct(tuple(shp), s_.dtype))

    def body(*refs):
        ins, outs = refs[:n], refs[n:2 * n]
        send_sems, recv_sems, loc_sems = refs[2 * n:]
        x, y, c = lax.axis_index("x"), lax.axis_index("y"), lax.axis_index("c")
        mine = 2 * x + y
        chips = [(1 - x, y), (x, 1 - y), (1 - x, 1 - y)]
        copies = []
        for t in range(n):
            size = ins[t].shape[axes[t]]
            dst = _axis_slice(outs[t], axes[t], mine * size, size)
            copies.append(pltpu.make_async_copy(ins[t], dst, loc_sems.at[t]))
            for k, chip in enumerate(chips):
                copies.append(pltpu.make_async_remote_copy(src_ref=ins[t], dst_ref=dst, send_sem=send_sems.at[t, k],
                                                           recv_sem=recv_sems.at[t, k], device_id=(*chip, c),
                                                           device_id_type=MESH))
        for cp in copies:
            cp.start()
        for cp in copies:
            cp.wait()

    return pl.pallas_call(body, name="gather_tiny", in_specs=[ANY] * n, out_specs=[ANY] * n, out_shape=out_shape,
                          scratch_shapes=[pltpu.SemaphoreType.DMA((n, 3)), pltpu.SemaphoreType.DMA((n, 3)),
                                          pltpu.SemaphoreType.DMA((n,))])(*tiny)


def _grad_slice(ref, kind, d0, nd0, chip, core, rh, cs):
    if kind == "row":
        return ref.at[pl.ds(d0, nd0), pl.ds((2 * chip + core) * rh, rh), :]
    return ref.at[pl.ds(d0, nd0), pl.ds(core * rh, rh), pl.ds(chip * cs, cs)]


def _scatter_start(grads, recv, kinds, items, name, after=()):
    m, na = len(grads), len(after)

    def body(*refs):
        send_sems, recv_sems = refs[2 * m + na], refs[2 * m + na + 1]
        gout, rout = refs[2 * m + na + 2:3 * m + na + 2], refs[3 * m + na + 2:]
        x, y, c = lax.axis_index("x"), lax.axis_index("y"), lax.axis_index("c")
        for t in range(m):
            _, _, rh, cs = rout[t].shape
            d0, nd0 = items[t]
            for k in range(1, 8):
                px, py, pc = _peer(x, y, c, k)
                src = _grad_slice(gout[t], kinds[t], d0, nd0, 2 * px + py, pc, rh, cs)
                pltpu.make_async_remote_copy(src_ref=src, dst_ref=rout[t].at[k - 1, pl.ds(d0, nd0)],
                                             send_sem=send_sems.at[7 * t + k - 1], recv_sem=recv_sems.at[7 * t + k - 1],
                                             device_id=(px, py, pc), device_id_type=MESH).start()

    sems = pltpu.SemaphoreType.DMA((7 * m,))
    res = pl.pallas_call(body, name=name, in_specs=[HBM] * (2 * m) + [ANY] * na, out_specs=[SEM, SEM] + [HBM] * (2 * m),
                         out_shape=[sems, sems] + [pltpu.HBM(a.shape, a.dtype) for a in list(grads) + list(recv)],
                         input_output_aliases={j: j + 2 for j in range(2 * m)},
                         compiler_params=pltpu.CompilerParams(has_side_effects=DATAFLOW))(*map(_hbm, grads), *map(_hbm, recv),
                                                                                          *after)
    return res[0], res[1], list(res[2:m + 2]), list(res[m + 2:])


def _scatter_wait(recv, items, send_sems, recv_sems, name, after=()):
    m = len(recv)

    def body(*refs):
        send, rcv = refs[m], refs[m + 1]
        outs = refs[m + 2 + len(after):]
        x, y, c = lax.axis_index("x"), lax.axis_index("y"), lax.axis_index("c")
        for t in range(m):
            d0, nd0 = items[t]
            for k in range(1, 8):
                land = outs[t].at[k - 1, pl.ds(d0, nd0)]
                cp = pltpu.make_async_remote_copy(src_ref=land, dst_ref=land, send_sem=send.at[7 * t + k - 1],
                                                  recv_sem=rcv.at[7 * t + k - 1], device_id=_peer(x, y, c, k), device_id_type=MESH)
                cp.wait_send()
                cp.wait_recv()

    res = pl.pallas_call(body, name=name, in_specs=[HBM] * m + [SEM, SEM] + [ANY] * len(after), out_specs=[HBM] * m,
                         out_shape=[pltpu.HBM(a.shape, a.dtype) for a in recv],
                         input_output_aliases={j: j for j in range(m)},
                         compiler_params=pltpu.CompilerParams(has_side_effects=DATAFLOW))(*map(_hbm, recv), send_sems, recv_sems,
                                                                                          *map(_hbm, after))
    return list(res)


def _share_small(packed):
    rows = packed.shape[0]

    def body(in_ref, out_ref, send_sems, recv_sems, loc_sem):
        x, y, c = lax.axis_index("x"), lax.axis_index("y"), lax.axis_index("c")
        me = 4 * x + 2 * y + c
        copies = [pltpu.make_async_copy(in_ref, out_ref.at[me], loc_sem)]
        for k in range(1, 8):
            copies.append(pltpu.make_async_remote_copy(src_ref=in_ref, dst_ref=out_ref.at[me], send_sem=send_sems.at[k - 1],
                                                       recv_sem=recv_sems.at[k - 1], device_id=_peer(x, y, c, k),
                                                       device_id_type=MESH))
        for cp in copies:
            cp.start()
        for cp in copies:
            cp.wait()

    return pl.pallas_call(body, name="share_small", in_specs=[ANY], out_specs=ANY,
                          out_shape=jax.ShapeDtypeStruct((8, rows, 128), F32),
                          scratch_shapes=[pltpu.SemaphoreType.DMA((7,)), pltpu.SemaphoreType.DMA((7,)),
                                          pltpu.SemaphoreType.DMA])(packed)


def _sum_slots(buf, tile, name):
    _, rows, cols = buf.shape

    def body(b_ref, o_ref):
        acc = b_ref[0].astype(F32)
        for s in range(1, 8):
            acc = acc + b_ref[s].astype(F32)
        o_ref[...] = acc

    return pl.pallas_call(body, name=name, grid=(rows // tile,), in_specs=[pl.BlockSpec((8, tile, cols), lambda i: (0, i, 0))],
                          out_specs=pl.BlockSpec((tile, cols), lambda i: (i, 0)),
                          out_shape=jax.ShapeDtypeStruct((rows, cols), F32), compiler_params=_params(("parallel",)))(buf)


def _adamw(w, g, m, v, tile, name, after=()):
    rows, cols = w.shape

    def body(w_ref, g_ref, m_ref, v_ref, *rest):
        d_ref, mo_ref, vo_ref = rest[len(after):]
        gv = g_ref[...]
        mn = ADAM_B1 * m_ref[...] + (1.0 - ADAM_B1) * gv
        vn = ADAM_B2 * v_ref[...] + (1.0 - ADAM_B2) * (gv * gv)
        m_hat = mn / (1.0 - ADAM_B1 ** ADAM_STEP)
        v_hat = vn / (1.0 - ADAM_B2 ** ADAM_STEP)
        d_ref[...] = -ADAM_LR * (m_hat / (jnp.sqrt(v_hat) + ADAM_EPS) + ADAM_WD * w_ref[...])
        mo_ref[...] = mn
        vo_ref[...] = vn

    spec = pl.BlockSpec((tile, cols), lambda i: (i, 0))
    sds = jax.ShapeDtypeStruct((rows, cols), F32)
    return pl.pallas_call(body, name=name, grid=(rows // tile,), in_specs=[spec] * 4 + [ANY] * len(after), out_specs=[spec] * 3,
                          out_shape=[sds] * 3, compiler_params=_params(("parallel",)))(w, g, m, v, *map(_hbm, after))


def _row_tile(rows, cap):
    best = 8
    for t in range(8, min(rows, cap) + 1, 8):
        if rows % t == 0:
            best = t
    return best


def _pack(arrs):
    parts = []
    for a in arrs:
        r = a.size // 128
        r8 = -(-r // 8) * 8
        parts.append(jnp.pad(a.reshape(r, 128).astype(F32), ((0, r8 - r), (0, 0))))
    return jnp.concatenate(parts, axis=0)


def _unpack(packed, shapes):
    out, r0 = [], 0
    for shp in shapes:
        r = math.prod(shp) // 128
        out.append(packed[r0:r0 + r].reshape(shp))
        r0 += -(-r // 8) * 8
    return out


BIG = {
    "ffn1_w_gu": "col", "ffn1_w_down": "row", "ev_w_in": "col", "ev_w_out": "row", "od_w_in": "row", "od_w_group": "row",
    "od_w_out": "row", "xa_w_q": "row", "xa_w_kv": "col", "xa_w_o": "row", "ffn2_w_gu": "col", "ffn2_w_down": "row",
}
TINY_SHARDED = {"ev_conv_w": 2, "od_scale": 1}
WEIGHTS = ['ffn1_pre_g', 'ffn1_w_gu', 'ffn1_w_down', 'ffn1_post_g', 'mix_pre_g', 'mix_post_g', 'ev_w_in', 'ev_conv_w',
           'ev_conv_b', 'ev_conv_ln_g', 'ev_conv_ln_b', 'ev_sgu_ln_g', 'ev_sgu_ln_b', 'ev_sgu_w', 'ev_sgu_b', 'ev_w_out',
           'od_w_in', 'od_w_group', 'od_scale', 'od_w_out', 'xa_pre_g', 'xa_mem_g', 'xa_w_q', 'xa_w_kv', 'xa_w_o', 'xa_post_g',
           'ffn2_pre_g', 'ffn2_w_gu', 'ffn2_w_down', 'ffn2_post_g']


def _as3d(a):
    return a.reshape((-1,) + a.shape[-2:]) if a.ndim == 4 else a


class _WeightView:
    def __init__(self, store, view):
        self.store, self.view = store, view

    def __getitem__(self, k):
        return self.view(k)


class _Grads:
    def __init__(self):
        self.buf = {}
        self.fresh = []

    def add(self, name, layer, nlayers, a, b, tm, tn, n_outer=True):
        self.buf[name] = _matmul(a, b, "tn", BF16, "dw_" + name, tm, tn, out_l=layer, out_stack=nlayers,
                                 out_buf=self.buf.get(name), n_outer=n_outer)
        self.fresh.append(name)

    def take(self):
        names, self.fresh = self.fresh, []
        return names


def _local_step(x, mem, target, W, S, tiles=None, before_sub=None, after_sub_bwd=None):
    T, TW, tm = tiles or (min(512, S), min(256, S), min(512, S))
    TF = T
    row = lambda v: v.reshape(1, -1)
    subs = []
    small = {k: [None] * W[k].shape[0] for k in WEIGHTS if k not in BIG}
    g = _Grads()

    def ffn_fwd(tag, l, h, n, g_next):
        gu, a = _up_swiglu(n, W[tag + "_w_gu"], l, TF)
        f, h2, n2 = _post_mm(h, a, W[tag + "_w_down"], l, row(W[tag + "_post_g"][l]), 0.5, g_next, T)
        subs.append(dict(kind="ffn", tag=tag, l=l, h=h, n=n, gu=gu, a=a, f=f, scale=0.5, pre=tag + "_pre_g", post=tag + "_post_g"))
        return h2, n2

    def ffn_bwd(s, df):
        tag, l = s["tag"], s["l"]
        g.add(tag + "_w_down", l, 4, s["a"], df, 256, D, n_outer=False)
        dgu = _down_dx_swiglu_bwd(df, s["gu"], W[tag + "_w_down"], l, TF)
        g.add(tag + "_w_gu", l, 4, s["n"], dgu, D, 512, n_outer=False)
        return dgu, tag + "_w_gu", l, TF

    def xa_fwd(l, h, n, g_next):
        q = _matmul(n, W["xa_w_q"], "nn", BF16, "xa_q", tm, D, b_l=l)
        mn = _mem_norm(mem, row(W["xa_mem_g"][l]))
        kv = _matmul(mn, W["xa_w_kv"], "nn", BF16, "xa_kv", NMEM, D, b_l=l)
        o = _attn(q, kv, T)
        cx, h2, n2 = _post_mm(h, o, W["xa_w_o"], l, row(W["xa_post_g"][l]), 1.0, g_next, T)
        subs.append(dict(kind="xa", l=l, h=h, n=n, q=q, mn=mn, kv=kv, o=o, f=cx, scale=1.0, pre="xa_pre_g", post="xa_post_g"))
        return h2, n2

    def xa_bwd(s, dc):
        l = s["l"]
        do = _matmul(dc, W["xa_w_o"], "nt", BF16, "xa_o_dx", tm, D, b_l=l)
        g.add("xa_w_o", l, 4, s["o"], dc, D, 512)
        dq, dkv = _attn_bwd(s["q"], s["kv"], do, T)
        g.add("xa_w_q", l, 4, s["n"], dq, D, 512)
        dkvb = dkv.astype(BF16)
        g.add("xa_w_kv", l, 4, s["mn"], dkvb, D, 512)
        dmn = _matmul(dkvb, W["xa_w_kv"], "nt", F32, "xa_kv_dx", NMEM, D, b_l=l)
        small["xa_mem_g"][l] = _mem_gain_bwd(mem, dmn)[0]
        return dq, "xa_w_q", l, T

    def even_params(e):
        return (W["ev_conv_w"][e], row(W["ev_conv_b"][e]), row(W["ev_conv_ln_g"][e]), row(W["ev_conv_ln_b"][e]),
                row(W["ev_sgu_ln_g"][e]), row(W["ev_sgu_ln_b"][e]), W["ev_sgu_w"][e])

    def even_fwd(l, h, n, g_next):
        e = l // 2
        p = _matmul(n, W["ev_w_in"], "nn", F32, "ev_in", tm, D, b_l=e)
        cw, cb, clg, clb, slg, slb, ws = even_params(e)
        y, z = _even_core(p, cw, cb, clg, clb, slg, slb, ws, W["ev_sgu_b"][e].T, TW)
        m, h2, n2 = _post_mm(h, y, W["ev_w_out"], e, row(W["mix_post_g"][l]), 1.0, g_next, T)
        subs.append(dict(kind="even", l=l, h=h, n=n, p=p, y=y, z=z, f=m, scale=1.0, pre="mix_pre_g", post="mix_post_g"))
        return h2, n2

    def even_bwd(s, dm):
        l = s["l"]
        e = l // 2
        dy = _matmul(dm, W["ev_w_out"], "nt", F32, "ev_out_dx", tm, D, b_l=e)
        g.add("ev_w_out", e, 2, s["y"], dm, D, 512)
        cw, cb, clg, clb, slg, slb, ws = even_params(e)
        dconv, dpb, vecs, dws, dbst = _even_bwd_a(s["p"], s["z"], dy, cw, cb, clg, clb, slg, slb, ws,
                                                   jnp.swapaxes(ws, 1, 2), W["ev_sgu_b"][e].T, TW)
        dp, dcw = _even_bwd_b(s["p"], s["z"], dconv, dpb, cw, TW)
        for r, name in enumerate(["ev_conv_b", "ev_conv_ln_g", "ev_conv_ln_b", "ev_sgu_ln_g", "ev_sgu_ln_b"]):
            small[name][e] = vecs[r]
        small["ev_sgu_w"][e] = dws
        small["ev_sgu_b"][e] = dbst.T
        small["ev_conv_w"][e] = dcw[:CONV_W]
        g.add("ev_w_in", e, 2, s["n"], dp, D, 512)
        return dp, "ev_w_in", e, T

    def group_w(o):
        ng = len(POOL_WINDOWS)
        return (W["od_w_group"].reshape(-1, POOL_GD, POOL_GD), (ng, POOL_GD, POOL_GD), lambda i: (o, 0, 0))

    def odd_fwd(l, h, n, g_next):
        o = l // 2
        p = _matmul(n, W["od_w_in"], "nn", F32, "od_in", tm, D, b_l=o)
        d, e, es = _odd_core(p, group_w(o), row(W["od_scale"][o]), T)
        m, h2, n2 = _post_mm(h, es, W["od_w_out"], o, row(W["mix_post_g"][l]), 1.0, g_next, T)
        subs.append(dict(kind="odd", l=l, h=h, n=n, d=d, e=e, es=es, f=m, scale=1.0, pre="mix_pre_g", post="mix_post_g"))
        return h2, n2

    def odd_bwd(s, dm):
        l = s["l"]
        o = l // 2
        des = _matmul(dm, W["od_w_out"], "nt", F32, "od_out_dx", tm, D, b_l=o)
        g.add("od_w_out", o, 2, s["es"], dm, D, 512)
        dp, dsc, dwg = _odd_bwd(des, s["e"], s["d"], group_w(o), row(W["od_scale"][o]), T)
        small["od_scale"][o] = dsc[0]
        small["od_w_group_full"][o] = dwg
        g.add("od_w_in", o, 2, s["n"], dp, D, 512)
        return dp, "od_w_in", o, T

    small["od_w_group_full"] = [None, None]
    order = []
    for l in range(4):
        order += [("ffn1", l), ("even" if l % 2 == 0 else "odd", l), ("xa", l), ("ffn2", l)]
    pre_of = {"ffn1": "ffn1_pre_g", "even": "mix_pre_g", "odd": "mix_pre_g", "xa": "xa_pre_g", "ffn2": "ffn2_pre_g"}
    h = x
    n = _prenorm(h, row(W["ffn1_pre_g"][0]), T)
    for idx, (kind, l) in enumerate(order):
        if before_sub is not None:
            before_sub(kind, l, h)
        g_next = row(W[pre_of[order[idx + 1][0]]][order[idx + 1][1]]) if idx + 1 < len(order) else None
        if kind in ("ffn1", "ffn2"):
            h, n = ffn_fwd(kind, l, h, n, g_next)
        elif kind == "xa":
            h, n = xa_fwd(l, h, n, g_next)
        elif kind == "even":
            h, n = even_fwd(l, h, n, g_next)
        else:
            h, n = odd_fwd(l, h, n, g_next)

    top = subs[-1]
    dh, df, loss_acc, dgp = _loss_top(h, target, top["f"], row(W[top["post"]][top["l"]]), top["scale"], T)
    small[top["post"]][top["l"]] = dgp[0]
    for idx in range(len(subs) - 1, -1, -1):
        s = subs[idx]
        da, wname, wl, bt = {"ffn": ffn_bwd, "xa": xa_bwd, "even": even_bwd, "odd": odd_bwd}[s["kind"]](s, df)
        names = g.take()
        deps = [g.buf[k] for k in names]
        if idx > 0:
            sp = subs[idx - 1]
            dh, df, dg_pre, dg_post = _boundary_mm(da, W[wname], wl, s["h"], row(W[s["pre"]][s["l"]]), dh, sp["f"],
                                                   row(W[sp["post"]][sp["l"]]), sp["scale"], bt, deps=deps)
            small[sp["post"]][sp["l"]] = dg_post[0]
        else:
            dh, dg_pre = _boundary_mm(da, W[wname], wl, s["h"], row(W[s["pre"]][s["l"]]), dh, None, None, None, bt, deps=deps)
        small[s["pre"]][s["l"]] = dg_pre[0]
        if after_sub_bwd is not None:
            after_sub_bwd(order[idx][0], s["l"], g.buf, small)
            g.fresh = names + g.fresh
    return loss_acc[0, 0], dh, g.buf, small


def _step(P, S):
    x, mem, target = P["x"][0], P["mem"][0], P["loss_target"][0]
    chip = 2 * lax.axis_index("x") + lax.axis_index("y")
    pos = jnp.stack([chip, lax.axis_index("c")]).astype(jnp.int32)
    order = list(BIG)
    tix = {k: t for t, k in enumerate(order)}
    kinds = [BIG[k] for k in order]
    per = {k: (4 if k == "od_w_group" else 1) for k in order}
    members = {"ffn1": ["ffn1_w_gu", "ffn1_w_down"], "ffn2": ["ffn2_w_gu", "ffn2_w_down"], "xa": ["xa_w_q", "xa_w_kv", "xa_w_o"],
               "even": ["ev_w_in", "ev_w_out"], "odd": ["od_w_in", "od_w_group", "od_w_out"]}
    index_of = lambda kind, l: l // 2 if kind in ("even", "odd") else l
    sub_order = []
    for l in range(4):
        sub_order += [("ffn1", l), ("even" if l % 2 == 0 else "odd", l), ("xa", l), ("ffn2", l)]
    groups = {(kind, l): [(tix[k], index_of(kind, l), index_of(kind, l) * per[k], per[k]) for k in members[kind]]
              for kind, l in sub_order}
    tiny = _gather_tiny([P[k] for k in TINY_SHARDED], list(TINY_SHARDED.values()))
    slots = [_cast_into_slot(pos, _as3d(P[k]), BIG[k], "cast_" + k) for k in order]
    g_send, g_recv, bufs = _gather_start(slots, kinds, [groups[s_] for s_ in sub_order], tiny)
    W = {k: P[k] for k in WEIGHTS if k not in BIG}
    W.update(zip(TINY_SHARDED, tiny))
    W.update(zip(order, bufs))

    def view(k):
        return W[k].reshape(2, 4, POOL_GD, POOL_GD) if k == "od_w_group" else W[k]

    Wv = _WeightView(W, view)

    forwarded = {}

    def forward(sub, after):
        ks = members[sub[0]]
        s2, r2, thru = _gather_forward([W[k] for k in ks], [BIG[k] for k in ks], groups[sub], g_send, g_recv, after,
                                       "gather_forward_%s%d" % sub)
        W.update(zip(ks, thru))
        forwarded[sub] = (s2, r2)

    def before_sub(kind, l, h):
        sub = (kind, l)
        if sub == sub_order[0]:
            forward(sub, (h,))
        ks = members[kind]
        s2, r2 = forwarded.pop(sub)
        W.update(zip(ks, _gather_wait([W[k] for k in ks], [BIG[k] for k in ks], groups[sub], s2, r2, (h,),
                                      "gather_wait_%s%d" % sub)))
        nxt = sub_order.index(sub) + 1
        if nxt < len(sub_order):
            forward(sub_order[nxt], (h, W[ks[0]]))

    recv = {k: None for k in order}
    pending = []
    small_names = [k for k in WEIGHTS if k not in BIG]
    gsmall = {}

    def after_sub_bwd(kind, l, big, small):
        ks = [k for k in members[kind] if k != "od_w_group"]
        items = [(index_of(kind, l), 1)] * len(ks)
        after = ()
        if (kind, l) == ("odd", 1):
            big["od_w_group"] = jnp.stack(small.pop("od_w_group_full")).astype(BF16).reshape(8, POOL_GD, POOL_GD)
            ks, items = ks + ["od_w_group"], items + [(0, 8)]
        if (kind, l) == ("ffn1", 0):
            small_full = [jnp.stack(small[k]) for k in small_names]
            packed = _pack(small_full)
            summed = _sum_slots(_share_small(packed), _row_tile(packed.shape[0], 512), "sum_small")
            gsmall.update(zip(small_names, _unpack(summed, [a.shape for a in small_full])))
            after = (summed,)
        for k in ks:
            if recv[k] is None:
                recv[k] = lax.empty((7,) + _half_geometry(big[k].shape, BIG[k]), BF16)
        s_send, s_recv, g_thru, r_thru = _scatter_start([big[k] for k in ks], [recv[k] for k in ks], [BIG[k] for k in ks],
                                                        items, "scatter_start_%s%d" % (kind, l), after)
        big.update(zip(ks, g_thru))
        recv.update(zip(ks, r_thru))
        pending.append((kind, l, ks, items, s_send, s_recv))

    loss, grad_x, big, small = _local_step(x, mem, target, Wv, S, before_sub=before_sub, after_sub_bwd=after_sub_bwd)
    last = (big[members["ffn1"][0]],)
    for kind, l, ks, items, s_send, s_recv in pending:
        recv.update(zip(ks, _scatter_wait([recv[k] for k in ks], items, s_send, s_recv, "scatter_wait_%s%d" % (kind, l), last)))
    gsh = dict(zip(order, _swap_halves([_sum_into_half(pos, recv[k], big[k], BIG[k], "sum_" + k) for k in order])))
    for k, ax in TINY_SHARDED.items():
        size = P[k].shape[ax]
        gsmall[k] = lax.dynamic_slice_in_dim(gsmall[k], chip * size, size, axis=ax)

    grads, delta, new_m, new_v = {}, {}, {}, {}
    for k in order:
        shp = P[k].shape
        cols = shp[-1]
        flat = lambda a: a.reshape(-1, cols)
        gk = flat(gsh[k])
        d_, m_, v_ = _adamw(flat(P[k]), gk, flat(P["m_" + k]), flat(P["v_" + k]), _row_tile(gk.shape[0], 256), "adamw_" + k)
        grads[k], delta[k], new_m[k], new_v[k] = gk.reshape(shp), d_.reshape(shp), m_.reshape(shp), v_.reshape(shp)
    pk = lambda pre: _pack([P[pre + k] for k in small_names])
    d_, m_, v_ = _adamw(pk(""), _pack([gsmall[k] for k in small_names]), pk("m_"), pk("v_"),
                        pk("").shape[0], "adamw_small", last)
    shapes = [P[k].shape for k in small_names]
    for dst, src in ((delta, d_), (new_m, m_), (new_v, v_)):
        dst.update(zip(small_names, _unpack(src, shapes)))
    grads.update(gsmall)

    loss = lax.psum(loss, ("x", "y", "c"))
    out = [loss, grad_x[None]]
    for grp in (grads, delta, new_m, new_v):
        out += [grp[k] for k in WEIGHTS]
    return tuple(out)


def kernel(x, mem, ffn1_pre_g, ffn1_w_gu, ffn1_w_down, ffn1_post_g, mix_pre_g, mix_post_g, ev_w_in, ev_conv_w, ev_conv_b, ev_conv_ln_g, ev_conv_ln_b, ev_sgu_ln_g, ev_sgu_ln_b, ev_sgu_w, ev_sgu_b, ev_w_out, od_w_in, od_w_group, od_scale, od_w_out, xa_pre_g, xa_mem_g, xa_w_q, xa_w_kv, xa_w_o, xa_post_g, ffn2_pre_g, ffn2_w_gu, ffn2_w_down, ffn2_post_g, loss_target, m_ffn1_pre_g, m_ffn1_w_gu, m_ffn1_w_down, m_ffn1_post_g, m_mix_pre_g, m_mix_post_g, m_ev_w_in, m_ev_conv_w, m_ev_conv_b, m_ev_conv_ln_g, m_ev_conv_ln_b, m_ev_sgu_ln_g, m_ev_sgu_ln_b, m_ev_sgu_w, m_ev_sgu_b, m_ev_w_out, m_od_w_in, m_od_w_group, m_od_scale, m_od_w_out, m_xa_pre_g, m_xa_mem_g, m_xa_w_q, m_xa_w_kv, m_xa_w_o, m_xa_post_g, m_ffn2_pre_g, m_ffn2_w_gu, m_ffn2_w_down, m_ffn2_post_g, v_ffn1_pre_g, v_ffn1_w_gu, v_ffn1_w_down, v_ffn1_post_g, v_mix_pre_g, v_mix_post_g, v_ev_w_in, v_ev_conv_w, v_ev_conv_b, v_ev_conv_ln_g, v_ev_conv_ln_b, v_ev_sgu_ln_g, v_ev_sgu_ln_b, v_ev_sgu_w, v_ev_sgu_b, v_ev_w_out, v_od_w_in, v_od_w_group, v_od_scale, v_od_w_out, v_xa_pre_g, v_xa_mem_g, v_xa_w_q, v_xa_w_kv, v_xa_w_o, v_xa_post_g, v_ffn2_pre_g, v_ffn2_w_gu, v_ffn2_w_down, v_ffn2_post_g):
    P = dict(locals())
    return _step(P, x.shape[1])
```

```python
import functools
import math

import jax
import jax.numpy as jnp
from jax import lax
from jax.experimental import pallas as pl
from jax.experimental.pallas import tpu as pltpu

F32 = jnp.float32
BF16 = jnp.bfloat16
MESH = pl.DeviceIdType.MESH
ANY = pl.BlockSpec(memory_space=pl.ANY)
HBM = pl.BlockSpec(memory_space=pltpu.HBM)


def _hbm(x):
    return pltpu.with_memory_space_constraint(x, pltpu.HBM)

D = 1024
DFF = 2816
NMEM = 256
DC = 512
CONV_W = 31
CHUNK = 128
NHEAD_SGU = 4
POOL_WINDOWS = (2, 4, 8, 16)
POOL_GD = 256
XA_HEADS = 4
XA_HD = 256
EPS = 1e-6
NCHIP = 4
HALO = 32

ADAM_LR, ADAM_B1, ADAM_B2, ADAM_EPS, ADAM_WD, ADAM_STEP = 0.001, 0.9, 0.999, 1e-08, 0.01, 10

V7X_VMEM_BYTES = 64 * 1024 * 1024
VMEM_LIMIT = V7X_VMEM_BYTES - 8 * 1024 * 1024


def _params(sem):
    return pltpu.CompilerParams(dimension_semantics=sem, vmem_limit_bytes=VMEM_LIMIT)


def _matmul(a, b, kind, out_dtype, name, tm, tn, a_l=None, b_l=None, out_l=None, out_stack=None, out_buf=None,
            n_outer=True):
    a2, b2 = a.shape[-2:], b.shape[-2:]
    if kind == "nn":
        (m, k), (k2, n) = a2, b2
        dims = (((1,), (0,)), ((), ()))
    elif kind == "nt":
        (m, k), (n, k2) = a2, b2
        dims = (((1,), (1,)), ((), ()))
    else:
        (k, m), (k2, n) = a2, b2
        dims = (((0,), (0,)), ((), ()))
    assert k == k2 and m % tm == 0 and n % tn == 0, (name, a.shape, b.shape, tm, tn)
    if n_outer:
        grid = (n // tn, m // tm)
        ij = lambda p, q: (q, p)
    else:
        grid = (m // tm, n // tn)
        ij = lambda p, q: (p, q)

    def spec(arr, layer, blk, idx):
        if arr.ndim == 3:
            return pl.BlockSpec((None,) + blk, lambda p, q: (layer,) + idx(*ij(p, q)))
        return pl.BlockSpec(blk, lambda p, q: idx(*ij(p, q)))

    a_spec = spec(a, a_l, (k, tm) if kind == "tn" else (tm, k), (lambda i, j: (0, i)) if kind == "tn" else (lambda i, j: (i, 0)))
    b_spec = spec(b, b_l, (tn, k) if kind == "nt" else (k, tn), (lambda i, j: (j, 0)) if kind == "nt" else (lambda i, j: (0, j)))
    if out_stack is None:
        out_shape = jax.ShapeDtypeStruct((m, n), out_dtype)
        o_spec = pl.BlockSpec((tm, tn), lambda p, q: ij(p, q))
    else:
        out_shape = jax.ShapeDtypeStruct((out_stack, m, n), out_dtype)
        o_spec = pl.BlockSpec((None, tm, tn), lambda p, q: (out_l,) + ij(p, q))

    def body(a_ref, b_ref, *rest):
        o_ref = rest[-1]
        o_ref[...] = lax.dot_general(a_ref[...], b_ref[...], dims, preferred_element_type=F32).astype(o_ref.dtype)

    in_specs, args, aliases = [a_spec, b_spec], [a, _hbm(b) if b.ndim == 3 else b], {}
    if out_buf is not None:
        in_specs.append(ANY)
        args.append(_hbm(out_buf))
        aliases = {2: 0}
    return pl.pallas_call(body, name=name, grid=grid, in_specs=in_specs, out_specs=o_spec, out_shape=out_shape,
                          input_output_aliases=aliases, compiler_params=_params(("parallel", "parallel")))(*args)


def _rowwise(fn, name, rows, tile, row_ins, full_ins, row_outs, acc_outs=(), scratch=(), deps=()):
    assert rows % tile == 0, (name, rows, tile)
    in_specs, args = [], []
    for r in row_ins:
        if isinstance(r, tuple):
            arr, br, bc, imap = r
            in_specs.append(pl.BlockSpec((br, bc), imap))
        else:
            arr = r
            in_specs.append(pl.BlockSpec((tile, arr.shape[1]), lambda i: (i, 0)))
        args.append(arr)
    for f in full_ins:
        if isinstance(f, tuple):
            arr, blk, imap = f
            in_specs.append(pl.BlockSpec(blk, imap, pipeline_mode=pl.Buffered(1)))
            args.append(_hbm(arr))
        else:
            in_specs.append(pl.BlockSpec(f.shape, functools.partial(lambda nd, i: (0,) * nd, f.ndim)))
            args.append(f)
    for d in deps:
        in_specs.append(ANY)
        args.append(_hbm(d))
    out_specs, out_shape = [], []
    for w, dt in row_outs:
        out_specs.append(pl.BlockSpec((tile, w), lambda i: (i, 0)))
        out_shape.append(jax.ShapeDtypeStruct((rows, w), dt))
    for shp, dt in acc_outs:
        out_specs.append(pl.BlockSpec(shp, functools.partial(lambda nd, i: (0,) * nd, len(shp))))
        out_shape.append(jax.ShapeDtypeStruct(shp, dt))
    n_in, n_out = len(args), len(out_shape)

    def body(*refs):
        fn(pl.program_id(0), refs[:n_in], refs[n_in:n_in + n_out], refs[n_in + n_out:])

    sem = ("arbitrary",) if acc_outs else ("parallel",)
    res = pl.pallas_call(body, name=name, grid=(rows // tile,), in_specs=in_specs, out_specs=out_specs, out_shape=out_shape,
                         scratch_shapes=list(scratch), compiler_params=_params(sem))(*args)
    return res


def _accum(ref, i, val):
    @pl.when(i == 0)
    def _():
        ref[...] = val

    @pl.when(i > 0)
    def _():
        ref[...] += val


def _prev_halo(arr, tile, cols=None):
    r = tile // HALO
    return (arr, HALO, cols or arr.shape[1], lambda i: (jnp.maximum(i * r - 1, 0), 0))


def _next_halo(arr, tile, cols=None):
    r = tile // HALO
    last = arr.shape[0] // HALO - 1
    return (arr, HALO, cols or arr.shape[1], lambda i: (jnp.minimum((i + 1) * r, last), 0))


def _sigmoid(x):
    return 1.0 / (1.0 + jnp.exp(-x))


def _rms_hat(x):
    r = lax.rsqrt(jnp.mean(x * x, axis=-1, keepdims=True) + EPS)
    return x * r, r


def _rms_bwd(x, g, dy):
    xhat, r = _rms_hat(x)
    dxhat = dy * g
    dx = r * (dxhat - xhat * jnp.mean(dxhat * xhat, axis=-1, keepdims=True))
    return dx, jnp.sum(dy * xhat, axis=0, keepdims=True)


def _ln_hat(x):
    mu = jnp.mean(x, axis=-1, keepdims=True)
    xc = x - mu
    r = lax.rsqrt(jnp.mean(xc * xc, axis=-1, keepdims=True) + EPS)
    return xc * r, r


def _ln_bwd(xhat, r, g, dy):
    dxhat = dy * g
    dx = r * (dxhat - jnp.mean(dxhat, axis=-1, keepdims=True) - xhat * jnp.mean(dxhat * xhat, axis=-1, keepdims=True))
    return dx, jnp.sum(dy * xhat, axis=0, keepdims=True), jnp.sum(dy, axis=0, keepdims=True)


def _silu_grad(x):
    s = _sigmoid(x)
    return s * (1.0 + x * (1.0 - s))


_SQRT_HALF = math.sqrt(0.5)
_INV_SQRT_2PI = 1.0 / math.sqrt(2.0 * math.pi)


def _gelu(x):
    return 0.5 * x * (1.0 + lax.erf(x * _SQRT_HALF))


def _gelu_grad(x):
    return 0.5 * (1.0 + lax.erf(x * _SQRT_HALF)) + x * jnp.exp(-0.5 * x * x) * _INV_SQRT_2PI


def _dot(a, b, kind="nn"):
    dims = {"nn": (((1,), (0,)), ((), ())), "nt": (((1,), (1,)), ((), ())), "tn": (((0,), (0,)), ((), ()))}[kind]
    return lax.dot_general(a, b, dims, preferred_element_type=F32)


def _prenorm(h, g, tile):
    def fn(i, ins, outs, _):
        h_ref, g_ref = ins
        outs[0][...] = (_rms_hat(h_ref[...])[0] * g_ref[...]).astype(BF16)

    return _rowwise(fn, "prenorm", h.shape[0], tile, [h], [g], [(D, BF16)])[0]


def _loss_top(h, target, f, g_post, scale, tile):
    def fn(i, ins, outs, _):
        err = ins[0][...] - ins[1][...]
        per_row = jnp.mean(err * err, axis=-1, keepdims=True)
        _accum(outs[2], i, jnp.broadcast_to(0.5 * jnp.sum(per_row, axis=0, keepdims=True), (1, 128)))
        dh = err * (1.0 / D)
        outs[0][...] = dh
        df, dg = _rms_bwd(ins[2][...], ins[3][...], scale * dh)
        outs[1][...] = df.astype(BF16)
        _accum(outs[3], i, dg)

    return _rowwise(fn, "loss_top", h.shape[0], tile, [h, target, f], [g_post], [(D, F32), (D, BF16)],
                    [((1, 128), F32), ((1, D), F32)])


ROW_BLOCK = 256


def _row_blocks(tile):
    return [pl.ds(r0, min(ROW_BLOCK, tile)) for r0 in range(0, tile, ROW_BLOCK)]


def _layer_of(w3, l):
    return (w3, (None,) + tuple(w3.shape[1:]), lambda i: (l, 0, 0))


def _post_mm(h, a, w3, l, g_post, scale, g_next, tile):
    def fn(i, ins, outs, _):
        for r in _row_blocks(tile):
            f = _dot(ins[1][r, :], ins[2][...])
            outs[0][r, :] = f
            hn = ins[0][r, :] + scale * (_rms_hat(f)[0] * ins[3][...])
            outs[1][r, :] = hn
            if g_next is not None:
                outs[2][r, :] = (_rms_hat(hn)[0] * ins[4][...]).astype(BF16)

    fulls = [_layer_of(w3, l), g_post] + ([g_next] if g_next is not None else [])
    outs = [(D, F32), (D, F32)] + ([(D, BF16)] if g_next is not None else [])
    res = _rowwise(fn, "post_mm", h.shape[0], tile, [h, a], fulls, outs)
    return res[0], res[1], (res[2] if g_next is not None else None)


def _boundary_mm(a, w3, l, h, g_pre, dh_in, f_prev, g_post_prev, scale_prev, tile, deps=()):
    has_prev = f_prev is not None

    def fn(i, ins, outs, _):
        a_ref, h_ref, dhin_ref = ins[:3]
        w_ref, g_ref = ins[3 + has_prev], ins[4 + has_prev]
        dg_sum, dgp_sum = None, None
        for r in _row_blocks(tile):
            dn = _dot(a_ref[r, :], w_ref[...], "nt")
            dx, dg = _rms_bwd(h_ref[r, :], g_ref[...], dn)
            dh = dhin_ref[r, :] + dx
            outs[0][r, :] = dh
            dg_sum = dg if dg_sum is None else dg_sum + dg
            if has_prev:
                df, dgp = _rms_bwd(ins[3][r, :], ins[6][...], scale_prev * dh)
                outs[1][r, :] = df.astype(BF16)
                dgp_sum = dgp if dgp_sum is None else dgp_sum + dgp
        _accum(outs[1 + has_prev], i, dg_sum)
        if has_prev:
            _accum(outs[3], i, dgp_sum)

    rows = [a, h, dh_in] + ([f_prev] if has_prev else [])
    fulls = [_layer_of(w3, l), g_pre] + ([g_post_prev] if has_prev else [])
    outs = [(D, F32)] + ([(D, BF16)] if has_prev else [])
    accs = [((1, D), F32)] + ([((1, D), F32)] if has_prev else [])
    return _rowwise(fn, "boundary_mm", h.shape[0], tile, rows, fulls, outs, accs, deps=deps)


FF_CHUNK = DFF // 2


def _up_swiglu(n, w3, l, tile):
    def fn(i, ins, outs, _):
        n_ref, w_ref = ins
        nv = n_ref[...]
        for c0 in range(0, DFF, FF_CHUNK):
            g = _dot(nv, w_ref[:, c0:c0 + FF_CHUNK])
            u = _dot(nv, w_ref[:, DFF + c0:DFF + c0 + FF_CHUNK])
            outs[0][:, c0:c0 + FF_CHUNK] = g.astype(BF16)
            outs[0][:, DFF + c0:DFF + c0 + FF_CHUNK] = u.astype(BF16)
            outs[1][:, c0:c0 + FF_CHUNK] = (g * _sigmoid(g) * u).astype(BF16)

    return _rowwise(fn, "up_swiglu", n.shape[0], tile, [n], [_layer_of(w3, l)], [(2 * DFF, BF16), (DFF, BF16)])


def _down_dx_swiglu_bwd(df, gu, w3, l, tile):
    def fn(i, ins, outs, _):
        df_ref, gu_ref, w_ref = ins
        dfv = df_ref[...]
        for c0 in range(0, DFF, FF_CHUNK):
            da = _dot(dfv, w_ref[c0:c0 + FF_CHUNK, :], "nt")
            g = gu_ref[:, c0:c0 + FF_CHUNK].astype(F32)
            u = gu_ref[:, DFF + c0:DFF + c0 + FF_CHUNK].astype(F32)
            outs[0][:, c0:c0 + FF_CHUNK] = (da * u * _silu_grad(g)).astype(BF16)
            outs[0][:, DFF + c0:DFF + c0 + FF_CHUNK] = (da * g * _sigmoid(g)).astype(BF16)

    return _rowwise(fn, "down_dx_swiglu_bwd", df.shape[0], tile, [df, gu], [_layer_of(w3, l)], [(2 * DFF, BF16)])[0]


def _softmax_rows(s):
    e = jnp.exp(s - jnp.max(s, axis=-1, keepdims=True))
    return e / jnp.sum(e, axis=-1, keepdims=True)


def _attn(q, kv, tile):
    def fn(i, ins, outs, _):
        q_ref, kv_ref = ins
        for hd in range(XA_HEADS):
            c0 = hd * XA_HD
            p = _softmax_rows(_dot(q_ref[:, c0:c0 + XA_HD], kv_ref[:, c0:c0 + XA_HD], "nt") * (XA_HD ** -0.5))
            outs[0][:, c0:c0 + XA_HD] = _dot(p.astype(BF16), kv_ref[:, D + c0:D + c0 + XA_HD]).astype(BF16)

    return _rowwise(fn, "attn", q.shape[0], tile, [q], [kv], [(D, BF16)])[0]


def _attn_bwd(q, kv, do, tile):
    def fn(i, ins, outs, _):
        q_ref, do_ref, kv_ref = ins
        for hd in range(XA_HEADS):
            c0 = hd * XA_HD
            qh, kh, vh = q_ref[:, c0:c0 + XA_HD], kv_ref[:, c0:c0 + XA_HD], kv_ref[:, D + c0:D + c0 + XA_HD]
            doh = do_ref[:, c0:c0 + XA_HD]
            p = _softmax_rows(_dot(qh, kh, "nt") * (XA_HD ** -0.5))
            dp = _dot(doh, vh, "nt")
            ds = (p * (dp - jnp.sum(dp * p, axis=-1, keepdims=True)) * (XA_HD ** -0.5)).astype(BF16)
            outs[0][:, c0:c0 + XA_HD] = _dot(ds, kh).astype(BF16)
            dk = _dot(ds, qh, "tn")
            dv = _dot(p.astype(BF16), doh, "tn")

            @pl.when(i == 0)
            def _():
                outs[1][:, c0:c0 + XA_HD] = dk
                outs[1][:, D + c0:D + c0 + XA_HD] = dv

            @pl.when(i > 0)
            def _():
                outs[1][:, c0:c0 + XA_HD] += dk
                outs[1][:, D + c0:D + c0 + XA_HD] += dv

    return _rowwise(fn, "attn_bwd", q.shape[0], tile, [q, do], [kv], [(D, BF16)], [((NMEM, 2 * D), F32)])


def _mem_norm(mem, g):
    def fn(i, ins, outs, _):
        outs[0][...] = (_rms_hat(ins[0][...])[0] * ins[1][...]).astype(BF16)

    return _rowwise(fn, "mem_norm", NMEM, NMEM, [mem], [g], [(D, BF16)])[0]


def _mem_gain_bwd(mem, dmn):
    def fn(i, ins, outs, _):
        outs[0][...] = jnp.sum(ins[1][...] * _rms_hat(ins[0][...])[0], axis=0, keepdims=True)

    return _rowwise(fn, "mem_gain_bwd", NMEM, NMEM, [mem, dmn], [], [], [((1, D), F32)])[0]


SHIFT_ROWS = 8


def _shifted_scratch(tile):
    return pltpu.VMEM((SHIFT_ROWS - 1, tile + HALO - SHIFT_ROWS, DC), F32)


def _fill_shifted(zs_ref, zext_ref):
    rows = zs_ref.shape[1]
    for b in range(1, SHIFT_ROWS):
        zs_ref[b - 1] = zext_ref[pl.ds(b, rows), :]


def _rows_at(zext_ref, zs_ref, offset, tile):
    a, b = divmod(offset, SHIFT_ROWS)
    if b == 0:
        return zext_ref[pl.ds(offset, tile), :]
    return zs_ref[b - 1, pl.ds(SHIFT_ROWS * a, tile), :]


def _conv_taps(w_ref, zext_ref, zs_ref, tile, shift0):
    acc = None
    for k in range(CONV_W):
        term = w_ref[k:k + 1, :] * _rows_at(zext_ref, zs_ref, shift0(k), tile)
        acc = term if acc is None else acc + term
    return acc


def _tril_mask(transpose=False):
    r, c = (lax.broadcasted_iota(jnp.int32, (CHUNK, CHUNK), a) for a in (0, 1))
    return (r <= c) if transpose else (r >= c)


def _even_core(p, cw, cb, clg, clb, slg, slb, ws, bst, tile):
    def fn(i, ins, outs, scr):
        p_ref, ph_ref, cw_ref, cb_ref, clg_ref, clb_ref, slg_ref, slb_ref, ws_ref, bst_ref = ins
        y_ref, z_ref = outs
        zext = scr[0]
        z = p_ref[:, :DC] * _sigmoid(p_ref[:, DC:2 * DC])
        zh = ph_ref[:, :DC] * _sigmoid(ph_ref[:, DC:2 * DC])
        zext[:HALO, :] = jnp.where(i > 0, zh, 0.0)
        zext[HALO:, :] = z
        z_ref[...] = z
        _fill_shifted(scr[1], zext)
        conv = _conv_taps(cw_ref, zext, scr[1], tile, lambda k: HALO - (CONV_W - 1) + k) + cb_ref[...]
        yl = _ln_hat(conv)[0] * clg_ref[...] + clb_ref[...]
        y_ref[:, :DC] = (yl * _sigmoid(yl)).astype(BF16)
        zb = _gelu(p_ref[:, 2 * DC:])
        vln = (_ln_hat(zb[:, DC:])[0] * slg_ref[...] + slb_ref[...]).astype(BF16)
        mask = _tril_mask()
        for hd in range(NHEAD_SGU):
            wm = jnp.where(mask, ws_ref[hd], 0.0).astype(BF16)
            for ci in range(tile // CHUNK):
                r0, c0 = ci * CHUNK, hd * CHUNK
                mixed = _dot(wm, vln[r0:r0 + CHUNK, c0:c0 + CHUNK]) + bst_ref[:, hd:hd + 1]
                y_ref[r0:r0 + CHUNK, DC + c0:DC + c0 + CHUNK] = (zb[r0:r0 + CHUNK, c0:c0 + CHUNK] * mixed).astype(BF16)

    return _rowwise(fn, "even_core", p.shape[0], tile, [p, _prev_halo(p, tile, D)], [cw, cb, clg, clb, slg, slb, ws, bst],
                    [(D, BF16), (DC, F32)], scratch=[pltpu.VMEM((tile + HALO, DC), F32), _shifted_scratch(tile)])


def _even_bwd_a(p, z, dy, cw, cb, clg, clb, slg, slb, ws, wst, bst, tile):
    def fn(i, ins, outs, scr):
        p_ref, z_ref, zh_ref, dy_ref, cw_ref, cb_ref, clg_ref, clb_ref, slg_ref, slb_ref, ws_ref, wst_ref, bst_ref = ins
        dc_ref, dpb_ref, vec_ref, dws_ref, dbst_ref = outs
        zext, dvln_s, zs = scr
        zext[:HALO, :] = jnp.where(i > 0, zh_ref[...], 0.0)
        zext[HALO:, :] = z_ref[...]
        _fill_shifted(zs, zext)
        conv = _conv_taps(cw_ref, zext, zs, tile, lambda k: HALO - (CONV_W - 1) + k) + cb_ref[...]
        chat, cr = _ln_hat(conv)
        yl = chat * clg_ref[...] + clb_ref[...]
        dyl = dy_ref[:, :DC] * _silu_grad(yl)
        dconv, dclg, dclb = _ln_bwd(chat, cr, clg_ref[...], dyl)
        dc_ref[...] = dconv
        xb = p_ref[:, 2 * DC:]
        zb = _gelu(xb)
        vhat, vr = _ln_hat(zb[:, DC:])
        vln = (vhat * slg_ref[...] + slb_ref[...]).astype(BF16)
        dyb = dy_ref[:, DC:]
        mask, mask_t = _tril_mask(), _tril_mask(transpose=True)
        dbu_parts = []
        for hd in range(NHEAD_SGU):
            wm = jnp.where(mask, ws_ref[hd], 0.0).astype(BF16)
            wt = jnp.where(mask_t, wst_ref[hd], 0.0).astype(BF16)
            dws_h, dbs_h, rows = None, None, []
            for ci in range(tile // CHUNK):
                r0, c0 = ci * CHUNK, hd * CHUNK
                vblk = vln[r0:r0 + CHUNK, c0:c0 + CHUNK]
                mixed = _dot(wm, vblk) + bst_ref[:, hd:hd + 1]
                dyb_blk = dyb[r0:r0 + CHUNK, c0:c0 + CHUNK]
                rows.append(dyb_blk * mixed)
                dmixed = dyb_blk * zb[r0:r0 + CHUNK, c0:c0 + CHUNK]
                dmb = dmixed.astype(BF16)
                dvln_s[r0:r0 + CHUNK, c0:c0 + CHUNK] = _dot(wt, dmb)
                dw = _dot(dmb, vblk, "nt")
                db = jnp.sum(dmixed, axis=-1, keepdims=True)
                dws_h = dw if dws_h is None else dws_h + dw
                dbs_h = db if dbs_h is None else dbs_h + db
            dbu_parts.append(jnp.concatenate(rows, axis=0))
            dws_h = jnp.where(mask, dws_h, 0.0)

            @pl.when(i == 0)
            def _():
                dws_ref[hd] = dws_h
                dbst_ref[:, hd:hd + 1] = dbs_h

            @pl.when(i > 0)
            def _():
                dws_ref[hd] += dws_h
                dbst_ref[:, hd:hd + 1] += dbs_h

        dbu = jnp.concatenate(dbu_parts, axis=1)
        dbv, dslg, dslb = _ln_bwd(vhat, vr, slg_ref[...], dvln_s[...])
        dpb_ref[:, :DC] = (dbu * _gelu_grad(xb[:, :DC])).astype(BF16)
        dpb_ref[:, DC:] = (dbv * _gelu_grad(xb[:, DC:])).astype(BF16)
        @pl.when(i == 0)
        def _():
            vec_ref[...] = jnp.zeros_like(vec_ref)

        for r, val in enumerate([jnp.sum(dconv, axis=0, keepdims=True), dclg, dclb, dslg, dslb]):
            vec_ref[r:r + 1, :] += val

    return _rowwise(fn, "even_bwd_a", p.shape[0], tile, [p, z, _prev_halo(z, tile), dy],
                    [cw, cb, clg, clb, slg, slb, ws, wst, bst], [(DC, F32), (D, BF16)],
                    [((8, DC), F32), ((NHEAD_SGU, CHUNK, CHUNK), F32), ((CHUNK, NHEAD_SGU), F32)],
                    scratch=[pltpu.VMEM((tile + HALO, DC), F32), pltpu.VMEM((tile, DC), F32), _shifted_scratch(tile)])


def _even_bwd_b(p, z, dconv, dpb, cw, tile):
    def fn(i, ins, outs, scr):
        p_ref, z_ref, zh_ref, dc_ref, dcn_ref, dpb_ref, cw_ref = ins
        dp_ref, dcw_ref = outs
        zext, dcext, zs, dcs = scr
        last = pl.num_programs(0) - 1
        zext[:HALO, :] = jnp.where(i > 0, zh_ref[...], 0.0)
        zext[HALO:, :] = z_ref[...]
        dcext[:tile, :] = dc_ref[...]
        dcext[tile:, :] = jnp.where(i < last, dcn_ref[...], 0.0)
        _fill_shifted(zs, zext)
        _fill_shifted(dcs, dcext)
        dz = _conv_taps(cw_ref, dcext, dcs, tile, lambda k: (CONV_W - 1) - k)
        gate = p_ref[:, DC:2 * DC]
        s = _sigmoid(gate)
        dp_ref[:, :DC] = (dz * s).astype(BF16)
        dp_ref[:, DC:2 * DC] = (dz * p_ref[:, :DC] * s * (1.0 - s)).astype(BF16)
        dp_ref[:, 2 * DC:] = dpb_ref[...]
        dcv = dc_ref[...]

        @pl.when(i == 0)
        def _():
            dcw_ref[...] = jnp.zeros_like(dcw_ref)

        for k in range(CONV_W):
            dcw_ref[k:k + 1, :] += jnp.sum(dcv * _rows_at(zext, zs, HALO - (CONV_W - 1) + k, tile), axis=0, keepdims=True)

    return _rowwise(fn, "even_bwd_b", p.shape[0], tile,
                    [(p, tile, D, lambda i: (i, 0)), z, _prev_halo(z, tile), dconv, _next_halo(dconv, tile), dpb], [cw],
                    [(2 * D, BF16)], [((HALO, DC), F32)],
                    scratch=[pltpu.VMEM((tile + HALO, DC), F32), pltpu.VMEM((tile + HALO, DC), F32), _shifted_scratch(tile),
                             _shifted_scratch(tile)])


def _row_count(i, tile, nrows, offset, w):
    t = i * tile + offset + lax.broadcasted_iota(jnp.int32, (nrows, POOL_GD), 0)
    return jnp.minimum(t + 1, w).astype(F32)


def _odd_core(p, wg, scale, tile):
    def fn(i, ins, outs, scr):
        p_ref, ph_ref, wg_ref, sc_ref = ins
        d_ref, e_ref, es_ref = outs
        pext = scr[0]
        pext[:HALO, :] = jnp.where(i > 0, ph_ref[...], 0.0)
        pext[HALO:, :] = p_ref[...]
        for g, w in enumerate(POOL_WINDOWS):
            c0 = g * POOL_GD
            s = p_ref[:, c0:c0 + POOL_GD]
            for r in range(1, w):
                s = s + pext[pl.ds(HALO - r, tile), c0:c0 + POOL_GD]
            dg = (s / _row_count(i, tile, tile, 0, w) - p_ref[:, c0:c0 + POOL_GD]).astype(BF16)
            d_ref[:, c0:c0 + POOL_GD] = dg
            e = _dot(dg, wg_ref[g])
            e_ref[:, c0:c0 + POOL_GD] = e
            es_ref[:, c0:c0 + POOL_GD] = (e * sc_ref[:, c0:c0 + POOL_GD]).astype(BF16)

    return _rowwise(fn, "odd_core", p.shape[0], tile, [p, _prev_halo(p, tile)], [wg, scale],
                    [(D, BF16), (D, F32), (D, BF16)], scratch=[pltpu.VMEM((tile + HALO, D), F32)])


def _odd_bwd(des, e, d, wg, scale, tile):
    def fn(i, ins, outs, scr):
        des_ref, desn_ref, e_ref, d_ref, wg_ref, sc_ref = ins
        dp_ref, dsc_ref, dwg_ref = outs
        qext = scr[0]
        last = pl.num_programs(0) - 1
        desv = des_ref[...]
        _accum(dsc_ref, i, jnp.sum(desv * e_ref[...], axis=0, keepdims=True))
        de = (desv * sc_ref[...]).astype(BF16)
        den = (jnp.where(i < last, desn_ref[...], 0.0) * sc_ref[...]).astype(BF16)
        for g, w in enumerate(POOL_WINDOWS):
            c0 = g * POOL_GD
            deg = de[:, c0:c0 + POOL_GD]
            dd = _dot(deg, wg_ref[g], "nt")
            ddn = _dot(den[:, c0:c0 + POOL_GD], wg_ref[g], "nt")
            qext[:tile, c0:c0 + POOL_GD] = dd / _row_count(i, tile, tile, 0, w)
            qext[tile:, c0:c0 + POOL_GD] = ddn / _row_count(i, tile, HALO, tile, w)
            s = qext[:tile, c0:c0 + POOL_GD]
            for r in range(1, w):
                s = s + qext[pl.ds(r, tile), c0:c0 + POOL_GD]
            dp_ref[:, c0:c0 + POOL_GD] = (s - dd).astype(BF16)
            dw = _dot(d_ref[:, c0:c0 + POOL_GD], deg, "tn")

            @pl.when(i == 0)
            def _():
                dwg_ref[g] = dw

            @pl.when(i > 0)
            def _():
                dwg_ref[g] += dw

    return _rowwise(fn, "odd_bwd", des.shape[0], tile, [des, _next_halo(des, tile), e, d], [wg, scale], [(D, BF16)],
                    [((1, D), F32), ((len(POOL_WINDOWS), POOL_GD, POOL_GD), F32)],
                    scratch=[pltpu.VMEM((tile + HALO, D), F32)])


def _axis_slice(ref, axis, start, size):
    idx = [slice(None)] * len(ref.shape)
    idx[axis] = pl.ds(start, size)
    return ref.at[tuple(idx)]


def _cast_into_slot(pos, shard, kind, name):
    L, rs, cs = shard.shape
    tr = _row_tile(rs, 512)
    nr = rs // tr
    if kind == "row":
        full, omap = (L, NCHIP * rs, cs), (lambda l, i, p: (l, p[0] * nr + i, 0))
    else:
        full, omap = (L, rs, NCHIP * cs), (lambda l, i, p: (l, i, p[0]))

    def body(p_ref, s_ref, o_ref):
        o_ref[...] = s_ref[...].astype(BF16)

    grid_spec = pltpu.PrefetchScalarGridSpec(
        num_scalar_prefetch=1, grid=(L, nr), in_specs=[pl.BlockSpec((None, tr, cs), lambda l, i, p: (l, i, 0))],
        out_specs=pl.BlockSpec((None, tr, cs), omap))
    return pl.pallas_call(body, name=name, grid_spec=grid_spec, out_shape=jax.ShapeDtypeStruct(full, BF16),
                          compiler_params=_params(("parallel", "parallel")))(pos, shard)


def _peer(x, y, c, k):
    return (1 - x if k & 4 else x, 1 - y if k & 2 else y, 1 - c if k & 1 else c)


def _half_geometry(shape, kind):
    L, R, C = shape
    return (L, R // (2 * NCHIP), C) if kind == "row" else (L, R // 2, C // NCHIP)


def _sum_into_half(pos, recv, grad, kind, name):
    _, L, rh, cs = recv.shape
    if kind == "row":
        gmap = lambda l, p: (l, 2 * p[0] + p[1], 0)
    else:
        gmap = lambda l, p: (l, p[1], p[0])

    def body(p_ref, r_ref, g_ref, o_ref):
        acc = g_ref[...].astype(F32)
        for s in range(7):
            acc = acc + r_ref[s].astype(F32)
        o_ref[...] = acc

    grid_spec = pltpu.PrefetchScalarGridSpec(
        num_scalar_prefetch=1, grid=(L,),
        in_specs=[pl.BlockSpec((7, None, rh, cs), lambda l, p: (0, l, 0, 0)), pl.BlockSpec((None, rh, cs), gmap)],
        out_specs=pl.BlockSpec((None, rh, cs), lambda l, p: (l, p[1], 0)))
    return pl.pallas_call(body, name=name, grid_spec=grid_spec, out_shape=jax.ShapeDtypeStruct((L, 2 * rh, cs), F32),
                          compiler_params=_params(("parallel",)))(pos, _hbm(recv), _hbm(grad))


def _swap_halves(shards):
    n = len(shards)

    def body(*refs):
        outs = refs[n:2 * n]
        send_sems, recv_sems = refs[2 * n:]
        x, y, c = lax.axis_index("x"), lax.axis_index("y"), lax.axis_index("c")
        copies = []
        for t in range(n):
            rh = outs[t].shape[1] // 2
            half = outs[t].at[:, pl.ds(c * rh, rh), :]
            cp = pltpu.make_async_remote_copy(src_ref=half, dst_ref=half, send_sem=send_sems.at[t], recv_sem=recv_sems.at[t],
                                              device_id=(x, y, 1 - c), device_id_type=MESH)
            cp.start()
            copies.append(cp)
        for cp in copies:
            cp.wait()

    return pl.pallas_call(body, name="swap_halves", in_specs=[HBM] * n, out_specs=[HBM] * n,
                          out_shape=[pltpu.HBM(s.shape, s.dtype) for s in shards],
                          input_output_aliases={t: t for t in range(n)},
                          scratch_shapes=[pltpu.SemaphoreType.DMA((n,)), pltpu.SemaphoreType.DMA((n,))])(*map(_hbm, shards))


SEM = pl.BlockSpec(memory_space=pltpu.SEMAPHORE)
DATAFLOW = pltpu.SideEffectType.DATAFLOW_SIDE_EFFECTING
MAX_LAYERS = 4


def _gsem(t, layer, k):
    return (t * MAX_LAYERS + layer) * 3 + k


def _half_slot(ref, kind, d0, nd0, chip, core):
    if kind == "row":
        half = ref.shape[1] // (2 * NCHIP)
        return ref.at[pl.ds(d0, nd0), pl.ds((2 * chip + core) * half, half), :]
    size, half = ref.shape[2] // NCHIP, ref.shape[1] // 2
    return ref.at[pl.ds(d0, nd0), pl.ds(core * half, half), pl.ds(chip * size, size)]


def _other_chips(x, y):
    return [(1 - x, y, 2 * (1 - x) + y), (x, 1 - y, 2 * x + 1 - y), (1 - x, 1 - y, 2 * (1 - x) + 1 - y)]


def _gather_start(bufs, kinds, groups, after):
    n, na = len(bufs), len(after)

    def body(*refs):
        send_sems, recv_sems = refs[n + na], refs[n + na + 1]
        outs = refs[n + na + 2:]
        x, y, c = lax.axis_index("x"), lax.axis_index("y"), lax.axis_index("c")
        for group in groups:
            for t, li, d0, nd0 in group:
                mine = _half_slot(outs[t], kinds[t], d0, nd0, 2 * x + y, c)
                for k, (px, py, _) in enumerate(_other_chips(x, y)):
                    pltpu.make_async_remote_copy(src_ref=mine, dst_ref=mine, send_sem=send_sems.at[_gsem(t, li, k)],
                                                 recv_sem=recv_sems.at[_gsem(t, li, k)], device_id=(px, py, c),
                                                 device_id_type=MESH).start()

    sems = pltpu.SemaphoreType.DMA((n * MAX_LAYERS * 3,))
    res = pl.pallas_call(body, name="gather_start", in_specs=[HBM] * n + [ANY] * na, out_specs=[SEM, SEM] + [HBM] * n,
                         out_shape=[sems, sems] + [pltpu.HBM(b.shape, b.dtype) for b in bufs],
                         input_output_aliases={t: t + 2 for t in range(n)},
                         compiler_params=pltpu.CompilerParams(has_side_effects=DATAFLOW))(*map(_hbm, bufs), *after)
    return res[0], res[1], list(res[2:])


def _gather_forward(bufs, kinds, group, send_sems, recv_sems, after, name):
    m, na = len(bufs), len(after)

    def body(*refs):
        send, recv = refs[m], refs[m + 1]
        send2, recv2 = refs[m + 2 + na], refs[m + 3 + na]
        outs = refs[m + 4 + na:]
        x, y, c = lax.axis_index("x"), lax.axis_index("y"), lax.axis_index("c")
        for j, (t, li, d0, nd0) in enumerate(group):
            mine = _half_slot(outs[j], kinds[j], d0, nd0, 2 * x + y, c)
            for k, (px, py, chip) in enumerate(_other_chips(x, y)):
                got = _half_slot(outs[j], kinds[j], d0, nd0, chip, c)
                first = pltpu.make_async_remote_copy(src_ref=mine, dst_ref=got, send_sem=send.at[_gsem(t, li, k)],
                                                     recv_sem=recv.at[_gsem(t, li, k)], device_id=(px, py, c),
                                                     device_id_type=MESH)
                first.wait_send()
                first.wait_recv()
                pltpu.make_async_remote_copy(src_ref=got, dst_ref=got, send_sem=send2.at[3 * j + k], recv_sem=recv2.at[3 * j + k],
                                             device_id=(x, y, 1 - c), device_id_type=MESH).start()

    sems = pltpu.SemaphoreType.DMA((3 * m,))
    res = pl.pallas_call(body, name=name, in_specs=[HBM] * m + [SEM, SEM] + [ANY] * na, out_specs=[SEM, SEM] + [HBM] * m,
                         out_shape=[sems, sems] + [pltpu.HBM(b.shape, b.dtype) for b in bufs],
                         input_output_aliases={j: j + 2 for j in range(m)},
                         compiler_params=pltpu.CompilerParams(has_side_effects=DATAFLOW))(*map(_hbm, bufs), send_sems, recv_sems,
                                                                                          *map(_hbm, after))
    return res[0], res[1], list(res[2:])


def _gather_wait(bufs, kinds, group, send_sems, recv_sems, after, name):
    m, na = len(bufs), len(after)

    def body(*refs):
        send, recv = refs[m], refs[m + 1]
        outs = refs[m + 2 + na:]
        x, y, c = lax.axis_index("x"), lax.axis_index("y"), lax.axis_index("c")
        for j, (t, li, d0, nd0) in enumerate(group):
            for k, (_, _, chip) in enumerate(_other_chips(x, y)):
                cp = pltpu.make_async_remote_copy(src_ref=_half_slot(outs[j], kinds[j], d0, nd0, chip, c),
                                                  dst_ref=_half_slot(outs[j], kinds[j], d0, nd0, chip, 1 - c),
                                                  send_sem=send.at[3 * j + k], recv_sem=recv.at[3 * j + k],
                                                  device_id=(x, y, 1 - c), device_id_type=MESH)
                cp.wait_send()
                cp.wait_recv()

    res = pl.pallas_call(body, name=name, in_specs=[HBM] * m + [SEM, SEM] + [ANY] * na, out_specs=[HBM] * m,
                         out_shape=[pltpu.HBM(b.shape, b.dtype) for b in bufs],
                         input_output_aliases={j: j for j in range(m)},
                         compiler_params=pltpu.CompilerParams(has_side_effects=DATAFLOW))(*map(_hbm, bufs), send_sems, recv_sems,
                                                                                          *map(_hbm, after))
    return list(res)


def _gather_tiny(tiny, axes):
    n = len(tiny)
    out_shape = []
    for s_, ax in zip(tiny, axes):
        shp = list(s_.shape)
        shp[ax] *= NCHIP
        out_shape.append(jax.ShapeDtypeStruct(tuple(shp), s_.dtype))

    def body(*refs):
        ins, outs = refs[:n], refs[n:2 * n]
        send_sems, recv_sems, loc_sems = refs[2 * n:]
        x, y, c = lax.axis_index("x"), lax.axis_index("y"), lax.axis_index("c")
        mine = 2 * x + y
        chips = [(1 - x, y), (x, 1 - y), (1 - x, 1 - y)]
        copies = []
        for t in range(n):
            size = ins[t].shape[axes[t]]
            dst = _axis_slice(outs[t], axes[t], mine * size, size)
            copies.append(pltpu.make_async_copy(ins[t], dst, loc_sems.at[t]))
            for k, chip in enumerate(chips):
                copies.append(pltpu.make_async_remote_copy(src_ref=ins[t], dst_ref=dst, send_sem=send_sems.at[t, k],
                                                           recv_sem=recv_sems.at[t, k], device_id=(*chip, c),
                                                           device_id_type=MESH))
        for cp in copies:
            cp.start()
        for cp in copies:
            cp.wait()

    return pl.pallas_call(body, name="gather_tiny", in_specs=[ANY] * n, out_specs=[ANY] * n, out_shape=out_shape,
                          scratch_shapes=[pltpu.SemaphoreType.DMA((n, 3)), pltpu.SemaphoreType.DMA((n, 3)),
                                          pltpu.SemaphoreType.DMA((n,))])(*tiny)


def _grad_slice(ref, kind, d0, nd0, chip, core, rh, cs):
    if kind == "row":
        return ref.at[pl.ds(d0, nd0), pl.ds((2 * chip + core) * rh, rh), :]
    return ref.at[pl.ds(d0, nd0), pl.ds(core * rh, rh), pl.ds(chip * cs, cs)]


def _scatter_start(grads, recv, kinds, items, name, after=()):
    m, na = len(grads), len(after)

    def body(*refs):
        send_sems, recv_sems = refs[2 * m + na], refs[2 * m + na + 1]
        gout, rout = refs[2 * m + na + 2:3 * m + na + 2], refs[3 * m + na + 2:]
        x, y, c = lax.axis_index("x"), lax.axis_index("y"), lax.axis_index("c")
        for t in range(m):
            _, _, rh, cs = rout[t].shape
            d0, nd0 = items[t]
            for k in range(1, 8):
                px, py, pc = _peer(x, y, c, k)
                src = _grad_slice(gout[t], kinds[t], d0, nd0, 2 * px + py, pc, rh, cs)
                pltpu.make_async_remote_copy(src_ref=src, dst_ref=rout[t].at[k - 1, pl.ds(d0, nd0)],
                                             send_sem=send_sems.at[7 * t + k - 1], recv_sem=recv_sems.at[7 * t + k - 1],
                                             device_id=(px, py, pc), device_id_type=MESH).start()

    sems = pltpu.SemaphoreType.DMA((7 * m,))
    res = pl.pallas_call(body, name=name, in_specs=[HBM] * (2 * m) + [ANY] * na, out_specs=[SEM, SEM] + [HBM] * (2 * m),
                         out_shape=[sems, sems] + [pltpu.HBM(a.shape, a.dtype) for a in list(grads) + list(recv)],
                         input_output_aliases={j: j + 2 for j in range(2 * m)},
                         compiler_params=pltpu.CompilerParams(has_side_effects=DATAFLOW))(*map(_hbm, grads), *map(_hbm, recv),
                                                                                          *after)
    return res[0], res[1], list(res[2:m + 2]), list(res[m + 2:])


def _scatter_wait(recv, items, send_sems, recv_sems, name, after=()):
    m = len(recv)

    def body(*refs):
        send, rcv = refs[m], refs[m + 1]
        outs = refs[m + 2 + len(after):]
        x, y, c = lax.axis_index("x"), lax.axis_index("y"), lax.axis_index("c")
        for t in range(m):
            d0, nd0 = items[t]
            for k in range(1, 8):
                land = outs[t].at[k - 1, pl.ds(d0, nd0)]
                cp = pltpu.make_async_remote_copy(src_ref=land, dst_ref=land, send_sem=send.at[7 * t + k - 1],
                                                  recv_sem=rcv.at[7 * t + k - 1], device_id=_peer(x, y, c, k), device_id_type=MESH)
                cp.wait_send()
                cp.wait_recv()

    res = pl.pallas_call(body, name=name, in_specs=[HBM] * m + [SEM, SEM] + [ANY] * len(after), out_specs=[HBM] * m,
                         out_shape=[pltpu.HBM(a.shape, a.dtype) for a in recv],
                         input_output_aliases={j: j for j in range(m)},
                         compiler_params=pltpu.CompilerParams(has_side_effects=DATAFLOW))(*map(_hbm, recv), send_sems, recv_sems,
                                                                                          *map(_hbm, after))
    return list(res)


def _share_small(packed):
    rows = packed.shape[0]

    def body(in_ref, out_ref, send_sems, recv_sems, loc_sem):
        x, y, c = lax.axis_index("x"), lax.axis_index("y"), lax.axis_index("c")
        me = 4 * x + 2 * y + c
        copies = [pltpu.make_async_copy(in_ref, out_ref.at[me], loc_sem)]
        for k in range(1, 8):
            copies.append(pltpu.make_async_remote_copy(src_ref=in_ref, dst_ref=out_ref.at[me], send_sem=send_sems.at[k - 1],
                                                       recv_sem=recv_sems.at[k - 1], device_id=_peer(x, y, c, k),
                                                       device_id_type=MESH))
        for cp in copies:
            cp.start()
        for cp in copies:
            cp.wait()

    return pl.pallas_call(body, name="share_small", in_specs=[ANY], out_specs=ANY,
                          out_shape=jax.ShapeDtypeStruct((8, rows, 128), F32),
                          scratch_shapes=[pltpu.SemaphoreType.DMA((7,)), pltpu.SemaphoreType.DMA((7,)),
                                          pltpu.SemaphoreType.DMA])(packed)


def _sum_slots(buf, tile, name):
    _, rows, cols = buf.shape

    def body(b_ref, o_ref):
        acc = b_ref[0].astype(F32)
        for s in range(1, 8):
            acc = acc + b_ref[s].astype(F32)
        o_ref[...] = acc

    return pl.pallas_call(body, name=name, grid=(rows // tile,), in_specs=[pl.BlockSpec((8, tile, cols), lambda i: (0, i, 0))],
                          out_specs=pl.BlockSpec((tile, cols), lambda i: (i, 0)),
                          out_shape=jax.ShapeDtypeStruct((rows, cols), F32), compiler_params=_params(("parallel",)))(buf)


def _adamw(w, g, m, v, tile, name, after=()):
    rows, cols = w.shape

    def body(w_ref, g_ref, m_ref, v_ref, *rest):
        d_ref, mo_ref, vo_ref = rest[len(after):]
        gv = g_ref[...]
        mn = ADAM_B1 * m_ref[...] + (1.0 - ADAM_B1) * gv
        vn = ADAM_B2 * v_ref[...] + (1.0 - ADAM_B2) * (gv * gv)
        m_hat = mn / (1.0 - ADAM_B1 ** ADAM_STEP)
        v_hat = vn / (1.0 - ADAM_B2 ** ADAM_STEP)
        d_ref[...] = -ADAM_LR * (m_hat / (jnp.sqrt(v_hat) + ADAM_EPS) + ADAM_WD * w_ref[...])
        mo_ref[...] = mn
        vo_ref[...] = vn

    spec = pl.BlockSpec((tile, cols), lambda i: (i, 0))
    sds = jax.ShapeDtypeStruct((rows, cols), F32)
    return pl.pallas_call(body, name=name, grid=(rows // tile,), in_specs=[spec] * 4 + [ANY] * len(after), out_specs=[spec] * 3,
                          out_shape=[sds] * 3, compiler_params=_params(("parallel",)))(w, g, m, v, *map(_hbm, after))


def _row_tile(rows, cap):
    best = 8
    for t in range(8, min(rows, cap) + 1, 8):
        if rows % t == 0:
            best = t
    return best


def _pack(arrs):
    parts = []
    for a in arrs:
        r = a.size // 128
        r8 = -(-r // 8) * 8
        parts.append(jnp.pad(a.reshape(r, 128).astype(F32), ((0, r8 - r), (0, 0))))
    return jnp.concatenate(parts, axis=0)


def _unpack(packed, shapes):
    out, r0 = [], 0
    for shp in shapes:
        r = math.prod(shp) // 128
        out.append(packed[r0:r0 + r].reshape(shp))
        r0 += -(-r // 8) * 8
    return out


BIG = {
    "ffn1_w_gu": "col", "ffn1_w_down": "row", "ev_w_in": "col", "ev_w_out": "row", "od_w_in": "row", "od_w_group": "row",
    "od_w_out": "row", "xa_w_q": "row", "xa_w_kv": "col", "xa_w_o": "row", "ffn2_w_gu": "col", "ffn2_w_down": "row",
}
TINY_SHARDED = {"ev_conv_w": 2, "od_scale": 1}
WEIGHTS = ['ffn1_pre_g', 'ffn1_w_gu', 'ffn1_w_down', 'ffn1_post_g', 'mix_pre_g', 'mix_post_g', 'ev_w_in', 'ev_conv_w',
           'ev_conv_b', 'ev_conv_ln_g', 'ev_conv_ln_b', 'ev_sgu_ln_g', 'ev_sgu_ln_b', 'ev_sgu_w', 'ev_sgu_b', 'ev_w_out',
           'od_w_in', 'od_w_group', 'od_scale', 'od_w_out', 'xa_pre_g', 'xa_mem_g', 'xa_w_q', 'xa_w_kv', 'xa_w_o', 'xa_post_g',
           'ffn2_pre_g', 'ffn2_w_gu', 'ffn2_w_down', 'ffn2_post_g']


def _as3d(a):
    return a.reshape((-1,) + a.shape[-2:]) if a.ndim == 4 else a


class _WeightView:
    def __init__(self, store, view):
        self.store, self.view = store, view

    def __getitem__(self, k):
        return self.view(k)


class _Grads:
    def __init__(self):
        self.buf = {}
        self.fresh = []

    def add(self, name, layer, nlayers, a, b, tm, tn, n_outer=True):
        self.buf[name] = _matmul(a, b, "tn", BF16, "dw_" + name, tm, tn, out_l=layer, out_stack=nlayers,
                                 out_buf=self.buf.get(name), n_outer=n_outer)
        self.fresh.append(name)

    def take(self):
        names, self.fresh = self.fresh, []
        return names


def _local_step(x, mem, target, W, S, tiles=None, before_sub=None, after_sub_bwd=None):
    T, TW, tm = tiles or (min(512, S), min(256, S), min(512, S))
    TF = T
    row = lambda v: v.reshape(1, -1)
    subs = []
    small = {k: [None] * W[k].shape[0] for k in WEIGHTS if k not in BIG}
    g = _Grads()

    def ffn_fwd(tag, l, h, n, g_next):
        gu, a = _up_swiglu(n, W[tag + "_w_gu"], l, TF)
        f, h2, n2 = _post_mm(h, a, W[tag + "_w_down"], l, row(W[tag + "_post_g"][l]), 0.5, g_next, T)
        subs.append(dict(kind="ffn", tag=tag, l=l, h=h, n=n, gu=gu, a=a, f=f, scale=0.5, pre=tag + "_pre_g", post=tag + "_post_g"))
        return h2, n2

    def ffn_bwd(s, df):
        tag, l = s["tag"], s["l"]
        g.add(tag + "_w_down", l, 4, s["a"], df, 256, D, n_outer=False)
        dgu = _down_dx_swiglu_bwd(df, s["gu"], W[tag + "_w_down"], l, TF)
        g.add(tag + "_w_gu", l, 4, s["n"], dgu, D, 512, n_outer=False)
        return dgu, tag + "_w_gu", l, TF

    def xa_fwd(l, h, n, g_next):
        q = _matmul(n, W["xa_w_q"], "nn", BF16, "xa_q", tm, D, b_l=l)
        mn = _mem_norm(mem, row(W["xa_mem_g"][l]))
        kv = _matmul(mn, W["xa_w_kv"], "nn", BF16, "xa_kv", NMEM, D, b_l=l)
        o = _attn(q, kv, T)
        cx, h2, n2 = _post_mm(h, o, W["xa_w_o"], l, row(W["xa_post_g"][l]), 1.0, g_next, T)
        subs.append(dict(kind="xa", l=l, h=h, n=n, q=q, mn=mn, kv=kv, o=o, f=cx, scale=1.0, pre="xa_pre_g", post="xa_post_g"))
        return h2, n2

    def xa_bwd(s, dc):
        l = s["l"]
        do = _matmul(dc, W["xa_w_o"], "nt", BF16, "xa_o_dx", tm, D, b_l=l)
        g.add("xa_w_o", l, 4, s["o"], dc, D, 512)
        dq, dkv = _attn_bwd(s["q"], s["kv"], do, T)
        g.add("xa_w_q", l, 4, s["n"], dq, D, 512)
        dkvb = dkv.astype(BF16)
        g.add("xa_w_kv", l, 4, s["mn"], dkvb, D, 512)
        dmn = _matmul(dkvb, W["xa_w_kv"], "nt", F32, "xa_kv_dx", NMEM, D, b_l=l)
        small["xa_mem_g"][l] = _mem_gain_bwd(mem, dmn)[0]
        return dq, "xa_w_q", l, T

    def even_params(e):
        return (W["ev_conv_w"][e], row(W["ev_conv_b"][e]), row(W["ev_conv_ln_g"][e]), row(W["ev_conv_ln_b"][e]),
                row(W["ev_sgu_ln_g"][e]), row(W["ev_sgu_ln_b"][e]), W["ev_sgu_w"][e])

    def even_fwd(l, h, n, g_next):
        e = l // 2
        p = _matmul(n, W["ev_w_in"], "nn", F32, "ev_in", tm, D, b_l=e)
        cw, cb, clg, clb, slg, slb, ws = even_params(e)
        y, z = _even_core(p, cw, cb, clg, clb, slg, slb, ws, W["ev_sgu_b"][e].T, TW)
        m, h2, n2 = _post_mm(h, y, W["ev_w_out"], e, row(W["mix_post_g"][l]), 1.0, g_next, T)
        subs.append(dict(kind="even", l=l, h=h, n=n, p=p, y=y, z=z, f=m, scale=1.0, pre="mix_pre_g", post="mix_post_g"))
        return h2, n2

    def even_bwd(s, dm):
        l = s["l"]
        e = l // 2
        dy = _matmul(dm, W["ev_w_out"], "nt", F32, "ev_out_dx", tm, D, b_l=e)
        g.add("ev_w_out", e, 2, s["y"], dm, D, 512)
        cw, cb, clg, clb, slg, slb, ws = even_params(e)
        dconv, dpb, vecs, dws, dbst = _even_bwd_a(s["p"], s["z"], dy, cw, cb, clg, clb, slg, slb, ws,
                                                   jnp.swapaxes(ws, 1, 2), W["ev_sgu_b"][e].T, TW)
        dp, dcw = _even_bwd_b(s["p"], s["z"], dconv, dpb, cw, TW)
        for r, name in enumerate(["ev_conv_b", "ev_conv_ln_g", "ev_conv_ln_b", "ev_sgu_ln_g", "ev_sgu_ln_b"]):
            small[name][e] = vecs[r]
        small["ev_sgu_w"][e] = dws
        small["ev_sgu_b"][e] = dbst.T
        small["ev_conv_w"][e] = dcw[:CONV_W]
        g.add("ev_w_in", e, 2, s["n"], dp, D, 512)
        return dp, "ev_w_in", e, T

    def group_w(o):
        ng = len(POOL_WINDOWS)
        return (W["od_w_group"].reshape(-1, POOL_GD, POOL_GD), (ng, POOL_GD, POOL_GD), lambda i: (o, 0, 0))

    def odd_fwd(l, h, n, g_next):
        o = l // 2
        p = _matmul(n, W["od_w_in"], "nn", F32, "od_in", tm, D, b_l=o)
        d, e, es = _odd_core(p, group_w(o), row(W["od_scale"][o]), T)
        m, h2, n2 = _post_mm(h, es, W["od_w_out"], o, row(W["mix_post_g"][l]), 1.0, g_next, T)
        subs.append(dict(kind="odd", l=l, h=h, n=n, d=d, e=e, es=es, f=m, scale=1.0, pre="mix_pre_g", post="mix_post_g"))
        return h2, n2

    def odd_bwd(s, dm):
        l = s["l"]
        o = l // 2
        des = _matmul(dm, W["od_w_out"], "nt", F32, "od_out_dx", tm, D, b_l=o)
        g.add("od_w_out", o, 2, s["es"], dm, D, 512)
        dp, dsc, dwg = _odd_bwd(des, s["e"], s["d"], group_w(o), row(W["od_scale"][o]), T)
        small["od_scale"][o] = dsc[0]
        small["od_w_group_full"][o] = dwg
        g.add("od_w_in", o, 2, s["n"], dp, D, 512)
        return dp, "od_w_in", o, T

    small["od_w_group_full"] = [None, None]
    order = []
    for l in range(4):
        order += [("ffn1", l), ("even" if l % 2 == 0 else "odd", l), ("xa", l), ("ffn2", l)]
    pre_of = {"ffn1": "ffn1_pre_g", "even": "mix_pre_g", "odd": "mix_pre_g", "xa": "xa_pre_g", "ffn2": "ffn2_pre_g"}
    h = x
    n = _prenorm(h, row(W["ffn1_pre_g"][0]), T)
    for idx, (kind, l) in enumerate(order):
        if before_sub is not None:
            before_sub(kind, l, h)
        g_next = row(W[pre_of[order[idx + 1][0]]][order[idx + 1][1]]) if idx + 1 < len(order) else None
        if kind in ("ffn1", "ffn2"):
            h, n = ffn_fwd(kind, l, h, n, g_next)
        elif kind == "xa":
            h, n = xa_fwd(l, h, n, g_next)
        elif kind == "even":
            h, n = even_fwd(l, h, n, g_next)
        else:
            h, n = odd_fwd(l, h, n, g_next)

    top = subs[-1]
    dh, df, loss_acc, dgp = _loss_top(h, target, top["f"], row(W[top["post"]][top["l"]]), top["scale"], T)
    small[top["post"]][top["l"]] = dgp[0]
    for idx in range(len(subs) - 1, -1, -1):
        s = subs[idx]
        da, wname, wl, bt = {"ffn": ffn_bwd, "xa": xa_bwd, "even": even_bwd, "odd": odd_bwd}[s["kind"]](s, df)
        names = g.take()
        deps = [g.buf[k] for k in names]
        if idx > 0:
            sp = subs[idx - 1]
            dh, df, dg_pre, dg_post = _boundary_mm(da, W[wname], wl, s["h"], row(W[s["pre"]][s["l"]]), dh, sp["f"],
                                                   row(W[sp["post"]][sp["l"]]), sp["scale"], bt, deps=deps)
            small[sp["post"]][sp["l"]] = dg_post[0]
        else:
            dh, dg_pre = _boundary_mm(da, W[wname], wl, s["h"], row(W[s["pre"]][s["l"]]), dh, None, None, None, bt, deps=deps)
        small[s["pre"]][s["l"]] = dg_pre[0]
        if after_sub_bwd is not None:
            after_sub_bwd(order[idx][0], s["l"], g.buf, small)
            g.fresh = names + g.fresh
    return loss_acc[0, 0], dh, g.buf, small


def _step(P, S):
    x, mem, target = P["x"][0], P["mem"][0], P["loss_target"][0]
    chip = 2 * lax.axis_index("x") + lax.axis_index("y")
    pos = jnp.stack([chip, lax.axis_index("c")]).astype(jnp.int32)
    order = list(BIG)
    tix = {k: t for t, k in enumerate(order)}
    kinds = [BIG[k] for k in order]
    per = {k: (4 if k == "od_w_group" else 1) for k in order}
    members = {"ffn1": ["ffn1_w_gu", "ffn1_w_down"], "ffn2": ["ffn2_w_gu", "ffn2_w_down"], "xa": ["xa_w_q", "xa_w_kv", "xa_w_o"],
               "even": ["ev_w_in", "ev_w_out"], "odd": ["od_w_in", "od_w_group", "od_w_out"]}
    index_of = lambda kind, l: l // 2 if kind in ("even", "odd") else l
    sub_order = []
    for l in range(4):
        sub_order += [("ffn1", l), ("even" if l % 2 == 0 else "odd", l), ("xa", l), ("ffn2", l)]
    groups = {(kind, l): [(tix[k], index_of(kind, l), index_of(kind, l) * per[k], per[k]) for k in members[kind]]
              for kind, l in sub_order}
    tiny = _gather_tiny([P[k] for k in TINY_SHARDED], list(TINY_SHARDED.values()))
    slots = [_cast_into_slot(pos, _as3d(P[k]), BIG[k], "cast_" + k) for k in order]
    g_send, g_recv, bufs = _gather_start(slots, kinds, [groups[s_] for s_ in sub_order], tiny)
    W = {k: P[k] for k in WEIGHTS if k not in BIG}
    W.update(zip(TINY_SHARDED, tiny))
    W.update(zip(order, bufs))

    def view(k):
        return W[k].reshape(2, 4, POOL_GD, POOL_GD) if k == "od_w_group" else W[k]

    Wv = _WeightView(W, view)

    forwarded = {}

    def forward(sub, after):
        ks = members[sub[0]]
        s2, r2, thru = _gather_forward([W[k] for k in ks], [BIG[k] for k in ks], groups[sub], g_send, g_recv, after,
                                       "gather_forward_%s%d" % sub)
        W.update(zip(ks, thru))
        forwarded[sub] = (s2, r2)

    def before_sub(kind, l, h):
        sub = (kind, l)
        if sub == sub_order[0]:
            forward(sub, (h,))
        ks = members[kind]
        s2, r2 = forwarded.pop(sub)
        W.update(zip(ks, _gather_wait([W[k] for k in ks], [BIG[k] for k in ks], groups[sub], s2, r2, (h,),
                                      "gather_wait_%s%d" % sub)))
        nxt = sub_order.index(sub) + 1
        if nxt < len(sub_order):
            forward(sub_order[nxt], (h, W[ks[0]]))

    recv = {k: None for k in order}
    pending = []
    small_names = [k for k in WEIGHTS if k not in BIG]
    gsmall = {}

    def after_sub_bwd(kind, l, big, small):
        ks = [k for k in members[kind] if k != "od_w_group"]
        items = [(index_of(kind, l), 1)] * len(ks)
        after = ()
        if (kind, l) == ("odd", 1):
            big["od_w_group"] = jnp.stack(small.pop("od_w_group_full")).astype(BF16).reshape(8, POOL_GD, POOL_GD)
            ks, items = ks + ["od_w_group"], items + [(0, 8)]
        if (kind, l) == ("ffn1", 0):
            small_full = [jnp.stack(small[k]) for k in small_names]
            packed = _pack(small_full)
            summed = _sum_slots(_share_small(packed), _row_tile(packed.shape[0], 512), "sum_small")
            gsmall.update(zip(small_names, _unpack(summed, [a.shape for a in small_full])))
            after = (summed,)
        for k in ks:
            if recv[k] is None:
                recv[k] = lax.empty((7,) + _half_geometry(big[k].shape, BIG[k]), BF16)
        s_send, s_recv, g_thru, r_thru = _scatter_start([big[k] for k in ks], [recv[k] for k in ks], [BIG[k] for k in ks],
                                                        items, "scatter_start_%s%d" % (kind, l), after)
        big.update(zip(ks, g_thru))
        recv.update(zip(ks, r_thru))
        pending.append((kind, l, ks, items, s_send, s_recv))

    loss, grad_x, big, small = _local_step(x, mem, target, Wv, S, before_sub=before_sub, after_sub_bwd=after_sub_bwd)
    last = (big[members["ffn1"][0]],)
    for kind, l, ks, items, s_send, s_recv in pending:
        recv.update(zip(ks, _scatter_wait([recv[k] for k in ks], items, s_send, s_recv, "scatter_wait_%s%d" % (kind, l), last)))
    gsh = dict(zip(order, _swap_halves([_sum_into_half(pos, recv[k], big[k], BIG[k], "sum_" + k) for k in order])))
    for k, ax in TINY_SHARDED.items():
        size = P[k].shape[ax]
        gsmall[k] = lax.dynamic_slice_in_dim(gsmall[k], chip * size, size, axis=ax)

    grads, delta, new_m, new_v = {}, {}, {}, {}
    for k in order:
        shp = P[k].shape
        cols = shp[-1]
        flat = lambda a: a.reshape(-1, cols)
        gk = flat(gsh[k])
        d_, m_, v_ = _adamw(flat(P[k]), gk, flat(P["m_" + k]), flat(P["v_" + k]), _row_tile(gk.shape[0], 256), "adamw_" + k)
        grads[k], delta[k], new_m[k], new_v[k] = gk.reshape(shp), d_.reshape(shp), m_.reshape(shp), v_.reshape(shp)
    pk = lambda pre: _pack([P[pre + k] for k in small_names])
    d_, m_, v_ = _adamw(pk(""), _pack([gsmall[k] for k in small_names]), pk("m_"), pk("v_"),
                        pk("").shape[0], "adamw_small", last)
    shapes = [P[k].shape for k in small_names]
    for dst, src in ((delta, d_), (new_m, m_), (new_v, v_)):
        dst.update(zip(small_names, _unpack(src, shapes)))
    grads.update(gsmall)

    loss = lax.psum(loss, ("x", "y", "c"))
    out = [loss, grad_x[None]]
    for grp in (grads, delta, new_m, new_v):
        out += [grp[k] for k in WEIGHTS]
    return tuple(out)


def kernel(x, mem, ffn1_pre_g, ffn1_w_gu, ffn1_w_down, ffn1_post_g, mix_pre_g, mix_post_g, ev_w_in, ev_conv_w, ev_conv_b, ev_conv_ln_g, ev_conv_ln_b, ev_sgu_ln_g, ev_sgu_ln_b, ev_sgu_w, ev_sgu_b, ev_w_out, od_w_in, od_w_group, od_scale, od_w_out, xa_pre_g, xa_mem_g, xa_w_q, xa_w_kv, xa_w_o, xa_post_g, ffn2_pre_g, ffn2_w_gu, ffn2_w_down, ffn2_post_g, loss_target, m_ffn1_pre_g, m_ffn1_w_gu, m_ffn1_w_down, m_ffn1_post_g, m_mix_pre_g, m_mix_post_g, m_ev_w_in, m_ev_conv_w, m_ev_conv_b, m_ev_conv_ln_g, m_ev_conv_ln_b, m_ev_sgu_ln_g, m_ev_sgu_ln_b, m_ev_sgu_w, m_ev_sgu_b, m_ev_w_out, m_od_w_in, m_od_w_group, m_od_scale, m_od_w_out, m_xa_pre_g, m_xa_mem_g, m_xa_w_q, m_xa_w_kv, m_xa_w_o, m_xa_post_g, m_ffn2_pre_g, m_ffn2_w_gu, m_ffn2_w_down, m_ffn2_post_g, v_ffn1_pre_g, v_ffn1_w_gu, v_ffn1_w_down, v_ffn1_post_g, v_mix_pre_g, v_mix_post_g, v_ev_w_in, v_ev_conv_w, v_ev_conv_b, v_ev_conv_ln_g, v_ev_conv_ln_b, v_ev_sgu_ln_g, v_ev_sgu_ln_b, v_ev_sgu_w, v_ev_sgu_b, v_ev_w_out, v_od_w_in, v_od_w_group, v_od_scale, v_od_w_out, v_xa_pre_g, v_xa_mem_g, v_xa_w_q, v_xa_w_kv, v_xa_w_o, v_xa_post_g, v_ffn2_pre_g, v_ffn2_w_gu, v_ffn2_w_down, v_ffn2_post_g):
    P = dict(locals())
    return _step(P, x.shape[1])
```

```python
import functools
import math

import jax
import jax.numpy as jnp
from jax import lax
from jax.experimental import pallas as pl
from jax.experimental.pallas import tpu as pltpu

F32 = jnp.float32
BF16 = jnp.bfloat16
MESH = pl.DeviceIdType.MESH
ANY = pl.BlockSpec(memory_space=pl.ANY)
HBM = pl.BlockSpec(memory_space=pltpu.HBM)


def _hbm(x):
    return pltpu.with_memory_space_constraint(x, pltpu.HBM)

D = 1024
DFF = 2816
NMEM = 256
DC = 512
CONV_W = 31
CHUNK = 128
NHEAD_SGU = 4
POOL_WINDOWS = (2, 4, 8, 16)
POOL_GD = 256
XA_HEADS = 4
XA_HD = 256
EPS = 1e-6
NCHIP = 4
HALO = 32

ADAM_LR, ADAM_B1, ADAM_B2, ADAM_EPS, ADAM_WD, ADAM_STEP = 0.001, 0.9, 0.999, 1e-08, 0.01, 10

V7X_VMEM_BYTES = 64 * 1024 * 1024
VMEM_LIMIT = V7X_VMEM_BYTES - 8 * 1024 * 1024


def _params(sem):
    return pltpu.CompilerParams(dimension_semantics=sem, vmem_limit_bytes=VMEM_LIMIT)


def _matmul(a, b, kind, out_dtype, name, tm, tn, a_l=None, b_l=None, out_l=None, out_stack=None, out_buf=None,
            n_outer=True):
    a2, b2 = a.shape[-2:], b.shape[-2:]
    if kind == "nn":
        (m, k), (k2, n) = a2, b2
        dims = (((1,), (0,)), ((), ()))
    elif kind == "nt":
        (m, k), (n, k2) = a2, b2
        dims = (((1,), (1,)), ((), ()))
    else:
        (k, m), (k2, n) = a2, b2
        dims = (((0,), (0,)), ((), ()))
    assert k == k2 and m % tm == 0 and n % tn == 0, (name, a.shape, b.shape, tm, tn)
    if n_outer:
        grid = (n // tn, m // tm)
        ij = lambda p, q: (q, p)
    else:
        grid = (m // tm, n // tn)
        ij = lambda p, q: (p, q)

    def spec(arr, layer, blk, idx):
        if arr.ndim == 3:
            return pl.BlockSpec((None,) + blk, lambda p, q: (layer,) + idx(*ij(p, q)))
        return pl.BlockSpec(blk, lambda p, q: idx(*ij(p, q)))

    a_spec = spec(a, a_l, (k, tm) if kind == "tn" else (tm, k), (lambda i, j: (0, i)) if kind == "tn" else (lambda i, j: (i, 0)))
    b_spec = spec(b, b_l, (tn, k) if kind == "nt" else (k, tn), (lambda i, j: (j, 0)) if kind == "nt" else (lambda i, j: (0, j)))
    if out_stack is None:
        out_shape = jax.ShapeDtypeStruct((m, n), out_dtype)
        o_spec = pl.BlockSpec((tm, tn), lambda p, q: ij(p, q))
    else:
        out_shape = jax.ShapeDtypeStruct((out_stack, m, n), out_dtype)
        o_spec = pl.BlockSpec((None, tm, tn), lambda p, q: (out_l,) + ij(p, q))

    def body(a_ref, b_ref, *rest):
        o_ref = rest[-1]
        o_ref[...] = lax.dot_general(a_ref[...], b_ref[...], dims, preferred_element_type=F32).astype(o_ref.dtype)

    in_specs, args, aliases = [a_spec, b_spec], [a, _hbm(b) if b.ndim == 3 else b], {}
    if out_buf is not None:
        in_specs.append(ANY)
        args.append(_hbm(out_buf))
        aliases = {2: 0}
    return pl.pallas_call(body, name=name, grid=grid, in_specs=in_specs, out_specs=o_spec, out_shape=out_shape,
                          input_output_aliases=aliases, compiler_params=_params(("parallel", "parallel")))(*args)


def _rowwise(fn, name, rows, tile, row_ins, full_ins, row_outs, acc_outs=(), scratch=(), deps=()):
    assert rows % tile == 0, (name, rows, tile)
    in_specs, args = [], []
    for r in row_ins:
        if isinstance(r, tuple):
            arr, br, bc, imap = r
            in_specs.append(pl.BlockSpec((br, bc), imap))
        else:
            arr = r
            in_specs.append(pl.BlockSpec((tile, arr.shape[1]), lambda i: (i, 0)))
        args.append(arr)
    for f in full_ins:
        if isinstance(f, tuple):
            arr, blk, imap = f
            in_specs.append(pl.BlockSpec(blk, imap, pipeline_mode=pl.Buffered(1)))
            args.append(_hbm(arr))
        else:
            in_specs.append(pl.BlockSpec(f.shape, functools.partial(lambda nd, i: (0,) * nd, f.ndim)))
            args.append(f)
    for d in deps:
        in_specs.append(ANY)
        args.append(_hbm(d))
    out_specs, out_shape = [], []
    for w, dt in row_outs:
        out_specs.append(pl.BlockSpec((tile, w), lambda i: (i, 0)))
        out_shape.append(jax.ShapeDtypeStruct((rows, w), dt))
    for shp, dt in acc_outs:
        out_specs.append(pl.BlockSpec(shp, functools.partial(lambda nd, i: (0,) * nd, len(shp))))
        out_shape.append(jax.ShapeDtypeStruct(shp, dt))
    n_in, n_out = len(args), len(out_shape)

    def body(*refs):
        fn(pl.program_id(0), refs[:n_in], refs[n_in:n_in + n_out], refs[n_in + n_out:])

    sem = ("arbitrary",) if acc_outs else ("parallel",)
    res = pl.pallas_call(body, name=name, grid=(rows // tile,), in_specs=in_specs, out_specs=out_specs, out_shape=out_shape,
                         scratch_shapes=list(scratch), compiler_params=_params(sem))(*args)
    return res


def _accum(ref, i, val):
    @pl.when(i == 0)
    def _():
        ref[...] = val

    @pl.when(i > 0)
    def _():
        ref[...] += val


def _prev_halo(arr, tile, cols=None):
    r = tile // HALO
    return (arr, HALO, cols or arr.shape[1], lambda i: (jnp.maximum(i * r - 1, 0), 0))


def _next_halo(arr, tile, cols=None):
    r = tile // HALO
    last = arr.shape[0] // HALO - 1
    return (arr, HALO, cols or arr.shape[1], lambda i: (jnp.minimum((i + 1) * r, last), 0))


def _sigmoid(x):
    return 1.0 / (1.0 + jnp.exp(-x))


def _rms_hat(x):
    r = lax.rsqrt(jnp.mean(x * x, axis=-1, keepdims=True) + EPS)
    return x * r, r


def _rms_bwd(x, g, dy):
    xhat, r = _rms_hat(x)
    dxhat = dy * g
    dx = r * (dxhat - xhat * jnp.mean(dxhat * xhat, axis=-1, keepdims=True))
    return dx, jnp.sum(dy * xhat, axis=0, keepdims=True)


def _ln_hat(x):
    mu = jnp.mean(x, axis=-1, keepdims=True)
    xc = x - mu
    r = lax.rsqrt(jnp.mean(xc * xc, axis=-1, keepdims=True) + EPS)
    return xc * r, r


def _ln_bwd(xhat, r, g, dy):
    dxhat = dy * g
    dx = r * (dxhat - jnp.mean(dxhat, axis=-1, keepdims=True) - xhat * jnp.mean(dxhat * xhat, axis=-1, keepdims=True))
    return dx, jnp.sum(dy * xhat, axis=0, keepdims=True), jnp.sum(dy, axis=0, keepdims=True)


def _silu_grad(x):
    s = _sigmoid(x)
    return s * (1.0 + x * (1.0 - s))


_SQRT_HALF = math.sqrt(0.5)
_INV_SQRT_2PI = 1.0 / math.sqrt(2.0 * math.pi)


def _gelu(x):
    return 0.5 * x * (1.0 + lax.erf(x * _SQRT_HALF))


def _gelu_grad(x):
    return 0.5 * (1.0 + lax.erf(x * _SQRT_HALF)) + x * jnp.exp(-0.5 * x * x) * _INV_SQRT_2PI


def _dot(a, b, kind="nn"):
    dims = {"nn": (((1,), (0,)), ((), ())), "nt": (((1,), (1,)), ((), ())), "tn": (((0,), (0,)), ((), ()))}[kind]
    return lax.dot_general(a, b, dims, preferred_element_type=F32)


def _prenorm(h, g, tile):
    def fn(i, ins, outs, _):
        h_ref, g_ref = ins
        outs[0][...] = (_rms_hat(h_ref[...])[0] * g_ref[...]).astype(BF16)

    return _rowwise(fn, "prenorm", h.shape[0], tile, [h], [g], [(D, BF16)])[0]


def _loss_top(h, target, f, g_post, scale, tile):
    def fn(i, ins, outs, _):
        err = ins[0][...] - ins[1][...]
        per_row = jnp.mean(err * err, axis=-1, keepdims=True)
        _accum(outs[2], i, jnp.broadcast_to(0.5 * jnp.sum(per_row, axis=0, keepdims=True), (1, 128)))
        dh = err * (1.0 / D)
        outs[0][...] = dh
        df, dg = _rms_bwd(ins[2][...], ins[3][...], scale * dh)
        outs[1][...] = df.astype(BF16)
        _accum(outs[3], i, dg)

    return _rowwise(fn, "loss_top", h.shape[0], tile, [h, target, f], [g_post], [(D, F32), (D, BF16)],
                    [((1, 128), F32), ((1, D), F32)])


ROW_BLOCK = 256


def _row_blocks(tile):
    return [pl.ds(r0, min(ROW_BLOCK, tile)) for r0 in range(0, tile, ROW_BLOCK)]


def _layer_of(w3, l):
    return (w3, (None,) + tuple(w3.shape[1:]), lambda i: (l, 0, 0))


def _post_mm(h, a, w3, l, g_post, scale, g_next, tile):
    def fn(i, ins, outs, _):
        for r in _row_blocks(tile):
            f = _dot(ins[1][r, :], ins[2][...])
            outs[0][r, :] = f
            hn = ins[0][r, :] + scale * (_rms_hat(f)[0] * ins[3][...])
            outs[1][r, :] = hn
            if g_next is not None:
                outs[2][r, :] = (_rms_hat(hn)[0] * ins[4][...]).astype(BF16)

    fulls = [_layer_of(w3, l), g_post] + ([g_next] if g_next is not None else [])
    outs = [(D, F32), (D, F32)] + ([(D, BF16)] if g_next is not None else [])
    res = _rowwise(fn, "post_mm", h.shape[0], tile, [h, a], fulls, outs)
    return res[0], res[1], (res[2] if g_next is not None else None)


def _boundary_mm(a, w3, l, h, g_pre, dh_in, f_prev, g_post_prev, scale_prev, tile, deps=()):
    has_prev = f_prev is not None

    def fn(i, ins, outs, _):
        a_ref, h_ref, dhin_ref = ins[:3]
        w_ref, g_ref = ins[3 + has_prev], ins[4 + has_prev]
        dg_sum, dgp_sum = None, None
        for r in _row_blocks(tile):
            dn = _dot(a_ref[r, :], w_ref[...], "nt")
            dx, dg = _rms_bwd(h_ref[r, :], g_ref[...], dn)
            dh = dhin_ref[r, :] + dx
            outs[0][r, :] = dh
            dg_sum = dg if dg_sum is None else dg_sum + dg
            if has_prev:
                df, dgp = _rms_bwd(ins[3][r, :], ins[6][...], scale_prev * dh)
                outs[1][r, :] = df.astype(BF16)
                dgp_sum = dgp if dgp_sum is None else dgp_sum + dgp
        _accum(outs[1 + has_prev], i, dg_sum)
        if has_prev:
            _accum(outs[3], i, dgp_sum)

    rows = [a, h, dh_in] + ([f_prev] if has_prev else [])
    fulls = [_layer_of(w3, l), g_pre] + ([g_post_prev] if has_prev else [])
    outs = [(D, F32)] + ([(D, BF16)] if has_prev else [])
    accs = [((1, D), F32)] + ([((1, D), F32)] if has_prev else [])
    return _rowwise(fn, "boundary_mm", h.shape[0], tile, rows, fulls, outs, accs, deps=deps)


FF_CHUNK = DFF // 2


def _up_swiglu(n, w3, l, tile):
    def fn(i, ins, outs, _):
        n_ref, w_ref = ins
        nv = n_ref[...]
        for c0 in range(0, DFF, FF_CHUNK):
            g = _dot(nv, w_ref[:, c0:c0 + FF_CHUNK])
            u = _dot(nv, w_ref[:, DFF + c0:DFF + c0 + FF_CHUNK])
            outs[0][:, c0:c0 + FF_CHUNK] = g.astype(BF16)
            outs[0][:, DFF + c0:DFF + c0 + FF_CHUNK] = u.astype(BF16)
            outs[1][:, c0:c0 + FF_CHUNK] = (g * _sigmoid(g) * u).astype(BF16)

    return _rowwise(fn, "up_swiglu", n.shape[0], tile, [n], [_layer_of(w3, l)], [(2 * DFF, BF16), (DFF, BF16)])


def _down_dx_swiglu_bwd(df, gu, w3, l, tile):
    def fn(i, ins, outs, _):
        df_ref, gu_ref, w_ref = ins
        dfv = df_ref[...]
        for c0 in range(0, DFF, FF_CHUNK):
            da = _dot(dfv, w_ref[c0:c0 + FF_CHUNK, :], "nt")
            g = gu_ref[:, c0:c0 + FF_CHUNK].astype(F32)
            u = gu_ref[:, DFF + c0:DFF + c0 + FF_CHUNK].astype(F32)
            s = _sigmoid(g)
            das = da * s
            dup = das * g
            outs[0][:, c0:c0 + FF_CHUNK] = (u * (das + dup * (1.0 - s))).astype(BF16)
            outs[0][:, DFF + c0:DFF + c0 + FF_CHUNK] = dup.astype(BF16)

    return _rowwise(fn, "down_dx_swiglu_bwd", df.shape[0], tile, [df, gu], [_layer_of(w3, l)], [(2 * DFF, BF16)])[0]


def _softmax_rows(s):
    e = jnp.exp(s - jnp.max(s, axis=-1, keepdims=True))
    return e / jnp.sum(e, axis=-1, keepdims=True)


def _attn(q, kv, tile):
    def fn(i, ins, outs, _):
        q_ref, kv_ref = ins
        for hd in range(XA_HEADS):
            c0 = hd * XA_HD
            p = _softmax_rows(_dot(q_ref[:, c0:c0 + XA_HD], kv_ref[:, c0:c0 + XA_HD], "nt") * (XA_HD ** -0.5))
            outs[0][:, c0:c0 + XA_HD] = _dot(p.astype(BF16), kv_ref[:, D + c0:D + c0 + XA_HD]).astype(BF16)

    return _rowwise(fn, "attn", q.shape[0], tile, [q], [kv], [(D, BF16)])[0]


def _attn_bwd(q, kv, do, tile):
    def fn(i, ins, outs, _):
        q_ref, do_ref, kv_ref = ins
        for hd in range(XA_HEADS):
            c0 = hd * XA_HD
            qh, kh, vh = q_ref[:, c0:c0 + XA_HD], kv_ref[:, c0:c0 + XA_HD], kv_ref[:, D + c0:D + c0 + XA_HD]
            doh = do_ref[:, c0:c0 + XA_HD]
            p = _softmax_rows(_dot(qh, kh, "nt") * (XA_HD ** -0.5))
            dp = _dot(doh, vh, "nt")
            ds = (p * (dp - jnp.sum(dp * p, axis=-1, keepdims=True)) * (XA_HD ** -0.5)).astype(BF16)
            outs[0][:, c0:c0 + XA_HD] = _dot(ds, kh).astype(BF16)
            dk = _dot(ds, qh, "tn")
            dv = _dot(p.astype(BF16), doh, "tn")

            @pl.when(i == 0)
            def _():
                outs[1][:, c0:c0 + XA_HD] = dk
                outs[1][:, D + c0:D + c0 + XA_HD] = dv

            @pl.when(i > 0)
            def _():
                outs[1][:, c0:c0 + XA_HD] += dk
                outs[1][:, D + c0:D + c0 + XA_HD] += dv

    return _rowwise(fn, "attn_bwd", q.shape[0], tile, [q, do], [kv], [(D, BF16)], [((NMEM, 2 * D), F32)])


def _mem_norm(mem, g):
    def fn(i, ins, outs, _):
        outs[0][...] = (_rms_hat(ins[0][...])[0] * ins[1][...]).astype(BF16)

    return _rowwise(fn, "mem_norm", NMEM, NMEM, [mem], [g], [(D, BF16)])[0]


def _mem_gain_bwd(mem, dmn):
    def fn(i, ins, outs, _):
        outs[0][...] = jnp.sum(ins[1][...] * _rms_hat(ins[0][...])[0], axis=0, keepdims=True)

    return _rowwise(fn, "mem_gain_bwd", NMEM, NMEM, [mem, dmn], [], [], [((1, D), F32)])[0]


SHIFT_ROWS = 8


def _shifted_scratch(tile):
    return pltpu.VMEM((SHIFT_ROWS - 1, tile + HALO - SHIFT_ROWS, DC), F32)


def _fill_shifted(zs_ref, zext_ref):
    rows = zs_ref.shape[1]
    for b in range(1, SHIFT_ROWS):
        zs_ref[b - 1] = zext_ref[pl.ds(b, rows), :]


def _rows_at(zext_ref, zs_ref, offset, tile):
    a, b = divmod(offset, SHIFT_ROWS)
    if b == 0:
        return zext_ref[pl.ds(offset, tile), :]
    return zs_ref[b - 1, pl.ds(SHIFT_ROWS * a, tile), :]


def _conv_taps(w_ref, zext_ref, zs_ref, tile, shift0):
    acc = None
    for k in range(CONV_W):
        term = w_ref[k:k + 1, :] * _rows_at(zext_ref, zs_ref, shift0(k), tile)
        acc = term if acc is None else acc + term
    return acc


def _tril_mask(transpose=False):
    r, c = (lax.broadcasted_iota(jnp.int32, (CHUNK, CHUNK), a) for a in (0, 1))
    return (r <= c) if transpose else (r >= c)


def _even_core(p, cw, cb, clg, clb, slg, slb, ws, bst, tile):
    def fn(i, ins, outs, scr):
        p_ref, ph_ref, cw_ref, cb_ref, clg_ref, clb_ref, slg_ref, slb_ref, ws_ref, bst_ref = ins
        y_ref, z_ref = outs
        zext = scr[0]
        z = p_ref[:, :DC] * _sigmoid(p_ref[:, DC:2 * DC])
        zh = ph_ref[:, :DC] * _sigmoid(ph_ref[:, DC:2 * DC])
        zext[:HALO, :] = jnp.where(i > 0, zh, 0.0)
        zext[HALO:, :] = z
        z_ref[...] = z
        _fill_shifted(scr[1], zext)
        conv = _conv_taps(cw_ref, zext, scr[1], tile, lambda k: HALO - (CONV_W - 1) + k) + cb_ref[...]
        yl = _ln_hat(conv)[0] * clg_ref[...] + clb_ref[...]
        y_ref[:, :DC] = (yl * _sigmoid(yl)).astype(BF16)
        zb = _gelu(p_ref[:, 2 * DC:])
        vln = (_ln_hat(zb[:, DC:])[0] * slg_ref[...] + slb_ref[...]).astype(BF16)
        mask = _tril_mask()
        for hd in range(NHEAD_SGU):
            wm = jnp.where(mask, ws_ref[hd], 0.0).astype(BF16)
            for ci in range(tile // CHUNK):
                r0, c0 = ci * CHUNK, hd * CHUNK
                mixed = _dot(wm, vln[r0:r0 + CHUNK, c0:c0 + CHUNK]) + bst_ref[:, hd:hd + 1]
                y_ref[r0:r0 + CHUNK, DC + c0:DC + c0 + CHUNK] = (zb[r0:r0 + CHUNK, c0:c0 + CHUNK] * mixed).astype(BF16)

    return _rowwise(fn, "even_core", p.shape[0], tile, [p, _prev_halo(p, tile, D)], [cw, cb, clg, clb, slg, slb, ws, bst],
                    [(D, BF16), (DC, F32)], scratch=[pltpu.VMEM((tile + HALO, DC), F32), _shifted_scratch(tile)])


def _even_bwd_a(p, z, dy, cw, cb, clg, clb, slg, slb, ws, wst, bst, tile):
    def fn(i, ins, outs, scr):
        p_ref, z_ref, zh_ref, dy_ref, cw_ref, cb_ref, clg_ref, clb_ref, slg_ref, slb_ref, ws_ref, wst_ref, bst_ref = ins
        dc_ref, dpb_ref, vec_ref, dws_ref, dbst_ref = outs
        zext, dvln_s, zs = scr
        zext[:HALO, :] = jnp.where(i > 0, zh_ref[...], 0.0)
        zext[HALO:, :] = z_ref[...]
        _fill_shifted(zs, zext)
        conv = _conv_taps(cw_ref, zext, zs, tile, lambda k: HALO - (CONV_W - 1) + k) + cb_ref[...]
        chat, cr = _ln_hat(conv)
        yl = chat * clg_ref[...] + clb_ref[...]
        dyl = dy_ref[:, :DC] * _silu_grad(yl)
        dconv, dclg, dclb = _ln_bwd(chat, cr, clg_ref[...], dyl)
        dc_ref[...] = dconv
        xb = p_ref[:, 2 * DC:]
        zb = _gelu(xb)
        vhat, vr = _ln_hat(zb[:, DC:])
        vln = (vhat * slg_ref[...] + slb_ref[...]).astype(BF16)
        dyb = dy_ref[:, DC:]
        mask, mask_t = _tril_mask(), _tril_mask(transpose=True)
        dbu_parts = []
        for hd in range(NHEAD_SGU):
            wm = jnp.where(mask, ws_ref[hd], 0.0).astype(BF16)
            wt = jnp.where(mask_t, wst_ref[hd], 0.0).astype(BF16)
            dws_h, dbs_h, rows = None, None, []
            for ci in range(tile // CHUNK):
                r0, c0 = ci * CHUNK, hd * CHUNK
                vblk = vln[r0:r0 + CHUNK, c0:c0 + CHUNK]
                mixed = _dot(wm, vblk) + bst_ref[:, hd:hd + 1]
                dyb_blk = dyb[r0:r0 + CHUNK, c0:c0 + CHUNK]
                rows.append(dyb_blk * mixed)
                dmixed = dyb_blk * zb[r0:r0 + CHUNK, c0:c0 + CHUNK]
                dmb = dmixed.astype(BF16)
                dvln_s[r0:r0 + CHUNK, c0:c0 + CHUNK] = _dot(wt, dmb)
                dw = _dot(dmb, vblk, "nt")
                db = jnp.sum(dmixed, axis=-1, keepdims=True)
                dws_h = dw if dws_h is None else dws_h + dw
                dbs_h = db if dbs_h is None else dbs_h + db
            dbu_parts.append(jnp.concatenate(rows, axis=0))
            dws_h = jnp.where(mask, dws_h, 0.0)

            @pl.when(i == 0)
            def _():
                dws_ref[hd] = dws_h
                dbst_ref[:, hd:hd + 1] = dbs_h

            @pl.when(i > 0)
            def _():
                dws_ref[hd] += dws_h
                dbst_ref[:, hd:hd + 1] += dbs_h

        dbu = jnp.concatenate(dbu_parts, axis=1)
        dbv, dslg, dslb = _ln_bwd(vhat, vr, slg_ref[...], dvln_s[...])
        dpb_ref[:, :DC] = (dbu * _gelu_grad(xb[:, :DC])).astype(BF16)
        dpb_ref[:, DC:] = (dbv * _gelu_grad(xb[:, DC:])).astype(BF16)
        @pl.when(i == 0)
        def _():
            vec_ref[...] = jnp.zeros_like(vec_ref)

        for r, val in enumerate([jnp.sum(dconv, axis=0, keepdims=True), dclg, dclb, dslg, dslb]):
            vec_ref[r:r + 1, :] += val

    return _rowwise(fn, "even_bwd_a", p.shape[0], tile, [p, z, _prev_halo(z, tile), dy],
                    [cw, cb, clg, clb, slg, slb, ws, wst, bst], [(DC, F32), (D, BF16)],
                    [((8, DC), F32), ((NHEAD_SGU, CHUNK, CHUNK), F32), ((CHUNK, NHEAD_SGU), F32)],
                    scratch=[pltpu.VMEM((tile + HALO, DC), F32), pltpu.VMEM((tile, DC), F32), _shifted_scratch(tile)])


def _even_bwd_b(p, z, dconv, dpb, cw, tile):
    def fn(i, ins, outs, scr):
        p_ref, z_ref, zh_ref, dc_ref, dcn_ref, dpb_ref, cw_ref = ins
        dp_ref, dcw_ref = outs
        zext, dcext, zs, dcs = scr
        last = pl.num_programs(0) - 1
        zext[:HALO, :] = jnp.where(i > 0, zh_ref[...], 0.0)
        zext[HALO:, :] = z_ref[...]
        dcext[:tile, :] = dc_ref[...]
        dcext[tile:, :] = jnp.where(i < last, dcn_ref[...], 0.0)
        _fill_shifted(zs, zext)
        _fill_shifted(dcs, dcext)
        dz = _conv_taps(cw_ref, dcext, dcs, tile, lambda k: (CONV_W - 1) - k)
        gate = p_ref[:, DC:2 * DC]
        s = _sigmoid(gate)
        dp_ref[:, :DC] = (dz * s).astype(BF16)
        dp_ref[:, DC:2 * DC] = (dz * p_ref[:, :DC] * s * (1.0 - s)).astype(BF16)
        dp_ref[:, 2 * DC:] = dpb_ref[...]
        dcv = dc_ref[...]

        @pl.when(i == 0)
        def _():
            dcw_ref[...] = jnp.zeros_like(dcw_ref)

        for k in range(CONV_W):
            dcw_ref[k:k + 1, :] += jnp.sum(dcv * _rows_at(zext, zs, HALO - (CONV_W - 1) + k, tile), axis=0, keepdims=True)

    return _rowwise(fn, "even_bwd_b", p.shape[0], tile,
                    [(p, tile, D, lambda i: (i, 0)), z, _prev_halo(z, tile), dconv, _next_halo(dconv, tile), dpb], [cw],
                    [(2 * D, BF16)], [((HALO, DC), F32)],
                    scratch=[pltpu.VMEM((tile + HALO, DC), F32), pltpu.VMEM((tile + HALO, DC), F32), _shifted_scratch(tile),
                             _shifted_scratch(tile)])


def _row_count(i, tile, nrows, offset, w):
    t = i * tile + offset + lax.broadcasted_iota(jnp.int32, (nrows, POOL_GD), 0)
    return jnp.minimum(t + 1, w).astype(F32)


def _odd_core(p, wg, scale, tile):
    def fn(i, ins, outs, scr):
        p_ref, ph_ref, wg_ref, sc_ref = ins
        d_ref, e_ref, es_ref = outs
        pext = scr[0]
        pext[:HALO, :] = jnp.where(i > 0, ph_ref[...], 0.0)
        pext[HALO:, :] = p_ref[...]
        for g, w in enumerate(POOL_WINDOWS):
            c0 = g * POOL_GD
            s = p_ref[:, c0:c0 + POOL_GD]
            for r in range(1, w):
                s = s + pext[pl.ds(HALO - r, tile), c0:c0 + POOL_GD]
            dg = (s / _row_count(i, tile, tile, 0, w) - p_ref[:, c0:c0 + POOL_GD]).astype(BF16)
            d_ref[:, c0:c0 + POOL_GD] = dg
            e = _dot(dg, wg_ref[g])
            e_ref[:, c0:c0 + POOL_GD] = e
            es_ref[:, c0:c0 + POOL_GD] = (e * sc_ref[:, c0:c0 + POOL_GD]).astype(BF16)

    return _rowwise(fn, "odd_core", p.shape[0], tile, [p, _prev_halo(p, tile)], [wg, scale],
                    [(D, BF16), (D, F32), (D, BF16)], scratch=[pltpu.VMEM((tile + HALO, D), F32)])


def _odd_bwd(des, e, d, wg, scale, tile):
    def fn(i, ins, outs, scr):
        des_ref, desn_ref, e_ref, d_ref, wg_ref, sc_ref = ins
        dp_ref, dsc_ref, dwg_ref = outs
        qext = scr[0]
        last = pl.num_programs(0) - 1
        desv = des_ref[...]
        _accum(dsc_ref, i, jnp.sum(desv * e_ref[...], axis=0, keepdims=True))
        de = (desv * sc_ref[...]).astype(BF16)
        den = (jnp.where(i < last, desn_ref[...], 0.0) * sc_ref[...]).astype(BF16)
        for g, w in enumerate(POOL_WINDOWS):
            c0 = g * POOL_GD
            deg = de[:, c0:c0 + POOL_GD]
            dd = _dot(deg, wg_ref[g], "nt")
            ddn = _dot(den[:, c0:c0 + POOL_GD], wg_ref[g], "nt")
            qext[:tile, c0:c0 + POOL_GD] = dd / _row_count(i, tile, tile, 0, w)
            qext[tile:, c0:c0 + POOL_GD] = ddn / _row_count(i, tile, HALO, tile, w)
            s = qext[:tile, c0:c0 + POOL_GD]
            for r in range(1, w):
                s = s + qext[pl.ds(r, tile), c0:c0 + POOL_GD]
            dp_ref[:, c0:c0 + POOL_GD] = (s - dd).astype(BF16)
            dw = _dot(d_ref[:, c0:c0 + POOL_GD], deg, "tn")

            @pl.when(i == 0)
            def _():
                dwg_ref[g] = dw

            @pl.when(i > 0)
            def _():
                dwg_ref[g] += dw

    return _rowwise(fn, "odd_bwd", des.shape[0], tile, [des, _next_halo(des, tile), e, d], [wg, scale], [(D, BF16)],
                    [((1, D), F32), ((len(POOL_WINDOWS), POOL_GD, POOL_GD), F32)],
                    scratch=[pltpu.VMEM((tile + HALO, D), F32)])


def _axis_slice(ref, axis, start, size):
    idx = [slice(None)] * len(ref.shape)
    idx[axis] = pl.ds(start, size)
    return ref.at[tuple(idx)]


def _cast_into_slot(pos, shard, kind, name):
    L, rs, cs = shard.shape
    tr = _row_tile(rs, 512)
    nr = rs // tr
    if kind == "row":
        full, omap = (L, NCHIP * rs, cs), (lambda l, i, p: (l, p[0] * nr + i, 0))
    else:
        full, omap = (L, rs, NCHIP * cs), (lambda l, i, p: (l, i, p[0]))

    def body(p_ref, s_ref, o_ref):
        o_ref[...] = s_ref[...].astype(BF16)

    grid_spec = pltpu.PrefetchScalarGridSpec(
        num_scalar_prefetch=1, grid=(L, nr), in_specs=[pl.BlockSpec((None, tr, cs), lambda l, i, p: (l, i, 0))],
        out_specs=pl.BlockSpec((None, tr, cs), omap))
    return pl.pallas_call(body, name=name, grid_spec=grid_spec, out_shape=jax.ShapeDtypeStruct(full, BF16),
                          compiler_params=_params(("parallel", "parallel")))(pos, shard)


def _peer(x, y, c, k):
    return (1 - x if k & 4 else x, 1 - y if k & 2 else y, 1 - c if k & 1 else c)


def _half_geometry(shape, kind):
    L, R, C = shape
    return (L, R // (2 * NCHIP), C) if kind == "row" else (L, R // 2, C // NCHIP)


def _sum_into_half(pos, recv, grad, kind, name):
    _, L, rh, cs = recv.shape
    if kind == "row":
        gmap = lambda l, p: (l, 2 * p[0] + p[1], 0)
    else:
        gmap = lambda l, p: (l, p[1], p[0])

    def body(p_ref, r_ref, g_ref, o_ref):
        acc = g_ref[...].astype(F32)
        for s in range(7):
            acc = acc + r_ref[s].astype(F32)
        o_ref[...] = acc

    grid_spec = pltpu.PrefetchScalarGridSpec(
        num_scalar_prefetch=1, grid=(L,),
        in_specs=[pl.BlockSpec((7, None, rh, cs), lambda l, p: (0, l, 0, 0)), pl.BlockSpec((None, rh, cs), gmap)],
        out_specs=pl.BlockSpec((None, rh, cs), lambda l, p: (l, p[1], 0)))
    return pl.pallas_call(body, name=name, grid_spec=grid_spec, out_shape=jax.ShapeDtypeStruct((L, 2 * rh, cs), F32),
                          compiler_params=_params(("parallel",)))(pos, _hbm(recv), _hbm(grad))


def _swap_halves(shards):
    n = len(shards)

    def body(*refs):
        outs = refs[n:2 * n]
        send_sems, recv_sems = refs[2 * n:]
        x, y, c = lax.axis_index("x"), lax.axis_index("y"), lax.axis_index("c")
        copies = []
        for t in range(n):
            rh = outs[t].shape[1] // 2
            half = outs[t].at[:, pl.ds(c * rh, rh), :]
            cp = pltpu.make_async_remote_copy(src_ref=half, dst_ref=half, send_sem=send_sems.at[t], recv_sem=recv_sems.at[t],
                                              device_id=(x, y, 1 - c), device_id_type=MESH)
            cp.start()
            copies.append(cp)
        for cp in copies:
            cp.wait()

    return pl.pallas_call(body, name="swap_halves", in_specs=[HBM] * n, out_specs=[HBM] * n,
                          out_shape=[pltpu.HBM(s.shape, s.dtype) for s in shards],
                          input_output_aliases={t: t for t in range(n)},
                          scratch_shapes=[pltpu.SemaphoreType.DMA((n,)), pltpu.SemaphoreType.DMA((n,))])(*map(_hbm, shards))


SEM = pl.BlockSpec(memory_space=pltpu.SEMAPHORE)
DATAFLOW = pltpu.SideEffectType.DATAFLOW_SIDE_EFFECTING
MAX_LAYERS = 4


def _gsem(t, layer, k):
    return (t * MAX_LAYERS + layer) * 3 + k


def _half_slot(ref, kind, d0, nd0, chip, core):
    if kind == "row":
        half = ref.shape[1] // (2 * NCHIP)
        return ref.at[pl.ds(d0, nd0), pl.ds((2 * chip + core) * half, half), :]
    size, half = ref.shape[2] // NCHIP, ref.shape[1] // 2
    return ref.at[pl.ds(d0, nd0), pl.ds(core * half, half), pl.ds(chip * size, size)]


def _other_chips(x, y):
    return [(1 - x, y, 2 * (1 - x) + y), (x, 1 - y, 2 * x + 1 - y), (1 - x, 1 - y, 2 * (1 - x) + 1 - y)]


def _gather_start(bufs, kinds, groups, after):
    n, na = len(bufs), len(after)

    def body(*refs):
        send_sems, recv_sems = refs[n + na], refs[n + na + 1]
        outs = refs[n + na + 2:]
        x, y, c = lax.axis_index("x"), lax.axis_index("y"), lax.axis_index("c")
        for group in groups:
            for t, li, d0, nd0 in group:
                mine = _half_slot(outs[t], kinds[t], d0, nd0, 2 * x + y, c)
                for k, (px, py, _) in enumerate(_other_chips(x, y)):
                    pltpu.make_async_remote_copy(src_ref=mine, dst_ref=mine, send_sem=send_sems.at[_gsem(t, li, k)],
                                                 recv_sem=recv_sems.at[_gsem(t, li, k)], device_id=(px, py, c),
                                                 device_id_type=MESH).start()

    sems = pltpu.SemaphoreType.DMA((n * MAX_LAYERS * 3,))
    res = pl.pallas_call(body, name="gather_start", in_specs=[HBM] * n + [ANY] * na, out_specs=[SEM, SEM] + [HBM] * n,
                         out_shape=[sems, sems] + [pltpu.HBM(b.shape, b.dtype) for b in bufs],
                         input_output_aliases={t: t + 2 for t in range(n)},
                         compiler_params=pltpu.CompilerParams(has_side_effects=DATAFLOW))(*map(_hbm, bufs), *after)
    return res[0], res[1], list(res[2:])


def _gather_forward(bufs, kinds, group, send_sems, recv_sems, after, name):
    m, na = len(bufs), len(after)

    def body(*refs):
        send, recv = refs[m], refs[m + 1]
        send2, recv2 = refs[m + 2 + na], refs[m + 3 + na]
        outs = refs[m + 4 + na:]
        x, y, c = lax.axis_index("x"), lax.axis_index("y"), lax.axis_index("c")
        for j, (t, li, d0, nd0) in enumerate(group):
            mine = _half_slot(outs[j], kinds[j], d0, nd0, 2 * x + y, c)
            for k, (px, py, chip) in enumerate(_other_chips(x, y)):
                got = _half_slot(outs[j], kinds[j], d0, nd0, chip, c)
                first = pltpu.make_async_remote_copy(src_ref=mine, dst_ref=got, send_sem=send.at[_gsem(t, li, k)],
                                                     recv_sem=recv.at[_gsem(t, li, k)], device_id=(px, py, c),
                                                     device_id_type=MESH)
                first.wait_send()
                first.wait_recv()
                pltpu.make_async_remote_copy(src_ref=got, dst_ref=got, send_sem=send2.at[3 * j + k], recv_sem=recv2.at[3 * j + k],
                                             device_id=(x, y, 1 - c), device_id_type=MESH).start()

    sems = pltpu.SemaphoreType.DMA((3 * m,))
    res = pl.pallas_call(body, name=name, in_specs=[HBM] * m + [SEM, SEM] + [ANY] * na, out_specs=[SEM, SEM] + [HBM] * m,
                         out_shape=[sems, sems] + [pltpu.HBM(b.shape, b.dtype) for b in bufs],
                         input_output_aliases={j: j + 2 for j in range(m)},
                         compiler_params=pltpu.CompilerParams(has_side_effects=DATAFLOW))(*map(_hbm, bufs), send_sems, recv_sems,
                                                                                          *map(_hbm, after))
    return res[0], res[1], list(res[2:])


def _gather_wait(bufs, kinds, group, send_sems, recv_sems, after, name):
    m, na = len(bufs), len(after)

    def body(*refs):
        send, recv = refs[m], refs[m + 1]
        outs = refs[m + 2 + na:]
        x, y, c = lax.axis_index("x"), lax.axis_index("y"), lax.axis_index("c")
        for j, (t, li, d0, nd0) in enumerate(group):
            for k, (_, _, chip) in enumerate(_other_chips(x, y)):
                cp = pltpu.make_async_remote_copy(src_ref=_half_slot(outs[j], kinds[j], d0, nd0, chip, c),
                                                  dst_ref=_half_slot(outs[j], kinds[j], d0, nd0, chip, 1 - c),
                                                  send_sem=send.at[3 * j + k], recv_sem=recv.at[3 * j + k],
                                                  device_id=(x, y, 1 - c), device_id_type=MESH)
                cp.wait_send()
                cp.wait_recv()

    res = pl.pallas_call(body, name=name, in_specs=[HBM] * m + [SEM, SEM] + [ANY] * na, out_specs=[HBM] * m,
                         out_shape=[pltpu.HBM(b.shape, b.dtype) for b in bufs],
                         input_output_aliases={j: j for j in range(m)},
                         compiler_params=pltpu.CompilerParams(has_side_effects=DATAFLOW))(*map(_hbm, bufs), send_sems, recv_sems,
                                                                                          *map(_hbm, after))
    return list(res)


def _gather_tiny(tiny, axes):
    n = len(tiny)
    out_shape = []
    for s_, ax in zip(tiny, axes):
        shp = list(s_.shape)
        shp[ax] *= NCHIP
        out_shape.append(jax.ShapeDtypeStruct(tuple(shp), s_.dtype))

    def body(*refs):
        ins, outs = refs[:n], refs[n:2 * n]
        send_sems, recv_sems, loc_sems = refs[2 * n:]
        x, y, c = lax.axis_index("x"), lax.axis_index("y"), lax.axis_index("c")
        mine = 2 * x + y
        chips = [(1 - x, y), (x, 1 - y), (1 - x, 1 - y)]
        copies = []
        for t in range(n):
            size = ins[t].shape[axes[t]]
            dst = _axis_slice(outs[t], axes[t], mine * size, size)
            copies.append(pltpu.make_async_copy(ins[t], dst, loc_sems.at[t]))
            for k, chip in enumerate(chips):
                copies.append(pltpu.make_async_remote_copy(src_ref=ins[t], dst_ref=dst, send_sem=send_sems.at[t, k],
                                                           recv_sem=recv_sems.at[t, k], device_id=(*chip, c),
                                                           device_id_type=MESH))
        for cp in copies:
            cp.start()
        for cp in copies:
            cp.wait()

    return pl.pallas_call(body, name="gather_tiny", in_specs=[ANY] * n, out_specs=[ANY] * n, out_shape=out_shape,
                          scratch_shapes=[pltpu.SemaphoreType.DMA((n, 3)), pltpu.SemaphoreType.DMA((n, 3)),
                                          pltpu.SemaphoreType.DMA((n,))])(*tiny)


def _grad_slice(ref, kind, d0, nd0, chip, core, rh, cs):
    if kind == "row":
        return ref.at[pl.ds(d0, nd0), pl.ds((2 * chip + core) * rh, rh), :]
    return ref.at[pl.ds(d0, nd0), pl.ds(core * rh, rh), pl.ds(chip * cs, cs)]


def _scatter_start(grads, recv, kinds, items, name, after=()):
    m, na = len(grads), len(after)

    def body(*refs):
        send_sems, recv_sems = refs[2 * m + na], refs[2 * m + na + 1]
        gout, rout = refs[2 * m + na + 2:3 * m + na + 2], refs[3 * m + na + 2:]
        x, y, c = lax.axis_index("x"), lax.axis_index("y"), lax.axis_index("c")
        for t in range(m):
            _, _, rh, cs = rout[t].shape
            d0, nd0 = items[t]
            for k in range(1, 8):
                px, py, pc = _peer(x, y, c, k)
                src = _grad_slice(gout[t], kinds[t], d0, nd0, 2 * px + py, pc, rh, cs)
                pltpu.make_async_remote_copy(src_ref=src, dst_ref=rout[t].at[k - 1, pl.ds(d0, nd0)],
                                             send_sem=send_sems.at[7 * t + k - 1], recv_sem=recv_sems.at[7 * t + k - 1],
                                             device_id=(px, py, pc), device_id_type=MESH).start()

    sems = pltpu.SemaphoreType.DMA((7 * m,))
    res = pl.pallas_call(body, name=name, in_specs=[HBM] * (2 * m) + [ANY] * na, out_specs=[SEM, SEM] + [HBM] * (2 * m),
                         out_shape=[sems, sems] + [pltpu.HBM(a.shape, a.dtype) for a in list(grads) + list(recv)],
                         input_output_aliases={j: j + 2 for j in range(2 * m)},
                         compiler_params=pltpu.CompilerParams(has_side_effects=DATAFLOW))(*map(_hbm, grads), *map(_hbm, recv),
                                                                                          *after)
    return res[0], res[1], list(res[2:m + 2]), list(res[m + 2:])


def _scatter_wait(recv, items, send_sems, recv_sems, name, after=()):
    m = len(recv)

    def body(*refs):
        send, rcv = refs[m], refs[m + 1]
        outs = refs[m + 2 + len(after):]
        x, y, c = lax.axis_index("x"), lax.axis_index("y"), lax.axis_index("c")
        for t in range(m):
            d0, nd0 = items[t]
            for k in range(1, 8):
                land = outs[t].at[k - 1, pl.ds(d0, nd0)]
                cp = pltpu.make_async_remote_copy(src_ref=land, dst_ref=land, send_sem=send.at[7 * t + k - 1],
                                                  recv_sem=rcv.at[7 * t + k - 1], device_id=_peer(x, y, c, k), device_id_type=MESH)
                cp.wait_send()
                cp.wait_recv()

    res = pl.pallas_call(body, name=name, in_specs=[HBM] * m + [SEM, SEM] + [ANY] * len(after), out_specs=[HBM] * m,
                         out_shape=[pltpu.HBM(a.shape, a.dtype) for a in recv],
                         input_output_aliases={j: j for j in range(m)},
                         compiler_params=pltpu.CompilerParams(has_side_effects=DATAFLOW))(*map(_hbm, recv), send_sems, recv_sems,
                                                                                          *map(_hbm, after))
    return list(res)


def _share_small(packed):
    rows = packed.shape[0]

    def body(in_ref, out_ref, send_sems, recv_sems, loc_sem):
        x, y, c = lax.axis_index("x"), lax.axis_index("y"), lax.axis_index("c")
        me = 4 * x + 2 * y + c
        copies = [pltpu.make_async_copy(in_ref, out_ref.at[me], loc_sem)]
        for k in range(1, 8):
            copies.append(pltpu.make_async_remote_copy(src_ref=in_ref, dst_ref=out_ref.at[me], send_sem=send_sems.at[k - 1],
                                                       recv_sem=recv_sems.at[k - 1], device_id=_peer(x, y, c, k),
                                                       device_id_type=MESH))
        for cp in copies:
            cp.start()
        for cp in copies:
            cp.wait()

    return pl.pallas_call(body, name="share_small", in_specs=[ANY], out_specs=ANY,
                          out_shape=jax.ShapeDtypeStruct((8, rows, 128), F32),
                          scratch_shapes=[pltpu.SemaphoreType.DMA((7,)), pltpu.SemaphoreType.DMA((7,)),
                                          pltpu.SemaphoreType.DMA])(packed)


def _sum_slots(buf, tile, name):
    _, rows, cols = buf.shape

    def body(b_ref, o_ref):
        acc = b_ref[0].astype(F32)
        for s in range(1, 8):
            acc = acc + b_ref[s].astype(F32)
        o_ref[...] = acc

    return pl.pallas_call(body, name=name, grid=(rows // tile,), in_specs=[pl.BlockSpec((8, tile, cols), lambda i: (0, i, 0))],
                          out_specs=pl.BlockSpec((tile, cols), lambda i: (i, 0)),
                          out_shape=jax.ShapeDtypeStruct((rows, cols), F32), compiler_params=_params(("parallel",)))(buf)


def _adamw(w, g, m, v, tile, name, after=()):
    rows, cols = w.shape

    def body(w_ref, g_ref, m_ref, v_ref, *rest):
        d_ref, mo_ref, vo_ref = rest[len(after):]
        gv = g_ref[...]
        mn = ADAM_B1 * m_ref[...] + (1.0 - ADAM_B1) * gv
        vn = ADAM_B2 * v_ref[...] + (1.0 - ADAM_B2) * (gv * gv)
        m_hat = mn / (1.0 - ADAM_B1 ** ADAM_STEP)
        v_hat = vn / (1.0 - ADAM_B2 ** ADAM_STEP)
        d_ref[...] = -ADAM_LR * (m_hat / (jnp.sqrt(v_hat) + ADAM_EPS) + ADAM_WD * w_ref[...])
        mo_ref[...] = mn
        vo_ref[...] = vn

    spec = pl.BlockSpec((tile, cols), lambda i: (i, 0))
    sds = jax.ShapeDtypeStruct((rows, cols), F32)
    return pl.pallas_call(body, name=name, grid=(rows // tile,), in_specs=[spec] * 4 + [ANY] * len(after), out_specs=[spec] * 3,
                          out_shape=[sds] * 3, compiler_params=_params(("parallel",)))(w, g, m, v, *map(_hbm, after))


def _row_tile(rows, cap):
    best = 8
    for t in range(8, min(rows, cap) + 1, 8):
        if rows % t == 0:
            best = t
    return best


def _pack(arrs):
    parts = []
    for a in arrs:
        r = a.size // 128
        r8 = -(-r // 8) * 8
        parts.append(jnp.pad(a.reshape(r, 128).astype(F32), ((0, r8 - r), (0, 0))))
    return jnp.concatenate(parts, axis=0)


def _unpack(packed, shapes):
    out, r0 = [], 0
    for shp in shapes:
        r = math.prod(shp) // 128
        out.append(packed[r0:r0 + r].reshape(shp))
        r0 += -(-r // 8) * 8
    return out


BIG = {
    "ffn1_w_gu": "col", "ffn1_w_down": "row", "ev_w_in": "col", "ev_w_out": "row", "od_w_in": "row", "od_w_group": "row",
    "od_w_out": "row", "xa_w_q": "row", "xa_w_kv": "col", "xa_w_o": "row", "ffn2_w_gu": "col", "ffn2_w_down": "row",
}
TINY_SHARDED = {"ev_conv_w": 2, "od_scale": 1}
WEIGHTS = ['ffn1_pre_g', 'ffn1_w_gu', 'ffn1_w_down', 'ffn1_post_g', 'mix_pre_g', 'mix_post_g', 'ev_w_in', 'ev_conv_w',
           'ev_conv_b', 'ev_conv_ln_g', 'ev_conv_ln_b', 'ev_sgu_ln_g', 'ev_sgu_ln_b', 'ev_sgu_w', 'ev_sgu_b', 'ev_w_out',
           'od_w_in', 'od_w_group', 'od_scale', 'od_w_out', 'xa_pre_g', 'xa_mem_g', 'xa_w_q', 'xa_w_kv', 'xa_w_o', 'xa_post_g',
           'ffn2_pre_g', 'ffn2_w_gu', 'ffn2_w_down', 'ffn2_post_g']


def _as3d(a):
    return a.reshape((-1,) + a.shape[-2:]) if a.ndim == 4 else a


class _WeightView:
    def __init__(self, store, view):
        self.store, self.view = store, view

    def __getitem__(self, k):
        return self.view(k)


class _Grads:
    def __init__(self):
        self.buf = {}
        self.fresh = []

    def add(self, name, layer, nlayers, a, b, tm, tn, n_outer=True):
        self.buf[name] = _matmul(a, b, "tn", BF16, "dw_" + name, tm, tn, out_l=layer, out_stack=nlayers,
                                 out_buf=self.buf.get(name), n_outer=n_outer)
        self.fresh.append(name)

    def take(self):
        names, self.fresh = self.fresh, []
        return names


def _local_step(x, mem, target, W, S, tiles=None, before_sub=None, after_sub_bwd=None):
    T, TW, tm = tiles or (min(512, S), min(256, S), min(512, S))
    TF = T
    row = lambda v: v.reshape(1, -1)
    subs = []
    small = {k: [None] * W[k].shape[0] for k in WEIGHTS if k not in BIG}
    g = _Grads()

    def ffn_fwd(tag, l, h, n, g_next):
        gu, a = _up_swiglu(n, W[tag + "_w_gu"], l, TF)
        f, h2, n2 = _post_mm(h, a, W[tag + "_w_down"], l, row(W[tag + "_post_g"][l]), 0.5, g_next, T)
        subs.append(dict(kind="ffn", tag=tag, l=l, h=h, n=n, gu=gu, a=a, f=f, scale=0.5, pre=tag + "_pre_g", post=tag + "_post_g"))
        return h2, n2

    def ffn_bwd(s, df):
        tag, l = s["tag"], s["l"]
        g.add(tag + "_w_down", l, 4, s["a"], df, 256, D, n_outer=False)
        dgu = _down_dx_swiglu_bwd(df, s["gu"], W[tag + "_w_down"], l, TF)
        g.add(tag + "_w_gu", l, 4, s["n"], dgu, D, 512, n_outer=False)
        return dgu, tag + "_w_gu", l, TF

    def xa_fwd(l, h, n, g_next):
        q = _matmul(n, W["xa_w_q"], "nn", BF16, "xa_q", tm, D, b_l=l)
        mn = _mem_norm(mem, row(W["xa_mem_g"][l]))
        kv = _matmul(mn, W["xa_w_kv"], "nn", BF16, "xa_kv", NMEM, D, b_l=l)
        o = _attn(q, kv, T)
        cx, h2, n2 = _post_mm(h, o, W["xa_w_o"], l, row(W["xa_post_g"][l]), 1.0, g_next, T)
        subs.append(dict(kind="xa", l=l, h=h, n=n, q=q, mn=mn, kv=kv, o=o, f=cx, scale=1.0, pre="xa_pre_g", post="xa_post_g"))
        return h2, n2

    def xa_bwd(s, dc):
        l = s["l"]
        do = _matmul(dc, W["xa_w_o"], "nt", BF16, "xa_o_dx", tm, D, b_l=l)
        g.add("xa_w_o", l, 4, s["o"], dc, D, 512)
        dq, dkv = _attn_bwd(s["q"], s["kv"], do, T)
        g.add("xa_w_q", l, 4, s["n"], dq, D, 512)
        dkvb = dkv.astype(BF16)
        g.add("xa_w_kv", l, 4, s["mn"], dkvb, D, 512)
        dmn = _matmul(dkvb, W["xa_w_kv"], "nt", F32, "xa_kv_dx", NMEM, D, b_l=l)
        small["xa_mem_g"][l] = _mem_gain_bwd(mem, dmn)[0]
        return dq, "xa_w_q", l, T

    def even_params(e):
        return (W["ev_conv_w"][e], row(W["ev_conv_b"][e]), row(W["ev_conv_ln_g"][e]), row(W["ev_conv_ln_b"][e]),
                row(W["ev_sgu_ln_g"][e]), row(W["ev_sgu_ln_b"][e]), W["ev_sgu_w"][e])

    def even_fwd(l, h, n, g_next):
        e = l // 2
        p = _matmul(n, W["ev_w_in"], "nn", F32, "ev_in", tm, D, b_l=e)
        cw, cb, clg, clb, slg, slb, ws = even_params(e)
        y, z = _even_core(p, cw, cb, clg, clb, slg, slb, ws, W["ev_sgu_b"][e].T, TW)
        m, h2, n2 = _post_mm(h, y, W["ev_w_out"], e, row(W["mix_post_g"][l]), 1.0, g_next, T)
        subs.append(dict(kind="even", l=l, h=h, n=n, p=p, y=y, z=z, f=m, scale=1.0, pre="mix_pre_g", post="mix_post_g"))
        return h2, n2

    def even_bwd(s, dm):
        l = s["l"]
        e = l // 2
        dy = _matmul(dm, W["ev_w_out"], "nt", F32, "ev_out_dx", tm, D, b_l=e)
        g.add("ev_w_out", e, 2, s["y"], dm, D, 512)
        cw, cb, clg, clb, slg, slb, ws = even_params(e)
        dconv, dpb, vecs, dws, dbst = _even_bwd_a(s["p"], s["z"], dy, cw, cb, clg, clb, slg, slb, ws,
                                                   jnp.swapaxes(ws, 1, 2), W["ev_sgu_b"][e].T, TW)
        dp, dcw = _even_bwd_b(s["p"], s["z"], dconv, dpb, cw, TW)
        for r, name in enumerate(["ev_conv_b", "ev_conv_ln_g", "ev_conv_ln_b", "ev_sgu_ln_g", "ev_sgu_ln_b"]):
            small[name][e] = vecs[r]
        small["ev_sgu_w"][e] = dws
        small["ev_sgu_b"][e] = dbst.T
        small["ev_conv_w"][e] = dcw[:CONV_W]
        g.add("ev_w_in", e, 2, s["n"], dp, D, 512)
        return dp, "ev_w_in", e, T

    def group_w(o):
        ng = len(POOL_WINDOWS)
        return (W["od_w_group"].reshape(-1, POOL_GD, POOL_GD), (ng, POOL_GD, POOL_GD), lambda i: (o, 0, 0))

    def odd_fwd(l, h, n, g_next):
        o = l // 2
        p = _matmul(n, W["od_w_in"], "nn", F32, "od_in", tm, D, b_l=o)
        d, e, es = _odd_core(p, group_w(o), row(W["od_scale"][o]), T)
        m, h2, n2 = _post_mm(h, es, W["od_w_out"], o, row(W["mix_post_g"][l]), 1.0, g_next, T)
        subs.append(dict(kind="odd", l=l, h=h, n=n, d=d, e=e, es=es, f=m, scale=1.0, pre="mix_pre_g", post="mix_post_g"))
        return h2, n2

    def odd_bwd(s, dm):
        l = s["l"]
        o = l // 2
        des = _matmul(dm, W["od_w_out"], "nt", F32, "od_out_dx", tm, D, b_l=o)
        g.add("od_w_out", o, 2, s["es"], dm, D, 512)
        dp, dsc, dwg = _odd_bwd(des, s["e"], s["d"], group_w(o), row(W["od_scale"][o]), T)
        small["od_scale"][o] = dsc[0]
        small["od_w_group_full"][o] = dwg
        g.add("od_w_in", o, 2, s["n"], dp, D, 512)
        return dp, "od_w_in", o, T

    small["od_w_group_full"] = [None, None]
    order = []
    for l in range(4):
        order += [("ffn1", l), ("even" if l % 2 == 0 else "odd", l), ("xa", l), ("ffn2", l)]
    pre_of = {"ffn1": "ffn1_pre_g", "even": "mix_pre_g", "odd": "mix_pre_g", "xa": "xa_pre_g", "ffn2": "ffn2_pre_g"}
    h = x
    n = _prenorm(h, row(W["ffn1_pre_g"][0]), T)
    for idx, (kind, l) in enumerate(order):
        if before_sub is not None:
            before_sub(kind, l, h)
        g_next = row(W[pre_of[order[idx + 1][0]]][order[idx + 1][1]]) if idx + 1 < len(order) else None
        if kind in ("ffn1", "ffn2"):
            h, n = ffn_fwd(kind, l, h, n, g_next)
        elif kind == "xa":
            h, n = xa_fwd(l, h, n, g_next)
        elif kind == "even":
            h, n = even_fwd(l, h, n, g_next)
        else:
            h, n = odd_fwd(l, h, n, g_next)

    top = subs[-1]
    dh, df, loss_acc, dgp = _loss_top(h, target, top["f"], row(W[top["post"]][top["l"]]), top["scale"], T)
    small[top["post"]][top["l"]] = dgp[0]
    for idx in range(len(subs) - 1, -1, -1):
        s = subs[idx]
        da, wname, wl, bt = {"ffn": ffn_bwd, "xa": xa_bwd, "even": even_bwd, "odd": odd_bwd}[s["kind"]](s, df)
        names = g.take()
        deps = [g.buf[k] for k in names]
        if idx > 0:
            sp = subs[idx - 1]
            dh, df, dg_pre, dg_post = _boundary_mm(da, W[wname], wl, s["h"], row(W[s["pre"]][s["l"]]), dh, sp["f"],
                                                   row(W[sp["post"]][sp["l"]]), sp["scale"], bt, deps=deps)
            small[sp["post"]][sp["l"]] = dg_post[0]
        else:
            dh, dg_pre = _boundary_mm(da, W[wname], wl, s["h"], row(W[s["pre"]][s["l"]]), dh, None, None, None, bt, deps=deps)
        small[s["pre"]][s["l"]] = dg_pre[0]
        if after_sub_bwd is not None:
            after_sub_bwd(order[idx][0], s["l"], g.buf, small)
            g.fresh = names + g.fresh
    return loss_acc[0, 0], dh, g.buf, small


def _step(P, S):
    x, mem, target = P["x"][0], P["mem"][0], P["loss_target"][0]
    chip = 2 * lax.axis_index("x") + lax.axis_index("y")
    pos = jnp.stack([chip, lax.axis_index("c")]).astype(jnp.int32)
    order = list(BIG)
    tix = {k: t for t, k in enumerate(order)}
    kinds = [BIG[k] for k in order]
    per = {k: (4 if k == "od_w_group" else 1) for k in order}
    members = {"ffn1": ["ffn1_w_gu", "ffn1_w_down"], "ffn2": ["ffn2_w_gu", "ffn2_w_down"], "xa": ["xa_w_q", "xa_w_kv", "xa_w_o"],
               "even": ["ev_w_in", "ev_w_out"], "odd": ["od_w_in", "od_w_group", "od_w_out"]}
    index_of = lambda kind, l: l // 2 if kind in ("even", "odd") else l
    sub_order = []
    for l in range(4):
        sub_order += [("ffn1", l), ("even" if l % 2 == 0 else "odd", l), ("xa", l), ("ffn2", l)]
    groups = {(kind, l): [(tix[k], index_of(kind, l), index_of(kind, l) * per[k], per[k]) for k in members[kind]]
              for kind, l in sub_order}
    tiny = _gather_tiny([P[k] for k in TINY_SHARDED], list(TINY_SHARDED.values()))
    slots = [_cast_into_slot(pos, _as3d(P[k]), BIG[k], "cast_" + k) for k in order]
    g_send, g_recv, bufs = _gather_start(slots, kinds, [groups[s_] for s_ in sub_order], tiny)
    W = {k: P[k] for k in WEIGHTS if k not in BIG}
    W.update(zip(TINY_SHARDED, tiny))
    W.update(zip(order, bufs))

    def view(k):
        return W[k].reshape(2, 4, POOL_GD, POOL_GD) if k == "od_w_group" else W[k]

    Wv = _WeightView(W, view)

    forwarded = {}

    def forward(sub, after):
        ks = members[sub[0]]
        s2, r2, thru = _gather_forward([W[k] for k in ks], [BIG[k] for k in ks], groups[sub], g_send, g_recv, after,
                                       "gather_forward_%s%d" % sub)
        W.update(zip(ks, thru))
        forwarded[sub] = (s2, r2)

    def before_sub(kind, l, h):
        sub = (kind, l)
        if sub == sub_order[0]:
            forward(sub, (h,))
        ks = members[kind]
        s2, r2 = forwarded.pop(sub)
        W.update(zip(ks, _gather_wait([W[k] for k in ks], [BIG[k] for k in ks], groups[sub], s2, r2, (h,),
                                      "gather_wait_%s%d" % sub)))
        nxt = sub_order.index(sub) + 1
        if nxt < len(sub_order):
            forward(sub_order[nxt], (h, W[ks[0]]))

    recv = {k: None for k in order}
    pending = []
    small_names = [k for k in WEIGHTS if k not in BIG]
    gsmall = {}

    def after_sub_bwd(kind, l, big, small):
        ks = [k for k in members[kind] if k != "od_w_group"]
        items = [(index_of(kind, l), 1)] * len(ks)
        after = ()
        if (kind, l) == ("odd", 1):
            big["od_w_group"] = jnp.stack(small.pop("od_w_group_full")).astype(BF16).reshape(8, POOL_GD, POOL_GD)
            ks, items = ks + ["od_w_group"], items + [(0, 8)]
        if (kind, l) == ("ffn1", 0):
            small_full = [jnp.stack(small[k]) for k in small_names]
            packed = _pack(small_full)
            summed = _sum_slots(_share_small(packed), _row_tile(packed.shape[0], 512), "sum_small")
            gsmall.update(zip(small_names, _unpack(summed, [a.shape for a in small_full])))
            after = (summed,)
        for k in ks:
            if recv[k] is None:
                recv[k] = lax.empty((7,) + _half_geometry(big[k].shape, BIG[k]), BF16)
        s_send, s_recv, g_thru, r_thru = _scatter_start([big[k] for k in ks], [recv[k] for k in ks], [BIG[k] for k in ks],
                                                        items, "scatter_start_%s%d" % (kind, l), after)
        big.update(zip(ks, g_thru))
        recv.update(zip(ks, r_thru))
        pending.append((kind, l, ks, items, s_send, s_recv))

    loss, grad_x, big, small = _local_step(x, mem, target, Wv, S, before_sub=before_sub, after_sub_bwd=after_sub_bwd)
    last = (big[members["ffn1"][0]],)
    for kind, l, ks, items, s_send, s_recv in pending:
        recv.update(zip(ks, _scatter_wait([recv[k] for k in ks], items, s_send, s_recv, "scatter_wait_%s%d" % (kind, l), last)))
    gsh = dict(zip(order, _swap_halves([_sum_into_half(pos, recv[k], big[k], BIG[k], "sum_" + k) for k in order])))
    for k, ax in TINY_SHARDED.items():
        size = P[k].shape[ax]
        gsmall[k] = lax.dynamic_slice_in_dim(gsmall[k], chip * size, size, axis=ax)

    grads, delta, new_m, new_v = {}, {}, {}, {}
    for k in order:
        shp = P[k].shape
        cols = shp[-1]
        flat = lambda a: a.reshape(-1, cols)
        gk = flat(gsh[k])
        d_, m_, v_ = _adamw(flat(P[k]), gk, flat(P["m_" + k]), flat(P["v_" + k]), _row_tile(gk.shape[0], 256), "adamw_" + k)
        grads[k], delta[k], new_m[k], new_v[k] = gk.reshape(shp), d_.reshape(shp), m_.reshape(shp), v_.reshape(shp)
    pk = lambda pre: _pack([P[pre + k] for k in small_names])
    d_, m_, v_ = _adamw(pk(""), _pack([gsmall[k] for k in small_names]), pk("m_"), pk("v_"),
                        pk("").shape[0], "adamw_small", last)
    shapes = [P[k].shape for k in small_names]
    for dst, src in ((delta, d_), (new_m, m_), (new_v, v_)):
        dst.update(zip(small_names, _unpack(src, shapes)))
    grads.update(gsmall)

    loss = lax.psum(loss, ("x", "y", "c"))
    out = [loss, grad_x[None]]
    for grp in (grads, delta, new_m, new_v):
        out += [grp[k] for k in WEIGHTS]
    return tuple(out)


def kernel(x, mem, ffn1_pre_g, ffn1_w_gu, ffn1_w_down, ffn1_post_g, mix_pre_g, mix_post_g, ev_w_in, ev_conv_w, ev_conv_b, ev_conv_ln_g, ev_conv_ln_b, ev_sgu_ln_g, ev_sgu_ln_b, ev_sgu_w, ev_sgu_b, ev_w_out, od_w_in, od_w_group, od_scale, od_w_out, xa_pre_g, xa_mem_g, xa_w_q, xa_w_kv, xa_w_o, xa_post_g, ffn2_pre_g, ffn2_w_gu, ffn2_w_down, ffn2_post_g, loss_target, m_ffn1_pre_g, m_ffn1_w_gu, m_ffn1_w_down, m_ffn1_post_g, m_mix_pre_g, m_mix_post_g, m_ev_w_in, m_ev_conv_w, m_ev_conv_b, m_ev_conv_ln_g, m_ev_conv_ln_b, m_ev_sgu_ln_g, m_ev_sgu_ln_b, m_ev_sgu_w, m_ev_sgu_b, m_ev_w_out, m_od_w_in, m_od_w_group, m_od_scale, m_od_w_out, m_xa_pre_g, m_xa_mem_g, m_xa_w_q, m_xa_w_kv, m_xa_w_o, m_xa_post_g, m_ffn2_pre_g, m_ffn2_w_gu, m_ffn2_w_down, m_ffn2_post_g, v_ffn1_pre_g, v_ffn1_w_gu, v_ffn1_w_down, v_ffn1_post_g, v_mix_pre_g, v_mix_post_g, v_ev_w_in, v_ev_conv_w, v_ev_conv_b, v_ev_conv_ln_g, v_ev_conv_ln_b, v_ev_sgu_ln_g, v_ev_sgu_ln_b, v_ev_sgu_w, v_ev_sgu_b, v_ev_w_out, v_od_w_in, v_od_w_group, v_od_scale, v_od_w_out, v_xa_pre_g, v_xa_mem_g, v_xa_w_q, v_xa_w_kv, v_xa_w_o, v_xa_post_g, v_ffn2_pre_g, v_ffn2_w_gu, v_ffn2_w_down, v_ffn2_post_g):
    P = dict(locals())
    return _step(P, x.shape[1])
```

```python
import functools
import math

import jax
import jax.numpy as jnp
from jax import lax
from jax.experimental import pallas as pl
from jax.experimental.pallas import tpu as pltpu

F32 = jnp.float32
BF16 = jnp.bfloat16
MESH = pl.DeviceIdType.MESH
ANY = pl.BlockSpec(memory_space=pl.ANY)
HBM = pl.BlockSpec(memory_space=pltpu.HBM)


def _hbm(x):
    return pltpu.with_memory_space_constraint(x, pltpu.HBM)

D = 1024
DFF = 2816
NMEM = 256
DC = 512
CONV_W = 31
CHUNK = 128
NHEAD_SGU = 4
POOL_WINDOWS = (2, 4, 8, 16)
POOL_GD = 256
XA_HEADS = 4
XA_HD = 256
EPS = 1e-6
NCHIP = 4
HALO = 32

ADAM_LR, ADAM_B1, ADAM_B2, ADAM_EPS, ADAM_WD, ADAM_STEP = 0.001, 0.9, 0.999, 1e-08, 0.01, 10

V7X_VMEM_BYTES = 64 * 1024 * 1024
VMEM_LIMIT = V7X_VMEM_BYTES - 8 * 1024 * 1024


def _params(sem):
    return pltpu.CompilerParams(dimension_semantics=sem, vmem_limit_bytes=VMEM_LIMIT)


def _matmul(a, b, kind, out_dtype, name, tm, tn, a_l=None, b_l=None, out_l=None, out_stack=None, out_buf=None,
            n_outer=True):
    a2, b2 = a.shape[-2:], b.shape[-2:]
    if kind == "nn":
        (m, k), (k2, n) = a2, b2
        dims = (((1,), (0,)), ((), ()))
    elif kind == "nt":
        (m, k), (n, k2) = a2, b2
        dims = (((1,), (1,)), ((), ()))
    else:
        (k, m), (k2, n) = a2, b2
        dims = (((0,), (0,)), ((), ()))
    assert k == k2 and m % tm == 0 and n % tn == 0, (name, a.shape, b.shape, tm, tn)
    if n_outer:
        grid = (n // tn, m // tm)
        ij = lambda p, q: (q, p)
    else:
        grid = (m // tm, n // tn)
        ij = lambda p, q: (p, q)

    def spec(arr, layer, blk, idx):
        if arr.ndim == 3:
            return pl.BlockSpec((None,) + blk, lambda p, q: (layer,) + idx(*ij(p, q)))
        return pl.BlockSpec(blk, lambda p, q: idx(*ij(p, q)))

    a_spec = spec(a, a_l, (k, tm) if kind == "tn" else (tm, k), (lambda i, j: (0, i)) if kind == "tn" else (lambda i, j: (i, 0)))
    b_spec = spec(b, b_l, (tn, k) if kind == "nt" else (k, tn), (lambda i, j: (j, 0)) if kind == "nt" else (lambda i, j: (0, j)))
    if out_stack is None:
        out_shape = jax.ShapeDtypeStruct((m, n), out_dtype)
        o_spec = pl.BlockSpec((tm, tn), lambda p, q: ij(p, q))
    else:
        out_shape = jax.ShapeDtypeStruct((out_stack, m, n), out_dtype)
        o_spec = pl.BlockSpec((None, tm, tn), lambda p, q: (out_l,) + ij(p, q))

    def body(a_ref, b_ref, *rest):
        o_ref = rest[-1]
        o_ref[...] = lax.dot_general(a_ref[...], b_ref[...], dims, preferred_element_type=F32).astype(o_ref.dtype)

    in_specs, args, aliases = [a_spec, b_spec], [a, _hbm(b) if b.ndim == 3 else b], {}
    if out_buf is not None:
        in_specs.append(ANY)
        args.append(_hbm(out_buf))
        aliases = {2: 0}
    return pl.pallas_call(body, name=name, grid=grid, in_specs=in_specs, out_specs=o_spec, out_shape=out_shape,
                          input_output_aliases=aliases, compiler_params=_params(("parallel", "parallel")))(*args)


def _rowwise(fn, name, rows, tile, row_ins, full_ins, row_outs, acc_outs=(), scratch=(), deps=()):
    assert rows % tile == 0, (name, rows, tile)
    in_specs, args = [], []
    for r in row_ins:
        if isinstance(r, tuple):
            arr, br, bc, imap = r
            in_specs.append(pl.BlockSpec((br, bc), imap))
        else:
            arr = r
            in_specs.append(pl.BlockSpec((tile, arr.shape[1]), lambda i: (i, 0)))
        args.append(arr)
    for f in full_ins:
        if isinstance(f, tuple):
            arr, blk, imap = f
            in_specs.append(pl.BlockSpec(blk, imap, pipeline_mode=pl.Buffered(1)))
            args.append(_hbm(arr))
        else:
            in_specs.append(pl.BlockSpec(f.shape, functools.partial(lambda nd, i: (0,) * nd, f.ndim)))
            args.append(f)
    for d in deps:
        in_specs.append(ANY)
        args.append(_hbm(d))
    out_specs, out_shape = [], []
    for w, dt in row_outs:
        out_specs.append(pl.BlockSpec((tile, w), lambda i: (i, 0)))
        out_shape.append(jax.ShapeDtypeStruct((rows, w), dt))
    for shp, dt in acc_outs:
        out_specs.append(pl.BlockSpec(shp, functools.partial(lambda nd, i: (0,) * nd, len(shp))))
        out_shape.append(jax.ShapeDtypeStruct(shp, dt))
    n_in, n_out = len(args), len(out_shape)

    def body(*refs):
        fn(pl.program_id(0), refs[:n_in], refs[n_in:n_in + n_out], refs[n_in + n_out:])

    sem = ("arbitrary",) if acc_outs else ("parallel",)
    res = pl.pallas_call(body, name=name, grid=(rows // tile,), in_specs=in_specs, out_specs=out_specs, out_shape=out_shape,
                         scratch_shapes=list(scratch), compiler_params=_params(sem))(*args)
    return res


def _accum(ref, i, val):
    @pl.when(i == 0)
    def _():
        ref[...] = val

    @pl.when(i > 0)
    def _():
        ref[...] += val


def _prev_halo(arr, tile, cols=None):
    r = tile // HALO
    return (arr, HALO, cols or arr.shape[1], lambda i: (jnp.maximum(i * r - 1, 0), 0))


def _next_halo(arr, tile, cols=None):
    r = tile // HALO
    last = arr.shape[0] // HALO - 1
    return (arr, HALO, cols or arr.shape[1], lambda i: (jnp.minimum((i + 1) * r, last), 0))


def _sigmoid(x):
    return 1.0 / (1.0 + jnp.exp(-x))


def _rms_hat(x):
    r = lax.rsqrt(jnp.mean(x * x, axis=-1, keepdims=True) + EPS)
    return x * r, r


def _rms_bwd(x, g, dy):
    xhat, r = _rms_hat(x)
    dxhat = dy * g
    dx = r * (dxhat - xhat * jnp.mean(dxhat * xhat, axis=-1, keepdims=True))
    return dx, jnp.sum(dy * xhat, axis=0, keepdims=True)


def _ln_hat(x):
    mu = jnp.mean(x, axis=-1, keepdims=True)
    xc = x - mu
    r = lax.rsqrt(jnp.mean(xc * xc, axis=-1, keepdims=True) + EPS)
    return xc * r, r


def _ln_bwd(xhat, r, g, dy):
    dxhat = dy * g
    dx = r * (dxhat - jnp.mean(dxhat, axis=-1, keepdims=True) - xhat * jnp.mean(dxhat * xhat, axis=-1, keepdims=True))
    return dx, jnp.sum(dy * xhat, axis=0, keepdims=True), jnp.sum(dy, axis=0, keepdims=True)


def _silu_grad(x):
    s = _sigmoid(x)
    return s * (1.0 + x * (1.0 - s))


_SQRT_HALF = math.sqrt(0.5)
_INV_SQRT_2PI = 1.0 / math.sqrt(2.0 * math.pi)


def _gelu(x):
    return 0.5 * x * (1.0 + lax.erf(x * _SQRT_HALF))


def _gelu_grad(x):
    return 0.5 * (1.0 + lax.erf(x * _SQRT_HALF)) + x * jnp.exp(-0.5 * x * x) * _INV_SQRT_2PI


def _dot(a, b, kind="nn"):
    dims = {"nn": (((1,), (0,)), ((), ())), "nt": (((1,), (1,)), ((), ())), "tn": (((0,), (0,)), ((), ()))}[kind]
    return lax.dot_general(a, b, dims, preferred_element_type=F32)


def _prenorm(h, g, tile):
    def fn(i, ins, outs, _):
        h_ref, g_ref = ins
        outs[0][...] = (_rms_hat(h_ref[...])[0] * g_ref[...]).astype(BF16)

    return _rowwise(fn, "prenorm", h.shape[0], tile, [h], [g], [(D, BF16)])[0]


def _loss_top(h, target, f, g_post, scale, tile):
    def fn(i, ins, outs, _):
        err = ins[0][...] - ins[1][...]
        per_row = jnp.mean(err * err, axis=-1, keepdims=True)
        _accum(outs[2], i, jnp.broadcast_to(0.5 * jnp.sum(per_row, axis=0, keepdims=True), (1, 128)))
        dh = err * (1.0 / D)
        outs[0][...] = dh
        df, dg = _rms_bwd(ins[2][...], ins[3][...], scale * dh)
        outs[1][...] = df.astype(BF16)
        _accum(outs[3], i, dg)

    return _rowwise(fn, "loss_top", h.shape[0], tile, [h, target, f], [g_post], [(D, F32), (D, BF16)],
                    [((1, 128), F32), ((1, D), F32)])


ROW_BLOCK = 256


def _row_blocks(tile):
    return [pl.ds(r0, min(ROW_BLOCK, tile)) for r0 in range(0, tile, ROW_BLOCK)]


def _layer_of(w3, l):
    return (w3, (None,) + tuple(w3.shape[1:]), lambda i: (l, 0, 0))


def _post_mm(h, a, w3, l, g_post, scale, g_next, tile):
    def fn(i, ins, outs, _):
        for r in _row_blocks(tile):
            f = _dot(ins[1][r, :], ins[2][...])
            outs[0][r, :] = f
            hn = ins[0][r, :] + scale * (_rms_hat(f)[0] * ins[3][...])
            outs[1][r, :] = hn
            if g_next is not None:
                outs[2][r, :] = (_rms_hat(hn)[0] * ins[4][...]).astype(BF16)

    fulls = [_layer_of(w3, l), g_post] + ([g_next] if g_next is not None else [])
    outs = [(D, F32), (D, F32)] + ([(D, BF16)] if g_next is not None else [])
    res = _rowwise(fn, "post_mm", h.shape[0], tile, [h, a], fulls, outs)
    return res[0], res[1], (res[2] if g_next is not None else None)


def _boundary_mm(a, w3, l, h, g_pre, dh_in, f_prev, g_post_prev, scale_prev, tile, deps=()):
    has_prev = f_prev is not None

    def fn(i, ins, outs, _):
        a_ref, h_ref, dhin_ref = ins[:3]
        w_ref, g_ref = ins[3 + has_prev], ins[4 + has_prev]
        dg_sum, dgp_sum = None, None
        for r in _row_blocks(tile):
            dn = _dot(a_ref[r, :], w_ref[...], "nt")
            dx, dg = _rms_bwd(h_ref[r, :], g_ref[...], dn)
            dh = dhin_ref[r, :] + dx
            outs[0][r, :] = dh
            dg_sum = dg if dg_sum is None else dg_sum + dg
            if has_prev:
                df, dgp = _rms_bwd(ins[3][r, :], ins[6][...], scale_prev * dh)
                outs[1][r, :] = df.astype(BF16)
                dgp_sum = dgp if dgp_sum is None else dgp_sum + dgp
        _accum(outs[1 + has_prev], i, dg_sum)
        if has_prev:
            _accum(outs[3], i, dgp_sum)

    rows = [a, h, dh_in] + ([f_prev] if has_prev else [])
    fulls = [_layer_of(w3, l), g_pre] + ([g_post_prev] if has_prev else [])
    outs = [(D, F32)] + ([(D, BF16)] if has_prev else [])
    accs = [((1, D), F32)] + ([((1, D), F32)] if has_prev else [])
    return _rowwise(fn, "boundary_mm", h.shape[0], tile, rows, fulls, outs, accs, deps=deps)


FF_CHUNK = DFF // 2


def _up_swiglu(n, w3, l, tile):
    def fn(i, ins, outs, _):
        n_ref, w_ref = ins
        nv = n_ref[...]
        for c0 in range(0, DFF, FF_CHUNK):
            g = _dot(nv, w_ref[:, c0:c0 + FF_CHUNK])
            u = _dot(nv, w_ref[:, DFF + c0:DFF + c0 + FF_CHUNK])
            outs[0][:, c0:c0 + FF_CHUNK] = g.astype(BF16)
            outs[0][:, DFF + c0:DFF + c0 + FF_CHUNK] = u.astype(BF16)
            outs[1][:, c0:c0 + FF_CHUNK] = (g * _sigmoid(g) * u).astype(BF16)

    return _rowwise(fn, "up_swiglu", n.shape[0], tile, [n], [_layer_of(w3, l)], [(2 * DFF, BF16), (DFF, BF16)])


def _down_dx_swiglu_bwd(df, gu, w3, l, tile):
    def fn(i, ins, outs, _):
        df_ref, gu_ref, w_ref = ins
        dfv = df_ref[...]
        for c0 in range(0, DFF, FF_CHUNK):
            da = _dot(dfv, w_ref[c0:c0 + FF_CHUNK, :], "nt")
            g = gu_ref[:, c0:c0 + FF_CHUNK].astype(F32)
            u = gu_ref[:, DFF + c0:DFF + c0 + FF_CHUNK].astype(F32)
            s = _sigmoid(g)
            das = da * s
            dup = das * g
            outs[0][:, c0:c0 + FF_CHUNK] = (u * (das + dup * (1.0 - s))).astype(BF16)
            outs[0][:, DFF + c0:DFF + c0 + FF_CHUNK] = dup.astype(BF16)

    return _rowwise(fn, "down_dx_swiglu_bwd", df.shape[0], tile, [df, gu], [_layer_of(w3, l)], [(2 * DFF, BF16)])[0]


def _softmax_rows(s):
    e = jnp.exp(s - jnp.max(s, axis=-1, keepdims=True))
    return e / jnp.sum(e, axis=-1, keepdims=True)


def _attn(q, kv, tile):
    def fn(i, ins, outs, _):
        q_ref, kv_ref = ins
        for hd in range(XA_HEADS):
            c0 = hd * XA_HD
            p = _softmax_rows(_dot(q_ref[:, c0:c0 + XA_HD], kv_ref[:, c0:c0 + XA_HD], "nt") * (XA_HD ** -0.5))
            outs[0][:, c0:c0 + XA_HD] = _dot(p.astype(BF16), kv_ref[:, D + c0:D + c0 + XA_HD]).astype(BF16)

    return _rowwise(fn, "attn", q.shape[0], tile, [q], [kv], [(D, BF16)])[0]


def _attn_bwd(q, kv, do, tile):
    def fn(i, ins, outs, _):
        q_ref, do_ref, kv_ref = ins
        for hd in range(XA_HEADS):
            c0 = hd * XA_HD
            qh, kh, vh = q_ref[:, c0:c0 + XA_HD], kv_ref[:, c0:c0 + XA_HD], kv_ref[:, D + c0:D + c0 + XA_HD]
            doh = do_ref[:, c0:c0 + XA_HD]
            p = _softmax_rows(_dot(qh, kh, "nt") * (XA_HD ** -0.5))
            dp = _dot(doh, vh, "nt")
            ds = (p * (dp - jnp.sum(dp * p, axis=-1, keepdims=True)) * (XA_HD ** -0.5)).astype(BF16)
            outs[0][:, c0:c0 + XA_HD] = _dot(ds, kh).astype(BF16)
            dk = _dot(ds, qh, "tn")
            dv = _dot(p.astype(BF16), doh, "tn")

            @pl.when(i == 0)
            def _():
                outs[1][:, c0:c0 + XA_HD] = dk
                outs[1][:, D + c0:D + c0 + XA_HD] = dv

            @pl.when(i > 0)
            def _():
                outs[1][:, c0:c0 + XA_HD] += dk
                outs[1][:, D + c0:D + c0 + XA_HD] += dv

    return _rowwise(fn, "attn_bwd", q.shape[0], tile, [q, do], [kv], [(D, BF16)], [((NMEM, 2 * D), F32)])


def _mem_norm(mem, g):
    def fn(i, ins, outs, _):
        outs[0][...] = (_rms_hat(ins[0][...])[0] * ins[1][...]).astype(BF16)

    return _rowwise(fn, "mem_norm", NMEM, NMEM, [mem], [g], [(D, BF16)])[0]


def _mem_gain_bwd(mem, dmn):
    def fn(i, ins, outs, _):
        outs[0][...] = jnp.sum(ins[1][...] * _rms_hat(ins[0][...])[0], axis=0, keepdims=True)

    return _rowwise(fn, "mem_gain_bwd", NMEM, NMEM, [mem, dmn], [], [], [((1, D), F32)])[0]


SHIFT_ROWS = 8


def _shifted_scratch(tile):
    return pltpu.VMEM((SHIFT_ROWS - 1, tile + HALO - SHIFT_ROWS, DC), F32)


def _fill_shifted(zs_ref, zext_ref):
    rows = zs_ref.shape[1]
    for b in range(1, SHIFT_ROWS):
        zs_ref[b - 1] = zext_ref[pl.ds(b, rows), :]


def _rows_at(zext_ref, zs_ref, offset, tile):
    a, b = divmod(offset, SHIFT_ROWS)
    if b == 0:
        return zext_ref[pl.ds(offset, tile), :]
    return zs_ref[b - 1, pl.ds(SHIFT_ROWS * a, tile), :]


def _conv_taps(w_ref, zext_ref, zs_ref, tile, shift0):
    acc = None
    for k in range(CONV_W):
        term = w_ref[k:k + 1, :] * _rows_at(zext_ref, zs_ref, shift0(k), tile)
        acc = term if acc is None else acc + term
    return acc


def _tril_mask(transpose=False):
    r, c = (lax.broadcasted_iota(jnp.int32, (CHUNK, CHUNK), a) for a in (0, 1))
    return (r <= c) if transpose else (r >= c)


def _even_core(p, cw, cb, clg, clb, slg, slb, ws, bst, tile):
    def fn(i, ins, outs, scr):
        p_ref, ph_ref, cw_ref, cb_ref, clg_ref, clb_ref, slg_ref, slb_ref, ws_ref, bst_ref = ins
        y_ref, z_ref = outs
        zext = scr[0]
        z = p_ref[:, :DC] * _sigmoid(p_ref[:, DC:2 * DC])
        zh = ph_ref[:, :DC] * _sigmoid(ph_ref[:, DC:2 * DC])
        zext[:HALO, :] = jnp.where(i > 0, zh, 0.0)
        zext[HALO:, :] = z
        z_ref[...] = z
        _fill_shifted(scr[1], zext)
        conv = _conv_taps(cw_ref, zext, scr[1], tile, lambda k: HALO - (CONV_W - 1) + k) + cb_ref[...]
        yl = _ln_hat(conv)[0] * clg_ref[...] + clb_ref[...]
        y_ref[:, :DC] = (yl * _sigmoid(yl)).astype(BF16)
        zb = _gelu(p_ref[:, 2 * DC:])
        vln = (_ln_hat(zb[:, DC:])[0] * slg_ref[...] + slb_ref[...]).astype(BF16)
        mask = _tril_mask()
        for hd in range(NHEAD_SGU):
            wm = jnp.where(mask, ws_ref[hd], 0.0).astype(BF16)
            for ci in range(tile // CHUNK):
                r0, c0 = ci * CHUNK, hd * CHUNK
                mixed = _dot(wm, vln[r0:r0 + CHUNK, c0:c0 + CHUNK]) + bst_ref[:, hd:hd + 1]
                y_ref[r0:r0 + CHUNK, DC + c0:DC + c0 + CHUNK] = (zb[r0:r0 + CHUNK, c0:c0 + CHUNK] * mixed).astype(BF16)

    return _rowwise(fn, "even_core", p.shape[0], tile, [p, _prev_halo(p, tile, D)], [cw, cb, clg, clb, slg, slb, ws, bst],
                    [(D, BF16), (DC, F32)], scratch=[pltpu.VMEM((tile + HALO, DC), F32), _shifted_scratch(tile)])


def _even_bwd_a(p, z, dy, cw, cb, clg, clb, slg, slb, ws, wst, bst, tile):
    def fn(i, ins, outs, scr):
        p_ref, z_ref, zh_ref, dy_ref, cw_ref, cb_ref, clg_ref, clb_ref, slg_ref, slb_ref, ws_ref, wst_ref, bst_ref = ins
        dc_ref, dpb_ref, vec_ref, dws_ref, dbst_ref = outs
        zext, dvln_s, zs = scr
        zext[:HALO, :] = jnp.where(i > 0, zh_ref[...], 0.0)
        zext[HALO:, :] = z_ref[...]
        _fill_shifted(zs, zext)
        conv = _conv_taps(cw_ref, zext, zs, tile, lambda k: HALO - (CONV_W - 1) + k) + cb_ref[...]
        chat, cr = _ln_hat(conv)
        yl = chat * clg_ref[...] + clb_ref[...]
        dyl = dy_ref[:, :DC] * _silu_grad(yl)
        dconv, dclg, dclb = _ln_bwd(chat, cr, clg_ref[...], dyl)
        dc_ref[...] = dconv
        xb = p_ref[:, 2 * DC:]
        zb = _gelu(xb)
        vhat, vr = _ln_hat(zb[:, DC:])
        vln = (vhat * slg_ref[...] + slb_ref[...]).astype(BF16)
        dyb = dy_ref[:, DC:]
        mask, mask_t = _tril_mask(), _tril_mask(transpose=True)
        dbu_parts = []
        for hd in range(NHEAD_SGU):
            wm = jnp.where(mask, ws_ref[hd], 0.0).astype(BF16)
            wt = jnp.where(mask_t, wst_ref[hd], 0.0).astype(BF16)
            dws_h, dbs_h, rows = None, None, []
            for ci in range(tile // CHUNK):
                r0, c0 = ci * CHUNK, hd * CHUNK
                vblk = vln[r0:r0 + CHUNK, c0:c0 + CHUNK]
                mixed = _dot(wm, vblk) + bst_ref[:, hd:hd + 1]
                dyb_blk = dyb[r0:r0 + CHUNK, c0:c0 + CHUNK]
                rows.append(dyb_blk * mixed)
                dmixed = dyb_blk * zb[r0:r0 + CHUNK, c0:c0 + CHUNK]
                dmb = dmixed.astype(BF16)
                dvln_s[r0:r0 + CHUNK, c0:c0 + CHUNK] = _dot(wt, dmb)
                dw = _dot(dmb, vblk, "nt")
                db = jnp.sum(dmixed, axis=-1, keepdims=True)
                dws_h = dw if dws_h is None else dws_h + dw
                dbs_h = db if dbs_h is None else dbs_h + db
            dbu_parts.append(jnp.concatenate(rows, axis=0))
            dws_h = jnp.where(mask, dws_h, 0.0)

            @pl.when(i == 0)
            def _():
                dws_ref[hd] = dws_h
                dbst_ref[:, hd:hd + 1] = dbs_h

            @pl.when(i > 0)
            def _():
                dws_ref[hd] += dws_h
                dbst_ref[:, hd:hd + 1] += dbs_h

        dbu = jnp.concatenate(dbu_parts, axis=1)
        dbv, dslg, dslb = _ln_bwd(vhat, vr, slg_ref[...], dvln_s[...])
        dpb_ref[:, :DC] = (dbu * _gelu_grad(xb[:, :DC])).astype(BF16)
        dpb_ref[:, DC:] = (dbv * _gelu_grad(xb[:, DC:])).astype(BF16)
        @pl.when(i == 0)
        def _():
            vec_ref[...] = jnp.zeros_like(vec_ref)

        for r, val in enumerate([jnp.sum(dconv, axis=0, keepdims=True), dclg, dclb, dslg, dslb]):
            vec_ref[r:r + 1, :] += val

    return _rowwise(fn, "even_bwd_a", p.shape[0], tile, [p, z, _prev_halo(z, tile), dy],
                    [cw, cb, clg, clb, slg, slb, ws, wst, bst], [(DC, F32), (D, BF16)],
                    [((8, DC), F32), ((NHEAD_SGU, CHUNK, CHUNK), F32), ((CHUNK, NHEAD_SGU), F32)],
                    scratch=[pltpu.VMEM((tile + HALO, DC), F32), pltpu.VMEM((tile, DC), F32), _shifted_scratch(tile)])


def _even_bwd_b(p, z, dconv, dpb, cw, tile):
    def fn(i, ins, outs, scr):
        p_ref, z_ref, zh_ref, dc_ref, dcn_ref, dpb_ref, cw_ref = ins
        dp_ref, dcw_ref = outs
        zext, dcext, zs, dcs = scr
        last = pl.num_programs(0) - 1
        zext[:HALO, :] = jnp.where(i > 0, zh_ref[...], 0.0)
        zext[HALO:, :] = z_ref[...]
        dcext[:tile, :] = dc_ref[...]
        dcext[tile:, :] = jnp.where(i < last, dcn_ref[...], 0.0)
        _fill_shifted(zs, zext)
        _fill_shifted(dcs, dcext)
        dz = _conv_taps(cw_ref, dcext, dcs, tile, lambda k: (CONV_W - 1) - k)
        gate = p_ref[:, DC:2 * DC]
        s = _sigmoid(gate)
        dp_ref[:, :DC] = (dz * s).astype(BF16)
        dp_ref[:, DC:2 * DC] = (dz * p_ref[:, :DC] * s * (1.0 - s)).astype(BF16)
        dp_ref[:, 2 * DC:] = dpb_ref[...]
        dcv = dc_ref[...]

        @pl.when(i == 0)
        def _():
            dcw_ref[...] = jnp.zeros_like(dcw_ref)

        for k in range(CONV_W):
            dcw_ref[k:k + 1, :] += jnp.sum(dcv * _rows_at(zext, zs, HALO - (CONV_W - 1) + k, tile), axis=0, keepdims=True)

    return _rowwise(fn, "even_bwd_b", p.shape[0], tile,
                    [(p, tile, D, lambda i: (i, 0)), z, _prev_halo(z, tile), dconv, _next_halo(dconv, tile), dpb], [cw],
                    [(2 * D, BF16)], [((HALO, DC), F32)],
                    scratch=[pltpu.VMEM((tile + HALO, DC), F32), pltpu.VMEM((tile + HALO, DC), F32), _shifted_scratch(tile),
                             _shifted_scratch(tile)])


def _row_count(i, tile, nrows, offset, w):
    t = i * tile + offset + lax.broadcasted_iota(jnp.int32, (nrows, POOL_GD), 0)
    return jnp.minimum(t + 1, w).astype(F32)


def _odd_core(p, wg, scale, tile):
    def fn(i, ins, outs, scr):
        p_ref, ph_ref, wg_ref, sc_ref = ins
        d_ref, e_ref, es_ref = outs
        pext = scr[0]
        pext[:HALO, :] = jnp.where(i > 0, ph_ref[...], 0.0)
        pext[HALO:, :] = p_ref[...]
        for g, w in enumerate(POOL_WINDOWS):
            c0 = g * POOL_GD
            s = p_ref[:, c0:c0 + POOL_GD]
            for r in range(1, w):
                s = s + pext[pl.ds(HALO - r, tile), c0:c0 + POOL_GD]
            dg = (s / _row_count(i, tile, tile, 0, w) - p_ref[:, c0:c0 + POOL_GD]).astype(BF16)
            d_ref[:, c0:c0 + POOL_GD] = dg
            e = _dot(dg, wg_ref[g])
            e_ref[:, c0:c0 + POOL_GD] = e
            es_ref[:, c0:c0 + POOL_GD] = (e * sc_ref[:, c0:c0 + POOL_GD]).astype(BF16)

    return _rowwise(fn, "odd_core", p.shape[0], tile, [p, _prev_halo(p, tile)], [wg, scale],
                    [(D, BF16), (D, F32), (D, BF16)], scratch=[pltpu.VMEM((tile + HALO, D), F32)])


def _odd_bwd(des, e, d, wg, scale, tile):
    def fn(i, ins, outs, scr):
        des_ref, desn_ref, e_ref, d_ref, wg_ref, sc_ref = ins
        dp_ref, dsc_ref, dwg_ref = outs
        qext = scr[0]
        last = pl.num_programs(0) - 1
        desv = des_ref[...]
        _accum(dsc_ref, i, jnp.sum(desv * e_ref[...], axis=0, keepdims=True))
        de = (desv * sc_ref[...]).astype(BF16)
        den = (jnp.where(i < last, desn_ref[...], 0.0) * sc_ref[...]).astype(BF16)
        for g, w in enumerate(POOL_WINDOWS):
            c0 = g * POOL_GD
            deg = de[:, c0:c0 + POOL_GD]
            dd = _dot(deg, wg_ref[g], "nt")
            ddn = _dot(den[:, c0:c0 + POOL_GD], wg_ref[g], "nt")
            qext[:tile, c0:c0 + POOL_GD] = dd / _row_count(i, tile, tile, 0, w)
            qext[tile:, c0:c0 + POOL_GD] = ddn / _row_count(i, tile, HALO, tile, w)
            s = qext[:tile, c0:c0 + POOL_GD]
            for r in range(1, w):
                s = s + qext[pl.ds(r, tile), c0:c0 + POOL_GD]
            dp_ref[:, c0:c0 + POOL_GD] = (s - dd).astype(BF16)
            dw = _dot(d_ref[:, c0:c0 + POOL_GD], deg, "tn")

            @pl.when(i == 0)
            def _():
                dwg_ref[g] = dw

            @pl.when(i > 0)
            def _():
                dwg_ref[g] += dw

    return _rowwise(fn, "odd_bwd", des.shape[0], tile, [des, _next_halo(des, tile), e, d], [wg, scale], [(D, BF16)],
                    [((1, D), F32), ((len(POOL_WINDOWS), POOL_GD, POOL_GD), F32)],
                    scratch=[pltpu.VMEM((tile + HALO, D), F32)])


def _axis_slice(ref, axis, start, size):
    idx = [slice(None)] * len(ref.shape)
    idx[axis] = pl.ds(start, size)
    return ref.at[tuple(idx)]


def _cast_into_slot(pos, shard, kind, name):
    L, rs, cs = shard.shape
    tr = _row_tile(rs, 512)
    nr = rs // tr
    if kind == "row":
        full, omap = (L, NCHIP * rs, cs), (lambda l, i, p: (l, p[0] * nr + i, 0))
    else:
        full, omap = (L, rs, NCHIP * cs), (lambda l, i, p: (l, i, p[0]))

    def body(p_ref, s_ref, o_ref):
        o_ref[...] = s_ref[...].astype(BF16)

    grid_spec = pltpu.PrefetchScalarGridSpec(
        num_scalar_prefetch=1, grid=(L, nr), in_specs=[pl.BlockSpec((None, tr, cs), lambda l, i, p: (l, i, 0))],
        out_specs=pl.BlockSpec((None, tr, cs), omap))
    return pl.pallas_call(body, name=name, grid_spec=grid_spec, out_shape=jax.ShapeDtypeStruct(full, BF16),
                          compiler_params=_params(("parallel", "parallel")))(pos, shard)


def _peer(x, y, c, k):
    return (1 - x if k & 4 else x, 1 - y if k & 2 else y, 1 - c if k & 1 else c)


def _half_geometry(shape, kind):
    L, R, C = shape
    return (L, R // (2 * NCHIP), C) if kind == "row" else (L, R // 2, C // NCHIP)


def _sum_into_half(pos, recv, grad, kind, name):
    _, L, rh, cs = recv.shape
    if kind == "row":
        gmap = lambda l, p: (l, 2 * p[0] + p[1], 0)
    else:
        gmap = lambda l, p: (l, p[1], p[0])

    def body(p_ref, r_ref, g_ref, o_ref):
        acc = g_ref[...].astype(F32)
        for s in range(7):
            acc = acc + r_ref[s].astype(F32)
        o_ref[...] = acc

    grid_spec = pltpu.PrefetchScalarGridSpec(
        num_scalar_prefetch=1, grid=(L,),
        in_specs=[pl.BlockSpec((7, None, rh, cs), lambda l, p: (0, l, 0, 0)), pl.BlockSpec((None, rh, cs), gmap)],
        out_specs=pl.BlockSpec((None, rh, cs), lambda l, p: (l, p[1], 0)))
    return pl.pallas_call(body, name=name, grid_spec=grid_spec, out_shape=jax.ShapeDtypeStruct((L, 2 * rh, cs), F32),
                          compiler_params=_params(("parallel",)))(pos, _hbm(recv), _hbm(grad))


def _swap_halves(shards):
    n = len(shards)

    def body(*refs):
        outs = refs[n:2 * n]
        send_sems, recv_sems = refs[2 * n:]
        x, y, c = lax.axis_index("x"), lax.axis_index("y"), lax.axis_index("c")
        copies = []
        for t in range(n):
            rh = outs[t].shape[1] // 2
            half = outs[t].at[:, pl.ds(c * rh, rh), :]
            cp = pltpu.make_async_remote_copy(src_ref=half, dst_ref=half, send_sem=send_sems.at[t], recv_sem=recv_sems.at[t],
                                              device_id=(x, y, 1 - c), device_id_type=MESH)
            cp.start()
            copies.append(cp)
        for cp in copies:
            cp.wait()

    return pl.pallas_call(body, name="swap_halves", in_specs=[HBM] * n, out_specs=[HBM] * n,
                          out_shape=[pltpu.HBM(s.shape, s.dtype) for s in shards],
                          input_output_aliases={t: t for t in range(n)},
                          scratch_shapes=[pltpu.SemaphoreType.DMA((n,)), pltpu.SemaphoreType.DMA((n,))])(*map(_hbm, shards))


SEM = pl.BlockSpec(memory_space=pltpu.SEMAPHORE)
DATAFLOW = pltpu.SideEffectType.DATAFLOW_SIDE_EFFECTING
MAX_LAYERS = 4


def _gsem(t, layer, k):
    return (t * MAX_LAYERS + layer) * 3 + k


def _half_slot(ref, kind, d0, nd0, chip, core):
    if kind == "row":
        half = ref.shape[1] // (2 * NCHIP)
        return ref.at[pl.ds(d0, nd0), pl.ds((2 * chip + core) * half, half), :]
    size, half = ref.shape[2] // NCHIP, ref.shape[1] // 2
    return ref.at[pl.ds(d0, nd0), pl.ds(core * half, half), pl.ds(chip * size, size)]


def _other_chips(x, y):
    return [(1 - x, y, 2 * (1 - x) + y), (x, 1 - y, 2 * x + 1 - y), (1 - x, 1 - y, 2 * (1 - x) + 1 - y)]


def _gather_start(bufs, kinds, groups, after):
    n, na = len(bufs), len(after)

    def body(*refs):
        send_sems, recv_sems = refs[n + na], refs[n + na + 1]
        outs = refs[n + na + 2:]
        x, y, c = lax.axis_index("x"), lax.axis_index("y"), lax.axis_index("c")
        for group in groups:
            for t, li, d0, nd0 in group:
                mine = _half_slot(outs[t], kinds[t], d0, nd0, 2 * x + y, c)
                for k, (px, py, _) in enumerate(_other_chips(x, y)):
                    pltpu.make_async_remote_copy(src_ref=mine, dst_ref=mine, send_sem=send_sems.at[_gsem(t, li, k)],
                                                 recv_sem=recv_sems.at[_gsem(t, li, k)], device_id=(px, py, c),
                                                 device_id_type=MESH).start()

    sems = pltpu.SemaphoreType.DMA((n * MAX_LAYERS * 3,))
    res = pl.pallas_call(body, name="gather_start", in_specs=[HBM] * n + [ANY] * na, out_specs=[SEM, SEM] + [HBM] * n,
                         out_shape=[sems, sems] + [pltpu.HBM(b.shape, b.dtype) for b in bufs],
                         input_output_aliases={t: t + 2 for t in range(n)},
                         compiler_params=pltpu.CompilerParams(has_side_effects=DATAFLOW))(*map(_hbm, bufs), *after)
    return res[0], res[1], list(res[2:])


def _gather_forward(bufs, kinds, group, send_sems, recv_sems, after, name):
    m, na = len(bufs), len(after)

    def body(*refs):
        send, recv = refs[m], refs[m + 1]
        send2, recv2 = refs[m + 2 + na], refs[m + 3 + na]
        outs = refs[m + 4 + na:]
        x, y, c = lax.axis_index("x"), lax.axis_index("y"), lax.axis_index("c")
        for j, (t, li, d0, nd0) in enumerate(group):
            mine = _half_slot(outs[j], kinds[j], d0, nd0, 2 * x + y, c)
            for k, (px, py, chip) in enumerate(_other_chips(x, y)):
                got = _half_slot(outs[j], kinds[j], d0, nd0, chip, c)
                first = pltpu.make_async_remote_copy(src_ref=mine, dst_ref=got, send_sem=send.at[_gsem(t, li, k)],
                                                     recv_sem=recv.at[_gsem(t, li, k)], device_id=(px, py, c),
                                                     device_id_type=MESH)
                first.wait_send()
                first.wait_recv()
                pltpu.make_async_remote_copy(src_ref=got, dst_ref=got, send_sem=send2.at[3 * j + k], recv_sem=recv2.at[3 * j + k],
                                             device_id=(x, y, 1 - c), device_id_type=MESH).start()

    sems = pltpu.SemaphoreType.DMA((3 * m,))
    res = pl.pallas_call(body, name=name, in_specs=[HBM] * m + [SEM, SEM] + [ANY] * na, out_specs=[SEM, SEM] + [HBM] * m,
                         out_shape=[sems, sems] + [pltpu.HBM(b.shape, b.dtype) for b in bufs],
                         input_output_aliases={j: j + 2 for j in range(m)},
                         compiler_params=pltpu.CompilerParams(has_side_effects=DATAFLOW))(*map(_hbm, bufs), send_sems, recv_sems,
                                                                                          *map(_hbm, after))
    return res[0], res[1], list(res[2:])


def _gather_wait(bufs, kinds, group, send_sems, recv_sems, after, name):
    m, na = len(bufs), len(after)

    def body(*refs):
        send, recv = refs[m], refs[m + 1]
        outs = refs[m + 2 + na:]
        x, y, c = lax.axis_index("x"), lax.axis_index("y"), lax.axis_index("c")
        for j, (t, li, d0, nd0) in enumerate(group):
            for k, (_, _, chip) in enumerate(_other_chips(x, y)):
                cp = pltpu.make_async_remote_copy(src_ref=_half_slot(outs[j], kinds[j], d0, nd0, chip, c),
                                                  dst_ref=_half_slot(outs[j], kinds[j], d0, nd0, chip, 1 - c),
                                                  send_sem=send.at[3 * j + k], recv_sem=recv.at[3 * j + k],
                                                  device_id=(x, y, 1 - c), device_id_type=MESH)
                cp.wait_send()
                cp.wait_recv()

    res = pl.pallas_call(body, name=name, in_specs=[HBM] * m + [SEM, SEM] + [ANY] * na, out_specs=[HBM] * m,
                         out_shape=[pltpu.HBM(b.shape, b.dtype) for b in bufs],
                         input_output_aliases={j: j for j in range(m)},
                         compiler_params=pltpu.CompilerParams(has_side_effects=DATAFLOW))(*map(_hbm, bufs), send_sems, recv_sems,
                                                                                          *map(_hbm, after))
    return list(res)


def _gather_tiny(tiny, axes):
    n = len(tiny)
    out_shape = []
    for s_, ax in zip(tiny, axes):
        shp = list(s_.shape)
        shp[ax] *= NCHIP
        out_shape.append(jax.ShapeDtypeStruct(tuple(shp), s_.dtype))

    def body(*refs):
        ins, outs = refs[:n], refs[n:2 * n]
        send_sems, recv_sems, loc_sems = refs[2 * n:]
        x, y, c = lax.axis_index("x"), lax.axis_index("y"), lax.axis_index("c")
        mine = 2 * x + y
        chips = [(1 - x, y), (x, 1 - y), (1 - x, 1 - y)]
        copies = []
        for t in range(n):
            size = ins[t].shape[axes[t]]
            dst = _axis_slice(outs[t], axes[t], mine * size, size)
            copies.append(pltpu.make_async_copy(ins[t], dst, loc_sems.at[t]))
            for k, chip in enumerate(chips):
                copies.append(pltpu.make_async_remote_copy(src_ref=ins[t], dst_ref=dst, send_sem=send_sems.at[t, k],
                                                           recv_sem=recv_sems.at[t, k], device_id=(*chip, c),
                                                           device_id_type=MESH))
        for cp in copies:
            cp.start()
        for cp in copies:
            cp.wait()

    return pl.pallas_call(body, name="gather_tiny", in_specs=[ANY] * n, out_specs=[ANY] * n, out_shape=out_shape,
                          scratch_shapes=[pltpu.SemaphoreType.DMA((n, 3)), pltpu.SemaphoreType.DMA((n, 3)),
                                          pltpu.SemaphoreType.DMA((n,))])(*tiny)


def _grad_slice(ref, kind, d0, nd0, chip, core, rh, cs):
    if kind == "row":
        return ref.at[pl.ds(d0, nd0), pl.ds((2 * chip + core) * rh, rh), :]
    return ref.at[pl.ds(d0, nd0), pl.ds(core * rh, rh), pl.ds(chip * cs, cs)]


def _scatter_start(grads, recv, kinds, items, name, after=()):
    m, na = len(grads), len(after)

    def body(*refs):
        send_sems, recv_sems = refs[2 * m + na], refs[2 * m + na + 1]
        gout, rout = refs[2 * m + na + 2:3 * m + na + 2], refs[3 * m + na + 2:]
        x, y, c = lax.axis_index("x"), lax.axis_index("y"), lax.axis_index("c")
        for t in range(m):
            _, _, rh, cs = rout[t].shape
            d0, nd0 = items[t]
            for k in range(1, 8):
                px, py, pc = _peer(x, y, c, k)
                src = _grad_slice(gout[t], kinds[t], d0, nd0, 2 * px + py, pc, rh, cs)
                pltpu.make_async_remote_copy(src_ref=src, dst_ref=rout[t].at[k - 1, pl.ds(d0, nd0)],
                                             send_sem=send_sems.at[7 * t + k - 1], recv_sem=recv_sems.at[7 * t + k - 1],
                                             device_id=(px, py, pc), device_id_type=MESH).start()

    sems = pltpu.SemaphoreType.DMA((7 * m,))
    res = pl.pallas_call(body, name=name, in_specs=[HBM] * (2 * m) + [ANY] * na, out_specs=[SEM, SEM] + [HBM] * (2 * m),
                         out_shape=[sems, sems] + [pltpu.HBM(a.shape, a.dtype) for a in list(grads) + list(recv)],
                         input_output_aliases={j: j + 2 for j in range(2 * m)},
                         compiler_params=pltpu.CompilerParams(has_side_effects=DATAFLOW))(*map(_hbm, grads), *map(_hbm, recv),
                                                                                          *after)
    return res[0], res[1], list(res[2:m + 2]), list(res[m + 2:])


def _scatter_wait(recv, items, send_sems, recv_sems, name, after=()):
    m = len(recv)

    def body(*refs):
        send, rcv = refs[m], refs[m + 1]
        outs = refs[m + 2 + len(after):]
        x, y, c = lax.axis_index("x"), lax.axis_index("y"), lax.axis_index("c")
        for t in range(m):
            d0, nd0 = items[t]
            for k in range(1, 8):
                land = outs[t].at[k - 1, pl.ds(d0, nd0)]
                cp = pltpu.make_async_remote_copy(src_ref=land, dst_ref=land, send_sem=send.at[7 * t + k - 1],
                                                  recv_sem=rcv.at[7 * t + k - 1], device_id=_peer(x, y, c, k), device_id_type=MESH)
                cp.wait_send()
                cp.wait_recv()

    res = pl.pallas_call(body, name=name, in_specs=[HBM] * m + [SEM, SEM] + [ANY] * len(after), out_specs=[HBM] * m,
                         out_shape=[pltpu.HBM(a.shape, a.dtype) for a in recv],
                         input_output_aliases={j: j for j in range(m)},
                         compiler_params=pltpu.CompilerParams(has_side_effects=DATAFLOW))(*map(_hbm, recv), send_sems, recv_sems,
                                                                                          *map(_hbm, after))
    return list(res)


def _share_small(packed):
    rows = packed.shape[0]

    def body(in_ref, out_ref, send_sems, recv_sems, loc_sem):
        x, y, c = lax.axis_index("x"), lax.axis_index("y"), lax.axis_index("c")
        me = 4 * x + 2 * y + c
        copies = [pltpu.make_async_copy(in_ref, out_ref.at[me], loc_sem)]
        for k in range(1, 8):
            copies.append(pltpu.make_async_remote_copy(src_ref=in_ref, dst_ref=out_ref.at[me], send_sem=send_sems.at[k - 1],
                                                       recv_sem=recv_sems.at[k - 1], device_id=_peer(x, y, c, k),
                                                       device_id_type=MESH))
        for cp in copies:
            cp.start()
        for cp in copies:
            cp.wait()

    return pl.pallas_call(body, name="share_small", in_specs=[ANY], out_specs=ANY,
                          out_shape=jax.ShapeDtypeStruct((8, rows, 128), F32),
                          scratch_shapes=[pltpu.SemaphoreType.DMA((7,)), pltpu.SemaphoreType.DMA((7,)),
                                          pltpu.SemaphoreType.DMA])(packed)


def _sum_slots(buf, tile, name):
    _, rows, cols = buf.shape

    def body(b_ref, o_ref):
        acc = b_ref[0].astype(F32)
        for s in range(1, 8):
            acc = acc + b_ref[s].astype(F32)
        o_ref[...] = acc

    return pl.pallas_call(body, name=name, grid=(rows // tile,), in_specs=[pl.BlockSpec((8, tile, cols), lambda i: (0, i, 0))],
                          out_specs=pl.BlockSpec((tile, cols), lambda i: (i, 0)),
                          out_shape=jax.ShapeDtypeStruct((rows, cols), F32), compiler_params=_params(("parallel",)))(buf)


def _adamw(w, g, m, v, tile, name, after=(), emit_g=False):
    rows, cols = w.shape

    def body(w_ref, g_ref, m_ref, v_ref, *rest):
        d_ref, mo_ref, vo_ref = rest[len(after):len(after) + 3]
        gv = g_ref[...]
        if emit_g:
            rest[-1][...] = gv
        mn = ADAM_B1 * m_ref[...] + (1.0 - ADAM_B1) * gv
        vn = ADAM_B2 * v_ref[...] + (1.0 - ADAM_B2) * (gv * gv)
        m_hat = mn / (1.0 - ADAM_B1 ** ADAM_STEP)
        v_hat = vn / (1.0 - ADAM_B2 ** ADAM_STEP)
        d_ref[...] = -ADAM_LR * (m_hat / (jnp.sqrt(v_hat) + ADAM_EPS) + ADAM_WD * w_ref[...])
        mo_ref[...] = mn
        vo_ref[...] = vn

    spec = pl.BlockSpec((tile, cols), lambda i: (i, 0))
    sds = jax.ShapeDtypeStruct((rows, cols), F32)
    n_out = 4 if emit_g else 3
    return pl.pallas_call(body, name=name, grid=(rows // tile,), in_specs=[spec] * 4 + [ANY] * len(after), out_specs=[spec] * n_out,
                          out_shape=[sds] * n_out, compiler_params=_params(("parallel",)))(w, g, m, v, *map(_hbm, after))


def _row_tile(rows, cap):
    best = 8
    for t in range(8, min(rows, cap) + 1, 8):
        if rows % t == 0:
            best = t
    return best


def _pack(arrs):
    parts = []
    for a in arrs:
        r = a.size // 128
        r8 = -(-r // 8) * 8
        parts.append(jnp.pad(a.reshape(r, 128).astype(F32), ((0, r8 - r), (0, 0))))
    return jnp.concatenate(parts, axis=0)


def _unpack(packed, shapes):
    out, r0 = [], 0
    for shp in shapes:
        r = math.prod(shp) // 128
        out.append(packed[r0:r0 + r].reshape(shp))
        r0 += -(-r // 8) * 8
    return out


BIG = {
    "ffn1_w_gu": "col", "ffn1_w_down": "row", "ev_w_in": "col", "ev_w_out": "row", "od_w_in": "row", "od_w_group": "row",
    "od_w_out": "row", "xa_w_q": "row", "xa_w_kv": "col", "xa_w_o": "row", "ffn2_w_gu": "col", "ffn2_w_down": "row",
}
TINY_SHARDED = {"ev_conv_w": 2, "od_scale": 1}
WEIGHTS = ['ffn1_pre_g', 'ffn1_w_gu', 'ffn1_w_down', 'ffn1_post_g', 'mix_pre_g', 'mix_post_g', 'ev_w_in', 'ev_conv_w',
           'ev_conv_b', 'ev_conv_ln_g', 'ev_conv_ln_b', 'ev_sgu_ln_g', 'ev_sgu_ln_b', 'ev_sgu_w', 'ev_sgu_b', 'ev_w_out',
           'od_w_in', 'od_w_group', 'od_scale', 'od_w_out', 'xa_pre_g', 'xa_mem_g', 'xa_w_q', 'xa_w_kv', 'xa_w_o', 'xa_post_g',
           'ffn2_pre_g', 'ffn2_w_gu', 'ffn2_w_down', 'ffn2_post_g']


def _as3d(a):
    return a.reshape((-1,) + a.shape[-2:]) if a.ndim == 4 else a


class _WeightView:
    def __init__(self, store, view):
        self.store, self.view = store, view

    def __getitem__(self, k):
        return self.view(k)


class _Grads:
    def __init__(self):
        self.buf = {}
        self.fresh = []

    def add(self, name, layer, nlayers, a, b, tm, tn, n_outer=True):
        self.buf[name] = _matmul(a, b, "tn", BF16, "dw_" + name, tm, tn, out_l=layer, out_stack=nlayers,
                                 out_buf=self.buf.get(name), n_outer=n_outer)
        self.fresh.append(name)

    def take(self):
        names, self.fresh = self.fresh, []
        return names


def _local_step(x, mem, target, W, S, tiles=None, before_sub=None, after_sub_bwd=None):
    T, TW, tm = tiles or (min(512, S), min(256, S), min(512, S))
    TF = T
    row = lambda v: v.reshape(1, -1)
    subs = []
    small = {k: [None] * W[k].shape[0] for k in WEIGHTS if k not in BIG}
    g = _Grads()

    def ffn_fwd(tag, l, h, n, g_next):
        gu, a = _up_swiglu(n, W[tag + "_w_gu"], l, TF)
        f, h2, n2 = _post_mm(h, a, W[tag + "_w_down"], l, row(W[tag + "_post_g"][l]), 0.5, g_next, T)
        subs.append(dict(kind="ffn", tag=tag, l=l, h=h, n=n, gu=gu, a=a, f=f, scale=0.5, pre=tag + "_pre_g", post=tag + "_post_g"))
        return h2, n2

    def ffn_bwd(s, df):
        tag, l = s["tag"], s["l"]
        g.add(tag + "_w_down", l, 4, s["a"], df, 256, D, n_outer=False)
        dgu = _down_dx_swiglu_bwd(df, s["gu"], W[tag + "_w_down"], l, TF)
        g.add(tag + "_w_gu", l, 4, s["n"], dgu, D, 512, n_outer=False)
        return dgu, tag + "_w_gu", l, TF

    def xa_fwd(l, h, n, g_next):
        q = _matmul(n, W["xa_w_q"], "nn", BF16, "xa_q", tm, D, b_l=l)
        mn = _mem_norm(mem, row(W["xa_mem_g"][l]))
        kv = _matmul(mn, W["xa_w_kv"], "nn", BF16, "xa_kv", NMEM, D, b_l=l)
        o = _attn(q, kv, T)
        cx, h2, n2 = _post_mm(h, o, W["xa_w_o"], l, row(W["xa_post_g"][l]), 1.0, g_next, T)
        subs.append(dict(kind="xa", l=l, h=h, n=n, q=q, mn=mn, kv=kv, o=o, f=cx, scale=1.0, pre="xa_pre_g", post="xa_post_g"))
        return h2, n2

    def xa_bwd(s, dc):
        l = s["l"]
        do = _matmul(dc, W["xa_w_o"], "nt", BF16, "xa_o_dx", tm, D, b_l=l)
        g.add("xa_w_o", l, 4, s["o"], dc, D, 512)
        dq, dkv = _attn_bwd(s["q"], s["kv"], do, T)
        g.add("xa_w_q", l, 4, s["n"], dq, D, 512)
        dkvb = dkv.astype(BF16)
        g.add("xa_w_kv", l, 4, s["mn"], dkvb, D, 512)
        dmn = _matmul(dkvb, W["xa_w_kv"], "nt", F32, "xa_kv_dx", NMEM, D, b_l=l)
        small["xa_mem_g"][l] = _mem_gain_bwd(mem, dmn)[0]
        return dq, "xa_w_q", l, T

    def even_params(e):
        return (W["ev_conv_w"][e], row(W["ev_conv_b"][e]), row(W["ev_conv_ln_g"][e]), row(W["ev_conv_ln_b"][e]),
                row(W["ev_sgu_ln_g"][e]), row(W["ev_sgu_ln_b"][e]), W["ev_sgu_w"][e])

    def even_fwd(l, h, n, g_next):
        e = l // 2
        p = _matmul(n, W["ev_w_in"], "nn", F32, "ev_in", tm, D, b_l=e)
        cw, cb, clg, clb, slg, slb, ws = even_params(e)
        y, z = _even_core(p, cw, cb, clg, clb, slg, slb, ws, W["ev_sgu_b"][e].T, TW)
        m, h2, n2 = _post_mm(h, y, W["ev_w_out"], e, row(W["mix_post_g"][l]), 1.0, g_next, T)
        subs.append(dict(kind="even", l=l, h=h, n=n, p=p, y=y, z=z, f=m, scale=1.0, pre="mix_pre_g", post="mix_post_g"))
        return h2, n2

    def even_bwd(s, dm):
        l = s["l"]
        e = l // 2
        dy = _matmul(dm, W["ev_w_out"], "nt", F32, "ev_out_dx", tm, D, b_l=e)
        g.add("ev_w_out", e, 2, s["y"], dm, D, 512)
        cw, cb, clg, clb, slg, slb, ws = even_params(e)
        dconv, dpb, vecs, dws, dbst = _even_bwd_a(s["p"], s["z"], dy, cw, cb, clg, clb, slg, slb, ws,
                                                   jnp.swapaxes(ws, 1, 2), W["ev_sgu_b"][e].T, TW)
        dp, dcw = _even_bwd_b(s["p"], s["z"], dconv, dpb, cw, TW)
        for r, name in enumerate(["ev_conv_b", "ev_conv_ln_g", "ev_conv_ln_b", "ev_sgu_ln_g", "ev_sgu_ln_b"]):
            small[name][e] = vecs[r]
        small["ev_sgu_w"][e] = dws
        small["ev_sgu_b"][e] = dbst.T
        small["ev_conv_w"][e] = dcw[:CONV_W]
        g.add("ev_w_in", e, 2, s["n"], dp, D, 512)
        return dp, "ev_w_in", e, T

    def group_w(o):
        ng = len(POOL_WINDOWS)
        return (W["od_w_group"].reshape(-1, POOL_GD, POOL_GD), (ng, POOL_GD, POOL_GD), lambda i: (o, 0, 0))

    def odd_fwd(l, h, n, g_next):
        o = l // 2
        p = _matmul(n, W["od_w_in"], "nn", F32, "od_in", tm, D, b_l=o)
        d, e, es = _odd_core(p, group_w(o), row(W["od_scale"][o]), T)
        m, h2, n2 = _post_mm(h, es, W["od_w_out"], o, row(W["mix_post_g"][l]), 1.0, g_next, T)
        subs.append(dict(kind="odd", l=l, h=h, n=n, d=d, e=e, es=es, f=m, scale=1.0, pre="mix_pre_g", post="mix_post_g"))
        return h2, n2

    def odd_bwd(s, dm):
        l = s["l"]
        o = l // 2
        des = _matmul(dm, W["od_w_out"], "nt", F32, "od_out_dx", tm, D, b_l=o)
        g.add("od_w_out", o, 2, s["es"], dm, D, 512)
        dp, dsc, dwg = _odd_bwd(des, s["e"], s["d"], group_w(o), row(W["od_scale"][o]), T)
        small["od_scale"][o] = dsc[0]
        small["od_w_group_full"][o] = dwg
        g.add("od_w_in", o, 2, s["n"], dp, D, 512)
        return dp, "od_w_in", o, T

    small["od_w_group_full"] = [None, None]
    order = []
    for l in range(4):
        order += [("ffn1", l), ("even" if l % 2 == 0 else "odd", l), ("xa", l), ("ffn2", l)]
    pre_of = {"ffn1": "ffn1_pre_g", "even": "mix_pre_g", "odd": "mix_pre_g", "xa": "xa_pre_g", "ffn2": "ffn2_pre_g"}
    h = x
    n = _prenorm(h, row(W["ffn1_pre_g"][0]), T)
    for idx, (kind, l) in enumerate(order):
        if before_sub is not None:
            before_sub(kind, l, h)
        g_next = row(W[pre_of[order[idx + 1][0]]][order[idx + 1][1]]) if idx + 1 < len(order) else None
        if kind in ("ffn1", "ffn2"):
            h, n = ffn_fwd(kind, l, h, n, g_next)
        elif kind == "xa":
            h, n = xa_fwd(l, h, n, g_next)
        elif kind == "even":
            h, n = even_fwd(l, h, n, g_next)
        else:
            h, n = odd_fwd(l, h, n, g_next)

    top = subs[-1]
    dh, df, loss_acc, dgp = _loss_top(h, target, top["f"], row(W[top["post"]][top["l"]]), top["scale"], T)
    small[top["post"]][top["l"]] = dgp[0]
    for idx in range(len(subs) - 1, -1, -1):
        s = subs[idx]
        da, wname, wl, bt = {"ffn": ffn_bwd, "xa": xa_bwd, "even": even_bwd, "odd": odd_bwd}[s["kind"]](s, df)
        names = g.take()
        deps = [g.buf[k] for k in names]
        if idx > 0:
            sp = subs[idx - 1]
            dh, df, dg_pre, dg_post = _boundary_mm(da, W[wname], wl, s["h"], row(W[s["pre"]][s["l"]]), dh, sp["f"],
                                                   row(W[sp["post"]][sp["l"]]), sp["scale"], bt, deps=deps)
            small[sp["post"]][sp["l"]] = dg_post[0]
        else:
            dh, dg_pre = _boundary_mm(da, W[wname], wl, s["h"], row(W[s["pre"]][s["l"]]), dh, None, None, None, bt, deps=deps)
        small[s["pre"]][s["l"]] = dg_pre[0]
        if after_sub_bwd is not None:
            after_sub_bwd(order[idx][0], s["l"], g.buf, small)
            g.fresh = names + g.fresh
    return loss_acc[0, 0], dh, g.buf, small


def _step(P, S):
    x, mem, target = P["x"][0], P["mem"][0], P["loss_target"][0]
    chip = 2 * lax.axis_index("x") + lax.axis_index("y")
    pos = jnp.stack([chip, lax.axis_index("c")]).astype(jnp.int32)
    order = list(BIG)
    tix = {k: t for t, k in enumerate(order)}
    kinds = [BIG[k] for k in order]
    per = {k: (4 if k == "od_w_group" else 1) for k in order}
    members = {"ffn1": ["ffn1_w_gu", "ffn1_w_down"], "ffn2": ["ffn2_w_gu", "ffn2_w_down"], "xa": ["xa_w_q", "xa_w_kv", "xa_w_o"],
               "even": ["ev_w_in", "ev_w_out"], "odd": ["od_w_in", "od_w_group", "od_w_out"]}
    index_of = lambda kind, l: l // 2 if kind in ("even", "odd") else l
    sub_order = []
    for l in range(4):
        sub_order += [("ffn1", l), ("even" if l % 2 == 0 else "odd", l), ("xa", l), ("ffn2", l)]
    groups = {(kind, l): [(tix[k], index_of(kind, l), index_of(kind, l) * per[k], per[k]) for k in members[kind]]
              for kind, l in sub_order}
    tiny = _gather_tiny([P[k] for k in TINY_SHARDED], list(TINY_SHARDED.values()))
    slots = [_cast_into_slot(pos, _as3d(P[k]), BIG[k], "cast_" + k) for k in order]
    g_send, g_recv, bufs = _gather_start(slots, kinds, [groups[s_] for s_ in sub_order], tiny)
    W = {k: P[k] for k in WEIGHTS if k not in BIG}
    W.update(zip(TINY_SHARDED, tiny))
    W.update(zip(order, bufs))

    def view(k):
        return W[k].reshape(2, 4, POOL_GD, POOL_GD) if k == "od_w_group" else W[k]

    Wv = _WeightView(W, view)

    forwarded = {}

    def forward(sub, after):
        ks = members[sub[0]]
        s2, r2, thru = _gather_forward([W[k] for k in ks], [BIG[k] for k in ks], groups[sub], g_send, g_recv, after,
                                       "gather_forward_%s%d" % sub)
        W.update(zip(ks, thru))
        forwarded[sub] = (s2, r2)

    def before_sub(kind, l, h):
        sub = (kind, l)
        if sub == sub_order[0]:
            forward(sub, (h,))
        ks = members[kind]
        s2, r2 = forwarded.pop(sub)
        W.update(zip(ks, _gather_wait([W[k] for k in ks], [BIG[k] for k in ks], groups[sub], s2, r2, (h,),
                                      "gather_wait_%s%d" % sub)))
        nxt = sub_order.index(sub) + 1
        if nxt < len(sub_order):
            forward(sub_order[nxt], (h, W[ks[0]]))

    recv = {k: None for k in order}
    pending = []
    small_names = [k for k in WEIGHTS if k not in BIG]
    gsmall = {}

    def after_sub_bwd(kind, l, big, small):
        ks = [k for k in members[kind] if k != "od_w_group"]
        items = [(index_of(kind, l), 1)] * len(ks)
        after = ()
        if (kind, l) == ("odd", 1):
            big["od_w_group"] = jnp.stack(small.pop("od_w_group_full")).astype(BF16).reshape(8, POOL_GD, POOL_GD)
            ks, items = ks + ["od_w_group"], items + [(0, 8)]
        if (kind, l) == ("ffn1", 0):
            small_full = [jnp.stack(small[k]) for k in small_names]
            packed = _pack(small_full)
            summed = _sum_slots(_share_small(packed), _row_tile(packed.shape[0], 512), "sum_small")
            gsmall.update(zip(small_names, _unpack(summed, [a.shape for a in small_full])))
            after = (summed,)
        for k in ks:
            if recv[k] is None:
                recv[k] = lax.empty((7,) + _half_geometry(big[k].shape, BIG[k]), BF16)
        s_send, s_recv, g_thru, r_thru = _scatter_start([big[k] for k in ks], [recv[k] for k in ks], [BIG[k] for k in ks],
                                                        items, "scatter_start_%s%d" % (kind, l), after)
        big.update(zip(ks, g_thru))
        recv.update(zip(ks, r_thru))
        pending.append((kind, l, ks, items, s_send, s_recv))

    loss, grad_x, big, small = _local_step(x, mem, target, Wv, S, before_sub=before_sub, after_sub_bwd=after_sub_bwd)
    last = (big[members["ffn1"][0]],)
    for kind, l, ks, items, s_send, s_recv in pending:
        recv.update(zip(ks, _scatter_wait([recv[k] for k in ks], items, s_send, s_recv, "scatter_wait_%s%d" % (kind, l), last)))
    gsh = dict(zip(order, _swap_halves([_sum_into_half(pos, recv[k], big[k], BIG[k], "sum_" + k) for k in order])))
    for k, ax in TINY_SHARDED.items():
        size = P[k].shape[ax]
        gsmall[k] = lax.dynamic_slice_in_dim(gsmall[k], chip * size, size, axis=ax)

    grads, delta, new_m, new_v = {}, {}, {}, {}
    for k in order:
        shp = P[k].shape
        cols = shp[-1]
        flat = lambda a: a.reshape(-1, cols)
        gk = flat(gsh[k])
        d_, m_, v_, g_ = _adamw(flat(P[k]), gk, flat(P["m_" + k]), flat(P["v_" + k]), _row_tile(gk.shape[0], 256), "adamw_" + k,
                                emit_g=True)
        grads[k], delta[k], new_m[k], new_v[k] = g_.reshape(shp), d_.reshape(shp), m_.reshape(shp), v_.reshape(shp)
    pk = lambda pre: _pack([P[pre + k] for k in small_names])
    d_, m_, v_ = _adamw(pk(""), _pack([gsmall[k] for k in small_names]), pk("m_"), pk("v_"),
                        pk("").shape[0], "adamw_small", last)
    shapes = [P[k].shape for k in small_names]
    for dst, src in ((delta, d_), (new_m, m_), (new_v, v_)):
        dst.update(zip(small_names, _unpack(src, shapes)))
    grads.update(gsmall)

    loss = lax.psum(loss, ("x", "y", "c"))
    out = [loss, grad_x[None]]
    for grp in (grads, delta, new_m, new_v):
        out += [grp[k] for k in WEIGHTS]
    return tuple(out)


def kernel(x, mem, ffn1_pre_g, ffn1_w_gu, ffn1_w_down, ffn1_post_g, mix_pre_g, mix_post_g, ev_w_in, ev_conv_w, ev_conv_b, ev_conv_ln_g, ev_conv_ln_b, ev_sgu_ln_g, ev_sgu_ln_b, ev_sgu_w, ev_sgu_b, ev_w_out, od_w_in, od_w_group, od_scale, od_w_out, xa_pre_g, xa_mem_g, xa_w_q, xa_w_kv, xa_w_o, xa_post_g, ffn2_pre_g, ffn2_w_gu, ffn2_w_down, ffn2_post_g, loss_target, m_ffn1_pre_g, m_ffn1_w_gu, m_ffn1_w_down, m_ffn1_post_g, m_mix_pre_g, m_mix_post_g, m_ev_w_in, m_ev_conv_w, m_ev_conv_b, m_ev_conv_ln_g, m_ev_conv_ln_b, m_ev_sgu_ln_g, m_ev_sgu_ln_b, m_ev_sgu_w, m_ev_sgu_b, m_ev_w_out, m_od_w_in, m_od_w_group, m_od_scale, m_od_w_out, m_xa_pre_g, m_xa_mem_g, m_xa_w_q, m_xa_w_kv, m_xa_w_o, m_xa_post_g, m_ffn2_pre_g, m_ffn2_w_gu, m_ffn2_w_down, m_ffn2_post_g, v_ffn1_pre_g, v_ffn1_w_gu, v_ffn1_w_down, v_ffn1_post_g, v_mix_pre_g, v_mix_post_g, v_ev_w_in, v_ev_conv_w, v_ev_conv_b, v_ev_conv_ln_g, v_ev_conv_ln_b, v_ev_sgu_ln_g, v_ev_sgu_ln_b, v_ev_sgu_w, v_ev_sgu_b, v_ev_w_out, v_od_w_in, v_od_w_group, v_od_scale, v_od_w_out, v_xa_pre_g, v_xa_mem_g, v_xa_w_q, v_xa_w_kv, v_xa_w_o, v_xa_post_g, v_ffn2_pre_g, v_ffn2_w_gu, v_ffn2_w_down, v_ffn2_post_g):
    P = dict(locals())
    return _step(P, x.shape[1])
```

```python
import functools
import math

import jax
import jax.numpy as jnp
from jax import lax
from jax.experimental import pallas as pl
from jax.experimental.pallas import tpu as pltpu

F32 = jnp.float32
BF16 = jnp.bfloat16
MESH = pl.DeviceIdType.MESH
ANY = pl.BlockSpec(memory_space=pl.ANY)
HBM = pl.BlockSpec(memory_space=pltpu.HBM)


def _hbm(x):
    return pltpu.with_memory_space_constraint(x, pltpu.HBM)

D = 1024
DFF = 2816
NMEM = 256
DC = 512
CONV_W = 31
CHUNK = 128
NHEAD_SGU = 4
POOL_WINDOWS = (2, 4, 8, 16)
POOL_GD = 256
XA_HEADS = 4
XA_HD = 256
EPS = 1e-6
NCHIP = 4
HALO = 32

ADAM_LR, ADAM_B1, ADAM_B2, ADAM_EPS, ADAM_WD, ADAM_STEP = 0.001, 0.9, 0.999, 1e-08, 0.01, 10

V7X_VMEM_BYTES = 64 * 1024 * 1024
VMEM_LIMIT = V7X_VMEM_BYTES - 8 * 1024 * 1024


def _params(sem):
    return pltpu.CompilerParams(dimension_semantics=sem, vmem_limit_bytes=VMEM_LIMIT)


def _matmul(a, b, kind, out_dtype, name, tm, tn, a_l=None, b_l=None, out_l=None, out_stack=None, out_buf=None,
            n_outer=True):
    a2, b2 = a.shape[-2:], b.shape[-2:]
    if kind == "nn":
        (m, k), (k2, n) = a2, b2
        dims = (((1,), (0,)), ((), ()))
    elif kind == "nt":
        (m, k), (n, k2) = a2, b2
        dims = (((1,), (1,)), ((), ()))
    else:
        (k, m), (k2, n) = a2, b2
        dims = (((0,), (0,)), ((), ()))
    assert k == k2 and m % tm == 0 and n % tn == 0, (name, a.shape, b.shape, tm, tn)
    if n_outer:
        grid = (n // tn, m // tm)
        ij = lambda p, q: (q, p)
    else:
        grid = (m // tm, n // tn)
        ij = lambda p, q: (p, q)

    def spec(arr, layer, blk, idx):
        if arr.ndim == 3:
            return pl.BlockSpec((None,) + blk, lambda p, q: (layer,) + idx(*ij(p, q)))
        return pl.BlockSpec(blk, lambda p, q: idx(*ij(p, q)))

    a_spec = spec(a, a_l, (k, tm) if kind == "tn" else (tm, k), (lambda i, j: (0, i)) if kind == "tn" else (lambda i, j: (i, 0)))
    b_spec = spec(b, b_l, (tn, k) if kind == "nt" else (k, tn), (lambda i, j: (j, 0)) if kind == "nt" else (lambda i, j: (0, j)))
    if out_stack is None:
        out_shape = jax.ShapeDtypeStruct((m, n), out_dtype)
        o_spec = pl.BlockSpec((tm, tn), lambda p, q: ij(p, q))
    else:
        out_shape = jax.ShapeDtypeStruct((out_stack, m, n), out_dtype)
        o_spec = pl.BlockSpec((None, tm, tn), lambda p, q: (out_l,) + ij(p, q))

    def body(a_ref, b_ref, *rest):
        o_ref = rest[-1]
        o_ref[...] = lax.dot_general(a_ref[...], b_ref[...], dims, preferred_element_type=F32).astype(o_ref.dtype)

    in_specs, args, aliases = [a_spec, b_spec], [a, _hbm(b) if b.ndim == 3 else b], {}
    if out_buf is not None:
        in_specs.append(ANY)
        args.append(_hbm(out_buf))
        aliases = {2: 0}
    return pl.pallas_call(body, name=name, grid=grid, in_specs=in_specs, out_specs=o_spec, out_shape=out_shape,
                          input_output_aliases=aliases, compiler_params=_params(("parallel", "parallel")))(*args)


def _rowwise(fn, name, rows, tile, row_ins, full_ins, row_outs, acc_outs=(), scratch=(), deps=()):
    assert rows % tile == 0, (name, rows, tile)
    in_specs, args = [], []
    for r in row_ins:
        if isinstance(r, tuple):
            arr, br, bc, imap = r
            in_specs.append(pl.BlockSpec((br, bc), imap))
        else:
            arr = r
            in_specs.append(pl.BlockSpec((tile, arr.shape[1]), lambda i: (i, 0)))
        args.append(arr)
    for f in full_ins:
        if isinstance(f, tuple):
            arr, blk, imap = f
            in_specs.append(pl.BlockSpec(blk, imap, pipeline_mode=pl.Buffered(1)))
            args.append(_hbm(arr))
        else:
            in_specs.append(pl.BlockSpec(f.shape, functools.partial(lambda nd, i: (0,) * nd, f.ndim)))
            args.append(f)
    for d in deps:
        in_specs.append(ANY)
        args.append(_hbm(d))
    out_specs, out_shape = [], []
    for w, dt in row_outs:
        out_specs.append(pl.BlockSpec((tile, w), lambda i: (i, 0)))
        out_shape.append(jax.ShapeDtypeStruct((rows, w), dt))
    for shp, dt in acc_outs:
        out_specs.append(pl.BlockSpec(shp, functools.partial(lambda nd, i: (0,) * nd, len(shp))))
        out_shape.append(jax.ShapeDtypeStruct(shp, dt))
    n_in, n_out = len(args), len(out_shape)

    def body(*refs):
        fn(pl.program_id(0), refs[:n_in], refs[n_in:n_in + n_out], refs[n_in + n_out:])

    sem = ("arbitrary",) if acc_outs else ("parallel",)
    res = pl.pallas_call(body, name=name, grid=(rows // tile,), in_specs=in_specs, out_specs=out_specs, out_shape=out_shape,
                         scratch_shapes=list(scratch), compiler_params=_params(sem))(*args)
    return res


def _accum(ref, i, val):
    @pl.when(i == 0)
    def _():
        ref[...] = val

    @pl.when(i > 0)
    def _():
        ref[...] += val


def _prev_halo(arr, tile, cols=None):
    r = tile // HALO
    return (arr, HALO, cols or arr.shape[1], lambda i: (jnp.maximum(i * r - 1, 0), 0))


def _next_halo(arr, tile, cols=None):
    r = tile // HALO
    last = arr.shape[0] // HALO - 1
    return (arr, HALO, cols or arr.shape[1], lambda i: (jnp.minimum((i + 1) * r, last), 0))


def _sigmoid(x):
    return 1.0 / (1.0 + jnp.exp(-x))


def _rms_hat(x):
    r = lax.rsqrt(jnp.mean(x * x, axis=-1, keepdims=True) + EPS)
    return x * r, r


def _rms_bwd(x, g, dy):
    xhat, r = _rms_hat(x)
    dxhat = dy * g
    dx = r * (dxhat - xhat * jnp.mean(dxhat * xhat, axis=-1, keepdims=True))
    return dx, jnp.sum(dy * xhat, axis=0, keepdims=True)


def _ln_hat(x):
    mu = jnp.mean(x, axis=-1, keepdims=True)
    xc = x - mu
    r = lax.rsqrt(jnp.mean(xc * xc, axis=-1, keepdims=True) + EPS)
    return xc * r, r


def _ln_bwd(xhat, r, g, dy):
    dxhat = dy * g
    dx = r * (dxhat - jnp.mean(dxhat, axis=-1, keepdims=True) - xhat * jnp.mean(dxhat * xhat, axis=-1, keepdims=True))
    return dx, jnp.sum(dy * xhat, axis=0, keepdims=True), jnp.sum(dy, axis=0, keepdims=True)


def _silu_grad(x):
    s = _sigmoid(x)
    return s * (1.0 + x * (1.0 - s))


_SQRT_HALF = math.sqrt(0.5)
_INV_SQRT_2PI = 1.0 / math.sqrt(2.0 * math.pi)


def _gelu(x):
    return 0.5 * x * (1.0 + lax.erf(x * _SQRT_HALF))


def _gelu_grad(x):
    return 0.5 * (1.0 + lax.erf(x * _SQRT_HALF)) + x * jnp.exp(-0.5 * x * x) * _INV_SQRT_2PI


def _dot(a, b, kind="nn"):
    dims = {"nn": (((1,), (0,)), ((), ())), "nt": (((1,), (1,)), ((), ())), "tn": (((0,), (0,)), ((), ()))}[kind]
    return lax.dot_general(a, b, dims, preferred_element_type=F32)


def _prenorm(h, g, tile):
    def fn(i, ins, outs, _):
        h_ref, g_ref = ins
        outs[0][...] = (_rms_hat(h_ref[...])[0] * g_ref[...]).astype(BF16)

    return _rowwise(fn, "prenorm", h.shape[0], tile, [h], [g], [(D, BF16)])[0]


def _loss_top(h, target, f, g_post, scale, tile):
    def fn(i, ins, outs, _):
        err = ins[0][...] - ins[1][...]
        per_row = jnp.mean(err * err, axis=-1, keepdims=True)
        _accum(outs[2], i, jnp.broadcast_to(0.5 * jnp.sum(per_row, axis=0, keepdims=True), (1, 128)))
        dh = err * (1.0 / D)
        outs[0][...] = dh
        df, dg = _rms_bwd(ins[2][...], ins[3][...], scale * dh)
        outs[1][...] = df.astype(BF16)
        _accum(outs[3], i, dg)

    return _rowwise(fn, "loss_top", h.shape[0], tile, [h, target, f], [g_post], [(D, F32), (D, BF16)],
                    [((1, 128), F32), ((1, D), F32)])


ROW_BLOCK = 256


def _row_blocks(tile):
    return [pl.ds(r0, min(ROW_BLOCK, tile)) for r0 in range(0, tile, ROW_BLOCK)]


def _layer_of(w3, l):
    return (w3, (None,) + tuple(w3.shape[1:]), lambda i: (l, 0, 0))


def _post_mm(h, a, w3, l, g_post, scale, g_next, tile):
    def fn(i, ins, outs, _):
        for r in _row_blocks(tile):
            f = _dot(ins[1][r, :], ins[2][...])
            outs[0][r, :] = f
            hn = ins[0][r, :] + scale * (_rms_hat(f)[0] * ins[3][...])
            outs[1][r, :] = hn
            if g_next is not None:
                outs[2][r, :] = (_rms_hat(hn)[0] * ins[4][...]).astype(BF16)

    fulls = [_layer_of(w3, l), g_post] + ([g_next] if g_next is not None else [])
    outs = [(D, F32), (D, F32)] + ([(D, BF16)] if g_next is not None else [])
    res = _rowwise(fn, "post_mm", h.shape[0], tile, [h, a], fulls, outs)
    return res[0], res[1], (res[2] if g_next is not None else None)


def _boundary_mm(a, w3, l, h, g_pre, dh_in, f_prev, g_post_prev, scale_prev, tile, deps=()):
    has_prev = f_prev is not None

    def fn(i, ins, outs, _):
        a_ref, h_ref, dhin_ref = ins[:3]
        w_ref, g_ref = ins[3 + has_prev], ins[4 + has_prev]
        dg_sum, dgp_sum = None, None
        for r in _row_blocks(tile):
            dn = _dot(a_ref[r, :], w_ref[...], "nt")
            dx, dg = _rms_bwd(h_ref[r, :], g_ref[...], dn)
            dh = dhin_ref[r, :] + dx
            outs[0][r, :] = dh
            dg_sum = dg if dg_sum is None else dg_sum + dg
            if has_prev:
                df, dgp = _rms_bwd(ins[3][r, :], ins[6][...], scale_prev * dh)
                outs[1][r, :] = df.astype(BF16)
                dgp_sum = dgp if dgp_sum is None else dgp_sum + dgp
        _accum(outs[1 + has_prev], i, dg_sum)
        if has_prev:
            _accum(outs[3], i, dgp_sum)

    rows = [a, h, dh_in] + ([f_prev] if has_prev else [])
    fulls = [_layer_of(w3, l), g_pre] + ([g_post_prev] if has_prev else [])
    outs = [(D, F32)] + ([(D, BF16)] if has_prev else [])
    accs = [((1, D), F32)] + ([((1, D), F32)] if has_prev else [])
    return _rowwise(fn, "boundary_mm", h.shape[0], tile, rows, fulls, outs, accs, deps=deps)


FF_CHUNK = DFF // 2


def _up_swiglu(n, w3, l, tile):
    def fn(i, ins, outs, _):
        n_ref, w_ref = ins
        nv = n_ref[...]
        for c0 in range(0, DFF, FF_CHUNK):
            g = _dot(nv, w_ref[:, c0:c0 + FF_CHUNK])
            u = _dot(nv, w_ref[:, DFF + c0:DFF + c0 + FF_CHUNK])
            outs[0][:, c0:c0 + FF_CHUNK] = g.astype(BF16)
            outs[0][:, DFF + c0:DFF + c0 + FF_CHUNK] = u.astype(BF16)
            outs[1][:, c0:c0 + FF_CHUNK] = (g * _sigmoid(g) * u).astype(BF16)

    return _rowwise(fn, "up_swiglu", n.shape[0], tile, [n], [_layer_of(w3, l)], [(2 * DFF, BF16), (DFF, BF16)])


def _down_dx_swiglu_bwd(df, gu, w3, l, tile):
    def fn(i, ins, outs, _):
        df_ref, gu_ref, w_ref = ins
        dfv = df_ref[...]
        for c0 in range(0, DFF, FF_CHUNK):
            da = _dot(dfv, w_ref[c0:c0 + FF_CHUNK, :], "nt")
            g = gu_ref[:, c0:c0 + FF_CHUNK].astype(F32)
            u = gu_ref[:, DFF + c0:DFF + c0 + FF_CHUNK].astype(F32)
            s = _sigmoid(g)
            das = da * s
            dup = das * g
            outs[0][:, c0:c0 + FF_CHUNK] = (u * (das + dup * (1.0 - s))).astype(BF16)
            outs[0][:, DFF + c0:DFF + c0 + FF_CHUNK] = dup.astype(BF16)

    return _rowwise(fn, "down_dx_swiglu_bwd", df.shape[0], tile, [df, gu], [_layer_of(w3, l)], [(2 * DFF, BF16)])[0]


def _softmax_rows(s):
    e = jnp.exp(s - jnp.max(s, axis=-1, keepdims=True))
    return e / jnp.sum(e, axis=-1, keepdims=True)


def _attn(q, kv, tile):
    def fn(i, ins, outs, _):
        q_ref, kv_ref = ins
        for hd in range(XA_HEADS):
            c0 = hd * XA_HD
            p = _softmax_rows(_dot(q_ref[:, c0:c0 + XA_HD], kv_ref[:, c0:c0 + XA_HD], "nt") * (XA_HD ** -0.5))
            outs[0][:, c0:c0 + XA_HD] = _dot(p.astype(BF16), kv_ref[:, D + c0:D + c0 + XA_HD]).astype(BF16)

    return _rowwise(fn, "attn", q.shape[0], tile, [q], [kv], [(D, BF16)])[0]


def _attn_bwd(q, kv, do, tile):
    def fn(i, ins, outs, _):
        q_ref, do_ref, kv_ref = ins
        for hd in range(XA_HEADS):
            c0 = hd * XA_HD
            qh, kh, vh = q_ref[:, c0:c0 + XA_HD], kv_ref[:, c0:c0 + XA_HD], kv_ref[:, D + c0:D + c0 + XA_HD]
            doh = do_ref[:, c0:c0 + XA_HD]
            p = _softmax_rows(_dot(qh, kh, "nt") * (XA_HD ** -0.5))
            dp = _dot(doh, vh, "nt")
            ds = (p * (dp - jnp.sum(dp * p, axis=-1, keepdims=True)) * (XA_HD ** -0.5)).astype(BF16)
            outs[0][:, c0:c0 + XA_HD] = _dot(ds, kh).astype(BF16)
            dk = _dot(ds, qh, "tn")
            dv = _dot(p.astype(BF16), doh, "tn")

            @pl.when(i == 0)
            def _():
                outs[1][:, c0:c0 + XA_HD] = dk
                outs[1][:, D + c0:D + c0 + XA_HD] = dv

            @pl.when(i > 0)
            def _():
                outs[1][:, c0:c0 + XA_HD] += dk
                outs[1][:, D + c0:D + c0 + XA_HD] += dv

    return _rowwise(fn, "attn_bwd", q.shape[0], tile, [q, do], [kv], [(D, BF16)], [((NMEM, 2 * D), F32)])


def _mem_norm(mem, g):
    def fn(i, ins, outs, _):
        outs[0][...] = (_rms_hat(ins[0][...])[0] * ins[1][...]).astype(BF16)

    return _rowwise(fn, "mem_norm", NMEM, NMEM, [mem], [g], [(D, BF16)])[0]


def _mem_gain_bwd(mem, dmn):
    def fn(i, ins, outs, _):
        outs[0][...] = jnp.sum(ins[1][...] * _rms_hat(ins[0][...])[0], axis=0, keepdims=True)

    return _rowwise(fn, "mem_gain_bwd", NMEM, NMEM, [mem, dmn], [], [], [((1, D), F32)])[0]


SHIFT_ROWS = 8


def _shifted_scratch(tile):
    return pltpu.VMEM((SHIFT_ROWS - 1, tile + HALO - SHIFT_ROWS, DC), F32)


def _fill_shifted(zs_ref, zext_ref):
    rows = zs_ref.shape[1]
    for b in range(1, SHIFT_ROWS):
        zs_ref[b - 1] = zext_ref[pl.ds(b, rows), :]


def _rows_at(zext_ref, zs_ref, offset, tile):
    a, b = divmod(offset, SHIFT_ROWS)
    if b == 0:
        return zext_ref[pl.ds(offset, tile), :]
    return zs_ref[b - 1, pl.ds(SHIFT_ROWS * a, tile), :]


def _conv_taps(w_ref, zext_ref, zs_ref, tile, shift0):
    acc = None
    for k in range(CONV_W):
        term = w_ref[k:k + 1, :] * _rows_at(zext_ref, zs_ref, shift0(k), tile)
        acc = term if acc is None else acc + term
    return acc


def _tril_mask(transpose=False):
    r, c = (lax.broadcasted_iota(jnp.int32, (CHUNK, CHUNK), a) for a in (0, 1))
    return (r <= c) if transpose else (r >= c)


def _even_core(p, cw, cb, clg, clb, slg, slb, ws, bst, tile):
    def fn(i, ins, outs, scr):
        p_ref, ph_ref, cw_ref, cb_ref, clg_ref, clb_ref, slg_ref, slb_ref, ws_ref, bst_ref = ins
        y_ref, z_ref = outs
        zext = scr[0]
        z = p_ref[:, :DC] * _sigmoid(p_ref[:, DC:2 * DC])
        zh = ph_ref[:, :DC] * _sigmoid(ph_ref[:, DC:2 * DC])
        zext[:HALO, :] = jnp.where(i > 0, zh, 0.0)
        zext[HALO:, :] = z
        z_ref[...] = z
        _fill_shifted(scr[1], zext)
        conv = _conv_taps(cw_ref, zext, scr[1], tile, lambda k: HALO - (CONV_W - 1) + k) + cb_ref[...]
        yl = _ln_hat(conv)[0] * clg_ref[...] + clb_ref[...]
        y_ref[:, :DC] = (yl * _sigmoid(yl)).astype(BF16)
        zb = _gelu(p_ref[:, 2 * DC:])
        vln = (_ln_hat(zb[:, DC:])[0] * slg_ref[...] + slb_ref[...]).astype(BF16)
        mask = _tril_mask()
        for hd in range(NHEAD_SGU):
            wm = jnp.where(mask, ws_ref[hd], 0.0).astype(BF16)
            for ci in range(tile // CHUNK):
                r0, c0 = ci * CHUNK, hd * CHUNK
                mixed = _dot(wm, vln[r0:r0 + CHUNK, c0:c0 + CHUNK]) + bst_ref[:, hd:hd + 1]
                y_ref[r0:r0 + CHUNK, DC + c0:DC + c0 + CHUNK] = (zb[r0:r0 + CHUNK, c0:c0 + CHUNK] * mixed).astype(BF16)

    return _rowwise(fn, "even_core", p.shape[0], tile, [p, _prev_halo(p, tile, D)], [cw, cb, clg, clb, slg, slb, ws, bst],
                    [(D, BF16), (DC, F32)], scratch=[pltpu.VMEM((tile + HALO, DC), F32), _shifted_scratch(tile)])


def _even_bwd_a(p, z, dy, cw, cb, clg, clb, slg, slb, ws, wst, bst, tile):
    def fn(i, ins, outs, scr):
        p_ref, z_ref, zh_ref, dy_ref, cw_ref, cb_ref, clg_ref, clb_ref, slg_ref, slb_ref, ws_ref, wst_ref, bst_ref = ins
        dc_ref, dpb_ref, vec_ref, dws_ref, dbst_ref = outs
        zext, dvln_s, zs = scr
        zext[:HALO, :] = jnp.where(i > 0, zh_ref[...], 0.0)
        zext[HALO:, :] = z_ref[...]
        _fill_shifted(zs, zext)
        conv = _conv_taps(cw_ref, zext, zs, tile, lambda k: HALO - (CONV_W - 1) + k) + cb_ref[...]
        chat, cr = _ln_hat(conv)
        yl = chat * clg_ref[...] + clb_ref[...]
        dyl = dy_ref[:, :DC] * _silu_grad(yl)
        dconv, dclg, dclb = _ln_bwd(chat, cr, clg_ref[...], dyl)
        dc_ref[...] = dconv
        xb = p_ref[:, 2 * DC:]
        zb = _gelu(xb)
        vhat, vr = _ln_hat(zb[:, DC:])
        vln = (vhat * slg_ref[...] + slb_ref[...]).astype(BF16)
        dyb = dy_ref[:, DC:]
        mask, mask_t = _tril_mask(), _tril_mask(transpose=True)
        dbu_parts = []
        for hd in range(NHEAD_SGU):
            wm = jnp.where(mask, ws_ref[hd], 0.0).astype(BF16)
            wt = jnp.where(mask_t, wst_ref[hd], 0.0).astype(BF16)
            dws_h, dbs_h, rows = None, None, []
            for ci in range(tile // CHUNK):
                r0, c0 = ci * CHUNK, hd * CHUNK
                vblk = vln[r0:r0 + CHUNK, c0:c0 + CHUNK]
                mixed = _dot(wm, vblk) + bst_ref[:, hd:hd + 1]
                dyb_blk = dyb[r0:r0 + CHUNK, c0:c0 + CHUNK]
                rows.append(dyb_blk * mixed)
                dmixed = dyb_blk * zb[r0:r0 + CHUNK, c0:c0 + CHUNK]
                dmb = dmixed.astype(BF16)
                dvln_s[r0:r0 + CHUNK, c0:c0 + CHUNK] = _dot(wt, dmb)
                dw = _dot(dmb, vblk, "nt")
                db = jnp.sum(dmixed, axis=-1, keepdims=True)
                dws_h = dw if dws_h is None else dws_h + dw
                dbs_h = db if dbs_h is None else dbs_h + db
            dbu_parts.append(jnp.concatenate(rows, axis=0))
            dws_h = jnp.where(mask, dws_h, 0.0)

            @pl.when(i == 0)
            def _():
                dws_ref[hd] = dws_h
                dbst_ref[:, hd:hd + 1] = dbs_h

            @pl.when(i > 0)
            def _():
                dws_ref[hd] += dws_h
                dbst_ref[:, hd:hd + 1] += dbs_h

        dbu = jnp.concatenate(dbu_parts, axis=1)
        dbv, dslg, dslb = _ln_bwd(vhat, vr, slg_ref[...], dvln_s[...])
        dpb_ref[:, :DC] = (dbu * _gelu_grad(xb[:, :DC])).astype(BF16)
        dpb_ref[:, DC:] = (dbv * _gelu_grad(xb[:, DC:])).astype(BF16)
        @pl.when(i == 0)
        def _():
            vec_ref[...] = jnp.zeros_like(vec_ref)

        for r, val in enumerate([jnp.sum(dconv, axis=0, keepdims=True), dclg, dclb, dslg, dslb]):
            vec_ref[r:r + 1, :] += val

    return _rowwise(fn, "even_bwd_a", p.shape[0], tile, [p, z, _prev_halo(z, tile), dy],
                    [cw, cb, clg, clb, slg, slb, ws, wst, bst], [(DC, F32), (D, BF16)],
                    [((8, DC), F32), ((NHEAD_SGU, CHUNK, CHUNK), F32), ((CHUNK, NHEAD_SGU), F32)],
                    scratch=[pltpu.VMEM((tile + HALO, DC), F32), pltpu.VMEM((tile, DC), F32), _shifted_scratch(tile)])


def _even_bwd_b(p, z, dconv, dpb, cw, tile):
    def fn(i, ins, outs, scr):
        p_ref, z_ref, zh_ref, dc_ref, dcn_ref, dpb_ref, cw_ref = ins
        dp_ref, dcw_ref = outs
        zext, dcext, zs, dcs = scr
        last = pl.num_programs(0) - 1
        zext[:HALO, :] = jnp.where(i > 0, zh_ref[...], 0.0)
        zext[HALO:, :] = z_ref[...]
        dcext[:tile, :] = dc_ref[...]
        dcext[tile:, :] = jnp.where(i < last, dcn_ref[...], 0.0)
        _fill_shifted(zs, zext)
        _fill_shifted(dcs, dcext)
        dz = _conv_taps(cw_ref, dcext, dcs, tile, lambda k: (CONV_W - 1) - k)
        gate = p_ref[:, DC:2 * DC]
        s = _sigmoid(gate)
        dp_ref[:, :DC] = (dz * s).astype(BF16)
        dp_ref[:, DC:2 * DC] = (dz * p_ref[:, :DC] * s * (1.0 - s)).astype(BF16)
        dp_ref[:, 2 * DC:] = dpb_ref[...]
        dcv = dc_ref[...]

        @pl.when(i == 0)
        def _():
            dcw_ref[...] = jnp.zeros_like(dcw_ref)

        for k in range(CONV_W):
            dcw_ref[k:k + 1, :] += jnp.sum(dcv * _rows_at(zext, zs, HALO - (CONV_W - 1) + k, tile), axis=0, keepdims=True)

    return _rowwise(fn, "even_bwd_b", p.shape[0], tile,
                    [(p, tile, D, lambda i: (i, 0)), z, _prev_halo(z, tile), dconv, _next_halo(dconv, tile), dpb], [cw],
                    [(2 * D, BF16)], [((HALO, DC), F32)],
                    scratch=[pltpu.VMEM((tile + HALO, DC), F32), pltpu.VMEM((tile + HALO, DC), F32), _shifted_scratch(tile),
                             _shifted_scratch(tile)])


def _row_count(i, tile, nrows, offset, w):
    t = i * tile + offset + lax.broadcasted_iota(jnp.int32, (nrows, POOL_GD), 0)
    return jnp.minimum(t + 1, w).astype(F32)


def _odd_core(p, wg, scale, tile):
    def fn(i, ins, outs, scr):
        p_ref, ph_ref, wg_ref, sc_ref = ins
        d_ref, e_ref, es_ref = outs
        pext = scr[0]
        pext[:HALO, :] = jnp.where(i > 0, ph_ref[...], 0.0)
        pext[HALO:, :] = p_ref[...]
        for g, w in enumerate(POOL_WINDOWS):
            c0 = g * POOL_GD
            s = p_ref[:, c0:c0 + POOL_GD]
            for r in range(1, w):
                s = s + pext[pl.ds(HALO - r, tile), c0:c0 + POOL_GD]
            dg = (s / _row_count(i, tile, tile, 0, w) - p_ref[:, c0:c0 + POOL_GD]).astype(BF16)
            d_ref[:, c0:c0 + POOL_GD] = dg
            e = _dot(dg, wg_ref[g])
            e_ref[:, c0:c0 + POOL_GD] = e
            es_ref[:, c0:c0 + POOL_GD] = (e * sc_ref[:, c0:c0 + POOL_GD]).astype(BF16)

    return _rowwise(fn, "odd_core", p.shape[0], tile, [p, _prev_halo(p, tile)], [wg, scale],
                    [(D, BF16), (D, F32), (D, BF16)], scratch=[pltpu.VMEM((tile + HALO, D), F32)])


def _odd_bwd(des, e, d, wg, scale, tile):
    def fn(i, ins, outs, scr):
        des_ref, desn_ref, e_ref, d_ref, wg_ref, sc_ref = ins
        dp_ref, dsc_ref, dwg_ref = outs
        qext = scr[0]
        last = pl.num_programs(0) - 1
        desv = des_ref[...]
        _accum(dsc_ref, i, jnp.sum(desv * e_ref[...], axis=0, keepdims=True))
        de = (desv * sc_ref[...]).astype(BF16)
        den = (jnp.where(i < last, desn_ref[...], 0.0) * sc_ref[...]).astype(BF16)
        for g, w in enumerate(POOL_WINDOWS):
            c0 = g * POOL_GD
            deg = de[:, c0:c0 + POOL_GD]
            dd = _dot(deg, wg_ref[g], "nt")
            ddn = _dot(den[:, c0:c0 + POOL_GD], wg_ref[g], "nt")
            qext[:tile, c0:c0 + POOL_GD] = dd / _row_count(i, tile, tile, 0, w)
            qext[tile:, c0:c0 + POOL_GD] = ddn / _row_count(i, tile, HALO, tile, w)
            s = qext[:tile, c0:c0 + POOL_GD]
            for r in range(1, w):
                s = s + qext[pl.ds(r, tile), c0:c0 + POOL_GD]
            dp_ref[:, c0:c0 + POOL_GD] = (s - dd).astype(BF16)
            dw = _dot(d_ref[:, c0:c0 + POOL_GD], deg, "tn")

            @pl.when(i == 0)
            def _():
                dwg_ref[g] = dw

            @pl.when(i > 0)
            def _():
                dwg_ref[g] += dw

    return _rowwise(fn, "odd_bwd", des.shape[0], tile, [des, _next_halo(des, tile), e, d], [wg, scale], [(D, BF16)],
                    [((1, D), F32), ((len(POOL_WINDOWS), POOL_GD, POOL_GD), F32)],
                    scratch=[pltpu.VMEM((tile + HALO, D), F32)])


def _axis_slice(ref, axis, start, size):
    idx = [slice(None)] * len(ref.shape)
    idx[axis] = pl.ds(start, size)
    return ref.at[tuple(idx)]


def _cast_into_slot(pos, shard, kind, name):
    L, rs, cs = shard.shape
    tr = _row_tile(rs, 512)
    nr = rs // tr
    if kind == "row":
        full, omap = (L, NCHIP * rs, cs), (lambda l, i, p: (l, p[0] * nr + i, 0))
    else:
        full, omap = (L, rs, NCHIP * cs), (lambda l, i, p: (l, i, p[0]))

    def body(p_ref, s_ref, o_ref):
        o_ref[...] = s_ref[...].astype(BF16)

    grid_spec = pltpu.PrefetchScalarGridSpec(
        num_scalar_prefetch=1, grid=(L, nr), in_specs=[pl.BlockSpec((None, tr, cs), lambda l, i, p: (l, i, 0))],
        out_specs=pl.BlockSpec((None, tr, cs), omap))
    return pl.pallas_call(body, name=name, grid_spec=grid_spec, out_shape=jax.ShapeDtypeStruct(full, BF16),
                          compiler_params=_params(("parallel", "parallel")))(pos, shard)


def _peer(x, y, c, k):
    return (1 - x if k & 4 else x, 1 - y if k & 2 else y, 1 - c if k & 1 else c)


def _half_geometry(shape, kind):
    L, R, C = shape
    return (L, R // (2 * NCHIP), C) if kind == "row" else (L, R // 2, C // NCHIP)


def _sum_into_half(pos, recv, grad, kind, name):
    _, L, rh, cs = recv.shape
    if kind == "row":
        gmap = lambda l, p: (l, 2 * p[0] + p[1], 0)
    else:
        gmap = lambda l, p: (l, p[1], p[0])

    def body(p_ref, r_ref, g_ref, o_ref):
        acc = g_ref[...].astype(F32)
        for s in range(7):
            acc = acc + r_ref[s].astype(F32)
        o_ref[...] = acc

    grid_spec = pltpu.PrefetchScalarGridSpec(
        num_scalar_prefetch=1, grid=(L,),
        in_specs=[pl.BlockSpec((7, None, rh, cs), lambda l, p: (0, l, 0, 0)), pl.BlockSpec((None, rh, cs), gmap)],
        out_specs=pl.BlockSpec((None, rh, cs), lambda l, p: (l, p[1], 0)))
    return pl.pallas_call(body, name=name, grid_spec=grid_spec, out_shape=jax.ShapeDtypeStruct((L, 2 * rh, cs), F32),
                          compiler_params=_params(("parallel",)))(pos, _hbm(recv), _hbm(grad))


def _swap_halves(shards):
    n = len(shards)

    def body(*refs):
        outs = refs[n:2 * n]
        send_sems, recv_sems = refs[2 * n:]
        x, y, c = lax.axis_index("x"), lax.axis_index("y"), lax.axis_index("c")
        copies = []
        for t in range(n):
            rh = outs[t].shape[1] // 2
            half = outs[t].at[:, pl.ds(c * rh, rh), :]
            cp = pltpu.make_async_remote_copy(src_ref=half, dst_ref=half, send_sem=send_sems.at[t], recv_sem=recv_sems.at[t],
                                              device_id=(x, y, 1 - c), device_id_type=MESH)
            cp.start()
            copies.append(cp)
        for cp in copies:
            cp.wait()

    return pl.pallas_call(body, name="swap_halves", in_specs=[HBM] * n, out_specs=[HBM] * n,
                          out_shape=[pltpu.HBM(s.shape, s.dtype) for s in shards],
                          input_output_aliases={t: t for t in range(n)},
                          scratch_shapes=[pltpu.SemaphoreType.DMA((n,)), pltpu.SemaphoreType.DMA((n,))])(*map(_hbm, shards))


SEM = pl.BlockSpec(memory_space=pltpu.SEMAPHORE)
DATAFLOW = pltpu.SideEffectType.DATAFLOW_SIDE_EFFECTING
MAX_LAYERS = 4


def _gsem(t, layer, k):
    return (t * MAX_LAYERS + layer) * 3 + k


def _half_slot(ref, kind, d0, nd0, chip, core):
    if kind == "row":
        half = ref.shape[1] // (2 * NCHIP)
        return ref.at[pl.ds(d0, nd0), pl.ds((2 * chip + core) * half, half), :]
    size, half = ref.shape[2] // NCHIP, ref.shape[1] // 2
    return ref.at[pl.ds(d0, nd0), pl.ds(core * half, half), pl.ds(chip * size, size)]


def _other_chips(x, y):
    return [(1 - x, y, 2 * (1 - x) + y), (x, 1 - y, 2 * x + 1 - y), (1 - x, 1 - y, 2 * (1 - x) + 1 - y)]


def _gather_start(bufs, kinds, groups, after):
    n, na = len(bufs), len(after)

    def body(*refs):
        send_sems, recv_sems = refs[n + na], refs[n + na + 1]
        outs = refs[n + na + 2:]
        x, y, c = lax.axis_index("x"), lax.axis_index("y"), lax.axis_index("c")
        for group in groups:
            for t, li, d0, nd0 in group:
                mine = _half_slot(outs[t], kinds[t], d0, nd0, 2 * x + y, c)
                for k, (px, py, _) in enumerate(_other_chips(x, y)):
                    pltpu.make_async_remote_copy(src_ref=mine, dst_ref=mine, send_sem=send_sems.at[_gsem(t, li, k)],
                                                 recv_sem=recv_sems.at[_gsem(t, li, k)], device_id=(px, py, c),
                                                 device_id_type=MESH).start()

    sems = pltpu.SemaphoreType.DMA((n * MAX_LAYERS * 3,))
    res = pl.pallas_call(body, name="gather_start", in_specs=[HBM] * n + [ANY] * na, out_specs=[SEM, SEM] + [HBM] * n,
                         out_shape=[sems, sems] + [pltpu.HBM(b.shape, b.dtype) for b in bufs],
                         input_output_aliases={t: t + 2 for t in range(n)},
                         compiler_params=pltpu.CompilerParams(has_side_effects=DATAFLOW))(*map(_hbm, bufs), *after)
    return res[0], res[1], list(res[2:])


def _gather_forward(bufs, kinds, group, send_sems, recv_sems, after, name):
    m, na = len(bufs), len(after)

    def body(*refs):
        send, recv = refs[m], refs[m + 1]
        send2, recv2 = refs[m + 2 + na], refs[m + 3 + na]
        outs = refs[m + 4 + na:]
        x, y, c = lax.axis_index("x"), lax.axis_index("y"), lax.axis_index("c")
        for j, (t, li, d0, nd0) in enumerate(group):
            mine = _half_slot(outs[j], kinds[j], d0, nd0, 2 * x + y, c)
            for k, (px, py, chip) in enumerate(_other_chips(x, y)):
                got = _half_slot(outs[j], kinds[j], d0, nd0, chip, c)
                first = pltpu.make_async_remote_copy(src_ref=mine, dst_ref=got, send_sem=send.at[_gsem(t, li, k)],
                                                     recv_sem=recv.at[_gsem(t, li, k)], device_id=(px, py, c),
                                                     device_id_type=MESH)
                first.wait_send()
                first.wait_recv()
                pltpu.make_async_remote_copy(src_ref=got, dst_ref=got, send_sem=send2.at[3 * j + k], recv_sem=recv2.at[3 * j + k],
                                             device_id=(x, y, 1 - c), device_id_type=MESH).start()

    sems = pltpu.SemaphoreType.DMA((3 * m,))
    res = pl.pallas_call(body, name=name, in_specs=[HBM] * m + [SEM, SEM] + [ANY] * na, out_specs=[SEM, SEM] + [HBM] * m,
                         out_shape=[sems, sems] + [pltpu.HBM(b.shape, b.dtype) for b in bufs],
                         input_output_aliases={j: j + 2 for j in range(m)},
                         compiler_params=pltpu.CompilerParams(has_side_effects=DATAFLOW))(*map(_hbm, bufs), send_sems, recv_sems,
                                                                                          *map(_hbm, after))
    return res[0], res[1], list(res[2:])


def _gather_wait(bufs, kinds, group, send_sems, recv_sems, after, name):
    m, na = len(bufs), len(after)

    def body(*refs):
        send, recv = refs[m], refs[m + 1]
        outs = refs[m + 2 + na:]
        x, y, c = lax.axis_index("x"), lax.axis_index("y"), lax.axis_index("c")
        for j, (t, li, d0, nd0) in enumerate(group):
            for k, (_, _, chip) in enumerate(_other_chips(x, y)):
                cp = pltpu.make_async_remote_copy(src_ref=_half_slot(outs[j], kinds[j], d0, nd0, chip, c),
                                                  dst_ref=_half_slot(outs[j], kinds[j], d0, nd0, chip, 1 - c),
                                                  send_sem=send.at[3 * j + k], recv_sem=recv.at[3 * j + k],
                                                  device_id=(x, y, 1 - c), device_id_type=MESH)
                cp.wait_send()
                cp.wait_recv()

    res = pl.pallas_call(body, name=name, in_specs=[HBM] * m + [SEM, SEM] + [ANY] * na, out_specs=[HBM] * m,
                         out_shape=[pltpu.HBM(b.shape, b.dtype) for b in bufs],
                         input_output_aliases={j: j for j in range(m)},
                         compiler_params=pltpu.CompilerParams(has_side_effects=DATAFLOW))(*map(_hbm, bufs), send_sems, recv_sems,
                                                                                          *map(_hbm, after))
    return list(res)


def _gather_tiny(tiny, axes):
    n = len(tiny)
    out_shape = []
    for s_, ax in zip(tiny, axes):
        shp = list(s_.shape)
        shp[ax] *= NCHIP
        out_shape.append(jax.ShapeDtypeStruct(tuple(shp), s_.dtype))

    def body(*refs):
        ins, outs = refs[:n], refs[n:2 * n]
        send_sems, recv_sems, loc_sems = refs[2 * n:]
        x, y, c = lax.axis_index("x"), lax.axis_index("y"), lax.axis_index("c")
        mine = 2 * x + y
        chips = [(1 - x, y), (x, 1 - y), (1 - x, 1 - y)]
        copies = []
        for t in range(n):
            size = ins[t].shape[axes[t]]
            dst = _axis_slice(outs[t], axes[t], mine * size, size)
            copies.append(pltpu.make_async_copy(ins[t], dst, loc_sems.at[t]))
            for k, chip in enumerate(chips):
                copies.append(pltpu.make_async_remote_copy(src_ref=ins[t], dst_ref=dst, send_sem=send_sems.at[t, k],
                                                           recv_sem=recv_sems.at[t, k], device_id=(*chip, c),
                                                           device_id_type=MESH))
        for cp in copies:
            cp.start()
        for cp in copies:
            cp.wait()

    return pl.pallas_call(body, name="gather_tiny", in_specs=[ANY] * n, out_specs=[ANY] * n, out_shape=out_shape,
                          scratch_shapes=[pltpu.SemaphoreType.DMA((n, 3)), pltpu.SemaphoreType.DMA((n, 3)),
                                          pltpu.SemaphoreType.DMA((n,))])(*tiny)


def _grad_slice(ref, kind, d0, nd0, chip, core, rh, cs):
    if kind == "row":
        return ref.at[pl.ds(d0, nd0), pl.ds((2 * chip + core) * rh, rh), :]
    return ref.at[pl.ds(d0, nd0), pl.ds(core * rh, rh), pl.ds(chip * cs, cs)]


def _scatter_start(grads, recv, kinds, items, name, after=()):
    m, na = len(grads), len(after)

    def body(*refs):
        send_sems, recv_sems = refs[2 * m + na], refs[2 * m + na + 1]
        gout, rout = refs[2 * m + na + 2:3 * m + na + 2], refs[3 * m + na + 2:]
        x, y, c = lax.axis_index("x"), lax.axis_index("y"), lax.axis_index("c")
        for t in range(m):
            _, _, rh, cs = rout[t].shape
            d0, nd0 = items[t]
            for k in range(1, 8):
                px, py, pc = _peer(x, y, c, k)
                src = _grad_slice(gout[t], kinds[t], d0, nd0, 2 * px + py, pc, rh, cs)
                pltpu.make_async_remote_copy(src_ref=src, dst_ref=rout[t].at[k - 1, pl.ds(d0, nd0)],
                                             send_sem=send_sems.at[7 * t + k - 1], recv_sem=recv_sems.at[7 * t + k - 1],
                                             device_id=(px, py, pc), device_id_type=MESH).start()

    sems = pltpu.SemaphoreType.DMA((7 * m,))
    res = pl.pallas_call(body, name=name, in_specs=[HBM] * (2 * m) + [ANY] * na, out_specs=[SEM, SEM] + [HBM] * (2 * m),
                         out_shape=[sems, sems] + [pltpu.HBM(a.shape, a.dtype) for a in list(grads) + list(recv)],
                         input_output_aliases={j: j + 2 for j in range(2 * m)},
                         compiler_params=pltpu.CompilerParams(has_side_effects=DATAFLOW))(*map(_hbm, grads), *map(_hbm, recv),
                                                                                          *after)
    return res[0], res[1], list(res[2:m + 2]), list(res[m + 2:])


def _scatter_wait(recv, items, send_sems, recv_sems, name, after=()):
    m = len(recv)

    def body(*refs):
        send, rcv = refs[m], refs[m + 1]
        outs = refs[m + 2 + len(after):]
        x, y, c = lax.axis_index("x"), lax.axis_index("y"), lax.axis_index("c")
        for t in range(m):
            d0, nd0 = items[t]
            for k in range(1, 8):
                land = outs[t].at[k - 1, pl.ds(d0, nd0)]
                cp = pltpu.make_async_remote_copy(src_ref=land, dst_ref=land, send_sem=send.at[7 * t + k - 1],
                                                  recv_sem=rcv.at[7 * t + k - 1], device_id=_peer(x, y, c, k), device_id_type=MESH)
                cp.wait_send()
                cp.wait_recv()

    res = pl.pallas_call(body, name=name, in_specs=[HBM] * m + [SEM, SEM] + [ANY] * len(after), out_specs=[HBM] * m,
                         out_shape=[pltpu.HBM(a.shape, a.dtype) for a in recv],
                         input_output_aliases={j: j for j in range(m)},
                         compiler_params=pltpu.CompilerParams(has_side_effects=DATAFLOW))(*map(_hbm, recv), send_sems, recv_sems,
                                                                                          *map(_hbm, after))
    return list(res)


def _share_small(packed):
    rows = packed.shape[0]

    def body(in_ref, out_ref, send_sems, recv_sems, loc_sem):
        x, y, c = lax.axis_index("x"), lax.axis_index("y"), lax.axis_index("c")
        me = 4 * x + 2 * y + c
        copies = [pltpu.make_async_copy(in_ref, out_ref.at[me], loc_sem)]
        for k in range(1, 8):
            copies.append(pltpu.make_async_remote_copy(src_ref=in_ref, dst_ref=out_ref.at[me], send_sem=send_sems.at[k - 1],
                                                       recv_sem=recv_sems.at[k - 1], device_id=_peer(x, y, c, k),
                                                       device_id_type=MESH))
        for cp in copies:
            cp.start()
        for cp in copies:
            cp.wait()

    return pl.pallas_call(body, name="share_small", in_specs=[ANY], out_specs=ANY,
                          out_shape=jax.ShapeDtypeStruct((8, rows, 128), F32),
                          scratch_shapes=[pltpu.SemaphoreType.DMA((7,)), pltpu.SemaphoreType.DMA((7,)),
                                          pltpu.SemaphoreType.DMA])(packed)


def _sum_slots(buf, tile, name):
    _, rows, cols = buf.shape

    def body(b_ref, o_ref):
        acc = b_ref[0].astype(F32)
        for s in range(1, 8):
            acc = acc + b_ref[s].astype(F32)
        o_ref[...] = acc

    return pl.pallas_call(body, name=name, grid=(rows // tile,), in_specs=[pl.BlockSpec((8, tile, cols), lambda i: (0, i, 0))],
                          out_specs=pl.BlockSpec((tile, cols), lambda i: (i, 0)),
                          out_shape=jax.ShapeDtypeStruct((rows, cols), F32), compiler_params=_params(("parallel",)))(buf)


def _adamw(w, g, m, v, tile, name, after=(), emit_g=False):
    rows, cols = w.shape

    def body(w_ref, g_ref, m_ref, v_ref, *rest):
        d_ref, mo_ref, vo_ref = rest[len(after):len(after) + 3]
        gv = g_ref[...]
        if emit_g:
            rest[-1][...] = gv
        mn = ADAM_B1 * m_ref[...] + (1.0 - ADAM_B1) * gv
        vn = ADAM_B2 * v_ref[...] + (1.0 - ADAM_B2) * (gv * gv)
        m_hat = mn / (1.0 - ADAM_B1 ** ADAM_STEP)
        v_hat = vn / (1.0 - ADAM_B2 ** ADAM_STEP)
        d_ref[...] = -ADAM_LR * (m_hat / (jnp.sqrt(v_hat) + ADAM_EPS) + ADAM_WD * w_ref[...])
        mo_ref[...] = mn
        vo_ref[...] = vn

    spec = pl.BlockSpec((tile, cols), lambda i: (i, 0))
    sds = jax.ShapeDtypeStruct((rows, cols), F32)
    n_out = 4 if emit_g else 3
    return pl.pallas_call(body, name=name, grid=(rows // tile,), in_specs=[spec] * 4 + [ANY] * len(after), out_specs=[spec] * n_out,
                          out_shape=[sds] * n_out, compiler_params=_params(("parallel",)))(w, g, m, v, *map(_hbm, after))


def _row_tile(rows, cap):
    best = 8
    for t in range(8, min(rows, cap) + 1, 8):
        if rows % t == 0:
            best = t
    return best


def _pack(arrs):
    parts = []
    for a in arrs:
        r = a.size // 128
        r8 = -(-r // 8) * 8
        parts.append(jnp.pad(a.reshape(r, 128).astype(F32), ((0, r8 - r), (0, 0))))
    return jnp.concatenate(parts, axis=0)


def _unpack(packed, shapes):
    out, r0 = [], 0
    for shp in shapes:
        r = math.prod(shp) // 128
        out.append(packed[r0:r0 + r].reshape(shp))
        r0 += -(-r // 8) * 8
    return out


BIG = {
    "ffn1_w_gu": "col", "ffn1_w_down": "row", "ev_w_in": "col", "ev_w_out": "row", "od_w_in": "row", "od_w_group": "row",
    "od_w_out": "row", "xa_w_q": "row", "xa_w_kv": "col", "xa_w_o": "row", "ffn2_w_gu": "col", "ffn2_w_down": "row",
}
TINY_SHARDED = {"ev_conv_w": 2, "od_scale": 1}
WEIGHTS = ['ffn1_pre_g', 'ffn1_w_gu', 'ffn1_w_down', 'ffn1_post_g', 'mix_pre_g', 'mix_post_g', 'ev_w_in', 'ev_conv_w',
           'ev_conv_b', 'ev_conv_ln_g', 'ev_conv_ln_b', 'ev_sgu_ln_g', 'ev_sgu_ln_b', 'ev_sgu_w', 'ev_sgu_b', 'ev_w_out',
           'od_w_in', 'od_w_group', 'od_scale', 'od_w_out', 'xa_pre_g', 'xa_mem_g', 'xa_w_q', 'xa_w_kv', 'xa_w_o', 'xa_post_g',
           'ffn2_pre_g', 'ffn2_w_gu', 'ffn2_w_down', 'ffn2_post_g']


def _as3d(a):
    return a.reshape((-1,) + a.shape[-2:]) if a.ndim == 4 else a


class _WeightView:
    def __init__(self, store, view):
        self.store, self.view = store, view

    def __getitem__(self, k):
        return self.view(k)


class _Grads:
    def __init__(self):
        self.buf = {}
        self.fresh = []

    def add(self, name, layer, nlayers, a, b, tm, tn, n_outer=True):
        self.buf[name] = _matmul(a, b, "tn", BF16, "dw_" + name, tm, tn, out_l=layer, out_stack=nlayers,
                                 out_buf=self.buf.get(name), n_outer=n_outer)
        self.fresh.append(name)

    def take(self):
        names, self.fresh = self.fresh, []
        return names


def _local_step(x, mem, target, W, S, tiles=None, before_sub=None, after_sub_bwd=None):
    T, TW, tm = tiles or (min(512, S), min(256, S), min(512, S))
    TF = T
    row = lambda v: v.reshape(1, -1)
    subs = []
    small = {k: [None] * W[k].shape[0] for k in WEIGHTS if k not in BIG}
    g = _Grads()

    def ffn_fwd(tag, l, h, n, g_next):
        gu, a = _up_swiglu(n, W[tag + "_w_gu"], l, TF)
        f, h2, n2 = _post_mm(h, a, W[tag + "_w_down"], l, row(W[tag + "_post_g"][l]), 0.5, g_next, T)
        subs.append(dict(kind="ffn", tag=tag, l=l, h=h, n=n, gu=gu, a=a, f=f, scale=0.5, pre=tag + "_pre_g", post=tag + "_post_g"))
        return h2, n2

    def ffn_bwd(s, df):
        tag, l = s["tag"], s["l"]
        g.add(tag + "_w_down", l, 4, s["a"], df, DFF // 2, D, n_outer=False)
        dgu = _down_dx_swiglu_bwd(df, s["gu"], W[tag + "_w_down"], l, TF)
        g.add(tag + "_w_gu", l, 4, s["n"], dgu, D, 512, n_outer=False)
        return dgu, tag + "_w_gu", l, TF

    def xa_fwd(l, h, n, g_next):
        q = _matmul(n, W["xa_w_q"], "nn", BF16, "xa_q", tm, D, b_l=l)
        mn = _mem_norm(mem, row(W["xa_mem_g"][l]))
        kv = _matmul(mn, W["xa_w_kv"], "nn", BF16, "xa_kv", NMEM, D, b_l=l)
        o = _attn(q, kv, T)
        cx, h2, n2 = _post_mm(h, o, W["xa_w_o"], l, row(W["xa_post_g"][l]), 1.0, g_next, T)
        subs.append(dict(kind="xa", l=l, h=h, n=n, q=q, mn=mn, kv=kv, o=o, f=cx, scale=1.0, pre="xa_pre_g", post="xa_post_g"))
        return h2, n2

    def xa_bwd(s, dc):
        l = s["l"]
        do = _matmul(dc, W["xa_w_o"], "nt", BF16, "xa_o_dx", tm, D, b_l=l)
        g.add("xa_w_o", l, 4, s["o"], dc, D, 512)
        dq, dkv = _attn_bwd(s["q"], s["kv"], do, T)
        g.add("xa_w_q", l, 4, s["n"], dq, D, 512)
        dkvb = dkv.astype(BF16)
        g.add("xa_w_kv", l, 4, s["mn"], dkvb, D, 512)
        dmn = _matmul(dkvb, W["xa_w_kv"], "nt", F32, "xa_kv_dx", NMEM, D, b_l=l)
        small["xa_mem_g"][l] = _mem_gain_bwd(mem, dmn)[0]
        return dq, "xa_w_q", l, T

    def even_params(e):
        return (W["ev_conv_w"][e], row(W["ev_conv_b"][e]), row(W["ev_conv_ln_g"][e]), row(W["ev_conv_ln_b"][e]),
                row(W["ev_sgu_ln_g"][e]), row(W["ev_sgu_ln_b"][e]), W["ev_sgu_w"][e])

    def even_fwd(l, h, n, g_next):
        e = l // 2
        p = _matmul(n, W["ev_w_in"], "nn", F32, "ev_in", tm, D, b_l=e)
        cw, cb, clg, clb, slg, slb, ws = even_params(e)
        y, z = _even_core(p, cw, cb, clg, clb, slg, slb, ws, W["ev_sgu_b"][e].T, TW)
        m, h2, n2 = _post_mm(h, y, W["ev_w_out"], e, row(W["mix_post_g"][l]), 1.0, g_next, T)
        subs.append(dict(kind="even", l=l, h=h, n=n, p=p, y=y, z=z, f=m, scale=1.0, pre="mix_pre_g", post="mix_post_g"))
        return h2, n2

    def even_bwd(s, dm):
        l = s["l"]
        e = l // 2
        dy = _matmul(dm, W["ev_w_out"], "nt", F32, "ev_out_dx", tm, D, b_l=e)
        g.add("ev_w_out", e, 2, s["y"], dm, D, 512)
        cw, cb, clg, clb, slg, slb, ws = even_params(e)
        dconv, dpb, vecs, dws, dbst = _even_bwd_a(s["p"], s["z"], dy, cw, cb, clg, clb, slg, slb, ws,
                                                   jnp.swapaxes(ws, 1, 2), W["ev_sgu_b"][e].T, TW)
        dp, dcw = _even_bwd_b(s["p"], s["z"], dconv, dpb, cw, TW)
        for r, name in enumerate(["ev_conv_b", "ev_conv_ln_g", "ev_conv_ln_b", "ev_sgu_ln_g", "ev_sgu_ln_b"]):
            small[name][e] = vecs[r]
        small["ev_sgu_w"][e] = dws
        small["ev_sgu_b"][e] = dbst.T
        small["ev_conv_w"][e] = dcw[:CONV_W]
        g.add("ev_w_in", e, 2, s["n"], dp, D, 512)
        return dp, "ev_w_in", e, T

    def group_w(o):
        ng = len(POOL_WINDOWS)
        return (W["od_w_group"].reshape(-1, POOL_GD, POOL_GD), (ng, POOL_GD, POOL_GD), lambda i: (o, 0, 0))

    def odd_fwd(l, h, n, g_next):
        o = l // 2
        p = _matmul(n, W["od_w_in"], "nn", F32, "od_in", tm, D, b_l=o)
        d, e, es = _odd_core(p, group_w(o), row(W["od_scale"][o]), T)
        m, h2, n2 = _post_mm(h, es, W["od_w_out"], o, row(W["mix_post_g"][l]), 1.0, g_next, T)
        subs.append(dict(kind="odd", l=l, h=h, n=n, d=d, e=e, es=es, f=m, scale=1.0, pre="mix_pre_g", post="mix_post_g"))
        return h2, n2

    def odd_bwd(s, dm):
        l = s["l"]
        o = l // 2
        des = _matmul(dm, W["od_w_out"], "nt", F32, "od_out_dx", tm, D, b_l=o)
        g.add("od_w_out", o, 2, s["es"], dm, D, 512)
        dp, dsc, dwg = _odd_bwd(des, s["e"], s["d"], group_w(o), row(W["od_scale"][o]), T)
        small["od_scale"][o] = dsc[0]
        small["od_w_group_full"][o] = dwg
        g.add("od_w_in", o, 2, s["n"], dp, D, 512)
        return dp, "od_w_in", o, T

    small["od_w_group_full"] = [None, None]
    order = []
    for l in range(4):
        order += [("ffn1", l), ("even" if l % 2 == 0 else "odd", l), ("xa", l), ("ffn2", l)]
    pre_of = {"ffn1": "ffn1_pre_g", "even": "mix_pre_g", "odd": "mix_pre_g", "xa": "xa_pre_g", "ffn2": "ffn2_pre_g"}
    h = x
    n = _prenorm(h, row(W["ffn1_pre_g"][0]), T)
    for idx, (kind, l) in enumerate(order):
        if before_sub is not None:
            before_sub(kind, l, h)
        g_next = row(W[pre_of[order[idx + 1][0]]][order[idx + 1][1]]) if idx + 1 < len(order) else None
        if kind in ("ffn1", "ffn2"):
            h, n = ffn_fwd(kind, l, h, n, g_next)
        elif kind == "xa":
            h, n = xa_fwd(l, h, n, g_next)
        elif kind == "even":
            h, n = even_fwd(l, h, n, g_next)
        else:
            h, n = odd_fwd(l, h, n, g_next)

    top = subs[-1]
    dh, df, loss_acc, dgp = _loss_top(h, target, top["f"], row(W[top["post"]][top["l"]]), top["scale"], T)
    small[top["post"]][top["l"]] = dgp[0]
    for idx in range(len(subs) - 1, -1, -1):
        s = subs[idx]
        da, wname, wl, bt = {"ffn": ffn_bwd, "xa": xa_bwd, "even": even_bwd, "odd": odd_bwd}[s["kind"]](s, df)
        names = g.take()
        deps = [g.buf[k] for k in names]
        if idx > 0:
            sp = subs[idx - 1]
            dh, df, dg_pre, dg_post = _boundary_mm(da, W[wname], wl, s["h"], row(W[s["pre"]][s["l"]]), dh, sp["f"],
                                                   row(W[sp["post"]][sp["l"]]), sp["scale"], bt, deps=deps)
            small[sp["post"]][sp["l"]] = dg_post[0]
        else:
            dh, dg_pre = _boundary_mm(da, W[wname], wl, s["h"], row(W[s["pre"]][s["l"]]), dh, None, None, None, bt, deps=deps)
        small[s["pre"]][s["l"]] = dg_pre[0]
        if after_sub_bwd is not None:
            after_sub_bwd(order[idx][0], s["l"], g.buf, small)
            g.fresh = names + g.fresh
    return loss_acc[0, 0], dh, g.buf, small


def _step(P, S):
    x, mem, target = P["x"][0], P["mem"][0], P["loss_target"][0]
    chip = 2 * lax.axis_index("x") + lax.axis_index("y")
    pos = jnp.stack([chip, lax.axis_index("c")]).astype(jnp.int32)
    order = list(BIG)
    tix = {k: t for t, k in enumerate(order)}
    kinds = [BIG[k] for k in order]
    per = {k: (4 if k == "od_w_group" else 1) for k in order}
    members = {"ffn1": ["ffn1_w_gu", "ffn1_w_down"], "ffn2": ["ffn2_w_gu", "ffn2_w_down"], "xa": ["xa_w_q", "xa_w_kv", "xa_w_o"],
               "even": ["ev_w_in", "ev_w_out"], "odd": ["od_w_in", "od_w_group", "od_w_out"]}
    index_of = lambda kind, l: l // 2 if kind in ("even", "odd") else l
    sub_order = []
    for l in range(4):
        sub_order += [("ffn1", l), ("even" if l % 2 == 0 else "odd", l), ("xa", l), ("ffn2", l)]
    groups = {(kind, l): [(tix[k], index_of(kind, l), index_of(kind, l) * per[k], per[k]) for k in members[kind]]
              for kind, l in sub_order}
    tiny = _gather_tiny([P[k] for k in TINY_SHARDED], list(TINY_SHARDED.values()))
    slots = [_cast_into_slot(pos, _as3d(P[k]), BIG[k], "cast_" + k) for k in order]
    g_send, g_recv, bufs = _gather_start(slots, kinds, [groups[s_] for s_ in sub_order], tiny)
    W = {k: P[k] for k in WEIGHTS if k not in BIG}
    W.update(zip(TINY_SHARDED, tiny))
    W.update(zip(order, bufs))

    def view(k):
        return W[k].reshape(2, 4, POOL_GD, POOL_GD) if k == "od_w_group" else W[k]

    Wv = _WeightView(W, view)

    forwarded = {}

    def forward(sub, after):
        ks = members[sub[0]]
        s2, r2, thru = _gather_forward([W[k] for k in ks], [BIG[k] for k in ks], groups[sub], g_send, g_recv, after,
                                       "gather_forward_%s%d" % sub)
        W.update(zip(ks, thru))
        forwarded[sub] = (s2, r2)

    def before_sub(kind, l, h):
        sub = (kind, l)
        if sub == sub_order[0]:
            forward(sub, (h,))
        ks = members[kind]
        s2, r2 = forwarded.pop(sub)
        W.update(zip(ks, _gather_wait([W[k] for k in ks], [BIG[k] for k in ks], groups[sub], s2, r2, (h,),
                                      "gather_wait_%s%d" % sub)))
        nxt = sub_order.index(sub) + 1
        if nxt < len(sub_order):
            forward(sub_order[nxt], (h, W[ks[0]]))

    recv = {k: None for k in order}
    pending = []
    small_names = [k for k in WEIGHTS if k not in BIG]
    gsmall = {}

    def after_sub_bwd(kind, l, big, small):
        ks = [k for k in members[kind] if k != "od_w_group"]
        items = [(index_of(kind, l), 1)] * len(ks)
        after = ()
        if (kind, l) == ("odd", 1):
            big["od_w_group"] = jnp.stack(small.pop("od_w_group_full")).astype(BF16).reshape(8, POOL_GD, POOL_GD)
            ks, items = ks + ["od_w_group"], items + [(0, 8)]
        if (kind, l) == ("ffn1", 0):
            small_full = [jnp.stack(small[k]) for k in small_names]
            packed = _pack(small_full)
            summed = _sum_slots(_share_small(packed), _row_tile(packed.shape[0], 512), "sum_small")
            gsmall.update(zip(small_names, _unpack(summed, [a.shape for a in small_full])))
            after = (summed,)
        for k in ks:
            if recv[k] is None:
                recv[k] = lax.empty((7,) + _half_geometry(big[k].shape, BIG[k]), BF16)
        s_send, s_recv, g_thru, r_thru = _scatter_start([big[k] for k in ks], [recv[k] for k in ks], [BIG[k] for k in ks],
                                                        items, "scatter_start_%s%d" % (kind, l), after)
        big.update(zip(ks, g_thru))
        recv.update(zip(ks, r_thru))
        pending.append((kind, l, ks, items, s_send, s_recv))

    loss, grad_x, big, small = _local_step(x, mem, target, Wv, S, before_sub=before_sub, after_sub_bwd=after_sub_bwd)
    last = (big[members["ffn1"][0]],)
    for kind, l, ks, items, s_send, s_recv in pending:
        recv.update(zip(ks, _scatter_wait([recv[k] for k in ks], items, s_send, s_recv, "scatter_wait_%s%d" % (kind, l), last)))
    gsh = dict(zip(order, _swap_halves([_sum_into_half(pos, recv[k], big[k], BIG[k], "sum_" + k) for k in order])))
    for k, ax in TINY_SHARDED.items():
        size = P[k].shape[ax]
        gsmall[k] = lax.dynamic_slice_in_dim(gsmall[k], chip * size, size, axis=ax)

    grads, delta, new_m, new_v = {}, {}, {}, {}
    for k in order:
        shp = P[k].shape
        cols = shp[-1]
        flat = lambda a: a.reshape(-1, cols)
        gk = flat(gsh[k])
        d_, m_, v_, g_ = _adamw(flat(P[k]), gk, flat(P["m_" + k]), flat(P["v_" + k]), _row_tile(gk.shape[0], 256), "adamw_" + k,
                                emit_g=True)
        grads[k], delta[k], new_m[k], new_v[k] = g_.reshape(shp), d_.reshape(shp), m_.reshape(shp), v_.reshape(shp)
    pk = lambda pre: _pack([P[pre + k] for k in small_names])
    d_, m_, v_ = _adamw(pk(""), _pack([gsmall[k] for k in small_names]), pk("m_"), pk("v_"),
                        pk("").shape[0], "adamw_small", last)
    shapes = [P[k].shape for k in small_names]
    for dst, src in ((delta, d_), (new_m, m_), (new_v, v_)):
        dst.update(zip(small_names, _unpack(src, shapes)))
    grads.update(gsmall)

    loss = lax.psum(loss, ("x", "y", "c"))
    out = [loss, grad_x[None]]
    for grp in (grads, delta, new_m, new_v):
        out += [grp[k] for k in WEIGHTS]
    return tuple(out)


def kernel(x, mem, ffn1_pre_g, ffn1_w_gu, ffn1_w_down, ffn1_post_g, mix_pre_g, mix_post_g, ev_w_in, ev_conv_w, ev_conv_b, ev_conv_ln_g, ev_conv_ln_b, ev_sgu_ln_g, ev_sgu_ln_b, ev_sgu_w, ev_sgu_b, ev_w_out, od_w_in, od_w_group, od_scale, od_w_out, xa_pre_g, xa_mem_g, xa_w_q, xa_w_kv, xa_w_o, xa_post_g, ffn2_pre_g, ffn2_w_gu, ffn2_w_down, ffn2_post_g, loss_target, m_ffn1_pre_g, m_ffn1_w_gu, m_ffn1_w_down, m_ffn1_post_g, m_mix_pre_g, m_mix_post_g, m_ev_w_in, m_ev_conv_w, m_ev_conv_b, m_ev_conv_ln_g, m_ev_conv_ln_b, m_ev_sgu_ln_g, m_ev_sgu_ln_b, m_ev_sgu_w, m_ev_sgu_b, m_ev_w_out, m_od_w_in, m_od_w_group, m_od_scale, m_od_w_out, m_xa_pre_g, m_xa_mem_g, m_xa_w_q, m_xa_w_kv, m_xa_w_o, m_xa_post_g, m_ffn2_pre_g, m_ffn2_w_gu, m_ffn2_w_down, m_ffn2_post_g, v_ffn1_pre_g, v_ffn1_w_gu, v_ffn1_w_down, v_ffn1_post_g, v_mix_pre_g, v_mix_post_g, v_ev_w_in, v_ev_conv_w, v_ev_conv_b, v_ev_conv_ln_g, v_ev_conv_ln_b, v_ev_sgu_ln_g, v_ev_sgu_ln_b, v_ev_sgu_w, v_ev_sgu_b, v_ev_w_out, v_od_w_in, v_od_w_group, v_od_scale, v_od_w_out, v_xa_pre_g, v_xa_mem_g, v_xa_w_q, v_xa_w_kv, v_xa_w_o, v_xa_post_g, v_ffn2_pre_g, v_ffn2_w_gu, v_ffn2_w_down, v_ffn2_post_g):
    P = dict(locals())
    return _step(P, x.shape[1])
```
